```python
import math
import jax, jax.numpy as jnp
from jax import lax
import numpy as np

D_MODEL = 1024
BATCH = 8
SEQ = 8192
DEPTH = 4

CHUNK = 64
N_META = 16
D_MIX = D_MODEL
D_POOL = D_MIX // 2
D_RNN = D_MIX // 2
POOL_WINDOWS = (2, 4, 8, 16)
N_POOL_GROUPS = len(POOL_WINDOWS)
POOL_GROUP_DIM = D_POOL // N_POOL_GROUPS
N_RNN_HEADS = 8
RNN_HEAD_DIM = D_RNN // N_RNN_HEADS
CONV_WIDTH = 4
LRU_C = 8.0
D_IN_PROJ = D_POOL + D_RNN + D_RNN
D_FF = 4 * D_MODEL
EPS = 1e-6

kernel_name = "hybrid_pool_rglru_encoder"


def rms_norm(x, g):
    xf = x.astype(jnp.float32)
    y = xf * lax.rsqrt(jnp.mean(xf * xf, axis=-1, keepdims=True) + EPS)
    return (y * g.astype(jnp.float32)).astype(x.dtype)


def multiscale_pool_mixer(u, pool_w, pool_b, pool_scale):
    B, T, _ = u.shape
    uf = u.astype(jnp.float32)
    cs = jnp.concatenate([jnp.zeros((B, 1, D_POOL), jnp.float32), jnp.cumsum(uf, axis=1)], axis=1)
    upper = cs[:, 1:]
    t_idx = jnp.arange(T, dtype=jnp.float32)[None, :, None]
    pooled = []
    for g, k in enumerate(POOL_WINDOWS):
        sl = slice(g * POOL_GROUP_DIM, (g + 1) * POOL_GROUP_DIM)
        cs_g = cs[:, :, sl]
        lower = jnp.pad(cs_g[:, :T + 1 - k], ((0, 0), (k - 1, 0), (0, 0)))
        count = jnp.minimum(t_idx + 1.0, float(k))
        pooled.append((upper[:, :, sl] - lower) / count)
    pooled = jnp.concatenate(pooled, axis=-1) - uf
    pg = pooled.astype(u.dtype).reshape(B, T, N_POOL_GROUPS, POOL_GROUP_DIM)
    mapped = jnp.einsum('btgi,gij->btgj', pg, pool_w).reshape(B, T, D_POOL) + pool_b
    return mapped * pool_scale


def rglru_mixer(u, gate, conv_w, conv_b, gate_r_w, gate_r_b, gate_i_w, gate_i_b, lru_lambda):
    B, T, _ = u.shape
    upad = jnp.pad(u, ((0, 0), (CONV_WIDTH - 1, 0), (0, 0)))
    xc = conv_b + sum(upad[:, k:k + T] * conv_w[k] for k in range(CONV_WIDTH))
    xh = xc.reshape(B, T, N_RNN_HEADS, RNN_HEAD_DIM)
    r = jax.nn.sigmoid((jnp.einsum('bthi,hij->bthj', xh, gate_r_w).reshape(B, T, D_RNN) + gate_r_b).astype(jnp.float32))
    i = jax.nn.sigmoid((jnp.einsum('bthi,hij->bthj', xh, gate_i_w).reshape(B, T, D_RNN) + gate_i_b).astype(jnp.float32))
    log_a = -LRU_C * r * jax.nn.softplus(-lru_lambda.astype(jnp.float32))
    a = jnp.exp(log_a)
    mult = jnp.sqrt(-jnp.expm1(2.0 * log_a))
    b = mult * (i * xc.astype(jnp.float32))

    def combine(left, right):
        a_l, b_l = left
        a_r, b_r = right
        return a_l * a_r, a_r * b_l + b_r

    _, h = lax.associative_scan(combine, (a, b), axis=1)
    return h.astype(u.dtype) * jax.nn.gelu(gate)


def hybrid_mixer(xn, w_in, pool_w, pool_b, pool_scale, conv_w, conv_b, gate_r_w, gate_r_b,
                 gate_i_w, gate_i_b, lru_lambda, group_norm_g, w_out):
    proj = xn @ w_in
    u_pool = proj[..., :D_POOL]
    u_rnn = proj[..., D_POOL:D_POOL + D_RNN]
    u_gate = proj[..., D_POOL + D_RNN:]
    y_pool = multiscale_pool_mixer(u_pool, pool_w, pool_b, pool_scale)
    y_rnn = rglru_mixer(u_rnn, u_gate, conv_w, conv_b, gate_r_w, gate_r_b, gate_i_w, gate_i_b, lru_lambda)
    y = jnp.concatenate([rms_norm(y_pool, group_norm_g[:D_POOL]),
                         rms_norm(y_rnn, group_norm_g[D_POOL:])], axis=-1)
    return y @ w_out


def sq_relu_mlp(xn, w_up, w_down):
    h = jax.nn.relu(xn @ w_up)
    return (h * h) @ w_down


def _fwd_setup_inputs(seed: int = 0) -> dict:
    key = jax.random.key(seed)
    ks = jax.random.split(key, 24)
    f32 = jnp.float32
    nrm = lambda k, shape, s: jax.random.normal(k, shape, f32) * s
    a0 = jax.random.uniform(ks[11], (DEPTH, D_RNN), f32, 0.9, 0.999)
    p = a0 ** (1.0 / LRU_C)
    lru_lambda = jnp.log(p) - jnp.log1p(-p)
    return {
        "x": nrm(ks[0], (BATCH, SEQ, D_MODEL), 1.0),
        "meta_tokens": nrm(ks[1], (N_META, D_MODEL), 1.0),
        "mix_norm_g": 1.0 + nrm(ks[2], (DEPTH, D_MODEL), 0.1),
        "w_in": nrm(ks[3], (DEPTH, D_MODEL, D_IN_PROJ), D_MODEL ** -0.5),
        "pool_w": nrm(ks[4], (DEPTH, N_POOL_GROUPS, POOL_GROUP_DIM, POOL_GROUP_DIM), POOL_GROUP_DIM ** -0.5),
        "pool_b": nrm(ks[5], (DEPTH, D_POOL), 0.02),
        "pool_scale": 0.5 + nrm(ks[6], (DEPTH, D_POOL), 0.1),
        "conv_w": nrm(ks[7], (DEPTH, CONV_WIDTH, D_RNN), CONV_WIDTH ** -0.5),
        "conv_b": nrm(ks[8], (DEPTH, D_RNN), 0.02),
        "gate_r_w": nrm(ks[9], (DEPTH, N_RNN_HEADS, RNN_HEAD_DIM, RNN_HEAD_DIM), RNN_HEAD_DIM ** -0.5),
        "gate_r_b": nrm(ks[10], (DEPTH, D_RNN), 0.02),
        "gate_i_w": nrm(ks[12], (DEPTH, N_RNN_HEADS, RNN_HEAD_DIM, RNN_HEAD_DIM), RNN_HEAD_DIM ** -0.5),
        "gate_i_b": nrm(ks[13], (DEPTH, D_RNN), 0.02),
        "lru_lambda": lru_lambda,
        "group_norm_g": 1.0 + nrm(ks[14], (DEPTH, D_MIX), 0.1),
        "w_out": nrm(ks[15], (DEPTH, D_MIX, D_MODEL), D_MIX ** -0.5),
        "mlp_norm_g": 1.0 + nrm(ks[16], (DEPTH, D_MODEL), 0.1),
        "w_up": nrm(ks[17], (DEPTH, D_MODEL, D_FF), D_MODEL ** -0.5),
        "w_down": nrm(ks[18], (DEPTH, D_FF, D_MODEL), D_FF ** -0.5),
        "final_norm_g": 1.0 + nrm(ks[19], (D_MODEL,), 0.1),
    }


def _fwd_reference(x, meta_tokens, mix_norm_g, w_in, pool_w, pool_b, pool_scale, conv_w, conv_b,
              gate_r_w, gate_r_b, gate_i_w, gate_i_b, lru_lambda, group_norm_g, w_out,
              mlp_norm_g, w_up, w_down, final_norm_g):
    B = x.shape[0]
    meta = jnp.broadcast_to(meta_tokens.astype(x.dtype)[None], (B, N_META, D_MODEL))
    h = jnp.concatenate([meta, x], axis=1)
    for l in range(DEPTH):
        h = h + hybrid_mixer(rms_norm(h, mix_norm_g[l]), w_in[l], pool_w[l], pool_b[l], pool_scale[l],
                             conv_w[l], conv_b[l], gate_r_w[l], gate_r_b[l], gate_i_w[l], gate_i_b[l],
                             lru_lambda[l], group_norm_g[l], w_out[l])
        h = h + sq_relu_mlp(rms_norm(h, mlp_norm_g[l]), w_up[l], w_down[l])
    h = rms_norm(h, final_norm_g)
    return h[:, N_META:]


import jax as _jax
import jax.numpy as _jnp

TWIN_FORMAT = 'train_step'
FWD_PARAMS = ['x', 'meta_tokens', 'mix_norm_g', 'w_in', 'pool_w', 'pool_b', 'pool_scale', 'conv_w', 'conv_b', 'gate_r_w', 'gate_r_b', 'gate_i_w', 'gate_i_b', 'lru_lambda', 'group_norm_g', 'w_out', 'mlp_norm_g', 'w_up', 'w_down', 'final_norm_g']
TWIN_WEIGHTS = ['meta_tokens', 'mix_norm_g', 'w_in', 'pool_w', 'pool_b', 'pool_scale', 'conv_w', 'conv_b', 'gate_r_w', 'gate_r_b', 'gate_i_w', 'gate_i_b', 'lru_lambda', 'group_norm_g', 'w_out', 'mlp_norm_g', 'w_up', 'w_down', 'final_norm_g']
TWIN_DIFF_INPUT = 'x'
TWIN_INPUTS = ['x', 'meta_tokens', 'mix_norm_g', 'w_in', 'pool_w', 'pool_b', 'pool_scale', 'conv_w', 'conv_b', 'gate_r_w', 'gate_r_b', 'gate_i_w', 'gate_i_b', 'lru_lambda', 'group_norm_g', 'w_out', 'mlp_norm_g', 'w_up', 'w_down', 'final_norm_g', 'loss_target', 'm_meta_tokens', 'm_mix_norm_g', 'm_w_in', 'm_pool_w', 'm_pool_b', 'm_pool_scale', 'm_conv_w', 'm_conv_b', 'm_gate_r_w', 'm_gate_r_b', 'm_gate_i_w', 'm_gate_i_b', 'm_lru_lambda', 'm_group_norm_g', 'm_w_out', 'm_mlp_norm_g', 'm_w_up', 'm_w_down', 'm_final_norm_g', 'v_meta_tokens', 'v_mix_norm_g', 'v_w_in', 'v_pool_w', 'v_pool_b', 'v_pool_scale', 'v_conv_w', 'v_conv_b', 'v_gate_r_w', 'v_gate_r_b', 'v_gate_i_w', 'v_gate_i_b', 'v_lru_lambda', 'v_group_norm_g', 'v_w_out', 'v_mlp_norm_g', 'v_w_up', 'v_w_down', 'v_final_norm_g']
TWIN_OUTPUTS = ['loss', 'grad_x', 'grad_meta_tokens', 'grad_mix_norm_g', 'grad_w_in', 'grad_pool_w', 'grad_pool_b', 'grad_pool_scale', 'grad_conv_w', 'grad_conv_b', 'grad_gate_r_w', 'grad_gate_r_b', 'grad_gate_i_w', 'grad_gate_i_b', 'grad_lru_lambda', 'grad_group_norm_g', 'grad_w_out', 'grad_mlp_norm_g', 'grad_w_up', 'grad_w_down', 'grad_final_norm_g', 'delta_meta_tokens', 'delta_mix_norm_g', 'delta_w_in', 'delta_pool_w', 'delta_pool_b', 'delta_pool_scale', 'delta_conv_w', 'delta_conv_b', 'delta_gate_r_w', 'delta_gate_r_b', 'delta_gate_i_w', 'delta_gate_i_b', 'delta_lru_lambda', 'delta_group_norm_g', 'delta_w_out', 'delta_mlp_norm_g', 'delta_w_up', 'delta_w_down', 'delta_final_norm_g', 'new_m_meta_tokens', 'new_m_mix_norm_g', 'new_m_w_in', 'new_m_pool_w', 'new_m_pool_b', 'new_m_pool_scale', 'new_m_conv_w', 'new_m_conv_b', 'new_m_gate_r_w', 'new_m_gate_r_b', 'new_m_gate_i_w', 'new_m_gate_i_b', 'new_m_lru_lambda', 'new_m_group_norm_g', 'new_m_w_out', 'new_m_mlp_norm_g', 'new_m_w_up', 'new_m_w_down', 'new_m_final_norm_g', 'new_v_meta_tokens', 'new_v_mix_norm_g', 'new_v_w_in', 'new_v_pool_w', 'new_v_pool_b', 'new_v_pool_scale', 'new_v_conv_w', 'new_v_conv_b', 'new_v_gate_r_w', 'new_v_gate_r_b', 'new_v_gate_i_w', 'new_v_gate_i_b', 'new_v_lru_lambda', 'new_v_group_norm_g', 'new_v_w_out', 'new_v_mlp_norm_g', 'new_v_w_up', 'new_v_w_down', 'new_v_final_norm_g']
TWIN_LEAF_KINDS = {'loss': 'loss', 'grad_x': 'grad_x', 'grad_meta_tokens': 'grad_w', 'grad_mix_norm_g': 'grad_w', 'grad_w_in': 'grad_w', 'grad_pool_w': 'grad_w', 'grad_pool_b': 'grad_w', 'grad_pool_scale': 'grad_w', 'grad_conv_w': 'grad_w', 'grad_conv_b': 'grad_w', 'grad_gate_r_w': 'grad_w', 'grad_gate_r_b': 'grad_w', 'grad_gate_i_w': 'grad_w', 'grad_gate_i_b': 'grad_w', 'grad_lru_lambda': 'grad_w', 'grad_group_norm_g': 'grad_w', 'grad_w_out': 'grad_w', 'grad_mlp_norm_g': 'grad_w', 'grad_w_up': 'grad_w', 'grad_w_down': 'grad_w', 'grad_final_norm_g': 'grad_w', 'delta_meta_tokens': 'delta_w', 'delta_mix_norm_g': 'delta_w', 'delta_w_in': 'delta_w', 'delta_pool_w': 'delta_w', 'delta_pool_b': 'delta_w', 'delta_pool_scale': 'delta_w', 'delta_conv_w': 'delta_w', 'delta_conv_b': 'delta_w', 'delta_gate_r_w': 'delta_w', 'delta_gate_r_b': 'delta_w', 'delta_gate_i_w': 'delta_w', 'delta_gate_i_b': 'delta_w', 'delta_lru_lambda': 'delta_w', 'delta_group_norm_g': 'delta_w', 'delta_w_out': 'delta_w', 'delta_mlp_norm_g': 'delta_w', 'delta_w_up': 'delta_w', 'delta_w_down': 'delta_w', 'delta_final_norm_g': 'delta_w', 'new_m_meta_tokens': 'new_m', 'new_m_mix_norm_g': 'new_m', 'new_m_w_in': 'new_m', 'new_m_pool_w': 'new_m', 'new_m_pool_b': 'new_m', 'new_m_pool_scale': 'new_m', 'new_m_conv_w': 'new_m', 'new_m_conv_b': 'new_m', 'new_m_gate_r_w': 'new_m', 'new_m_gate_r_b': 'new_m', 'new_m_gate_i_w': 'new_m', 'new_m_gate_i_b': 'new_m', 'new_m_lru_lambda': 'new_m', 'new_m_group_norm_g': 'new_m', 'new_m_w_out': 'new_m', 'new_m_mlp_norm_g': 'new_m', 'new_m_w_up': 'new_m', 'new_m_w_down': 'new_m', 'new_m_final_norm_g': 'new_m', 'new_v_meta_tokens': 'new_v', 'new_v_mix_norm_g': 'new_v', 'new_v_w_in': 'new_v', 'new_v_pool_w': 'new_v', 'new_v_pool_b': 'new_v', 'new_v_pool_scale': 'new_v', 'new_v_conv_w': 'new_v', 'new_v_conv_b': 'new_v', 'new_v_gate_r_w': 'new_v', 'new_v_gate_r_b': 'new_v', 'new_v_gate_i_w': 'new_v', 'new_v_gate_i_b': 'new_v', 'new_v_lru_lambda': 'new_v', 'new_v_group_norm_g': 'new_v', 'new_v_w_out': 'new_v', 'new_v_mlp_norm_g': 'new_v', 'new_v_w_up': 'new_v', 'new_v_w_down': 'new_v', 'new_v_final_norm_g': 'new_v'}


def _forward(args):
    return _fwd_reference(*[args[k] for k in FWD_PARAMS])


def _output_shape():
    def fwd():
        inp = _fwd_setup_inputs(0)
        return _fwd_reference(*[inp[k] for k in FWD_PARAMS])
    out = _jax.eval_shape(fwd)
    return out.shape, out.dtype

N_MICROBATCH = 1
ADAM_LR = 0.001
ADAM_B1 = 0.9
ADAM_B2 = 0.999
ADAM_EPS = 1e-08
ADAM_WD = 0.01
ADAM_STEP = 10
PER_EXAMPLE_BATCH_AXIS = {'x': 0, 'loss_target': 0}
SHARED_INPUTS = []
_WEIGHT_DTYPES = {'meta_tokens': _jnp.float32, 'mix_norm_g': _jnp.float32, 'w_in': _jnp.float32, 'pool_w': _jnp.float32, 'pool_b': _jnp.float32, 'pool_scale': _jnp.float32, 'conv_w': _jnp.float32, 'conv_b': _jnp.float32, 'gate_r_w': _jnp.float32, 'gate_r_b': _jnp.float32, 'gate_i_w': _jnp.float32, 'gate_i_b': _jnp.float32, 'lru_lambda': _jnp.float32, 'group_norm_g': _jnp.float32, 'w_out': _jnp.float32, 'mlp_norm_g': _jnp.float32, 'w_up': _jnp.float32, 'w_down': _jnp.float32, 'final_norm_g': _jnp.float32}
MOMENT_SCALE = {'meta_tokens': 1.617792e-02, 'mix_norm_g': 3.705248e-01, 'w_in': 3.327026e-01, 'pool_w': 1.719881e-01, 'pool_b': 2.138413e+00, 'pool_scale': 3.675212e-01, 'conv_w': 5.864243e-01, 'conv_b': 4.079121e+00, 'gate_r_w': 1.311915e-01, 'gate_r_b': 1.105471e-01, 'gate_i_w': 2.486909e-01, 'gate_i_b': 1.725023e-01, 'lru_lambda': 2.187543e-01, 'group_norm_g': 3.858715e-01, 'w_out': 4.258311e-01, 'mlp_norm_g': 3.593920e-01, 'w_up': 1.678407e-01, 'w_down': 8.557582e-01, 'final_norm_g': 6.547695e+01}


def _to_microbatches(a, axis):
    t = _jnp.moveaxis(a, axis, 0)
    t = t.reshape((N_MICROBATCH, t.shape[0] // N_MICROBATCH) + t.shape[1:])
    return _jnp.moveaxis(t, 1, axis + 1)


def setup_inputs(seed: int = 0) -> dict:
    inp = _fwd_setup_inputs(seed)
    key = _jax.random.fold_in(_jax.random.key(seed), 7919)
    shape, _ = _output_shape()
    out = dict(inp)
    out["loss_target"] = _jax.random.normal(_jax.random.fold_in(key, 0), shape, _jnp.float32)
    for i, name in enumerate(TWIN_WEIGHTS):
        w = inp[name].astype(_jnp.float32)
        if MOMENT_SCALE is None:
            s = _jnp.sqrt(_jnp.mean(_jnp.square(w)) + 1e-30)
        else:
            s = MOMENT_SCALE[name]
        km, kv = _jax.random.split(_jax.random.fold_in(key, i + 1))
        out[name] = w
        out["m_" + name] = s * _jax.random.normal(km, w.shape, _jnp.float32)
        out["v_" + name] = (s * s) * _jax.random.uniform(kv, w.shape, _jnp.float32, 0.5, 1.5)
    if N_MICROBATCH > 1:
        for name, axis in PER_EXAMPLE_BATCH_AXIS.items():
            out[name] = _to_microbatches(out[name], axis)
    return {'x': out['x'], 'meta_tokens': out['meta_tokens'], 'mix_norm_g': out['mix_norm_g'], 'w_in': out['w_in'], 'pool_w': out['pool_w'], 'pool_b': out['pool_b'], 'pool_scale': out['pool_scale'], 'conv_w': out['conv_w'], 'conv_b': out['conv_b'], 'gate_r_w': out['gate_r_w'], 'gate_r_b': out['gate_r_b'], 'gate_i_w': out['gate_i_w'], 'gate_i_b': out['gate_i_b'], 'lru_lambda': out['lru_lambda'], 'group_norm_g': out['group_norm_g'], 'w_out': out['w_out'], 'mlp_norm_g': out['mlp_norm_g'], 'w_up': out['w_up'], 'w_down': out['w_down'], 'final_norm_g': out['final_norm_g'], 'loss_target': out['loss_target'], 'm_meta_tokens': out['m_meta_tokens'], 'm_mix_norm_g': out['m_mix_norm_g'], 'm_w_in': out['m_w_in'], 'm_pool_w': out['m_pool_w'], 'm_pool_b': out['m_pool_b'], 'm_pool_scale': out['m_pool_scale'], 'm_conv_w': out['m_conv_w'], 'm_conv_b': out['m_conv_b'], 'm_gate_r_w': out['m_gate_r_w'], 'm_gate_r_b': out['m_gate_r_b'], 'm_gate_i_w': out['m_gate_i_w'], 'm_gate_i_b': out['m_gate_i_b'], 'm_lru_lambda': out['m_lru_lambda'], 'm_group_norm_g': out['m_group_norm_g'], 'm_w_out': out['m_w_out'], 'm_mlp_norm_g': out['m_mlp_norm_g'], 'm_w_up': out['m_w_up'], 'm_w_down': out['m_w_down'], 'm_final_norm_g': out['m_final_norm_g'], 'v_meta_tokens': out['v_meta_tokens'], 'v_mix_norm_g': out['v_mix_norm_g'], 'v_w_in': out['v_w_in'], 'v_pool_w': out['v_pool_w'], 'v_pool_b': out['v_pool_b'], 'v_pool_scale': out['v_pool_scale'], 'v_conv_w': out['v_conv_w'], 'v_conv_b': out['v_conv_b'], 'v_gate_r_w': out['v_gate_r_w'], 'v_gate_r_b': out['v_gate_r_b'], 'v_gate_i_w': out['v_gate_i_w'], 'v_gate_i_b': out['v_gate_i_b'], 'v_lru_lambda': out['v_lru_lambda'], 'v_group_norm_g': out['v_group_norm_g'], 'v_w_out': out['v_w_out'], 'v_mlp_norm_g': out['v_mlp_norm_g'], 'v_w_up': out['v_w_up'], 'v_w_down': out['v_w_down'], 'v_final_norm_g': out['v_final_norm_g']}


def _loss(weights, diff, rest, loss_target):
    with _jax.named_scope("forward"):
        args = {**rest, TWIN_DIFF_INPUT: diff, **{k: w.astype(_WEIGHT_DTYPES[k]) for k, w in weights.items()}}
        y = _forward(args)
    with _jax.named_scope("loss_head"):
        err = _jnp.square(y.astype(_jnp.float32) - loss_target)
        return 0.5 * _jnp.sum(_jnp.mean(err, axis=-1)) if err.ndim else 0.5 * err


def _adamw(w, g, m, v):
    m = ADAM_B1 * m + (1.0 - ADAM_B1) * g
    v = ADAM_B2 * v + (1.0 - ADAM_B2) * _jnp.square(g)
    m_hat = m / (1.0 - ADAM_B1 ** ADAM_STEP)
    v_hat = v / (1.0 - ADAM_B2 ** ADAM_STEP)
    delta = -ADAM_LR * (m_hat / (_jnp.sqrt(v_hat) + ADAM_EPS) + ADAM_WD * w)
    return delta, m, v


def reference(x, meta_tokens, mix_norm_g, w_in, pool_w, pool_b, pool_scale, conv_w, conv_b, gate_r_w, gate_r_b, gate_i_w, gate_i_b, lru_lambda, group_norm_g, w_out, mlp_norm_g, w_up, w_down, final_norm_g, loss_target, m_meta_tokens, m_mix_norm_g, m_w_in, m_pool_w, m_pool_b, m_pool_scale, m_conv_w, m_conv_b, m_gate_r_w, m_gate_r_b, m_gate_i_w, m_gate_i_b, m_lru_lambda, m_group_norm_g, m_w_out, m_mlp_norm_g, m_w_up, m_w_down, m_final_norm_g, v_meta_tokens, v_mix_norm_g, v_w_in, v_pool_w, v_pool_b, v_pool_scale, v_conv_w, v_conv_b, v_gate_r_w, v_gate_r_b, v_gate_i_w, v_gate_i_b, v_lru_lambda, v_group_norm_g, v_w_out, v_mlp_norm_g, v_w_up, v_w_down, v_final_norm_g):
    given = dict(x=x, meta_tokens=meta_tokens, mix_norm_g=mix_norm_g, w_in=w_in, pool_w=pool_w, pool_b=pool_b, pool_scale=pool_scale, conv_w=conv_w, conv_b=conv_b, gate_r_w=gate_r_w, gate_r_b=gate_r_b, gate_i_w=gate_i_w, gate_i_b=gate_i_b, lru_lambda=lru_lambda, group_norm_g=group_norm_g, w_out=w_out, mlp_norm_g=mlp_norm_g, w_up=w_up, w_down=w_down, final_norm_g=final_norm_g, loss_target=loss_target, m_meta_tokens=m_meta_tokens, m_mix_norm_g=m_mix_norm_g, m_w_in=m_w_in, m_pool_w=m_pool_w, m_pool_b=m_pool_b, m_pool_scale=m_pool_scale, m_conv_w=m_conv_w, m_conv_b=m_conv_b, m_gate_r_w=m_gate_r_w, m_gate_r_b=m_gate_r_b, m_gate_i_w=m_gate_i_w, m_gate_i_b=m_gate_i_b, m_lru_lambda=m_lru_lambda, m_group_norm_g=m_group_norm_g, m_w_out=m_w_out, m_mlp_norm_g=m_mlp_norm_g, m_w_up=m_w_up, m_w_down=m_w_down, m_final_norm_g=m_final_norm_g, v_meta_tokens=v_meta_tokens, v_mix_norm_g=v_mix_norm_g, v_w_in=v_w_in, v_pool_w=v_pool_w, v_pool_b=v_pool_b, v_pool_scale=v_pool_scale, v_conv_w=v_conv_w, v_conv_b=v_conv_b, v_gate_r_w=v_gate_r_w, v_gate_r_b=v_gate_r_b, v_gate_i_w=v_gate_i_w, v_gate_i_b=v_gate_i_b, v_lru_lambda=v_lru_lambda, v_group_norm_g=v_group_norm_g, v_w_out=v_w_out, v_mlp_norm_g=v_mlp_norm_g, v_w_up=v_w_up, v_w_down=v_w_down, v_final_norm_g=v_final_norm_g)
    weights = {n: given[n] for n in TWIN_WEIGHTS}
    shared = {n: given[n] for n in SHARED_INPUTS}
    per_example = {n: given[n] for n in ['x']}
    grad_fn = _jax.value_and_grad(_loss, argnums=(0, 1))

    def one_microbatch(ex, loss_target):
        ex = dict(ex)
        diff = ex.pop(TWIN_DIFF_INPUT)
        return grad_fn(weights, diff, {**shared, **ex}, loss_target)

    if N_MICROBATCH == 1:
        loss, (grad_w, grad_x) = one_microbatch(per_example, given["loss_target"])
    else:
        def body(carry, xs):
            loss_sum, grad_sum = carry
            l_k, (gw_k, gx_k) = one_microbatch(xs[0], xs[1])
            with _jax.named_scope("update"):
                return (loss_sum + l_k, _jax.tree.map(_jnp.add, grad_sum, gw_k)), gx_k

        init = (_jnp.zeros((), _jnp.float32), _jax.tree.map(_jnp.zeros_like, weights))
        (loss, grad_w), grad_x = _jax.lax.scan(body, init, (per_example, given["loss_target"]))
    with _jax.named_scope("update"):
        delta_w, new_m, new_v = {}, {}, {}
        for n in TWIN_WEIGHTS:
            delta_w[n], new_m[n], new_v[n] = _adamw(weights[n], grad_w[n], given["m_" + n], given["v_" + n])
    return (loss, grad_x, *[grad_w[n] for n in TWIN_WEIGHTS], *[delta_w[n] for n in TWIN_WEIGHTS],
            *[new_m[n] for n in TWIN_WEIGHTS], *[new_v[n] for n in TWIN_WEIGHTS])
```

```python
import functools

import jax
import jax.numpy as jnp
from jax import lax
from jax.experimental import pallas as pl
from jax.experimental.pallas import tpu as pltpu

F32 = jnp.float32
BF16 = jnp.bfloat16

D = 1024
DP = 512
DR = 512
DIN = 1536
DFF = 4096
DEPTH = 4
NMETA = 16
NQ = 4
WIN_S = DIN // NQ
FF_S = DFF // NQ
EPS = 1e-6
HALO = 16
TT = 432
TTW = 912
VMEM_LIMIT = 56 * 1024 * 1024

ADAM_LR = 0.001
ADAM_B1 = 0.9
ADAM_B2 = 0.999
ADAM_EPS = 1e-08
ADAM_WD = 0.01
ADAM_STEP = 10

MESH = pl.DeviceIdType.MESH
ANY = pl.BlockSpec(memory_space=pl.ANY)


def _params(sem=None, vmem=VMEM_LIMIT):
    return pltpu.CompilerParams(dimension_semantics=sem, vmem_limit_bytes=vmem)


def _resident(shape, index):
    return pl.BlockSpec(shape, index, pipeline_mode=pl.Buffered(1))


def _nt(x, w):
    return lax.dot_general(x, w, (((1,), (1,)), ((), ())), preferred_element_type=F32)


def _tn(a, b):
    return lax.dot_general(a, b, (((0,), (0,)), ((), ())), preferred_element_type=F32)


def _dot(x, w):
    return jnp.dot(x, w, preferred_element_type=F32)


def _rms(h):
    return lax.rsqrt(jnp.mean(h * h, axis=-1, keepdims=True) + EPS)


def _rms_bwd(n, r, dn):
    return r * (dn - n * jnp.mean(dn * n, axis=-1, keepdims=True))


def _down(x, s):
    return pltpu.roll(x, s, 0)


def _up(x, s):
    return pltpu.roll(x, x.shape[0] - s, 0)


GELU_C = 0.7978845608028654
GELU_K = 0.044715


def _gelu(x):
    t = jnp.tanh(GELU_C * (x + GELU_K * x * x * x))
    return 0.5 * x * (1.0 + t), t


def _softplus(x):
    return jnp.maximum(x, 0.0) + jnp.log1p(jnp.exp(-jnp.abs(x)))


VP_PB, VP_PS, VP_CB, VP_BR, VP_BI, VP_LAM, VP_CW, VP_GNP, VP_GNR = 0, 1, 2, 3, 4, 5, 6, 10, 11


def _row(vp, r):
    return vp[r:r + 1, :]


def _mixer_pre(ue, t0, vp, pw_ref, br_ref, bi_ref):
    tt = ue.shape[0] - HALO
    tf = (t0 + lax.broadcasted_iota(jnp.int32, (tt, 1), 0)).astype(F32) + 1.0
    pb, ps = _row(vp, VP_PB), _row(vp, VP_PS)
    pooled, mapped, inv_cnt = [], [], []
    for g in range(4):
        lanes = slice(128 * g, 128 * (g + 1))
        xe = ue[:, lanes]
        s = xe
        for j in range(g + 1):
            s = s + _down(s, 1 << j)
        inv = 1.0 / jnp.minimum(tf, float(2 << g))
        pg = s[HALO:] * inv - xe[HALO:]
        mg = _dot(pg.astype(BF16), pw_ref[g]) + pb[:, lanes]
        pooled.append(pg)
        mapped.append(mg)
        inv_cnt.append(inv)
    ypool = [mapped[g] * ps[:, 128 * g:128 * (g + 1)] for g in range(4)]

    xe = ue[:, DP:DP + DR]
    taps = [_down(xe, 3)[HALO:], _down(xe, 2)[HALO:], _down(xe, 1)[HALO:], xe[HALO:]]
    xc = _row(vp, VP_CB) + (taps[0] * _row(vp, VP_CW) + taps[1] * _row(vp, VP_CW + 1)
                            + taps[2] * _row(vp, VP_CW + 2) + taps[3] * _row(vp, VP_CW + 3))
    xcb = xc.astype(BF16)
    zr = jnp.concatenate([_dot(xcb[:, :256], br_ref[0]), _dot(xcb[:, 256:], br_ref[1])], axis=1) + _row(vp, VP_BR)
    zi = jnp.concatenate([_dot(xcb[:, :256], bi_ref[0]), _dot(xcb[:, 256:], bi_ref[1])], axis=1) + _row(vp, VP_BI)
    r = jax.nn.sigmoid(zr)
    ig = jax.nn.sigmoid(zi)
    sp = _softplus(-_row(vp, VP_LAM))
    la = (-8.0 * r) * sp
    a = jnp.exp(la)
    th = jnp.tanh(la)
    mult = jnp.sqrt((-2.0 * th) / (1.0 - th))
    gate = ue[HALO:, DP + DR:]
    gl, gt = _gelu(gate)
    return dict(pooled=pooled, mapped=mapped, ypool=ypool, inv_cnt=inv_cnt, taps=taps, xc=xc, xcb=xcb, r=r, ig=ig,
                sp=sp, a=a, mult=mult, gate=gate, gl=gl, gt=gt)


def _scan_fwd(a, b):
    tt = a.shape[0]
    row = lax.broadcasted_iota(jnp.int32, (tt, 1), 0)
    s = 1
    while s < tt:
        m = row >= s
        a_s = jnp.where(m, _down(a, s), 1.0)
        b_s = jnp.where(m, _down(b, s), 0.0)
        b = a * b_s + b
        a = a * a_s
        s *= 2
    return a, b


def _scan_rev(c, d):
    tt = c.shape[0]
    row = lax.broadcasted_iota(jnp.int32, (tt, 1), 0)
    s = 1
    while s < tt:
        m = row < tt - s
        c_s = jnp.where(m, _up(c, s), 1.0)
        d_s = jnp.where(m, _up(d, s), 0.0)
        d = c * d_s + d
        c = c * c_s
        s *= 2
    return c, d


def _halo_index(i, per_tile):
    return jnp.maximum(i * per_tile - 1, 0)


def fwd_in(h, g, w_g, layer):
    T = h.shape[0]

    def body(h_ref, g_ref, w_ref, o_ref):
        hh = h_ref[...]
        xn = (hh * _rms(hh) * g_ref[...]).astype(BF16)
        for q in range(NQ):
            o_ref[:, WIN_S * q:WIN_S * (q + 1)] = _dot(xn, w_ref[q])

    return pl.pallas_call(
        body, name="fwd_in", grid=(T // TT,),
        in_specs=[pl.BlockSpec((TT, D), lambda i: (i, 0)), _resident((1, D), lambda i: (0, 0)),
                  _resident((None, NQ, D, WIN_S), lambda i: (layer, 0, 0, 0))],
        out_specs=pl.BlockSpec((TT, DIN), lambda i: (i, 0)),
        out_shape=jax.ShapeDtypeStruct((T, DIN), F32),
        compiler_params=_params(("arbitrary",)),
    )(h, g, w_g)


def fwd_mixer(proj, vp, pw, br, bi):
    T = proj.shape[0]
    per = TT // HALO

    def body(p_ref, ph_ref, vp_ref, pw_ref, br_ref, bi_ref, y_ref, hs_ref, carry_ref):
        i = pl.program_id(0)

        @pl.when(i == 0)
        def _():
            carry_ref[...] = jnp.zeros_like(carry_ref)

        halo = jnp.where(i > 0, ph_ref[...], 0.0)
        ue = jnp.concatenate([halo, p_ref[...]], axis=0)
        vp_v = vp_ref[...]
        m = _mixer_pre(ue, i * TT, vp_v, pw_ref, br_ref, bi_ref)
        b = m["mult"] * (m["ig"] * m["xc"])
        ca, cb = _scan_fwd(m["a"], b)
        hs = ca * carry_ref[0:1, :] + cb
        hs_ref[...] = hs
        carry_ref[0:1, :] = hs_ref[TT - 1:TT, :]
        yr = hs * m["gl"]
        ssq = sum(jnp.sum(yp * yp, axis=-1, keepdims=True) for yp in m["ypool"])
        rp = lax.rsqrt(ssq * (1.0 / DP) + EPS)
        gnp = _row(vp_v, VP_GNP)
        for g in range(4):
            lanes = slice(128 * g, 128 * (g + 1))
            y_ref[:, lanes] = (m["ypool"][g] * rp * gnp[:, lanes]).astype(BF16)
        y_ref[:, DP:] = (yr * _rms(yr) * _row(vp_v, VP_GNR)).astype(BF16)

    return pl.pallas_call(
        body, name="fwd_mixer", grid=(T // TT,),
        in_specs=[pl.BlockSpec((TT, DIN), lambda i: (i, 0)),
                  pl.BlockSpec((HALO, DIN), lambda i: (_halo_index(i, per), 0)),
                  _resident((16, 512), lambda i: (0, 0)), _resident((4, 128, 128), lambda i: (0, 0, 0)),
                  _resident((2, 256, 256), lambda i: (0, 0, 0)), _resident((2, 256, 256), lambda i: (0, 0, 0))],
        out_specs=[pl.BlockSpec((TT, D), lambda i: (i, 0)), pl.BlockSpec((TT, DR), lambda i: (i, 0))],
        out_shape=[jax.ShapeDtypeStruct((T, D), BF16), jax.ShapeDtypeStruct((T, DR), F32)],
        scratch_shapes=[pltpu.VMEM((8, DR), F32)],
        compiler_params=_params(("arbitrary",)),
    )(proj, proj, vp, pw, br, bi)


def fwd_post(h, y, wo_g, g2, wu_g, wd_g, layer):
    T = h.shape[0]

    def body(h_ref, y_ref, wo_ref, g_ref, wu_ref, wd_ref, h1_ref, a_ref, h2_ref):
        h1 = h_ref[...] + _dot(y_ref[...], wo_ref[...])
        h1_ref[...] = h1
        xn = (h1 * _rms(h1) * g_ref[...]).astype(BF16)
        acc = h1
        for q in range(NQ):
            a = _dot(xn, wu_ref[q])
            a_ref[:, FF_S * q:FF_S * (q + 1)] = a.astype(BF16)
            ra = jnp.maximum(a, 0.0)
            acc = acc + _dot((ra * ra).astype(BF16), wd_ref[q])
        h2_ref[...] = acc

    return pl.pallas_call(
        body, name="fwd_post", grid=(T // TT,),
        in_specs=[pl.BlockSpec((TT, D), lambda i: (i, 0)), pl.BlockSpec((TT, D), lambda i: (i, 0)),
                  _resident((None, D, D), lambda i: (layer, 0, 0)), _resident((1, D), lambda i: (0, 0)),
                  _resident((None, NQ, D, FF_S), lambda i: (layer, 0, 0, 0)),
                  _resident((None, NQ, FF_S, D), lambda i: (layer, 0, 0, 0))],
        out_specs=[pl.BlockSpec((TT, D), lambda i: (i, 0)), pl.BlockSpec((TT, DFF), lambda i: (i, 0)),
                   pl.BlockSpec((TT, D), lambda i: (i, 0))],
        out_shape=[jax.ShapeDtypeStruct((T, D), F32), jax.ShapeDtypeStruct((T, DFF), BF16),
                   jax.ShapeDtypeStruct((T, D), F32)],
        compiler_params=_params(("arbitrary",)),
    )(h, y, wo_g, g2, wu_g, wd_g)


def fwd_loss(h, gf, tgt):
    T = h.shape[0]

    def body(h_ref, g_ref, t_ref, dh_ref, dg_ref, loss_ref):
        i = pl.program_id(0)

        @pl.when(i == 0)
        def _():
            dg_ref[...] = jnp.zeros_like(dg_ref)
            loss_ref[...] = jnp.zeros_like(loss_ref)

        hh = h_ref[...]
        r = _rms(hh)
        n = hh * r
        gfv = g_ref[...]
        row = i * TT + lax.broadcasted_iota(jnp.int32, (TT, 1), 0)
        e = jnp.where(row >= NMETA, n * gfv - t_ref[...], 0.0)
        loss_ref[...] += 0.5 * jnp.sum(jnp.sum(e * e, axis=-1, keepdims=True) * (1.0 / D), axis=0, keepdims=True)
        dy = e * (1.0 / D)
        dg_ref[...] += jnp.sum(dy * n, axis=0, keepdims=True)
        dh_ref[...] = _rms_bwd(n, r, dy * gfv)

    return pl.pallas_call(
        body, name="fwd_loss", grid=(T // TT,),
        in_specs=[pl.BlockSpec((TT, D), lambda i: (i, 0)), _resident((1, D), lambda i: (0, 0)),
                  pl.BlockSpec((TT, D), lambda i: (i, 0))],
        out_specs=[pl.BlockSpec((TT, D), lambda i: (i, 0)), pl.BlockSpec((1, D), lambda i: (0, 0)),
                   pl.BlockSpec((1, 1), lambda i: (0, 0))],
        out_shape=[jax.ShapeDtypeStruct((T, D), F32), jax.ShapeDtypeStruct((1, D), F32),
                   jax.ShapeDtypeStruct((1, 1), F32)],
        compiler_params=_params(("arbitrary",)),
    )(h, gf, tgt)


def bwd_mlp_dx(dh2, h1, a, g2, wu_g, wd_g, layer):
    T = dh2.shape[0]

    def body(dh2_ref, h1_ref, a_ref, g_ref, wu_ref, wd_ref, dh1_ref, da_ref, xn_ref, dg_ref):
        @pl.when(pl.program_id(0) == 0)
        def _():
            dg_ref[...] = jnp.zeros_like(dg_ref)

        h1v = h1_ref[...]
        r = _rms(h1v)
        n = h1v * r
        gv = g_ref[...]
        xn_ref[...] = (n * gv).astype(BF16)
        dh2v = dh2_ref[...]
        dh2b = dh2v.astype(BF16)
        dxn = jnp.zeros((TT, D), F32)
        for q in range(NQ):
            cols = slice(FF_S * q, FF_S * (q + 1))
            ra = jnp.maximum(a_ref[:, cols].astype(F32), 0.0)
            da = (_nt(dh2b, wd_ref[q]) * (2.0 * ra)).astype(BF16)
            da_ref[:, cols] = da
            dxn = dxn + _nt(da, wu_ref[q])
        dg_ref[...] += jnp.sum(dxn * n, axis=0, keepdims=True)
        dh1_ref[...] = dh2v + _rms_bwd(n, r, dxn * gv)

    return pl.pallas_call(
        body, name="bwd_mlp_dx", grid=(T // TT,),
        in_specs=[pl.BlockSpec((TT, D), lambda i: (i, 0)), pl.BlockSpec((TT, D), lambda i: (i, 0)),
                  pl.BlockSpec((TT, DFF), lambda i: (i, 0)), _resident((1, D), lambda i: (0, 0)),
                  _resident((None, NQ, D, FF_S), lambda i: (layer, 0, 0, 0)),
                  _resident((None, NQ, FF_S, D), lambda i: (layer, 0, 0, 0))],
        out_specs=[pl.BlockSpec((TT, D), lambda i: (i, 0)), pl.BlockSpec((TT, DFF), lambda i: (i, 0)),
                   pl.BlockSpec((TT, D), lambda i: (i, 0)), pl.BlockSpec((1, D), lambda i: (0, 0))],
        out_shape=[jax.ShapeDtypeStruct((T, D), F32), jax.ShapeDtypeStruct((T, DFF), BF16),
                   jax.ShapeDtypeStruct((T, D), BF16), jax.ShapeDtypeStruct((1, D), F32)],
        compiler_params=_params(("arbitrary",)),
    )(dh2, h1, a, g2, wu_g, wd_g)


def bwd_mlp_dw(xn, da, a, dh2):
    T = xn.shape[0]

    def body(xn_ref, da_ref, a_ref, dh2_ref, dwu_ref, dwd_ref):
        @pl.when(pl.program_id(1) == 0)
        def _():
            dwu_ref[...] = jnp.zeros_like(dwu_ref)
            dwd_ref[...] = jnp.zeros_like(dwd_ref)

        dwu_ref[...] += _tn(xn_ref[...], da_ref[...])
        ra = jnp.maximum(a_ref[...].astype(F32), 0.0)
        dwd_ref[...] += _tn((ra * ra).astype(BF16), dh2_ref[...].astype(BF16))

    return pl.pallas_call(
        body, name="bwd_mlp_dw", grid=(NQ, T // TTW),
        in_specs=[pl.BlockSpec((TTW, D), lambda q, i: (i, 0)), pl.BlockSpec((TTW, FF_S), lambda q, i: (i, q)),
                  pl.BlockSpec((TTW, FF_S), lambda q, i: (i, q)), pl.BlockSpec((TTW, D), lambda q, i: (i, 0))],
        out_specs=[pl.BlockSpec((None, D, FF_S), lambda q, i: (q, 0, 0)),
                   pl.BlockSpec((None, FF_S, D), lambda q, i: (q, 0, 0))],
        out_shape=[jax.ShapeDtypeStruct((NQ, D, FF_S), F32), jax.ShapeDtypeStruct((NQ, FF_S, D), F32)],
        compiler_params=_params(("arbitrary", "arbitrary")),
    )(xn, da, a, dh2)


def bwd_out(dh1, y, wo_g, layer):
    T = dh1.shape[0]

    def body(dh_ref, y_ref, wo_ref, dy_ref, dwo_ref):
        @pl.when(pl.program_id(0) == 0)
        def _():
            dwo_ref[...] = jnp.zeros_like(dwo_ref)

        dhb = dh_ref[...].astype(BF16)
        dy_ref[...] = _nt(dhb, wo_ref[...])
        dwo_ref[...] += _tn(y_ref[...], dhb)

    return pl.pallas_call(
        body, name="bwd_out", grid=(T // TT,),
        in_specs=[pl.BlockSpec((TT, D), lambda i: (i, 0)), pl.BlockSpec((TT, D), lambda i: (i, 0)),
                  _resident((None, D, D), lambda i: (layer, 0, 0))],
        out_specs=[pl.BlockSpec((TT, D), lambda i: (i, 0)), pl.BlockSpec((D, D), lambda i: (0, 0))],
        out_shape=[jax.ShapeDtypeStruct((T, D), F32), jax.ShapeDtypeStruct((D, D), F32)],
        compiler_params=_params(("arbitrary",)),
    )(dh1, y, wo_g)


GA_PS, GA_PB, GA_CB, GA_BR, GA_BI, GA_LAM, GA_CW, GA_GNP, GA_GNR = 0, 1, 2, 3, 4, 5, 6, 10, 11


def bwd_mixer(dy, proj, hs, vp, pw, br, bi):
    T = dy.shape[0]
    nt = T // TT
    per = TT // HALO

    def body(dy_ref, p_ref, ph_ref, hs_ref, hsh_ref, vp_ref, pw_ref, br_ref, bi_ref,
             dp_ref, ga_ref, dpw_ref, dwr_ref, dwi_ref, lam_ref, q_ref, dxc_ref):
        i = pl.program_id(0)
        ti = nt - 1 - i

        @pl.when(i == 0)
        def _():
            ga_ref[...] = jnp.zeros_like(ga_ref)
            dpw_ref[...] = jnp.zeros_like(dpw_ref)
            dwr_ref[...] = jnp.zeros_like(dwr_ref)
            dwi_ref[...] = jnp.zeros_like(dwi_ref)
            lam_ref[...] = jnp.zeros_like(lam_ref)
            q_ref[...] = jnp.zeros_like(q_ref)
            dxc_ref[...] = jnp.zeros_like(dxc_ref)

        first = ti > 0
        ue = jnp.concatenate([jnp.where(first, ph_ref[...], 0.0), p_ref[...]], axis=0)
        vp_v = vp_ref[...]
        m = _mixer_pre(ue, ti * TT, vp_v, pw_ref, br_ref, bi_ref)
        hs_v = hs_ref[...]
        hprev = _down(jnp.concatenate([jnp.where(first, hsh_ref[...], 0.0), hs_v], axis=0), 1)[HALO:]
        dyv = dy_ref[...]

        def acc(rw, v):
            ga_ref[rw:rw + 1, :] += jnp.sum(v, axis=0, keepdims=True)

        gnp = _row(vp_v, VP_GNP)
        ps = _row(vp_v, VP_PS)
        ssq = sum(jnp.sum(yp * yp, axis=-1, keepdims=True) for yp in m["ypool"])
        rp = lax.rsqrt(ssq * (1.0 / DP) + EPS)
        npool = [yp * rp for yp in m["ypool"]]
        dnp = [dyv[:, 128 * g:128 * (g + 1)] * gnp[:, 128 * g:128 * (g + 1)] for g in range(4)]
        mean_dn = sum(jnp.sum(dnp[g] * npool[g], axis=-1, keepdims=True) for g in range(4)) * (1.0 / DP)
        for g in range(4):
            lanes = slice(128 * g, 128 * (g + 1))
            ga_ref[GA_GNP:GA_GNP + 1, lanes] += jnp.sum(dyv[:, lanes] * npool[g], axis=0, keepdims=True)
            dyp = rp * (dnp[g] - npool[g] * mean_dn)
            ga_ref[GA_PS:GA_PS + 1, lanes] += jnp.sum(dyp * m["mapped"][g], axis=0, keepdims=True)
            dmap = dyp * ps[:, lanes]
            ga_ref[GA_PB:GA_PB + 1, lanes] += jnp.sum(dmap, axis=0, keepdims=True)
            dmb = dmap.astype(BF16)
            dpw_ref[g] += _tn(m["pooled"][g].astype(BF16), dmb)
            dpool = _nt(dmb, pw_ref[g])
            qv = dpool * m["inv_cnt"][g]
            s = jnp.concatenate([qv, q_ref[:, lanes]], axis=0)
            for j in range(g + 1):
                s = s + _up(s, 1 << j)
            q_ref[:, lanes] = qv[:HALO]
            dp_ref[:, lanes] = (s[:TT] - dpool).astype(BF16)

        gnr = _row(vp_v, VP_GNR)
        yr = hs_v * m["gl"]
        rr = _rms(yr)
        nr = yr * rr
        dyr_out = dyv[:, DP:]
        acc(GA_GNR, dyr_out * nr)
        dyr = _rms_bwd(nr, rr, dyr_out * gnr)
        gate, gt = m["gate"], m["gt"]
        dgl = 0.5 * (1.0 + gt) + 0.5 * gate * (1.0 - gt * gt) * (GELU_C * (1.0 + 3.0 * GELU_K * gate * gate))
        dp_ref[:, DP + DR:] = (dyr * hs_v * dgl).astype(BF16)
        dhs = dyr * m["gl"]
        a = m["a"]
        row = lax.broadcasted_iota(jnp.int32, (TT, 1), 0)
        c_next = jnp.where(row < TT - 1, _up(a, 1), 1.0)
        cc, ce = _scan_rev(c_next, dhs)
        lam = cc * lam_ref[0:1, :] + ce
        lam_ref[0:1, :] = a[0:1, :] * lam[0:1, :]
        xc, ig, r, mult = m["xc"], m["ig"], m["r"], m["mult"]
        dmult = lam * ig * xc
        dig = lam * mult * xc
        dxc = lam * mult * ig
        dla = lam * hprev * a - dmult * (a * a) / mult
        acc(GA_LAM, dla * r)
        dzr = (dla * (-8.0 * m["sp"])) * (r * (1.0 - r))
        dzi = dig * (ig * (1.0 - ig))
        acc(GA_BR, dzr)
        acc(GA_BI, dzi)
        dzrb, dzib = dzr.astype(BF16), dzi.astype(BF16)
        xcb = m["xcb"]
        halves = []
        for k in range(2):
            lanes = slice(256 * k, 256 * (k + 1))
            dwr_ref[k] += _tn(xcb[:, lanes], dzrb[:, lanes])
            dwi_ref[k] += _tn(xcb[:, lanes], dzib[:, lanes])
            halves.append(_nt(dzrb[:, lanes], br_ref[k]) + _nt(dzib[:, lanes], bi_ref[k]))
        dxc = dxc + jnp.concatenate(halves, axis=1)
        acc(GA_CB, dxc)
        for k in range(4):
            acc(GA_CW + k, dxc * m["taps"][k])
        dxe = jnp.concatenate([dxc, dxc_ref[...]], axis=0)
        du = (_up(dxe, 3)[:TT] * _row(vp_v, VP_CW) + _up(dxe, 2)[:TT] * _row(vp_v, VP_CW + 1)
              + _up(dxe, 1)[:TT] * _row(vp_v, VP_CW + 2) + dxc * _row(vp_v, VP_CW + 3))
        dxc_ref[...] = dxc[:HALO]
        dp_ref[:, DP:DP + DR] = du.astype(BF16)

        @pl.when(i == nt - 1)
        def _():
            lamp = _row(vp_v, VP_LAM)
            ga_ref[GA_LAM:GA_LAM + 1, :] = ga_ref[GA_LAM:GA_LAM + 1, :] * (8.0 * jax.nn.sigmoid(-lamp))

    rev = lambda i: (nt - 1 - i, 0)
    rev_halo = lambda i: (_halo_index(nt - 1 - i, per), 0)
    return pl.pallas_call(
        body, name="bwd_mixer", grid=(nt,),
        in_specs=[pl.BlockSpec((TT, D), rev), pl.BlockSpec((TT, DIN), rev), pl.BlockSpec((HALO, DIN), rev_halo),
                  pl.BlockSpec((TT, DR), rev), pl.BlockSpec((HALO, DR), rev_halo),
                  _resident((16, 512), lambda i: (0, 0)), _resident((4, 128, 128), lambda i: (0, 0, 0)),
                  _resident((2, 256, 256), lambda i: (0, 0, 0)), _resident((2, 256, 256), lambda i: (0, 0, 0))],
        out_specs=[pl.BlockSpec((TT, DIN), rev), pl.BlockSpec((16, 512), lambda i: (0, 0)),
                   pl.BlockSpec((4, 128, 128), lambda i: (0, 0, 0)), pl.BlockSpec((2, 256, 256), lambda i: (0, 0, 0)),
                   pl.BlockSpec((2, 256, 256), lambda i: (0, 0, 0))],
        out_shape=[jax.ShapeDtypeStruct((T, DIN), BF16), jax.ShapeDtypeStruct((16, 512), F32),
                   jax.ShapeDtypeStruct((4, 128, 128), F32), jax.ShapeDtypeStruct((2, 256, 256), F32),
                   jax.ShapeDtypeStruct((2, 256, 256), F32)],
        scratch_shapes=[pltpu.VMEM((8, DR), F32), pltpu.VMEM((HALO, DP), F32), pltpu.VMEM((HALO, DR), F32)],
        compiler_params=_params(("arbitrary",)),
    )(dy, proj, proj, hs, hs, vp, pw, br, bi)


def bwd_in(dproj, h, g1, w_g, dh1, layer):
    T = h.shape[0]

    def body(dp_ref, h_ref, g_ref, w_ref, dh1_ref, dh_ref, dw_ref, dg_ref):
        @pl.when(pl.program_id(0) == 0)
        def _():
            dw_ref[...] = jnp.zeros_like(dw_ref)
            dg_ref[...] = jnp.zeros_like(dg_ref)

        hv = h_ref[...]
        r = _rms(hv)
        n = hv * r
        gv = g_ref[...]
        xn = (n * gv).astype(BF16)
        dxn = jnp.zeros((TT, D), F32)
        for q in range(NQ):
            dpq = dp_ref[:, WIN_S * q:WIN_S * (q + 1)]
            dxn = dxn + _nt(dpq, w_ref[q])
            dw_ref[q] += _tn(xn, dpq)
        dg_ref[...] += jnp.sum(dxn * n, axis=0, keepdims=True)
        dh_ref[...] = dh1_ref[...] + _rms_bwd(n, r, dxn * gv)

    return pl.pallas_call(
        body, name="bwd_in", grid=(T // TT,),
        in_specs=[pl.BlockSpec((TT, DIN), lambda i: (i, 0)), pl.BlockSpec((TT, D), lambda i: (i, 0)),
                  _resident((1, D), lambda i: (0, 0)), _resident((None, NQ, D, WIN_S), lambda i: (layer, 0, 0, 0)),
                  pl.BlockSpec((TT, D), lambda i: (i, 0))],
        out_specs=[pl.BlockSpec((TT, D), lambda i: (i, 0)), pl.BlockSpec((NQ, D, WIN_S), lambda i: (0, 0, 0)),
                   pl.BlockSpec((1, D), lambda i: (0, 0))],
        out_shape=[jax.ShapeDtypeStruct((T, D), F32), jax.ShapeDtypeStruct((NQ, D, WIN_S), F32),
                   jax.ShapeDtypeStruct((1, D), F32)],
        compiler_params=_params(("arbitrary",)),
    )(dproj, h, g1, w_g, dh1)


def _row_block(rows, cols, itemsize=4, budget=2 * 1024 * 1024):
    best = None
    for b in range(16, rows + 1, 16):
        if rows % b == 0 and b * cols * itemsize <= budget:
            best = b
    return best if best is not None else rows


def add_pairs(mine, got, dtype):
    n = len(mine)
    outs = []
    for k in range(n):
        q, r, c = mine[k].shape
        rb = _row_block(r, c)

        def body(a_ref, b_ref, o_ref):
            o_ref[...] = (a_ref[...] + b_ref[...]).astype(dtype)

        spec = pl.BlockSpec((None, rb, c), lambda qi, i: (qi, i, 0))
        outs.append(pl.pallas_call(
            body, name="add_pairs", grid=(q, r // rb), in_specs=[spec, spec], out_specs=spec,
            out_shape=jax.ShapeDtypeStruct((q, r, c), dtype),
            compiler_params=_params(("arbitrary", "arbitrary")),
        )(mine[k], got[k]))
    return outs


def sum_chips(parts):
    outs = []
    for p in parts:
        _, r, c = p.shape
        rb = _row_block(r, c)

        def body(p_ref, o_ref):
            s = p_ref[0].astype(F32) + p_ref[1].astype(F32)
            s = s + p_ref[2].astype(F32)
            o_ref[...] = s + p_ref[3].astype(F32)

        outs.append(pl.pallas_call(
            body, name="sum_chips", grid=(r // rb,),
            in_specs=[pl.BlockSpec((NQ, rb, c), lambda i: (0, i, 0))],
            out_specs=pl.BlockSpec((rb, c), lambda i: (i, 0)),
            out_shape=jax.ShapeDtypeStruct((r, c), F32),
            compiler_params=_params(("arbitrary",)),
        )(p))
    return outs


def adamw(w, g, m, v):
    r, c = w.shape
    rb = _row_block(r, c, budget=1024 * 1024)
    c1 = 1.0 / (1.0 - ADAM_B1 ** ADAM_STEP)
    c2 = 1.0 / (1.0 - ADAM_B2 ** ADAM_STEP)

    def body(w_ref, g_ref, m_ref, v_ref, d_ref, nm_ref, nv_ref):
        gv = g_ref[...]
        nm = ADAM_B1 * m_ref[...] + (1.0 - ADAM_B1) * gv
        nv = ADAM_B2 * v_ref[...] + (1.0 - ADAM_B2) * (gv * gv)
        nm_ref[...] = nm
        nv_ref[...] = nv
        d_ref[...] = -ADAM_LR * ((nm * c1) / (jnp.sqrt(nv * c2) + ADAM_EPS) + ADAM_WD * w_ref[...])

    spec = pl.BlockSpec((rb, c), lambda i: (i, 0))
    return pl.pallas_call(
        body, name="adamw", grid=(r // rb,), in_specs=[spec] * 4, out_specs=[spec] * 3,
        out_shape=[jax.ShapeDtypeStruct((r, c), F32)] * 3,
        compiler_params=_params(("arbitrary",)),
    )(w, g, m, v)


def _place():
    return lax.axis_index("x"), lax.axis_index("y"), lax.axis_index("c")


def _other_chips(x, y):
    return [(1 - x, y), (x, 1 - y), (1 - x, 1 - y)]


def gather_weights(shards):
    n = len(shards)

    def body(*refs):
        ins, outs = refs[:n], refs[n:2 * n]
        send, recv, fsend, frecv, lsem = refs[2 * n:]
        x, y, c = _place()
        p = 2 * x + y
        sib = (x, y, 1 - c)
        chips = _other_chips(x, y)
        local = []
        for k in range(n):
            cp = pltpu.make_async_copy(ins[k], outs[k].at[:, p], lsem.at[k])
            cp.start()
            local.append(cp)

        def piece(ref, k, chip, half):
            hr = shards[k].shape[1] // 2
            return ref.at[:, 2 * chip[0] + chip[1], pl.ds(half * hr, hr), :]

        def mine(k, half):
            hr = shards[k].shape[1] // 2
            return ins[k].at[:, pl.ds(half * hr, hr), :]

        sends = []
        for k in range(n):
            for j, chip in enumerate(chips):
                cp = pltpu.make_async_remote_copy(mine(k, c), piece(outs[k], k, (x, y), c), send.at[k, j], recv.at[k, j],
                                                  device_id=(chip[0], chip[1], c), device_id_type=MESH)
                cp.start()
                sends.append(cp)
        passed = []
        for k in range(n):
            for j, chip in enumerate(chips):
                got = piece(outs[k], k, chip, c)
                pltpu.make_async_remote_copy(got, got, send.at[k, j], recv.at[k, j],
                                             device_id=(chip[0], chip[1], c), device_id_type=MESH).wait_recv()
                cp = pltpu.make_async_remote_copy(got, got, fsend.at[k, j], frecv.at[k, j], device_id=sib, device_id_type=MESH)
                cp.start()
                passed.append(cp)
        for k in range(n):
            for j, chip in enumerate(chips):
                theirs = piece(outs[k], k, chip, 1 - c)
                pltpu.make_async_remote_copy(theirs, theirs, fsend.at[k, j], frecv.at[k, j],
                                             device_id=sib, device_id_type=MESH).wait_recv()
        for cp in sends + passed:
            cp.wait_send()
        for cp in local:
            cp.wait()

    out_shape = [jax.ShapeDtypeStruct((s.shape[0], NQ) + s.shape[1:], s.dtype) for s in shards]
    return pl.pallas_call(
        body, name="gather_weights", in_specs=[ANY] * n, out_specs=[ANY] * n, out_shape=out_shape,
        scratch_shapes=[pltpu.SemaphoreType.DMA((n, 3)), pltpu.SemaphoreType.DMA((n, 3)),
                        pltpu.SemaphoreType.DMA((n, 3)), pltpu.SemaphoreType.DMA((n, 3)),
                        pltpu.SemaphoreType.DMA((n,))],
    )(*shards)


def gather_small(shard):
    r, c = shard.shape

    def body(in_ref, out_ref, send, recv, lsem):
        x, y, cc = _place()
        own = pltpu.make_async_copy(in_ref, out_ref.at[2 * x + y], lsem)
        own.start()
        sends = []
        for j, chip in enumerate(_other_chips(x, y)):
            cp = pltpu.make_async_remote_copy(in_ref, out_ref.at[2 * x + y], send.at[j], recv.at[j],
                                              device_id=(chip[0], chip[1], cc), device_id_type=MESH)
            cp.start()
            sends.append(cp)
        for j, chip in enumerate(_other_chips(x, y)):
            slot = out_ref.at[2 * chip[0] + chip[1]]
            pltpu.make_async_remote_copy(slot, slot, send.at[j], recv.at[j],
                                         device_id=(chip[0], chip[1], cc), device_id_type=MESH).wait_recv()
        for cp in sends:
            cp.wait_send()
        own.wait()

    vm = pl.BlockSpec(memory_space=pltpu.VMEM)
    return pl.pallas_call(
        body, name="gather_small", in_specs=[vm], out_specs=vm,
        out_shape=jax.ShapeDtypeStruct((NQ, r, c), shard.dtype),
        scratch_shapes=[pltpu.SemaphoreType.DMA((3,)), pltpu.SemaphoreType.DMA((3,)), pltpu.SemaphoreType.DMA],
    )(shard)


def swap_halves(arrs):
    n = len(arrs)

    def body(*refs):
        ins, kept, got = refs[:n], refs[n:2 * n], refs[2 * n:3 * n]
        send, recv, lsem = refs[3 * n:]
        x, y, c = _place()
        cps = []
        for k in range(n):
            hr = arrs[k].shape[1] // 2
            lc = pltpu.make_async_copy(ins[k].at[:, pl.ds(c * hr, hr), :], kept[k], lsem.at[k])
            lc.start()
            cp = pltpu.make_async_remote_copy(ins[k].at[:, pl.ds((1 - c) * hr, hr), :], got[k], send.at[k], recv.at[k],
                                              device_id=(x, y, 1 - c), device_id_type=MESH)
            cp.start()
            cps.append((lc, cp))
        for lc, cp in cps:
            cp.wait()
            lc.wait()

    half = [jax.ShapeDtypeStruct((a.shape[0], a.shape[1] // 2, a.shape[2]), a.dtype) for a in arrs]
    res = pl.pallas_call(
        body, name="swap_halves", in_specs=[ANY] * n, out_specs=[ANY] * (2 * n), out_shape=half + half,
        scratch_shapes=[pltpu.SemaphoreType.DMA((n,)), pltpu.SemaphoreType.DMA((n,)), pltpu.SemaphoreType.DMA((n,))],
    )(*arrs)
    return res[:n], res[n:]


def exchange_chips(arrs, scatter):
    n = len(arrs)

    def body(*refs):
        ins, outs = refs[:n], refs[n:2 * n]
        send, recv, lsem = refs[2 * n:]
        x, y, c = _place()
        p = 2 * x + y
        chips = _other_chips(x, y)
        cps = []
        for k in range(n):
            lc = pltpu.make_async_copy(ins[k].at[p if scatter[k] else 0], outs[k].at[p], lsem.at[k])
            lc.start()
            cps.append(lc)
        sends = []
        for k in range(n):
            for j, chip in enumerate(chips):
                src = ins[k].at[2 * chip[0] + chip[1] if scatter[k] else 0]
                cp = pltpu.make_async_remote_copy(src, outs[k].at[p], send.at[k, j], recv.at[k, j],
                                                  device_id=(chip[0], chip[1], c), device_id_type=MESH)
                cp.start()
                sends.append(cp)
        for k in range(n):
            for j, chip in enumerate(chips):
                slot = outs[k].at[2 * chip[0] + chip[1]]
                pltpu.make_async_remote_copy(slot, slot, send.at[k, j], recv.at[k, j],
                                             device_id=(chip[0], chip[1], c), device_id_type=MESH).wait_recv()
        for cp in sends:
            cp.wait_send()
        for lc in cps:
            lc.wait()

    out_shape = [jax.ShapeDtypeStruct((NQ,) + a.shape[1:], a.dtype) for a in arrs]
    return pl.pallas_call(
        body, name="exchange_chips", in_specs=[ANY] * n, out_specs=[ANY] * n, out_shape=out_shape,
        scratch_shapes=[pltpu.SemaphoreType.DMA((n, 3)), pltpu.SemaphoreType.DMA((n, 3)), pltpu.SemaphoreType.DMA((n,))],
    )(*arrs)


def join_halves(arrs):
    n = len(arrs)

    def body(*refs):
        ins, outs = refs[:n], refs[n:2 * n]
        send, recv, lsem = refs[2 * n:]
        x, y, c = _place()
        cps = []
        for k in range(n):
            hr = arrs[k].shape[0]
            lc = pltpu.make_async_copy(ins[k], outs[k].at[pl.ds(c * hr, hr), :], lsem.at[k])
            lc.start()
            cp = pltpu.make_async_remote_copy(ins[k], outs[k].at[pl.ds(c * hr, hr), :], send.at[k], recv.at[k],
                                              device_id=(x, y, 1 - c), device_id_type=MESH)
            cp.start()
            cps.append((lc, cp))
        for k, (lc, cp) in enumerate(cps):
            hr = arrs[k].shape[0]
            cp.wait_send()
            theirs = outs[k].at[pl.ds((1 - c) * hr, hr), :]
            pltpu.make_async_remote_copy(theirs, theirs, send.at[k], recv.at[k],
                                         device_id=(x, y, 1 - c), device_id_type=MESH).wait_recv()
            lc.wait()

    out_shape = [jax.ShapeDtypeStruct((2 * a.shape[0], a.shape[1]), a.dtype) for a in arrs]
    return pl.pallas_call(
        body, name="join_halves", in_specs=[ANY] * n, out_specs=[ANY] * n, out_shape=out_shape,
        scratch_shapes=[pltpu.SemaphoreType.DMA((n,)), pltpu.SemaphoreType.DMA((n,)), pltpu.SemaphoreType.DMA((n,))],
    )(*arrs)


def reduce_over_devices(arrs, scatter, dtype):
    kept, got = swap_halves(arrs)
    pair = add_pairs(kept, got, dtype)
    parts = exchange_chips(pair, scatter)
    return join_halves(sum_chips(parts))


def _block_diag(w):
    out = jnp.zeros((2, 256, 256), F32)
    for hd in range(8):
        k, o = hd // 4, 64 * (hd % 4)
        out = out.at[k, o:o + 64, o:o + 64].set(w[hd])
    return out.astype(BF16)


def _diag_blocks(b):
    return jnp.stack([b[hd // 4, 64 * (hd % 4):64 * (hd % 4) + 64, 64 * (hd % 4):64 * (hd % 4) + 64] for hd in range(8)])


def _vec_params(l, pool_b, pool_scale, conv_b, gate_r_b, gate_i_b, lru_lambda, conv_w_full, group_norm_g):
    rows = [pool_b[l], pool_scale[l], conv_b[l], gate_r_b[l], gate_i_b[l], lru_lambda[l],
            conv_w_full[l, 0], conv_w_full[l, 1], conv_w_full[l, 2], conv_w_full[l, 3],
            group_norm_g[l, :DP], group_norm_g[l, DP:]]
    return jnp.concatenate([jnp.stack(rows), jnp.zeros((4, 512), F32)], axis=0)


SMALL_ROWS_LAYER = 16 + 128 + 64 + 64 + 2 + 2
SMALL_ROWS = 1152


def _pack_small(layers, final_g, meta):
    rows = []
    for vec, pw, wr, wi, g1, g2 in layers:
        rows += [vec, pw.reshape(128, 512), wr.reshape(64, 512), wi.reshape(64, 512), g1.reshape(2, 512), g2.reshape(2, 512)]
    rows += [final_g.reshape(2, 512), meta.reshape(32, 512)]
    flat = jnp.concatenate(rows, axis=0)
    return jnp.concatenate([flat, jnp.zeros((SMALL_ROWS - flat.shape[0], 512), F32)], axis=0)


def _unpack_small(flat):
    layers, o = [], 0
    for _ in range(DEPTH):
        vec = flat[o:o + 16]; o += 16
        pw = flat[o:o + 128].reshape(4, 128, 128); o += 128
        wr = flat[o:o + 64].reshape(8, 64, 64); o += 64
        wi = flat[o:o + 64].reshape(8, 64, 64); o += 64
        g1 = flat[o:o + 2].reshape(1024); o += 2
        g2 = flat[o:o + 2].reshape(1024); o += 2
        layers.append((vec, pw, wr, wi, g1, g2))
    final_g = flat[o:o + 2].reshape(1024); o += 2
    meta = flat[o:o + 32].reshape(16, 1024)
    return layers, final_g, meta


def local_step(x2d, tgt2d, meta_full, conv_w_full, win_g, wo_g, wu_g, wd_g, sp):
    h = jnp.concatenate([meta_full, x2d], axis=0)
    tgt = jnp.concatenate([jnp.zeros((NMETA, D), F32), tgt2d], axis=0)
    saved = []
    for l in range(DEPTH):
        vp = _vec_params(l, sp["pool_b"], sp["pool_scale"], sp["conv_b"], sp["gate_r_b"], sp["gate_i_b"], sp["lru_lambda"],
                         conv_w_full, sp["group_norm_g"])
        pw = sp["pool_w"][l].astype(BF16)
        br, bi = _block_diag(sp["gate_r_w"][l]), _block_diag(sp["gate_i_w"][l])
        proj = fwd_in(h, sp["mix_norm_g"][l][None], win_g, l)
        y, hs = fwd_mixer(proj, vp, pw, br, bi)
        h1, a, h2 = fwd_post(h, y, wo_g, sp["mlp_norm_g"][l][None], wu_g, wd_g, l)
        saved.append((h, proj, y, hs, h1, a, vp, pw, br, bi))
        h = h2
    dh, dgf, loss_part = fwd_loss(h, sp["final_norm_g"][None], tgt)

    big = [None] * DEPTH
    small_layers = [None] * DEPTH
    for l in reversed(range(DEPTH)):
        h0, proj, y, hs, h1, a, vp, pw, br, bi = saved[l]
        dh1, da, xn2, dg2 = bwd_mlp_dx(dh, h1, a, sp["mlp_norm_g"][l][None], wu_g, wd_g, l)
        dwu, dwd = bwd_mlp_dw(xn2, da, a, dh)
        dy, dwo = bwd_out(dh1, y, wo_g, l)
        dproj, ga, dpw, dwr, dwi = bwd_mixer(dy, proj, hs, vp, pw, br, bi)
        dh, dwin, dg1 = bwd_in(dproj, h0, sp["mix_norm_g"][l][None], win_g, dh1, l)
        big[l] = [dwin, dwo.reshape(NQ, D // NQ, D), dwu, dwd]
        small_layers[l] = (ga, dpw, _diag_blocks(dwr), _diag_blocks(dwi), dg1[0], dg2[0])
    return loss_part, dh, big, small_layers, dgf


def kernel(x, meta_tokens, mix_norm_g, w_in, pool_w, pool_b, pool_scale, conv_w, conv_b, gate_r_w, gate_r_b, gate_i_w, gate_i_b, lru_lambda, group_norm_g, w_out, mlp_norm_g, w_up, w_down, final_norm_g, loss_target, m_meta_tokens, m_mix_norm_g, m_w_in, m_pool_w, m_pool_b, m_pool_scale, m_conv_w, m_conv_b, m_gate_r_w, m_gate_r_b, m_gate_i_w, m_gate_i_b, m_lru_lambda, m_group_norm_g, m_w_out, m_mlp_norm_g, m_w_up, m_w_down, m_final_norm_g, v_meta_tokens, v_mix_norm_g, v_w_in, v_pool_w, v_pool_b, v_pool_scale, v_conv_w, v_conv_b, v_gate_r_w, v_gate_r_b, v_gate_i_w, v_gate_i_b, v_lru_lambda, v_group_norm_g, v_w_out, v_mlp_norm_g, v_w_up, v_w_down, v_final_norm_g):
    p = 2 * lax.axis_index("x") + lax.axis_index("y")

    win_g, wo_g, wu_g, wd_g = gather_weights([w_in.astype(BF16), w_out.astype(BF16), w_up.astype(BF16), w_down.astype(BF16)])
    wo_g = wo_g.reshape(DEPTH, D, D)
    small = jnp.concatenate([meta_tokens, jnp.pad(conv_w.reshape(16, 128), ((0, 0), (0, 128)))], axis=0)
    small_g = gather_small(small)
    meta_full = jnp.transpose(small_g[:, :16, :], (1, 0, 2)).reshape(NMETA, D)
    conv_w_full = jnp.transpose(small_g[:, 16:, :128].reshape(NQ, DEPTH, 4, 128), (1, 2, 0, 3)).reshape(DEPTH, 4, DR)

    sp = dict(mix_norm_g=mix_norm_g, pool_w=pool_w, pool_b=pool_b, pool_scale=pool_scale, conv_b=conv_b, gate_r_w=gate_r_w,
              gate_r_b=gate_r_b, gate_i_w=gate_i_w, gate_i_b=gate_i_b, lru_lambda=lru_lambda, group_norm_g=group_norm_g,
              mlp_norm_g=mlp_norm_g, final_norm_g=final_norm_g)
    loss_part, dh, big_parts, small_layers, dgf = local_step(x[0], loss_target[0], meta_full, conv_w_full,
                                                             win_g, wo_g, wu_g, wd_g, sp)
    loss = lax.psum(loss_part[0, 0], ("x", "y", "c"))
    grad_x = dh[NMETA:][None]

    big = [reduce_over_devices(big_parts[l], [True] * 4, BF16) for l in reversed(range(DEPTH))][::-1]
    small_sum = reduce_over_devices([_pack_small(small_layers, dgf[0], dh[:NMETA])[None]], [False], F32)[0]
    g_layers, g_final, g_meta_full = _unpack_small(small_sum)

    g_vec = [gl[0] for gl in g_layers]
    grads = dict(
        meta_tokens=lax.dynamic_slice(g_meta_full, (0, p * (D // NQ)), (NMETA, D // NQ)),
        mix_norm_g=jnp.stack([gl[4] for gl in g_layers]),
        w_in=jnp.stack([big[l][0] for l in range(DEPTH)]),
        pool_w=jnp.stack([gl[1] for gl in g_layers]),
        pool_b=jnp.stack([gv[GA_PB] for gv in g_vec]),
        pool_scale=jnp.stack([gv[GA_PS] for gv in g_vec]),
        conv_w=lax.dynamic_slice(jnp.stack([gv[GA_CW:GA_CW + 4] for gv in g_vec]), (0, 0, p * 128), (DEPTH, 4, 128)),
        conv_b=jnp.stack([gv[GA_CB] for gv in g_vec]),
        gate_r_w=jnp.stack([gl[2] for gl in g_layers]),
        gate_r_b=jnp.stack([gv[GA_BR] for gv in g_vec]),
        gate_i_w=jnp.stack([gl[3] for gl in g_layers]),
        gate_i_b=jnp.stack([gv[GA_BI] for gv in g_vec]),
        lru_lambda=jnp.stack([gv[GA_LAM] for gv in g_vec]),
        group_norm_g=jnp.stack([jnp.concatenate([gv[GA_GNP], gv[GA_GNR]]) for gv in g_vec]),
        w_out=jnp.stack([big[l][1] for l in range(DEPTH)]),
        mlp_norm_g=jnp.stack([gl[5] for gl in g_layers]),
        w_up=jnp.stack([big[l][2] for l in range(DEPTH)]),
        w_down=jnp.stack([big[l][3] for l in range(DEPTH)]),
        final_norm_g=g_final,
    )
    weights = dict(meta_tokens=meta_tokens, mix_norm_g=mix_norm_g, w_in=w_in, pool_w=pool_w, pool_b=pool_b, pool_scale=pool_scale,
                   conv_w=conv_w, conv_b=conv_b, gate_r_w=gate_r_w, gate_r_b=gate_r_b, gate_i_w=gate_i_w, gate_i_b=gate_i_b,
                   lru_lambda=lru_lambda, group_norm_g=group_norm_g, w_out=w_out, mlp_norm_g=mlp_norm_g, w_up=w_up, w_down=w_down,
                   final_norm_g=final_norm_g)
    mom_m = dict(meta_tokens=m_meta_tokens, mix_norm_g=m_mix_norm_g, w_in=m_w_in, pool_w=m_pool_w, pool_b=m_pool_b,
                 pool_scale=m_pool_scale, conv_w=m_conv_w, conv_b=m_conv_b, gate_r_w=m_gate_r_w, gate_r_b=m_gate_r_b,
                 gate_i_w=m_gate_i_w, gate_i_b=m_gate_i_b, lru_lambda=m_lru_lambda, group_norm_g=m_group_norm_g, w_out=m_w_out,
                 mlp_norm_g=m_mlp_norm_g, w_up=m_w_up, w_down=m_w_down, final_norm_g=m_final_norm_g)
    mom_v = dict(meta_tokens=v_meta_tokens, mix_norm_g=v_mix_norm_g, w_in=v_w_in, pool_w=v_pool_w, pool_b=v_pool_b,
                 pool_scale=v_pool_scale, conv_w=v_conv_w, conv_b=v_conv_b, gate_r_w=v_gate_r_w, gate_r_b=v_gate_r_b,
                 gate_i_w=v_gate_i_w, gate_i_b=v_gate_i_b, lru_lambda=v_lru_lambda, group_norm_g=v_group_norm_g, w_out=v_w_out,
                 mlp_norm_g=v_mlp_norm_g, w_up=v_w_up, w_down=v_w_down, final_norm_g=v_final_norm_g)
    names = list(weights)

    delta, new_m, new_v = {}, {}, {}
    big_names = ("w_in", "w_out", "w_up", "w_down")
    for nm in big_names:
        shp = weights[nm].shape
        two_d = lambda t: t.reshape(shp[0] * shp[1], shp[2])
        d_, m_, v_ = adamw(two_d(weights[nm]), two_d(grads[nm]), two_d(mom_m[nm]), two_d(mom_v[nm]))
        delta[nm], new_m[nm], new_v[nm] = d_.reshape(shp), m_.reshape(shp), v_.reshape(shp)
    small_names = [nm for nm in names if nm not in big_names]
    sizes = [weights[nm].size for nm in small_names]
    total = sum(sizes)
    rows = -(-total // 512)
    rows = -(-rows // 16) * 16

    def flat(tree, fill):
        v_ = jnp.concatenate([tree[nm].reshape(-1) for nm in small_names])
        return jnp.concatenate([v_, jnp.full((rows * 512 - total,), fill, F32)]).reshape(rows, 512)

    d_, m_, v_ = adamw(flat(weights, 0.0), flat(grads, 0.0), flat(mom_m, 0.0), flat(mom_v, 1.0))
    o = 0
    for nm, sz in zip(small_names, sizes):
        shp = weights[nm].shape
        delta[nm] = d_.reshape(-1)[o:o + sz].reshape(shp)
        new_m[nm] = m_.reshape(-1)[o:o + sz].reshape(shp)
        new_v[nm] = v_.reshape(-1)[o:o + sz].reshape(shp)
        o += sz

    return (loss, grad_x, *[grads[nm] for nm in names], *[delta[nm] for nm in names],
            *[new_m[nm] for nm in names], *[new_v[nm] for nm in names])
```

```python
import functools

import jax
import jax.numpy as jnp
from jax import lax
from jax.experimental import pallas as pl
from jax.experimental.pallas import tpu as pltpu

F32 = jnp.float32
BF16 = jnp.bfloat16

D = 1024
DP = 512
DR = 512
DIN = 1536
DFF = 4096
DEPTH = 4
NMETA = 16
NQ = 4
WIN_S = DIN // NQ
FF_S = DFF // NQ
EPS = 1e-6
HALO = 16
TT = 432
TTW = 912
VMEM_LIMIT = 56 * 1024 * 1024

ADAM_LR = 0.001
ADAM_B1 = 0.9
ADAM_B2 = 0.999
ADAM_EPS = 1e-08
ADAM_WD = 0.01
ADAM_STEP = 10

MESH = pl.DeviceIdType.MESH
ANY = pl.BlockSpec(memory_space=pl.ANY)


def _params(sem=None, vmem=VMEM_LIMIT):
    return pltpu.CompilerParams(dimension_semantics=sem, vmem_limit_bytes=vmem)


def _resident(shape, index):
    return pl.BlockSpec(shape, index, pipeline_mode=pl.Buffered(1))


def _nt(x, w):
    return lax.dot_general(x, w, (((1,), (1,)), ((), ())), preferred_element_type=F32)


def _tn(a, b):
    return lax.dot_general(a, b, (((0,), (0,)), ((), ())), preferred_element_type=F32)


def _dot(x, w):
    return jnp.dot(x, w, preferred_element_type=F32)


def _rms(h):
    return lax.rsqrt(jnp.mean(h * h, axis=-1, keepdims=True) + EPS)


def _rms_bwd(n, r, dn):
    return r * (dn - n * jnp.mean(dn * n, axis=-1, keepdims=True))


def _down(x, s):
    return pltpu.roll(x, s, 0)


def _up(x, s):
    return pltpu.roll(x, x.shape[0] - s, 0)


GELU_C = 0.7978845608028654
GELU_K = 0.044715


def _gelu(x):
    t = jnp.tanh(GELU_C * (x + GELU_K * x * x * x))
    return 0.5 * x * (1.0 + t), t


def _softplus(x):
    return jnp.maximum(x, 0.0) + jnp.log1p(jnp.exp(-jnp.abs(x)))


VP_PB, VP_PS, VP_CB, VP_BR, VP_BI, VP_LAM, VP_CW, VP_GNP, VP_GNR = 0, 1, 2, 3, 4, 5, 6, 10, 11


def _row(vp, r):
    return vp[r:r + 1, :]


def _mixer_pre(ue, t0, vp, pw_ref, br_ref, bi_ref):
    tt = ue.shape[0] - HALO
    tf = (t0 + lax.broadcasted_iota(jnp.int32, (tt, 1), 0)).astype(F32) + 1.0
    pb, ps = _row(vp, VP_PB), _row(vp, VP_PS)
    pooled, mapped, inv_cnt = [], [], []
    for g in range(4):
        lanes = slice(128 * g, 128 * (g + 1))
        xe = ue[:, lanes]
        s = xe
        for j in range(g + 1):
            s = s + _down(s, 1 << j)
        inv = 1.0 / jnp.minimum(tf, float(2 << g))
        pg = s[HALO:] * inv - xe[HALO:]
        mg = _dot(pg.astype(BF16), pw_ref[g]) + pb[:, lanes]
        pooled.append(pg)
        mapped.append(mg)
        inv_cnt.append(inv)
    ypool = [mapped[g] * ps[:, 128 * g:128 * (g + 1)] for g in range(4)]

    xe = ue[:, DP:DP + DR]
    taps = [_down(xe, 3)[HALO:], _down(xe, 2)[HALO:], _down(xe, 1)[HALO:], xe[HALO:]]
    xc = _row(vp, VP_CB) + (taps[0] * _row(vp, VP_CW) + taps[1] * _row(vp, VP_CW + 1)
                            + taps[2] * _row(vp, VP_CW + 2) + taps[3] * _row(vp, VP_CW + 3))
    xcb = xc.astype(BF16)
    zr = jnp.concatenate([_dot(xcb[:, :256], br_ref[0]), _dot(xcb[:, 256:], br_ref[1])], axis=1) + _row(vp, VP_BR)
    zi = jnp.concatenate([_dot(xcb[:, :256], bi_ref[0]), _dot(xcb[:, 256:], bi_ref[1])], axis=1) + _row(vp, VP_BI)
    r = jax.nn.sigmoid(zr)
    ig = jax.nn.sigmoid(zi)
    sp = _softplus(-_row(vp, VP_LAM))
    la = (-8.0 * r) * sp
    a = jnp.exp(la)
    th = jnp.tanh(la)
    mult = jnp.sqrt((-2.0 * th) / (1.0 - th))
    gate = ue[HALO:, DP + DR:]
    gl, gt = _gelu(gate)
    return dict(pooled=pooled, mapped=mapped, ypool=ypool, inv_cnt=inv_cnt, taps=taps, xc=xc, xcb=xcb, r=r, ig=ig,
                sp=sp, a=a, mult=mult, gate=gate, gl=gl, gt=gt)


def _scan_fwd(a, b):
    tt = a.shape[0]
    row = lax.broadcasted_iota(jnp.int32, (tt, 1), 0)
    s = 1
    while s < tt:
        m = row >= s
        a_s = jnp.where(m, _down(a, s), 1.0)
        b_s = jnp.where(m, _down(b, s), 0.0)
        b = a * b_s + b
        a = a * a_s
        s *= 2
    return a, b


def _scan_rev(c, d):
    tt = c.shape[0]
    row = lax.broadcasted_iota(jnp.int32, (tt, 1), 0)
    s = 1
    while s < tt:
        m = row < tt - s
        c_s = jnp.where(m, _up(c, s), 1.0)
        d_s = jnp.where(m, _up(d, s), 0.0)
        d = c * d_s + d
        c = c * c_s
        s *= 2
    return c, d


def _halo_index(i, per_tile):
    return jnp.maximum(i * per_tile - 1, 0)


def fwd_in(h, g, w_g, layer):
    T = h.shape[0]

    def body(h_ref, g_ref, w_ref, o_ref):
        hh = h_ref[...]
        xn = (hh * _rms(hh) * g_ref[...]).astype(BF16)
        for q in range(NQ):
            o_ref[:, WIN_S * q:WIN_S * (q + 1)] = _dot(xn, w_ref[q])

    return pl.pallas_call(
        body, name="fwd_in", grid=(T // TT,),
        in_specs=[pl.BlockSpec((TT, D), lambda i: (i, 0)), _resident((1, D), lambda i: (0, 0)),
                  _resident((None, NQ, D, WIN_S), lambda i: (layer, 0, 0, 0))],
        out_specs=pl.BlockSpec((TT, DIN), lambda i: (i, 0)),
        out_shape=jax.ShapeDtypeStruct((T, DIN), F32),
        compiler_params=_params(("arbitrary",)),
    )(h, g, w_g)


def fwd_mixer(proj, vp, pw, br, bi):
    T = proj.shape[0]
    per = TT // HALO

    def body(p_ref, ph_ref, vp_ref, pw_ref, br_ref, bi_ref, y_ref, hs_ref, carry_ref):
        i = pl.program_id(0)

        @pl.when(i == 0)
        def _():
            carry_ref[...] = jnp.zeros_like(carry_ref)

        halo = jnp.where(i > 0, ph_ref[...], 0.0)
        ue = jnp.concatenate([halo, p_ref[...]], axis=0)
        vp_v = vp_ref[...]
        m = _mixer_pre(ue, i * TT, vp_v, pw_ref, br_ref, bi_ref)
        b = m["mult"] * (m["ig"] * m["xc"])
        ca, cb = _scan_fwd(m["a"], b)
        hs = ca * carry_ref[0:1, :] + cb
        hs_ref[...] = hs
        carry_ref[0:1, :] = hs_ref[TT - 1:TT, :]
        yr = hs * m["gl"]
        ssq = sum(jnp.sum(yp * yp, axis=-1, keepdims=True) for yp in m["ypool"])
        rp = lax.rsqrt(ssq * (1.0 / DP) + EPS)
        gnp = _row(vp_v, VP_GNP)
        for g in range(4):
            lanes = slice(128 * g, 128 * (g + 1))
            y_ref[:, lanes] = (m["ypool"][g] * rp * gnp[:, lanes]).astype(BF16)
        y_ref[:, DP:] = (yr * _rms(yr) * _row(vp_v, VP_GNR)).astype(BF16)

    return pl.pallas_call(
        body, name="fwd_mixer", grid=(T // TT,),
        in_specs=[pl.BlockSpec((TT, DIN), lambda i: (i, 0)),
                  pl.BlockSpec((HALO, DIN), lambda i: (_halo_index(i, per), 0)),
                  _resident((16, 512), lambda i: (0, 0)), _resident((4, 128, 128), lambda i: (0, 0, 0)),
                  _resident((2, 256, 256), lambda i: (0, 0, 0)), _resident((2, 256, 256), lambda i: (0, 0, 0))],
        out_specs=[pl.BlockSpec((TT, D), lambda i: (i, 0)), pl.BlockSpec((TT, DR), lambda i: (i, 0))],
        out_shape=[jax.ShapeDtypeStruct((T, D), BF16), jax.ShapeDtypeStruct((T, DR), F32)],
        scratch_shapes=[pltpu.VMEM((8, DR), F32)],
        compiler_params=_params(("arbitrary",)),
    )(proj, proj, vp, pw, br, bi)


def fwd_post(h, y, wo_g, g2, wu_g, wd_g, layer):
    T = h.shape[0]

    def body(h_ref, y_ref, wo_ref, g_ref, wu_ref, wd_ref, h1_ref, a_ref, h2_ref):
        h1 = h_ref[...] + _dot(y_ref[...], wo_ref[...])
        h1_ref[...] = h1
        xn = (h1 * _rms(h1) * g_ref[...]).astype(BF16)
        acc = h1
        for q in range(NQ):
            a = _dot(xn, wu_ref[q])
            a_ref[:, FF_S * q:FF_S * (q + 1)] = a.astype(BF16)
            ra = jnp.maximum(a, 0.0)
            acc = acc + _dot((ra * ra).astype(BF16), wd_ref[q])
        h2_ref[...] = acc

    return pl.pallas_call(
        body, name="fwd_post", grid=(T // TT,),
        in_specs=[pl.BlockSpec((TT, D), lambda i: (i, 0)), pl.BlockSpec((TT, D), lambda i: (i, 0)),
                  _resident((None, D, D), lambda i: (layer, 0, 0)), _resident((1, D), lambda i: (0, 0)),
                  _resident((None, NQ, D, FF_S), lambda i: (layer, 0, 0, 0)),
                  _resident((None, NQ, FF_S, D), lambda i: (layer, 0, 0, 0))],
        out_specs=[pl.BlockSpec((TT, D), lambda i: (i, 0)), pl.BlockSpec((TT, DFF), lambda i: (i, 0)),
                   pl.BlockSpec((TT, D), lambda i: (i, 0))],
        out_shape=[jax.ShapeDtypeStruct((T, D), F32), jax.ShapeDtypeStruct((T, DFF), BF16),
                   jax.ShapeDtypeStruct((T, D), F32)],
        compiler_params=_params(("arbitrary",)),
    )(h, y, wo_g, g2, wu_g, wd_g)


def fwd_loss(h, gf, tgt):
    T = h.shape[0]

    def body(h_ref, g_ref, t_ref, dh_ref, dg_ref, loss_ref):
        i = pl.program_id(0)

        @pl.when(i == 0)
        def _():
            dg_ref[...] = jnp.zeros_like(dg_ref)
            loss_ref[...] = jnp.zeros_like(loss_ref)

        hh = h_ref[...]
        r = _rms(hh)
        n = hh * r
        gfv = g_ref[...]
        row = i * TT + lax.broadcasted_iota(jnp.int32, (TT, 1), 0)
        e = jnp.where(row >= NMETA, n * gfv - t_ref[...], 0.0)
        loss_ref[...] += 0.5 * jnp.sum(jnp.sum(e * e, axis=-1, keepdims=True) * (1.0 / D), axis=0, keepdims=True)
        dy = e * (1.0 / D)
        dg_ref[...] += jnp.sum(dy * n, axis=0, keepdims=True)
        dh_ref[...] = _rms_bwd(n, r, dy * gfv)

    return pl.pallas_call(
        body, name="fwd_loss", grid=(T // TT,),
        in_specs=[pl.BlockSpec((TT, D), lambda i: (i, 0)), _resident((1, D), lambda i: (0, 0)),
                  pl.BlockSpec((TT, D), lambda i: (i, 0))],
        out_specs=[pl.BlockSpec((TT, D), lambda i: (i, 0)), pl.BlockSpec((1, D), lambda i: (0, 0)),
                   pl.BlockSpec((1, 1), lambda i: (0, 0))],
        out_shape=[jax.ShapeDtypeStruct((T, D), F32), jax.ShapeDtypeStruct((1, D), F32),
                   jax.ShapeDtypeStruct((1, 1), F32)],
        compiler_params=_params(("arbitrary",)),
    )(h, gf, tgt)


def bwd_mlp_dx(dh2, h1, a, g2, wu_g, wd_g, layer):
    T = dh2.shape[0]

    def body(dh2_ref, h1_ref, a_ref, g_ref, wu_ref, wd_ref, dh1_ref, da_ref, xn_ref, dg_ref):
        @pl.when(pl.program_id(0) == 0)
        def _():
            dg_ref[...] = jnp.zeros_like(dg_ref)

        h1v = h1_ref[...]
        r = _rms(h1v)
        n = h1v * r
        gv = g_ref[...]
        xn_ref[...] = (n * gv).astype(BF16)
        dh2v = dh2_ref[...]
        dh2b = dh2v.astype(BF16)
        dxn = jnp.zeros((TT, D), F32)
        for q in range(NQ):
            cols = slice(FF_S * q, FF_S * (q + 1))
            ra = jnp.maximum(a_ref[:, cols].astype(F32), 0.0)
            da = (_nt(dh2b, wd_ref[q]) * (2.0 * ra)).astype(BF16)
            da_ref[:, cols] = da
            dxn = dxn + _nt(da, wu_ref[q])
        dg_ref[...] += jnp.sum(dxn * n, axis=0, keepdims=True)
        dh1_ref[...] = dh2v + _rms_bwd(n, r, dxn * gv)

    return pl.pallas_call(
        body, name="bwd_mlp_dx", grid=(T // TT,),
        in_specs=[pl.BlockSpec((TT, D), lambda i: (i, 0)), pl.BlockSpec((TT, D), lambda i: (i, 0)),
                  pl.BlockSpec((TT, DFF), lambda i: (i, 0)), _resident((1, D), lambda i: (0, 0)),
                  _resident((None, NQ, D, FF_S), lambda i: (layer, 0, 0, 0)),
                  _resident((None, NQ, FF_S, D), lambda i: (layer, 0, 0, 0))],
        out_specs=[pl.BlockSpec((TT, D), lambda i: (i, 0)), pl.BlockSpec((TT, DFF), lambda i: (i, 0)),
                   pl.BlockSpec((TT, D), lambda i: (i, 0)), pl.BlockSpec((1, D), lambda i: (0, 0))],
        out_shape=[jax.ShapeDtypeStruct((T, D), F32), jax.ShapeDtypeStruct((T, DFF), BF16),
                   jax.ShapeDtypeStruct((T, D), BF16), jax.ShapeDtypeStruct((1, D), F32)],
        compiler_params=_params(("arbitrary",)),
    )(dh2, h1, a, g2, wu_g, wd_g)


def bwd_mlp_dw(xn, da, a, dh2):
    T = xn.shape[0]

    def body(xn_ref, da_ref, a_ref, dh2_ref, dwu_ref, dwd_ref):
        @pl.when(pl.program_id(1) == 0)
        def _():
            dwu_ref[...] = jnp.zeros_like(dwu_ref)
            dwd_ref[...] = jnp.zeros_like(dwd_ref)

        dwu_ref[...] += _tn(xn_ref[...], da_ref[...])
        ra = jnp.maximum(a_ref[...].astype(F32), 0.0)
        dwd_ref[...] += _tn((ra * ra).astype(BF16), dh2_ref[...].astype(BF16))

    return pl.pallas_call(
        body, name="bwd_mlp_dw", grid=(NQ, T // TTW),
        in_specs=[pl.BlockSpec((TTW, D), lambda q, i: (i, 0)), pl.BlockSpec((TTW, FF_S), lambda q, i: (i, q)),
                  pl.BlockSpec((TTW, FF_S), lambda q, i: (i, q)), pl.BlockSpec((TTW, D), lambda q, i: (i, 0))],
        out_specs=[pl.BlockSpec((None, D, FF_S), lambda q, i: (q, 0, 0)),
                   pl.BlockSpec((None, FF_S, D), lambda q, i: (q, 0, 0))],
        out_shape=[jax.ShapeDtypeStruct((NQ, D, FF_S), F32), jax.ShapeDtypeStruct((NQ, FF_S, D), F32)],
        compiler_params=_params(("arbitrary", "arbitrary")),
    )(xn, da, a, dh2)


def bwd_out(dh1, y, wo_g, layer):
    T = dh1.shape[0]

    def body(dh_ref, y_ref, wo_ref, dy_ref, dwo_ref):
        @pl.when(pl.program_id(0) == 0)
        def _():
            dwo_ref[...] = jnp.zeros_like(dwo_ref)

        dhb = dh_ref[...].astype(BF16)
        dy_ref[...] = _nt(dhb, wo_ref[...])
        dwo_ref[...] += _tn(y_ref[...], dhb)

    return pl.pallas_call(
        body, name="bwd_out", grid=(T // TT,),
        in_specs=[pl.BlockSpec((TT, D), lambda i: (i, 0)), pl.BlockSpec((TT, D), lambda i: (i, 0)),
                  _resident((None, D, D), lambda i: (layer, 0, 0))],
        out_specs=[pl.BlockSpec((TT, D), lambda i: (i, 0)), pl.BlockSpec((D, D), lambda i: (0, 0))],
        out_shape=[jax.ShapeDtypeStruct((T, D), F32), jax.ShapeDtypeStruct((D, D), F32)],
        compiler_params=_params(("arbitrary",)),
    )(dh1, y, wo_g)


GA_PS, GA_PB, GA_CB, GA_BR, GA_BI, GA_LAM, GA_CW, GA_GNP, GA_GNR = 0, 1, 2, 3, 4, 5, 6, 10, 11


def bwd_mixer(dy, proj, hs, vp, pw, br, bi):
    T = dy.shape[0]
    nt = T // TT
    per = TT // HALO

    def body(dy_ref, p_ref, ph_ref, hs_ref, hsh_ref, vp_ref, pw_ref, br_ref, bi_ref,
             dp_ref, ga_ref, dpw_ref, dwr_ref, dwi_ref, lam_ref, q_ref, dxc_ref):
        i = pl.program_id(0)
        ti = nt - 1 - i

        @pl.when(i == 0)
        def _():
            ga_ref[...] = jnp.zeros_like(ga_ref)
            dpw_ref[...] = jnp.zeros_like(dpw_ref)
            dwr_ref[...] = jnp.zeros_like(dwr_ref)
            dwi_ref[...] = jnp.zeros_like(dwi_ref)
            lam_ref[...] = jnp.zeros_like(lam_ref)
            q_ref[...] = jnp.zeros_like(q_ref)
            dxc_ref[...] = jnp.zeros_like(dxc_ref)

        first = ti > 0
        ue = jnp.concatenate([jnp.where(first, ph_ref[...], 0.0), p_ref[...]], axis=0)
        vp_v = vp_ref[...]
        m = _mixer_pre(ue, ti * TT, vp_v, pw_ref, br_ref, bi_ref)
        hs_v = hs_ref[...]
        hprev = _down(jnp.concatenate([jnp.where(first, hsh_ref[...], 0.0), hs_v], axis=0), 1)[HALO:]
        dyv = dy_ref[...]

        def acc(rw, v):
            ga_ref[rw:rw + 1, :] += jnp.sum(v, axis=0, keepdims=True)

        gnp = _row(vp_v, VP_GNP)
        ps = _row(vp_v, VP_PS)
        ssq = sum(jnp.sum(yp * yp, axis=-1, keepdims=True) for yp in m["ypool"])
        rp = lax.rsqrt(ssq * (1.0 / DP) + EPS)
        npool = [yp * rp for yp in m["ypool"]]
        dnp = [dyv[:, 128 * g:128 * (g + 1)] * gnp[:, 128 * g:128 * (g + 1)] for g in range(4)]
        mean_dn = sum(jnp.sum(dnp[g] * npool[g], axis=-1, keepdims=True) for g in range(4)) * (1.0 / DP)
        for g in range(4):
            lanes = slice(128 * g, 128 * (g + 1))
            ga_ref[GA_GNP:GA_GNP + 1, lanes] += jnp.sum(dyv[:, lanes] * npool[g], axis=0, keepdims=True)
            dyp = rp * (dnp[g] - npool[g] * mean_dn)
            ga_ref[GA_PS:GA_PS + 1, lanes] += jnp.sum(dyp * m["mapped"][g], axis=0, keepdims=True)
            dmap = dyp * ps[:, lanes]
            ga_ref[GA_PB:GA_PB + 1, lanes] += jnp.sum(dmap, axis=0, keepdims=True)
            dmb = dmap.astype(BF16)
            dpw_ref[g] += _tn(m["pooled"][g].astype(BF16), dmb)
            dpool = _nt(dmb, pw_ref[g])
            qv = dpool * m["inv_cnt"][g]
            s = jnp.concatenate([qv, q_ref[:, lanes]], axis=0)
            for j in range(g + 1):
                s = s + _up(s, 1 << j)
            q_ref[:, lanes] = qv[:HALO]
            dp_ref[:, lanes] = (s[:TT] - dpool).astype(BF16)

        gnr = _row(vp_v, VP_GNR)
        yr = hs_v * m["gl"]
        rr = _rms(yr)
        nr = yr * rr
        dyr_out = dyv[:, DP:]
        acc(GA_GNR, dyr_out * nr)
        dyr = _rms_bwd(nr, rr, dyr_out * gnr)
        gate, gt = m["gate"], m["gt"]
        dgl = 0.5 * (1.0 + gt) + 0.5 * gate * (1.0 - gt * gt) * (GELU_C * (1.0 + 3.0 * GELU_K * gate * gate))
        dp_ref[:, DP + DR:] = (dyr * hs_v * dgl).astype(BF16)
        dhs = dyr * m["gl"]
        a = m["a"]
        row = lax.broadcasted_iota(jnp.int32, (TT, 1), 0)
        c_next = jnp.where(row < TT - 1, _up(a, 1), 1.0)
        cc, ce = _scan_rev(c_next, dhs)
        lam = cc * lam_ref[0:1, :] + ce
        lam_ref[0:1, :] = a[0:1, :] * lam[0:1, :]
        xc, ig, r, mult = m["xc"], m["ig"], m["r"], m["mult"]
        dmult = lam * ig * xc
        dig = lam * mult * xc
        dxc = lam * mult * ig
        dla = lam * hprev * a - dmult * (a * a) / mult
        acc(GA_LAM, dla * r)
        dzr = (dla * (-8.0 * m["sp"])) * (r * (1.0 - r))
        dzi = dig * (ig * (1.0 - ig))
        acc(GA_BR, dzr)
        acc(GA_BI, dzi)
        dzrb, dzib = dzr.astype(BF16), dzi.astype(BF16)
        xcb = m["xcb"]
        halves = []
        for k in range(2):
            lanes = slice(256 * k, 256 * (k + 1))
            dwr_ref[k] += _tn(xcb[:, lanes], dzrb[:, lanes])
            dwi_ref[k] += _tn(xcb[:, lanes], dzib[:, lanes])
            halves.append(_nt(dzrb[:, lanes], br_ref[k]) + _nt(dzib[:, lanes], bi_ref[k]))
        dxc = dxc + jnp.concatenate(halves, axis=1)
        acc(GA_CB, dxc)
        for k in range(4):
            acc(GA_CW + k, dxc * m["taps"][k])
        dxe = jnp.concatenate([dxc, dxc_ref[...]], axis=0)
        du = (_up(dxe, 3)[:TT] * _row(vp_v, VP_CW) + _up(dxe, 2)[:TT] * _row(vp_v, VP_CW + 1)
              + _up(dxe, 1)[:TT] * _row(vp_v, VP_CW + 2) + dxc * _row(vp_v, VP_CW + 3))
        dxc_ref[...] = dxc[:HALO]
        dp_ref[:, DP:DP + DR] = du.astype(BF16)

        @pl.when(i == nt - 1)
        def _():
            lamp = _row(vp_v, VP_LAM)
            ga_ref[GA_LAM:GA_LAM + 1, :] = ga_ref[GA_LAM:GA_LAM + 1, :] * (8.0 * jax.nn.sigmoid(-lamp))

    rev = lambda i: (nt - 1 - i, 0)
    rev_halo = lambda i: (_halo_index(nt - 1 - i, per), 0)
    return pl.pallas_call(
        body, name="bwd_mixer", grid=(nt,),
        in_specs=[pl.BlockSpec((TT, D), rev), pl.BlockSpec((TT, DIN), rev), pl.BlockSpec((HALO, DIN), rev_halo),
                  pl.BlockSpec((TT, DR), rev), pl.BlockSpec((HALO, DR), rev_halo),
                  _resident((16, 512), lambda i: (0, 0)), _resident((4, 128, 128), lambda i: (0, 0, 0)),
                  _resident((2, 256, 256), lambda i: (0, 0, 0)), _resident((2, 256, 256), lambda i: (0, 0, 0))],
        out_specs=[pl.BlockSpec((TT, DIN), rev), pl.BlockSpec((16, 512), lambda i: (0, 0)),
                   pl.BlockSpec((4, 128, 128), lambda i: (0, 0, 0)), pl.BlockSpec((2, 256, 256), lambda i: (0, 0, 0)),
                   pl.BlockSpec((2, 256, 256), lambda i: (0, 0, 0))],
        out_shape=[jax.ShapeDtypeStruct((T, DIN), BF16), jax.ShapeDtypeStruct((16, 512), F32),
                   jax.ShapeDtypeStruct((4, 128, 128), F32), jax.ShapeDtypeStruct((2, 256, 256), F32),
                   jax.ShapeDtypeStruct((2, 256, 256), F32)],
        scratch_shapes=[pltpu.VMEM((8, DR), F32), pltpu.VMEM((HALO, DP), F32), pltpu.VMEM((HALO, DR), F32)],
        compiler_params=_params(("arbitrary",)),
    )(dy, proj, proj, hs, hs, vp, pw, br, bi)


def bwd_in(dproj, h, g1, w_g, dh1, layer):
    T = h.shape[0]

    def body(dp_ref, h_ref, g_ref, w_ref, dh1_ref, dh_ref, dw_ref, dg_ref):
        @pl.when(pl.program_id(0) == 0)
        def _():
            dw_ref[...] = jnp.zeros_like(dw_ref)
            dg_ref[...] = jnp.zeros_like(dg_ref)

        hv = h_ref[...]
        r = _rms(hv)
        n = hv * r
        gv = g_ref[...]
        xn = (n * gv).astype(BF16)
        dxn = jnp.zeros((TT, D), F32)
        for q in range(NQ):
            dpq = dp_ref[:, WIN_S * q:WIN_S * (q + 1)]
            dxn = dxn + _nt(dpq, w_ref[q])
            dw_ref[q] += _tn(xn, dpq)
        dg_ref[...] += jnp.sum(dxn * n, axis=0, keepdims=True)
        dh_ref[...] = dh1_ref[...] + _rms_bwd(n, r, dxn * gv)

    return pl.pallas_call(
        body, name="bwd_in", grid=(T // TT,),
        in_specs=[pl.BlockSpec((TT, DIN), lambda i: (i, 0)), pl.BlockSpec((TT, D), lambda i: (i, 0)),
                  _resident((1, D), lambda i: (0, 0)), _resident((None, NQ, D, WIN_S), lambda i: (layer, 0, 0, 0)),
                  pl.BlockSpec((TT, D), lambda i: (i, 0))],
        out_specs=[pl.BlockSpec((TT, D), lambda i: (i, 0)), pl.BlockSpec((NQ, D, WIN_S), lambda i: (0, 0, 0)),
                   pl.BlockSpec((1, D), lambda i: (0, 0))],
        out_shape=[jax.ShapeDtypeStruct((T, D), F32), jax.ShapeDtypeStruct((NQ, D, WIN_S), F32),
                   jax.ShapeDtypeStruct((1, D), F32)],
        compiler_params=_params(("arbitrary",)),
    )(dproj, h, g1, w_g, dh1)


def _row_block(rows, cols, itemsize=4, budget=2 * 1024 * 1024):
    best = None
    for b in range(16, rows + 1, 16):
        if rows % b == 0 and b * cols * itemsize <= budget:
            best = b
    return best if best is not None else rows


def add_pairs(full, got, dtype):
    core = lax.axis_index("c").astype(jnp.int32).reshape(1)
    outs = []
    for k in range(len(full)):
        q, hr, c = got[k].shape
        rb = _row_block(hr, c)
        nb = hr // rb

        def body(c_ref, a_ref, b_ref, o_ref):
            o_ref[...] = (a_ref[...] + b_ref[...]).astype(dtype)

        outs.append(pl.pallas_call(
            body, name="add_pairs",
            grid_spec=pltpu.PrefetchScalarGridSpec(
                num_scalar_prefetch=1, grid=(q, nb),
                in_specs=[pl.BlockSpec((None, rb, c), lambda qi, i, c_ref, nb=nb: (qi, c_ref[0] * nb + i, 0)),
                          pl.BlockSpec((None, rb, c), lambda qi, i, c_ref: (qi, i, 0))],
                out_specs=pl.BlockSpec((None, rb, c), lambda qi, i, c_ref: (qi, i, 0))),
            out_shape=jax.ShapeDtypeStruct((q, hr, c), dtype),
            compiler_params=_params(("arbitrary", "arbitrary")),
        )(core, full[k], got[k]))
    return outs


def sum_chips(parts):
    core = lax.axis_index("c").astype(jnp.int32).reshape(1)
    outs = []
    for p in parts:
        _, hr, c = p.shape
        rb = _row_block(hr, c)
        nb = hr // rb

        def body(c_ref, p_ref, o_ref):
            s = p_ref[0].astype(F32) + p_ref[1].astype(F32)
            s = s + p_ref[2].astype(F32)
            o_ref[...] = s + p_ref[3].astype(F32)

        outs.append(pl.pallas_call(
            body, name="sum_chips",
            grid_spec=pltpu.PrefetchScalarGridSpec(
                num_scalar_prefetch=1, grid=(nb,),
                in_specs=[pl.BlockSpec((NQ, rb, c), lambda i, c_ref: (0, i, 0))],
                out_specs=pl.BlockSpec((rb, c), lambda i, c_ref, nb=nb: (c_ref[0] * nb + i, 0))),
            out_shape=jax.ShapeDtypeStruct((2 * hr, c), F32),
            compiler_params=_params(("arbitrary",)),
        )(core, p))
    return outs


def adamw(w, g, m, v):
    r, c = w.shape
    rb = _row_block(r, c, budget=1024 * 1024)
    c1 = 1.0 / (1.0 - ADAM_B1 ** ADAM_STEP)
    c2 = 1.0 / (1.0 - ADAM_B2 ** ADAM_STEP)

    def body(w_ref, g_ref, m_ref, v_ref, d_ref, nm_ref, nv_ref):
        gv = g_ref[...]
        nm = ADAM_B1 * m_ref[...] + (1.0 - ADAM_B1) * gv
        nv = ADAM_B2 * v_ref[...] + (1.0 - ADAM_B2) * (gv * gv)
        nm_ref[...] = nm
        nv_ref[...] = nv
        d_ref[...] = -ADAM_LR * ((nm * c1) / (jnp.sqrt(nv * c2) + ADAM_EPS) + ADAM_WD * w_ref[...])

    spec = pl.BlockSpec((rb, c), lambda i: (i, 0))
    return pl.pallas_call(
        body, name="adamw", grid=(r // rb,), in_specs=[spec] * 4, out_specs=[spec] * 3,
        out_shape=[jax.ShapeDtypeStruct((r, c), F32)] * 3,
        compiler_params=_params(("arbitrary",)),
    )(w, g, m, v)


def _place():
    return lax.axis_index("x"), lax.axis_index("y"), lax.axis_index("c")


def _other_chips(x, y):
    return [(1 - x, y), (x, 1 - y), (1 - x, 1 - y)]


LOCAL_CHUNKS = 4


def gather_weights(shards):
    n = len(shards)
    depth = shards[0].shape[0]

    def body(*refs):
        ins, outs = refs[:n], refs[n:2 * n]
        send, recv, fsend, frecv, lsem = refs[2 * n:]
        x, y, c = _place()
        p = 2 * x + y
        sib = (x, y, 1 - c)
        chips = _other_chips(x, y)
        local = []
        for k in range(n):
            rc = shards[k].shape[1] // LOCAL_CHUNKS
            for l in range(depth):
                for j in range(LOCAL_CHUNKS):
                    pltpu.make_async_copy(ins[k].at[l, pl.ds(j * rc, rc), :], outs[k].at[l, p, pl.ds(j * rc, rc), :],
                                          lsem.at[k]).start()
            local.append(pltpu.make_async_copy(ins[k], outs[k].at[:, p], lsem.at[k]))

        def piece(ref, k, l, chip, half):
            hr = shards[k].shape[1] // 2
            return ref.at[l, 2 * chip[0] + chip[1], pl.ds(half * hr, hr), :]

        def mine(k, l, half):
            hr = shards[k].shape[1] // 2
            return ins[k].at[l, pl.ds(half * hr, hr), :]

        sends = []
        for l in range(depth):
            for k in range(n):
                for j, chip in enumerate(chips):
                    cp = pltpu.make_async_remote_copy(mine(k, l, c), piece(outs[k], k, l, (x, y), c), send.at[l, k, j],
                                                      recv.at[l, k, j], device_id=(chip[0], chip[1], c), device_id_type=MESH)
                    cp.start()
                    sends.append(cp)
        passed = []
        for l in range(depth):
            for k in range(n):
                for j, chip in enumerate(chips):
                    got = piece(outs[k], k, l, chip, c)
                    pltpu.make_async_remote_copy(got, got, send.at[l, k, j], recv.at[l, k, j],
                                                 device_id=(chip[0], chip[1], c), device_id_type=MESH).wait_recv()
                    cp = pltpu.make_async_remote_copy(got, got, fsend.at[l, k, j], frecv.at[l, k, j],
                                                      device_id=sib, device_id_type=MESH)
                    cp.start()
                    passed.append(cp)
        for l in range(depth):
            for k in range(n):
                for j, chip in enumerate(chips):
                    theirs = piece(outs[k], k, l, chip, 1 - c)
                    pltpu.make_async_remote_copy(theirs, theirs, fsend.at[l, k, j], frecv.at[l, k, j],
                                                 device_id=sib, device_id_type=MESH).wait_recv()
        for cp in sends + passed:
            cp.wait_send()
        for cp in local:
            cp.wait()

    out_shape = [jax.ShapeDtypeStruct((s.shape[0], NQ) + s.shape[1:], s.dtype) for s in shards]
    sems = pltpu.SemaphoreType.DMA((depth, n, 3))
    return pl.pallas_call(
        body, name="gather_weights", in_specs=[ANY] * n, out_specs=[ANY] * n, out_shape=out_shape,
        scratch_shapes=[sems, sems, sems, sems, pltpu.SemaphoreType.DMA((n,))],
    )(*shards)


def gather_small(shard):
    r, c = shard.shape

    def body(in_ref, out_ref, send, recv, lsem):
        x, y, cc = _place()
        own = pltpu.make_async_copy(in_ref, out_ref.at[2 * x + y], lsem)
        own.start()
        sends = []
        for j, chip in enumerate(_other_chips(x, y)):
            cp = pltpu.make_async_remote_copy(in_ref, out_ref.at[2 * x + y], send.at[j], recv.at[j],
                                              device_id=(chip[0], chip[1], cc), device_id_type=MESH)
            cp.start()
            sends.append(cp)
        for j, chip in enumerate(_other_chips(x, y)):
            slot = out_ref.at[2 * chip[0] + chip[1]]
            pltpu.make_async_remote_copy(slot, slot, send.at[j], recv.at[j],
                                         device_id=(chip[0], chip[1], cc), device_id_type=MESH).wait_recv()
        for cp in sends:
            cp.wait_send()
        own.wait()

    vm = pl.BlockSpec(memory_space=pltpu.VMEM)
    return pl.pallas_call(
        body, name="gather_small", in_specs=[vm], out_specs=vm,
        out_shape=jax.ShapeDtypeStruct((NQ, r, c), shard.dtype),
        scratch_shapes=[pltpu.SemaphoreType.DMA((3,)), pltpu.SemaphoreType.DMA((3,)), pltpu.SemaphoreType.DMA],
    )(shard)


D2D_CHUNKS = 4


def swap_halves(arrs):
    n = len(arrs)

    def body(*refs):
        ins, got = refs[:n], refs[n:2 * n]
        send, recv = refs[2 * n:]
        x, y, c = _place()
        sib = (x, y, 1 - c)
        for k in range(n):
            q, r, _ = arrs[k].shape
            hr = r // 2
            rc = hr // D2D_CHUNKS
            for qi in range(q):
                for j in range(D2D_CHUNKS):
                    pltpu.make_async_remote_copy(ins[k].at[qi, pl.ds((1 - c) * hr + j * rc, rc), :],
                                                 got[k].at[qi, pl.ds(j * rc, rc), :], send.at[k], recv.at[k],
                                                 device_id=sib, device_id_type=MESH).start()
        for k in range(n):
            hr = arrs[k].shape[1] // 2
            whole = pltpu.make_async_remote_copy(ins[k].at[:, pl.ds(0, hr), :], got[k], send.at[k], recv.at[k],
                                                 device_id=sib, device_id_type=MESH)
            whole.wait_send()
            whole.wait_recv()

    half = [jax.ShapeDtypeStruct((a.shape[0], a.shape[1] // 2, a.shape[2]), a.dtype) for a in arrs]
    return pl.pallas_call(
        body, name="swap_halves", in_specs=[ANY] * n, out_specs=[ANY] * n, out_shape=half,
        scratch_shapes=[pltpu.SemaphoreType.DMA((n,)), pltpu.SemaphoreType.DMA((n,))],
    )(*arrs)


def exchange_chips(arrs, scatter):
    n = len(arrs)

    def body(*refs):
        ins, outs = refs[:n], refs[n:2 * n]
        send, recv, lsem = refs[2 * n:]
        x, y, c = _place()
        p = 2 * x + y
        chips = _other_chips(x, y)
        cps = []
        for k in range(n):
            lc = pltpu.make_async_copy(ins[k].at[p if scatter[k] else 0], outs[k].at[p], lsem.at[k])
            lc.start()
            cps.append(lc)
        sends = []
        for k in range(n):
            for j, chip in enumerate(chips):
                src = ins[k].at[2 * chip[0] + chip[1] if scatter[k] else 0]
                cp = pltpu.make_async_remote_copy(src, outs[k].at[p], send.at[k, j], recv.at[k, j],
                                                  device_id=(chip[0], chip[1], c), device_id_type=MESH)
                cp.start()
                sends.append(cp)
        for k in range(n):
            for j, chip in enumerate(chips):
                slot = outs[k].at[2 * chip[0] + chip[1]]
                pltpu.make_async_remote_copy(slot, slot, send.at[k, j], recv.at[k, j],
                                             device_id=(chip[0], chip[1], c), device_id_type=MESH).wait_recv()
        for cp in sends:
            cp.wait_send()
        for lc in cps:
            lc.wait()

    out_shape = [jax.ShapeDtypeStruct((NQ,) + a.shape[1:], a.dtype) for a in arrs]
    return pl.pallas_call(
        body, name="exchange_chips", in_specs=[ANY] * n, out_specs=[ANY] * n, out_shape=out_shape,
        scratch_shapes=[pltpu.SemaphoreType.DMA((n, 3)), pltpu.SemaphoreType.DMA((n, 3)), pltpu.SemaphoreType.DMA((n,))],
    )(*arrs)


JOIN_CHUNKS = 8


def join_halves(arrs):
    n = len(arrs)

    def body(*refs):
        outs = refs[n:2 * n]
        send, recv = refs[2 * n:]
        x, y, c = _place()
        sib = (x, y, 1 - c)
        for k in range(n):
            hr = arrs[k].shape[0] // 2
            rc = hr // JOIN_CHUNKS
            for j in range(JOIN_CHUNKS):
                rows = outs[k].at[pl.ds(c * hr + j * rc, rc), :]
                pltpu.make_async_remote_copy(rows, rows, send.at[k], recv.at[k], device_id=sib, device_id_type=MESH).start()
        for k in range(n):
            hr = arrs[k].shape[0] // 2
            mine = outs[k].at[pl.ds(c * hr, hr), :]
            theirs = outs[k].at[pl.ds((1 - c) * hr, hr), :]
            whole = pltpu.make_async_remote_copy(mine, theirs, send.at[k], recv.at[k], device_id=sib, device_id_type=MESH)
            whole.wait_send()
            whole.wait_recv()

    out_shape = [jax.ShapeDtypeStruct(a.shape, a.dtype) for a in arrs]
    return pl.pallas_call(
        body, name="join_halves", in_specs=[ANY] * n, out_specs=[ANY] * n, out_shape=out_shape,
        input_output_aliases={k: k for k in range(n)},
        scratch_shapes=[pltpu.SemaphoreType.DMA((n,)), pltpu.SemaphoreType.DMA((n,))],
    )(*arrs)


def reduce_over_devices(arrs, scatter, dtype):
    pair = add_pairs(arrs, swap_halves(arrs), dtype)
    parts = exchange_chips(pair, scatter)
    return join_halves(sum_chips(parts))


def _block_diag(w):
    rows = [jnp.pad(w[hd], ((0, 0), (64 * (hd % 4), 192 - 64 * (hd % 4)))) for hd in range(8)]
    return jnp.stack([jnp.concatenate(rows[:4], axis=0), jnp.concatenate(rows[4:], axis=0)]).astype(BF16)


def _diag_blocks(b):
    return jnp.stack([b[hd // 4, 64 * (hd % 4):64 * (hd % 4) + 64, 64 * (hd % 4):64 * (hd % 4) + 64] for hd in range(8)])


def _vec_params(l, pool_b, pool_scale, conv_b, gate_r_b, gate_i_b, lru_lambda, conv_w_full, group_norm_g):
    rows = [pool_b[l], pool_scale[l], conv_b[l], gate_r_b[l], gate_i_b[l], lru_lambda[l],
            conv_w_full[l, 0], conv_w_full[l, 1], conv_w_full[l, 2], conv_w_full[l, 3],
            group_norm_g[l, :DP], group_norm_g[l, DP:]]
    return jnp.concatenate([jnp.stack(rows), jnp.zeros((4, 512), F32)], axis=0)


SMALL_ROWS_LAYER = 16 + 128 + 64 + 64 + 2 + 2
SMALL_ROWS = 1152


def _pack_small(layers, final_g, meta):
    rows = []
    for vec, pw, wr, wi, g1, g2 in layers:
        rows += [vec, pw.reshape(128, 512), wr.reshape(64, 512), wi.reshape(64, 512), g1.reshape(2, 512), g2.reshape(2, 512)]
    rows += [final_g.reshape(2, 512), meta.reshape(32, 512)]
    flat = jnp.concatenate(rows, axis=0)
    return jnp.concatenate([flat, jnp.zeros((SMALL_ROWS - flat.shape[0], 512), F32)], axis=0)


def _unpack_small(flat):
    layers, o = [], 0
    for _ in range(DEPTH):
        vec = flat[o:o + 16]; o += 16
        pw = flat[o:o + 128].reshape(4, 128, 128); o += 128
        wr = flat[o:o + 64].reshape(8, 64, 64); o += 64
        wi = flat[o:o + 64].reshape(8, 64, 64); o += 64
        g1 = flat[o:o + 2].reshape(1024); o += 2
        g2 = flat[o:o + 2].reshape(1024); o += 2
        layers.append((vec, pw, wr, wi, g1, g2))
    final_g = flat[o:o + 2].reshape(1024); o += 2
    meta = flat[o:o + 32].reshape(16, 1024)
    return layers, final_g, meta


def local_step(x2d, tgt2d, meta_full, conv_w_full, win_g, wo_g, wu_g, wd_g, sp):
    h = jnp.concatenate([meta_full, x2d], axis=0)
    tgt = jnp.concatenate([jnp.zeros((NMETA, D), F32), tgt2d], axis=0)
    saved = []
    for l in range(DEPTH):
        vp = _vec_params(l, sp["pool_b"], sp["pool_scale"], sp["conv_b"], sp["gate_r_b"], sp["gate_i_b"], sp["lru_lambda"],
                         conv_w_full, sp["group_norm_g"])
        pw = sp["pool_w"][l].astype(BF16)
        br, bi = _block_diag(sp["gate_r_w"][l]), _block_diag(sp["gate_i_w"][l])
        proj = fwd_in(h, sp["mix_norm_g"][l][None], win_g, l)
        y, hs = fwd_mixer(proj, vp, pw, br, bi)
        h1, a, h2 = fwd_post(h, y, wo_g, sp["mlp_norm_g"][l][None], wu_g, wd_g, l)
        saved.append((h, proj, y, hs, h1, a, vp, pw, br, bi))
        h = h2
    dh, dgf, loss_part = fwd_loss(h, sp["final_norm_g"][None], tgt)

    big = [None] * DEPTH
    small_layers = [None] * DEPTH
    for l in reversed(range(DEPTH)):
        h0, proj, y, hs, h1, a, vp, pw, br, bi = saved[l]
        dh1, da, xn2, dg2 = bwd_mlp_dx(dh, h1, a, sp["mlp_norm_g"][l][None], wu_g, wd_g, l)
        dwu, dwd = bwd_mlp_dw(xn2, da, a, dh)
        dy, dwo = bwd_out(dh1, y, wo_g, l)
        dproj, ga, dpw, dwr, dwi = bwd_mixer(dy, proj, hs, vp, pw, br, bi)
        dh, dwin, dg1 = bwd_in(dproj, h0, sp["mix_norm_g"][l][None], win_g, dh1, l)
        big[l] = [dwin, dwo.reshape(NQ, D // NQ, D), dwu, dwd]
        small_layers[l] = (ga, dpw, _diag_blocks(dwr), _diag_blocks(dwi), dg1[0], dg2[0])
    return loss_part, dh, big, small_layers, dgf


def kernel(x, meta_tokens, mix_norm_g, w_in, pool_w, pool_b, pool_scale, conv_w, conv_b, gate_r_w, gate_r_b, gate_i_w, gate_i_b, lru_lambda, group_norm_g, w_out, mlp_norm_g, w_up, w_down, final_norm_g, loss_target, m_meta_tokens, m_mix_norm_g, m_w_in, m_pool_w, m_pool_b, m_pool_scale, m_conv_w, m_conv_b, m_gate_r_w, m_gate_r_b, m_gate_i_w, m_gate_i_b, m_lru_lambda, m_group_norm_g, m_w_out, m_mlp_norm_g, m_w_up, m_w_down, m_final_norm_g, v_meta_tokens, v_mix_norm_g, v_w_in, v_pool_w, v_pool_b, v_pool_scale, v_conv_w, v_conv_b, v_gate_r_w, v_gate_r_b, v_gate_i_w, v_gate_i_b, v_lru_lambda, v_group_norm_g, v_w_out, v_mlp_norm_g, v_w_up, v_w_down, v_final_norm_g):
    p = 2 * lax.axis_index("x") + lax.axis_index("y")

    win_g, wo_g, wu_g, wd_g = gather_weights([w_in.astype(BF16), w_out.astype(BF16), w_up.astype(BF16), w_down.astype(BF16)])
    wo_g = wo_g.reshape(DEPTH, D, D)
    small = jnp.concatenate([meta_tokens, jnp.pad(conv_w.reshape(16, 128), ((0, 0), (0, 128)))], axis=0)
    small_g = gather_small(small)
    meta_full = jnp.transpose(small_g[:, :16, :], (1, 0, 2)).reshape(NMETA, D)
    conv_w_full = jnp.transpose(small_g[:, 16:, :128].reshape(NQ, DEPTH, 4, 128), (1, 2, 0, 3)).reshape(DEPTH, 4, DR)

    sp = dict(mix_norm_g=mix_norm_g, pool_w=pool_w, pool_b=pool_b, pool_scale=pool_scale, conv_b=conv_b, gate_r_w=gate_r_w,
              gate_r_b=gate_r_b, gate_i_w=gate_i_w, gate_i_b=gate_i_b, lru_lambda=lru_lambda, group_norm_g=group_norm_g,
              mlp_norm_g=mlp_norm_g, final_norm_g=final_norm_g)
    loss_part, dh, big_parts, small_layers, dgf = local_step(x[0], loss_target[0], meta_full, conv_w_full,
                                                             win_g, wo_g, wu_g, wd_g, sp)
    loss = lax.psum(loss_part[0, 0], ("x", "y", "c"))
    grad_x = dh[NMETA:][None]

    big = [reduce_over_devices(big_parts[l], [True] * 4, BF16) for l in reversed(range(DEPTH))][::-1]
    small_sum = reduce_over_devices([_pack_small(small_layers, dgf[0], dh[:NMETA])[None]], [False], F32)[0]
    g_layers, g_final, g_meta_full = _unpack_small(small_sum)

    g_vec = [gl[0] for gl in g_layers]
    grads = dict(
        meta_tokens=lax.dynamic_slice(g_meta_full, (0, p * (D // NQ)), (NMETA, D // NQ)),
        mix_norm_g=jnp.stack([gl[4] for gl in g_layers]),
        w_in=jnp.stack([big[l][0] for l in range(DEPTH)]),
        pool_w=jnp.stack([gl[1] for gl in g_layers]),
        pool_b=jnp.stack([gv[GA_PB] for gv in g_vec]),
        pool_scale=jnp.stack([gv[GA_PS] for gv in g_vec]),
        conv_w=lax.dynamic_slice(jnp.stack([gv[GA_CW:GA_CW + 4] for gv in g_vec]), (0, 0, p * 128), (DEPTH, 4, 128)),
        conv_b=jnp.stack([gv[GA_CB] for gv in g_vec]),
        gate_r_w=jnp.stack([gl[2] for gl in g_layers]),
        gate_r_b=jnp.stack([gv[GA_BR] for gv in g_vec]),
        gate_i_w=jnp.stack([gl[3] for gl in g_layers]),
        gate_i_b=jnp.stack([gv[GA_BI] for gv in g_vec]),
        lru_lambda=jnp.stack([gv[GA_LAM] for gv in g_vec]),
        group_norm_g=jnp.stack([jnp.concatenate([gv[GA_GNP], gv[GA_GNR]]) for gv in g_vec]),
        w_out=jnp.stack([big[l][1] for l in range(DEPTH)]),
        mlp_norm_g=jnp.stack([gl[5] for gl in g_layers]),
        w_up=jnp.stack([big[l][2] for l in range(DEPTH)]),
        w_down=jnp.stack([big[l][3] for l in range(DEPTH)]),
        final_norm_g=g_final,
    )
    weights = dict(meta_tokens=meta_tokens, mix_norm_g=mix_norm_g, w_in=w_in, pool_w=pool_w, pool_b=pool_b, pool_scale=pool_scale,
                   conv_w=conv_w, conv_b=conv_b, gate_r_w=gate_r_w, gate_r_b=gate_r_b, gate_i_w=gate_i_w, gate_i_b=gate_i_b,
                   lru_lambda=lru_lambda, group_norm_g=group_norm_g, w_out=w_out, mlp_norm_g=mlp_norm_g, w_up=w_up, w_down=w_down,
                   final_norm_g=final_norm_g)
    mom_m = dict(meta_tokens=m_meta_tokens, mix_norm_g=m_mix_norm_g, w_in=m_w_in, pool_w=m_pool_w, pool_b=m_pool_b,
                 pool_scale=m_pool_scale, conv_w=m_conv_w, conv_b=m_conv_b, gate_r_w=m_gate_r_w, gate_r_b=m_gate_r_b,
                 gate_i_w=m_gate_i_w, gate_i_b=m_gate_i_b, lru_lambda=m_lru_lambda, group_norm_g=m_group_norm_g, w_out=m_w_out,
                 mlp_norm_g=m_mlp_norm_g, w_up=m_w_up, w_down=m_w_down, final_norm_g=m_final_norm_g)
    mom_v = dict(meta_tokens=v_meta_tokens, mix_norm_g=v_mix_norm_g, w_in=v_w_in, pool_w=v_pool_w, pool_b=v_pool_b,
                 pool_scale=v_pool_scale, conv_w=v_conv_w, conv_b=v_conv_b, gate_r_w=v_gate_r_w, gate_r_b=v_gate_r_b,
                 gate_i_w=v_gate_i_w, gate_i_b=v_gate_i_b, lru_lambda=v_lru_lambda, group_norm_g=v_group_norm_g, w_out=v_w_out,
                 mlp_norm_g=v_mlp_norm_g, w_up=v_w_up, w_down=v_w_down, final_norm_g=v_final_norm_g)
    names = list(weights)

    delta, new_m, new_v = {}, {}, {}
    big_names = ("w_in", "w_out", "w_up", "w_down")
    for nm in big_names:
        shp = weights[nm].shape
        two_d = lambda t: t.reshape(shp[0] * shp[1], shp[2])
        d_, m_, v_ = adamw(two_d(weights[nm]), two_d(grads[nm]), two_d(mom_m[nm]), two_d(mom_v[nm]))
        delta[nm], new_m[nm], new_v[nm] = d_.reshape(shp), m_.reshape(shp), v_.reshape(shp)
    small_names = [nm for nm in names if nm not in big_names]
    sizes = [weights[nm].size for nm in small_names]
    total = sum(sizes)
    rows = -(-total // 512)
    rows = -(-rows // 16) * 16

    def flat(tree, fill):
        v_ = jnp.concatenate([tree[nm].reshape(-1) for nm in small_names])
        return jnp.concatenate([v_, jnp.full((rows * 512 - total,), fill, F32)]).reshape(rows, 512)

    d_, m_, v_ = adamw(flat(weights, 0.0), flat(grads, 0.0), flat(mom_m, 0.0), flat(mom_v, 1.0))
    o = 0
    for nm, sz in zip(small_names, sizes):
        shp = weights[nm].shape
        delta[nm] = d_.reshape(-1)[o:o + sz].reshape(shp)
        new_m[nm] = m_.reshape(-1)[o:o + sz].reshape(shp)
        new_v[nm] = v_.reshape(-1)[o:o + sz].reshape(shp)
        o += sz

    return (loss, grad_x, *[grads[nm] for nm in names], *[delta[nm] for nm in names],
            *[new_m[nm] for nm in names], *[new_v[nm] for nm in names])
```

```python
import jax
import jax.numpy as jnp
from jax import lax
from jax.experimental import pallas as pl
from jax.experimental.pallas import tpu as pltpu

F32 = jnp.float32
BF16 = jnp.bfloat16

D = 1024
DP = 512
DR = 512
DIN = 1536
DFF = 4096
DEPTH = 4
NMETA = 16
NQ = 4
WIN_S = DIN // NQ
FF_S = DFF // NQ
EPS = 1e-6
HALO = 16
TT = 432
TTW = 912
VMEM_LIMIT = 56 * 1024 * 1024

ADAM_LR = 0.001
ADAM_B1 = 0.9
ADAM_B2 = 0.999
ADAM_EPS = 1e-08
ADAM_WD = 0.01
ADAM_STEP = 10

MESH = pl.DeviceIdType.MESH
ANY = pl.BlockSpec(memory_space=pl.ANY)


def _params(sem=None, vmem=VMEM_LIMIT):
    return pltpu.CompilerParams(dimension_semantics=sem, vmem_limit_bytes=vmem)


def _resident(shape, index):
    return pl.BlockSpec(shape, index, pipeline_mode=pl.Buffered(1))


def _nt(x, w):
    return lax.dot_general(x, w, (((1,), (1,)), ((), ())), preferred_element_type=F32)


def _tn(a, b):
    return lax.dot_general(a, b, (((0,), (0,)), ((), ())), preferred_element_type=F32)


def _dot(x, w):
    return jnp.dot(x, w, preferred_element_type=F32)


def _rms(h):
    return lax.rsqrt(jnp.mean(h * h, axis=-1, keepdims=True) + EPS)


def _rms_bwd(n, r, dn):
    return r * (dn - n * jnp.mean(dn * n, axis=-1, keepdims=True))


def _down(x, s):
    return pltpu.roll(x, s, 0)


def _up(x, s):
    return pltpu.roll(x, x.shape[0] - s, 0)


GELU_C = 0.7978845608028654
GELU_K = 0.044715


def _gelu(x):
    t = jnp.tanh(GELU_C * (x + GELU_K * x * x * x))
    return 0.5 * x * (1.0 + t), t


def _softplus(x):
    return jnp.maximum(x, 0.0) + jnp.log1p(jnp.exp(-jnp.abs(x)))


VP_PB, VP_PS, VP_CB, VP_BR, VP_BI, VP_LAM, VP_CW, VP_GNP, VP_GNR = 0, 1, 2, 3, 4, 5, 6, 10, 11


def _row(vp, r):
    return vp[r:r + 1, :]


def _mixer_pre(ue, t0, vp, pw_ref, br_ref, bi_ref):
    tt = ue.shape[0] - HALO
    tf = (t0 + lax.broadcasted_iota(jnp.int32, (tt, 1), 0)).astype(F32) + 1.0
    pb, ps = _row(vp, VP_PB), _row(vp, VP_PS)
    pooled, mapped, inv_cnt = [], [], []
    for g in range(4):
        lanes = slice(128 * g, 128 * (g + 1))
        xe = ue[:, lanes]
        s = xe
        for j in range(g + 1):
            s = s + _down(s, 1 << j)
        inv = 1.0 / jnp.minimum(tf, float(2 << g))
        pg = s[HALO:] * inv - xe[HALO:]
        mg = _dot(pg.astype(BF16), pw_ref[g]) + pb[:, lanes]
        pooled.append(pg)
        mapped.append(mg)
        inv_cnt.append(inv)
    ypool = [mapped[g] * ps[:, 128 * g:128 * (g + 1)] for g in range(4)]

    xe = ue[:, DP:DP + DR]
    taps = [_down(xe, 3)[HALO:], _down(xe, 2)[HALO:], _down(xe, 1)[HALO:], xe[HALO:]]
    xc = _row(vp, VP_CB) + (taps[0] * _row(vp, VP_CW) + taps[1] * _row(vp, VP_CW + 1)
                            + taps[2] * _row(vp, VP_CW + 2) + taps[3] * _row(vp, VP_CW + 3))
    xcb = xc.astype(BF16)
    zr = jnp.concatenate([_dot(xcb[:, :256], br_ref[0]), _dot(xcb[:, 256:], br_ref[1])], axis=1) + _row(vp, VP_BR)
    zi = jnp.concatenate([_dot(xcb[:, :256], bi_ref[0]), _dot(xcb[:, 256:], bi_ref[1])], axis=1) + _row(vp, VP_BI)
    r = jax.nn.sigmoid(zr)
    ig = jax.nn.sigmoid(zi)
    sp = _softplus(-_row(vp, VP_LAM))
    la = (-8.0 * r) * sp
    a = jnp.exp(la)
    th = jnp.tanh(la)
    mult = jnp.sqrt((-2.0 * th) / (1.0 - th))
    gate = ue[HALO:, DP + DR:]
    gl, gt = _gelu(gate)
    return dict(pooled=pooled, mapped=mapped, ypool=ypool, inv_cnt=inv_cnt, taps=taps, xc=xc, xcb=xcb, r=r, ig=ig,
                sp=sp, a=a, mult=mult, gate=gate, gl=gl, gt=gt)


def _scan_fwd(a, b):
    tt = a.shape[0]
    row = lax.broadcasted_iota(jnp.int32, (tt, 1), 0)
    s = 1
    while s < tt:
        m = row >= s
        a_s = jnp.where(m, _down(a, s), 1.0)
        b_s = jnp.where(m, _down(b, s), 0.0)
        b = a * b_s + b
        a = a * a_s
        s *= 2
    return a, b


def _scan_rev(c, d):
    tt = c.shape[0]
    row = lax.broadcasted_iota(jnp.int32, (tt, 1), 0)
    s = 1
    while s < tt:
        m = row < tt - s
        c_s = jnp.where(m, _up(c, s), 1.0)
        d_s = jnp.where(m, _up(d, s), 0.0)
        d = c * d_s + d
        c = c * c_s
        s *= 2
    return c, d


def _halo_index(i, per_tile):
    return jnp.maximum(i * per_tile - 1, 0)


def fwd_in(h, g, w_g, job=None):
    T = h.shape[0]

    def body(h_ref, g_ref, w_ref, o_ref):
        hh = h_ref[...]
        xn = (hh * _rms(hh) * g_ref[...]).astype(BF16)
        for q in range(NQ):
            o_ref[:, WIN_S * q:WIN_S * (q + 1)] = _dot(xn, w_ref[q])

    return tiled_call(
        "fwd_in", T // TT, body, (h, g, w_g),
        in_specs=[pl.BlockSpec((TT, D), lambda i: (i, 0)), _resident((1, D), lambda i: (0, 0)),
                  _resident((NQ, D, WIN_S), lambda i: (0, 0, 0))],
        out_specs=[pl.BlockSpec((TT, DIN), lambda i: (i, 0))],
        out_shape=[jax.ShapeDtypeStruct((T, DIN), F32)], job=job, mid_step=T // TT - 1)


def fwd_mixer(proj, vp, pw, br, bi, job=None):
    T = proj.shape[0]
    per = TT // HALO

    def body(p_ref, ph_ref, vp_ref, pw_ref, br_ref, bi_ref, y_ref, hs_ref, carry_ref):
        i = pl.program_id(0)

        @pl.when(i == 0)
        def _():
            carry_ref[...] = jnp.zeros_like(carry_ref)

        halo = jnp.where(i > 0, ph_ref[...], 0.0)
        ue = jnp.concatenate([halo, p_ref[...]], axis=0)
        vp_v = vp_ref[...]
        m = _mixer_pre(ue, i * TT, vp_v, pw_ref, br_ref, bi_ref)
        b = m["mult"] * (m["ig"] * m["xc"])
        ca, cb = _scan_fwd(m["a"], b)
        hs = ca * carry_ref[0:1, :] + cb
        hs_ref[...] = hs
        carry_ref[0:1, :] = hs_ref[TT - 1:TT, :]
        yr = hs * m["gl"]
        ssq = sum(jnp.sum(yp * yp, axis=-1, keepdims=True) for yp in m["ypool"])
        rp = lax.rsqrt(ssq * (1.0 / DP) + EPS)
        gnp = _row(vp_v, VP_GNP)
        for g in range(4):
            lanes = slice(128 * g, 128 * (g + 1))
            y_ref[:, lanes] = (m["ypool"][g] * rp * gnp[:, lanes]).astype(BF16)
        y_ref[:, DP:] = (yr * _rms(yr) * _row(vp_v, VP_GNR)).astype(BF16)

    return tiled_call(
        "fwd_mixer", T // TT, body, (proj, proj, vp, pw, br, bi),
        in_specs=[pl.BlockSpec((TT, DIN), lambda i: (i, 0)),
                  pl.BlockSpec((HALO, DIN), lambda i: (_halo_index(i, per), 0)),
                  _resident((16, 512), lambda i: (0, 0)), _resident((4, 128, 128), lambda i: (0, 0, 0)),
                  _resident((2, 256, 256), lambda i: (0, 0, 0)), _resident((2, 256, 256), lambda i: (0, 0, 0))],
        out_specs=[pl.BlockSpec((TT, D), lambda i: (i, 0)), pl.BlockSpec((TT, DR), lambda i: (i, 0))],
        out_shape=[jax.ShapeDtypeStruct((T, D), BF16), jax.ShapeDtypeStruct((T, DR), F32)],
        scratch=[pltpu.VMEM((8, DR), F32)], job=job, mid_step=(T // TT) * 3 // 4)


def fwd_post(h, y, wo_g, g2, wu_g, wd_g, job=None):
    T = h.shape[0]

    def body(h_ref, y_ref, wo_ref, g_ref, wu_ref, wd_ref, h1_ref, a_ref, h2_ref):
        h1 = h_ref[...] + _dot(y_ref[...], wo_ref[...])
        h1_ref[...] = h1
        xn = (h1 * _rms(h1) * g_ref[...]).astype(BF16)
        acc = h1
        for q in range(NQ):
            a = _dot(xn, wu_ref[q])
            a_ref[:, FF_S * q:FF_S * (q + 1)] = a.astype(BF16)
            ra = jnp.maximum(a, 0.0)
            acc = acc + _dot((ra * ra).astype(BF16), wd_ref[q])
        h2_ref[...] = acc

    return tiled_call(
        "fwd_post", T // TT, body, (h, y, wo_g, g2, wu_g, wd_g),
        in_specs=[pl.BlockSpec((TT, D), lambda i: (i, 0)), pl.BlockSpec((TT, D), lambda i: (i, 0)),
                  _resident((D, D), lambda i: (0, 0)), _resident((1, D), lambda i: (0, 0)),
                  _resident((NQ, D, FF_S), lambda i: (0, 0, 0)), _resident((NQ, FF_S, D), lambda i: (0, 0, 0))],
        out_specs=[pl.BlockSpec((TT, D), lambda i: (i, 0)), pl.BlockSpec((TT, DFF), lambda i: (i, 0)),
                   pl.BlockSpec((TT, D), lambda i: (i, 0))],
        out_shape=[jax.ShapeDtypeStruct((T, D), F32), jax.ShapeDtypeStruct((T, DFF), BF16),
                   jax.ShapeDtypeStruct((T, D), F32)],
        job=job, mid_step=(T // TT) * 3 // 4)


def fwd_loss(h, gf, tgt):
    T = h.shape[0]

    def body(h_ref, g_ref, t_ref, dh_ref, dg_ref, loss_ref):
        i = pl.program_id(0)

        @pl.when(i == 0)
        def _():
            dg_ref[...] = jnp.zeros_like(dg_ref)
            loss_ref[...] = jnp.zeros_like(loss_ref)

        hh = h_ref[...]
        r = _rms(hh)
        n = hh * r
        gfv = g_ref[...]
        row = i * TT + lax.broadcasted_iota(jnp.int32, (TT, 1), 0)
        e = jnp.where(row >= NMETA, n * gfv - t_ref[...], 0.0)
        loss_ref[...] += 0.5 * jnp.sum(jnp.sum(e * e, axis=-1, keepdims=True) * (1.0 / D), axis=0, keepdims=True)
        dy = e * (1.0 / D)
        dg_ref[...] += jnp.sum(dy * n, axis=0, keepdims=True)
        dh_ref[...] = _rms_bwd(n, r, dy * gfv)

    return tiled_call(
        "fwd_loss", T // TT, body, (h, gf, tgt),
        in_specs=[pl.BlockSpec((TT, D), lambda i: (i, 0)), _resident((1, D), lambda i: (0, 0)),
                  pl.BlockSpec((TT, D), lambda i: (i, 0))],
        out_specs=[pl.BlockSpec((TT, D), lambda i: (i, 0)), pl.BlockSpec((1, D), lambda i: (0, 0)),
                   pl.BlockSpec((1, 1), lambda i: (0, 0))],
        out_shape=[jax.ShapeDtypeStruct((T, D), F32), jax.ShapeDtypeStruct((1, D), F32),
                   jax.ShapeDtypeStruct((1, 1), F32)])[0]


def bwd_mlp_dx(dh2, h1, a, g2, wu_g, wd_g, job=None):
    T = dh2.shape[0]

    def body(dh2_ref, h1_ref, a_ref, g_ref, wu_ref, wd_ref, dh1_ref, da_ref, xn_ref, dg_ref):
        @pl.when(pl.program_id(0) == 0)
        def _():
            dg_ref[...] = jnp.zeros_like(dg_ref)

        h1v = h1_ref[...]
        r = _rms(h1v)
        n = h1v * r
        gv = g_ref[...]
        xn_ref[...] = (n * gv).astype(BF16)
        dh2v = dh2_ref[...]
        dh2b = dh2v.astype(BF16)
        dxn = jnp.zeros((TT, D), F32)
        for q in range(NQ):
            cols = slice(FF_S * q, FF_S * (q + 1))
            ra = jnp.maximum(a_ref[:, cols].astype(F32), 0.0)
            da = (_nt(dh2b, wd_ref[q]) * (2.0 * ra)).astype(BF16)
            da_ref[:, cols] = da
            dxn = dxn + _nt(da, wu_ref[q])
        dg_ref[...] += jnp.sum(dxn * n, axis=0, keepdims=True)
        dh1_ref[...] = dh2v + _rms_bwd(n, r, dxn * gv)

    return tiled_call(
        "bwd_mlp_dx", T // TT, body, (dh2, h1, a, g2, wu_g, wd_g),
        in_specs=[pl.BlockSpec((TT, D), lambda i: (i, 0)), pl.BlockSpec((TT, D), lambda i: (i, 0)),
                  pl.BlockSpec((TT, DFF), lambda i: (i, 0)), _resident((1, D), lambda i: (0, 0)),
                  _resident((NQ, D, FF_S), lambda i: (0, 0, 0)), _resident((NQ, FF_S, D), lambda i: (0, 0, 0))],
        out_specs=[pl.BlockSpec((TT, D), lambda i: (i, 0)), pl.BlockSpec((TT, DFF), lambda i: (i, 0)),
                   pl.BlockSpec((TT, D), lambda i: (i, 0)), pl.BlockSpec((1, D), lambda i: (0, 0))],
        out_shape=[jax.ShapeDtypeStruct((T, D), F32), jax.ShapeDtypeStruct((T, DFF), BF16),
                   jax.ShapeDtypeStruct((T, D), BF16), jax.ShapeDtypeStruct((1, D), F32)],
        job=job)


def bwd_mlp_dw(xn, da, a, dh2):
    T = xn.shape[0]

    def body(xn_ref, da_ref, a_ref, dh2_ref, dwu_ref, dwd_ref):
        @pl.when(pl.program_id(1) == 0)
        def _():
            dwu_ref[...] = jnp.zeros_like(dwu_ref)
            dwd_ref[...] = jnp.zeros_like(dwd_ref)

        dwu_ref[...] += _tn(xn_ref[...], da_ref[...])
        ra = jnp.maximum(a_ref[...].astype(F32), 0.0)
        dwd_ref[...] += _tn((ra * ra).astype(BF16), dh2_ref[...].astype(BF16))

    return pl.pallas_call(
        body, name="bwd_mlp_dw", grid=(NQ, T // TTW),
        in_specs=[pl.BlockSpec((TTW, D), lambda q, i: (i, 0)), pl.BlockSpec((TTW, FF_S), lambda q, i: (i, q)),
                  pl.BlockSpec((TTW, FF_S), lambda q, i: (i, q)), pl.BlockSpec((TTW, D), lambda q, i: (i, 0))],
        out_specs=[pl.BlockSpec((None, D, FF_S), lambda q, i: (q, 0, 0)),
                   pl.BlockSpec((None, FF_S, D), lambda q, i: (q, 0, 0))],
        out_shape=[jax.ShapeDtypeStruct((NQ, D, FF_S), F32), jax.ShapeDtypeStruct((NQ, FF_S, D), F32)],
        compiler_params=_params(("arbitrary", "arbitrary")),
    )(xn, da, a, dh2)


def bwd_out(dh1, y, wo_g, job=None):
    T = dh1.shape[0]

    def body(dh_ref, y_ref, wo_ref, dy_ref, dwo_ref):
        @pl.when(pl.program_id(0) == 0)
        def _():
            dwo_ref[...] = jnp.zeros_like(dwo_ref)

        dhb = dh_ref[...].astype(BF16)
        dy_ref[...] = _nt(dhb, wo_ref[...])
        dwo_ref[...] += _tn(y_ref[...], dhb)

    return tiled_call(
        "bwd_out", T // TT, body, (dh1, y, wo_g),
        in_specs=[pl.BlockSpec((TT, D), lambda i: (i, 0)), pl.BlockSpec((TT, D), lambda i: (i, 0)),
                  _resident((D, D), lambda i: (0, 0))],
        out_specs=[pl.BlockSpec((TT, D), lambda i: (i, 0)), pl.BlockSpec((D, D), lambda i: (0, 0))],
        out_shape=[jax.ShapeDtypeStruct((T, D), F32), jax.ShapeDtypeStruct((D, D), F32)], job=job)


GA_PS, GA_PB, GA_CB, GA_BR, GA_BI, GA_LAM, GA_CW, GA_GNP, GA_GNR = 0, 1, 2, 3, 4, 5, 6, 10, 11


def bwd_mixer(dy, proj, hs, vp, pw, br, bi, job=None):
    T = dy.shape[0]
    nt = T // TT
    per = TT // HALO

    def body(dy_ref, p_ref, ph_ref, hs_ref, hsh_ref, vp_ref, pw_ref, br_ref, bi_ref,
             dp_ref, ga_ref, dpw_ref, dwr_ref, dwi_ref, lam_ref, q_ref, dxc_ref):
        i = pl.program_id(0)
        ti = nt - 1 - i

        @pl.when(i == 0)
        def _():
            ga_ref[...] = jnp.zeros_like(ga_ref)
            dpw_ref[...] = jnp.zeros_like(dpw_ref)
            dwr_ref[...] = jnp.zeros_like(dwr_ref)
            dwi_ref[...] = jnp.zeros_like(dwi_ref)
            lam_ref[...] = jnp.zeros_like(lam_ref)
            q_ref[...] = jnp.zeros_like(q_ref)
            dxc_ref[...] = jnp.zeros_like(dxc_ref)

        first = ti > 0
        ue = jnp.concatenate([jnp.where(first, ph_ref[...], 0.0), p_ref[...]], axis=0)
        vp_v = vp_ref[...]
        m = _mixer_pre(ue, ti * TT, vp_v, pw_ref, br_ref, bi_ref)
        hs_v = hs_ref[...]
        hprev = _down(jnp.concatenate([jnp.where(first, hsh_ref[...], 0.0), hs_v], axis=0), 1)[HALO:]
        dyv = dy_ref[...]

        def acc(rw, v):
            ga_ref[rw:rw + 1, :] += jnp.sum(v, axis=0, keepdims=True)

        gnp = _row(vp_v, VP_GNP)
        ps = _row(vp_v, VP_PS)
        ssq = sum(jnp.sum(yp * yp, axis=-1, keepdims=True) for yp in m["ypool"])
        rp = lax.rsqrt(ssq * (1.0 / DP) + EPS)
        npool = [yp * rp for yp in m["ypool"]]
        dnp = [dyv[:, 128 * g:128 * (g + 1)] * gnp[:, 128 * g:128 * (g + 1)] for g in range(4)]
        mean_dn = sum(jnp.sum(dnp[g] * npool[g], axis=-1, keepdims=True) for g in range(4)) * (1.0 / DP)
        for g in range(4):
            lanes = slice(128 * g, 128 * (g + 1))
            ga_ref[GA_GNP:GA_GNP + 1, lanes] += jnp.sum(dyv[:, lanes] * npool[g], axis=0, keepdims=True)
            dyp = rp * (dnp[g] - npool[g] * mean_dn)
            ga_ref[GA_PS:GA_PS + 1, lanes] += jnp.sum(dyp * m["mapped"][g], axis=0, keepdims=True)
            dmap = dyp * ps[:, lanes]
            ga_ref[GA_PB:GA_PB + 1, lanes] += jnp.sum(dmap, axis=0, keepdims=True)
            dmb = dmap.astype(BF16)
            dpw_ref[g] += _tn(m["pooled"][g].astype(BF16), dmb)
            dpool = _nt(dmb, pw_ref[g])
            qv = dpool * m["inv_cnt"][g]
            s = jnp.concatenate([qv, q_ref[:, lanes]], axis=0)
            for j in range(g + 1):
                s = s + _up(s, 1 << j)
            q_ref[:, lanes] = qv[:HALO]
            dp_ref[:, lanes] = (s[:TT] - dpool).astype(BF16)

        gnr = _row(vp_v, VP_GNR)
        yr = hs_v * m["gl"]
        rr = _rms(yr)
        nr = yr * rr
        dyr_out = dyv[:, DP:]
        acc(GA_GNR, dyr_out * nr)
        dyr = _rms_bwd(nr, rr, dyr_out * gnr)
        gate, gt = m["gate"], m["gt"]
        dgl = 0.5 * (1.0 + gt) + 0.5 * gate * (1.0 - gt * gt) * (GELU_C * (1.0 + 3.0 * GELU_K * gate * gate))
        dp_ref[:, DP + DR:] = (dyr * hs_v * dgl).astype(BF16)
        dhs = dyr * m["gl"]
        a = m["a"]
        row = lax.broadcasted_iota(jnp.int32, (TT, 1), 0)
        c_next = jnp.where(row < TT - 1, _up(a, 1), 1.0)
        cc, ce = _scan_rev(c_next, dhs)
        lam = cc * lam_ref[0:1, :] + ce
        lam_ref[0:1, :] = a[0:1, :] * lam[0:1, :]
        xc, ig, r, mult = m["xc"], m["ig"], m["r"], m["mult"]
        dmult = lam * ig * xc
        dig = lam * mult * xc
        dxc = lam * mult * ig
        dla = lam * hprev * a - dmult * (a * a) / mult
        acc(GA_LAM, dla * r)
        dzr = (dla * (-8.0 * m["sp"])) * (r * (1.0 - r))
        dzi = dig * (ig * (1.0 - ig))
        acc(GA_BR, dzr)
        acc(GA_BI, dzi)
        dzrb, dzib = dzr.astype(BF16), dzi.astype(BF16)
        xcb = m["xcb"]
        halves = []
        for k in range(2):
            lanes = slice(256 * k, 256 * (k + 1))
            dwr_ref[k] += _tn(xcb[:, lanes], dzrb[:, lanes])
            dwi_ref[k] += _tn(xcb[:, lanes], dzib[:, lanes])
            halves.append(_nt(dzrb[:, lanes], br_ref[k]) + _nt(dzib[:, lanes], bi_ref[k]))
        dxc = dxc + jnp.concatenate(halves, axis=1)
        acc(GA_CB, dxc)
        for k in range(4):
            acc(GA_CW + k, dxc * m["taps"][k])
        dxe = jnp.concatenate([dxc, dxc_ref[...]], axis=0)
        du = (_up(dxe, 3)[:TT] * _row(vp_v, VP_CW) + _up(dxe, 2)[:TT] * _row(vp_v, VP_CW + 1)
              + _up(dxe, 1)[:TT] * _row(vp_v, VP_CW + 2) + dxc * _row(vp_v, VP_CW + 3))
        dxc_ref[...] = dxc[:HALO]
        dp_ref[:, DP:DP + DR] = du.astype(BF16)

        @pl.when(i == nt - 1)
        def _():
            lamp = _row(vp_v, VP_LAM)
            ga_ref[GA_LAM:GA_LAM + 1, :] = ga_ref[GA_LAM:GA_LAM + 1, :] * (8.0 * jax.nn.sigmoid(-lamp))

    rev = lambda i: (nt - 1 - i, 0)
    rev_halo = lambda i: (_halo_index(nt - 1 - i, per), 0)
    return tiled_call(
        "bwd_mixer", nt, body, (dy, proj, proj, hs, hs, vp, pw, br, bi),
        in_specs=[pl.BlockSpec((TT, D), rev), pl.BlockSpec((TT, DIN), rev), pl.BlockSpec((HALO, DIN), rev_halo),
                  pl.BlockSpec((TT, DR), rev), pl.BlockSpec((HALO, DR), rev_halo),
                  _resident((16, 512), lambda i: (0, 0)), _resident((4, 128, 128), lambda i: (0, 0, 0)),
                  _resident((2, 256, 256), lambda i: (0, 0, 0)), _resident((2, 256, 256), lambda i: (0, 0, 0))],
        out_specs=[pl.BlockSpec((TT, DIN), rev), pl.BlockSpec((16, 512), lambda i: (0, 0)),
                   pl.BlockSpec((4, 128, 128), lambda i: (0, 0, 0)), pl.BlockSpec((2, 256, 256), lambda i: (0, 0, 0)),
                   pl.BlockSpec((2, 256, 256), lambda i: (0, 0, 0))],
        out_shape=[jax.ShapeDtypeStruct((T, DIN), BF16), jax.ShapeDtypeStruct((16, 512), F32),
                   jax.ShapeDtypeStruct((4, 128, 128), F32), jax.ShapeDtypeStruct((2, 256, 256), F32),
                   jax.ShapeDtypeStruct((2, 256, 256), F32)],
        scratch=[pltpu.VMEM((8, DR), F32), pltpu.VMEM((HALO, DP), F32), pltpu.VMEM((HALO, DR), F32)], job=job)


def bwd_in(dproj, h, g1, w_g, dh1):
    T = h.shape[0]

    def body(dp_ref, h_ref, g_ref, w_ref, dh1_ref, dh_ref, dw_ref, dg_ref):
        @pl.when(pl.program_id(0) == 0)
        def _():
            dw_ref[...] = jnp.zeros_like(dw_ref)
            dg_ref[...] = jnp.zeros_like(dg_ref)

        hv = h_ref[...]
        r = _rms(hv)
        n = hv * r
        gv = g_ref[...]
        xn = (n * gv).astype(BF16)
        dxn = jnp.zeros((TT, D), F32)
        for q in range(NQ):
            dpq = dp_ref[:, WIN_S * q:WIN_S * (q + 1)]
            dxn = dxn + _nt(dpq, w_ref[q])
            dw_ref[q] += _tn(xn, dpq)
        dg_ref[...] += jnp.sum(dxn * n, axis=0, keepdims=True)
        dh_ref[...] = dh1_ref[...] + _rms_bwd(n, r, dxn * gv)

    return tiled_call(
        "bwd_in", T // TT, body, (dproj, h, g1, w_g, dh1),
        in_specs=[pl.BlockSpec((TT, DIN), lambda i: (i, 0)), pl.BlockSpec((TT, D), lambda i: (i, 0)),
                  _resident((1, D), lambda i: (0, 0)), _resident((NQ, D, WIN_S), lambda i: (0, 0, 0)),
                  pl.BlockSpec((TT, D), lambda i: (i, 0))],
        out_specs=[pl.BlockSpec((TT, D), lambda i: (i, 0)), pl.BlockSpec((NQ, D, WIN_S), lambda i: (0, 0, 0)),
                   pl.BlockSpec((1, D), lambda i: (0, 0))],
        out_shape=[jax.ShapeDtypeStruct((T, D), F32), jax.ShapeDtypeStruct((NQ, D, WIN_S), F32),
                   jax.ShapeDtypeStruct((1, D), F32)])[0]


def _row_block(rows, cols, itemsize=4, budget=2 * 1024 * 1024):
    best = None
    for b in range(16, rows + 1, 16):
        if rows % b == 0 and b * cols * itemsize <= budget:
            best = b
    return best if best is not None else rows


def add_pairs(full, got, dtype):
    core = lax.axis_index("c").astype(jnp.int32).reshape(1)
    outs = []
    for k in range(len(full)):
        q, hr, c = got[k].shape
        rb = _row_block(hr, c)
        nb = hr // rb

        def body(c_ref, a_ref, b_ref, o_ref):
            o_ref[...] = (a_ref[...] + b_ref[...]).astype(dtype)

        outs.append(pl.pallas_call(
            body, name="add_pairs",
            grid_spec=pltpu.PrefetchScalarGridSpec(
                num_scalar_prefetch=1, grid=(q, nb),
                in_specs=[pl.BlockSpec((None, rb, c), lambda qi, i, c_ref, nb=nb: (qi, c_ref[0] * nb + i, 0)),
                          pl.BlockSpec((None, rb, c), lambda qi, i, c_ref: (qi, i, 0))],
                out_specs=pl.BlockSpec((None, rb, c), lambda qi, i, c_ref: (qi, i, 0))),
            out_shape=jax.ShapeDtypeStruct((q, hr, c), dtype),
            compiler_params=_params(("arbitrary", "arbitrary")),
        )(core, full[k], got[k]))
    return outs


def sum_chips(parts):
    core = lax.axis_index("c").astype(jnp.int32).reshape(1)
    outs = []
    for p in parts:
        _, hr, c = p.shape
        rb = _row_block(hr, c)
        nb = hr // rb

        def body(c_ref, p_ref, o_ref):
            s = p_ref[0].astype(F32) + p_ref[1].astype(F32)
            s = s + p_ref[2].astype(F32)
            o_ref[...] = s + p_ref[3].astype(F32)

        outs.append(pl.pallas_call(
            body, name="sum_chips",
            grid_spec=pltpu.PrefetchScalarGridSpec(
                num_scalar_prefetch=1, grid=(nb,),
                in_specs=[pl.BlockSpec((NQ, rb, c), lambda i, c_ref: (0, i, 0))],
                out_specs=pl.BlockSpec((rb, c), lambda i, c_ref, nb=nb: (c_ref[0] * nb + i, 0))),
            out_shape=jax.ShapeDtypeStruct((2 * hr, c), F32),
            compiler_params=_params(("arbitrary",)),
        )(core, p))
    return outs


def adamw(w, g, m, v):
    r, c = w.shape
    rb = _row_block(r, c, budget=1024 * 1024)
    c1 = 1.0 / (1.0 - ADAM_B1 ** ADAM_STEP)
    c2 = 1.0 / (1.0 - ADAM_B2 ** ADAM_STEP)

    def body(w_ref, g_ref, m_ref, v_ref, d_ref, nm_ref, nv_ref):
        gv = g_ref[...]
        nm = ADAM_B1 * m_ref[...] + (1.0 - ADAM_B1) * gv
        nv = ADAM_B2 * v_ref[...] + (1.0 - ADAM_B2) * (gv * gv)
        nm_ref[...] = nm
        nv_ref[...] = nv
        d_ref[...] = -ADAM_LR * ((nm * c1) / (jnp.sqrt(nv * c2) + ADAM_EPS) + ADAM_WD * w_ref[...])

    spec = pl.BlockSpec((rb, c), lambda i: (i, 0))
    return pl.pallas_call(
        body, name="adamw", grid=(r // rb,), in_specs=[spec] * 4, out_specs=[spec] * 3,
        out_shape=[jax.ShapeDtypeStruct((r, c), F32)] * 3,
        compiler_params=_params(("arbitrary",)),
    )(w, g, m, v)


def _place():
    return lax.axis_index("x"), lax.axis_index("y"), lax.axis_index("c")


def _other_chips(x, y):
    return [(1 - x, y), (x, 1 - y), (1 - x, 1 - y)]


LOCAL_CHUNKS = 4
ICI_CHUNKS = 2
FWD_CHUNKS = 4


def _start_remote(src_rows, dst_rows, rows, chunks, send_sem, recv_sem, dev):
    rc = rows // chunks
    for j in range(chunks):
        pltpu.make_async_remote_copy(src_rows(j * rc, rc), dst_rows(j * rc, rc), send_sem, recv_sem,
                                     device_id=dev, device_id_type=MESH).start()


def _waiter(src, dst, send_sem, recv_sem):
    x, y, c = _place()
    return pltpu.make_async_remote_copy(src, dst, send_sem, recv_sem, device_id=(x, y, c), device_id_type=MESH)


class GatherJob:
    def __init__(self, shards, layer):
        self.shards, self.layer, self.n = list(shards), layer, len(shards)
        self.operands = list(shards)
        self.out_shape = [jax.ShapeDtypeStruct((NQ,) + s.shape[1:], s.dtype) for s in shards]
        sems = pltpu.SemaphoreType.DMA((self.n, 3))
        self.scratch = [sems, sems, sems, sems, pltpu.SemaphoreType.DMA((self.n,))]

    def _half(self, k):
        return self.shards[k].shape[1] // 2

    def _src(self, ins, k, half):
        hr = self._half(k)
        return lambda r0, nr: ins[k].at[self.layer, pl.ds(half * hr + r0, nr), :]

    def _dst(self, outs, k, chip, half):
        hr = self._half(k)
        return lambda r0, nr: outs[k].at[2 * chip[0] + chip[1], pl.ds(half * hr + r0, nr), :]

    def start(self, ins, outs, scr):
        send, recv, fsend, frecv, lsem = scr
        x, y, c = _place()
        for k in range(self.n):
            rows = self.shards[k].shape[1]
            rc = rows // LOCAL_CHUNKS
            for j in range(LOCAL_CHUNKS):
                pltpu.make_async_copy(ins[k].at[self.layer, pl.ds(j * rc, rc), :],
                                      outs[k].at[2 * x + y, pl.ds(j * rc, rc), :], lsem.at[k]).start()
        for k in range(self.n):
            for j, chip in enumerate(_other_chips(x, y)):
                _start_remote(self._src(ins, k, c), self._dst(outs, k, (x, y), c), self._half(k), ICI_CHUNKS,
                              send.at[k, j], recv.at[k, j], (chip[0], chip[1], c))

    def mid(self, ins, outs, scr):
        send, recv, fsend, frecv, lsem = scr
        x, y, c = _place()
        for k in range(self.n):
            hr = self._half(k)
            for j, chip in enumerate(_other_chips(x, y)):
                got = self._dst(outs, k, chip, c)
                _waiter(got(0, hr), got(0, hr), send.at[k, j], recv.at[k, j]).wait_recv()
                _start_remote(got, got, hr, FWD_CHUNKS, fsend.at[k, j], frecv.at[k, j], (x, y, 1 - c))

    def finish(self, ins, outs, scr):
        send, recv, fsend, frecv, lsem = scr
        x, y, c = _place()
        for k in range(self.n):
            hr = self._half(k)
            for j, chip in enumerate(_other_chips(x, y)):
                theirs = self._dst(outs, k, chip, 1 - c)(0, hr)
                w = _waiter(theirs, theirs, fsend.at[k, j], frecv.at[k, j])
                w.wait_recv()
                w.wait_send()
                _waiter(theirs, theirs, send.at[k, j], recv.at[k, j]).wait_send()
            pltpu.make_async_copy(ins[k].at[self.layer], outs[k].at[2 * x + y], lsem.at[k]).wait()


class ExchangeJob:
    def __init__(self, arrs, scatter):
        self.arrs, self.scatter, self.n = list(arrs), list(scatter), len(arrs)
        self.operands = list(arrs)
        self.out_shape = [jax.ShapeDtypeStruct((NQ,) + a.shape[1:], a.dtype) for a in arrs]
        sems = pltpu.SemaphoreType.DMA((self.n, 3))
        self.scratch = [sems, sems, pltpu.SemaphoreType.DMA((self.n,))]

    def _slot(self, ref, s):
        return lambda r0, nr: ref.at[s, pl.ds(r0, nr), :]

    def start(self, ins, outs, scr):
        send, recv, lsem = scr
        x, y, c = _place()
        p = 2 * x + y
        for k in range(self.n):
            rows = self.arrs[k].shape[1]
            rc = rows // ICI_CHUNKS
            for j in range(ICI_CHUNKS):
                pltpu.make_async_copy(ins[k].at[p if self.scatter[k] else 0, pl.ds(j * rc, rc), :],
                                      outs[k].at[p, pl.ds(j * rc, rc), :], lsem.at[k]).start()
            for j, chip in enumerate(_other_chips(x, y)):
                q = 2 * chip[0] + chip[1]
                _start_remote(self._slot(ins[k], q if self.scatter[k] else 0), self._slot(outs[k], p), rows, ICI_CHUNKS,
                              send.at[k, j], recv.at[k, j], (chip[0], chip[1], c))

    def mid(self, ins, outs, scr):
        pass

    def finish(self, ins, outs, scr):
        send, recv, lsem = scr
        x, y, c = _place()
        for k in range(self.n):
            for j, chip in enumerate(_other_chips(x, y)):
                slot = outs[k].at[2 * chip[0] + chip[1]]
                w = _waiter(slot, slot, send.at[k, j], recv.at[k, j])
                w.wait_recv()
                w.wait_send()
            pltpu.make_async_copy(ins[k].at[0], outs[k].at[0], lsem.at[k]).wait()


def run_job(job, name):
    n_in, n_out = len(job.operands), len(job.out_shape)

    def body(*refs):
        ins, outs, scr = refs[:n_in], refs[n_in:n_in + n_out], refs[n_in + n_out:]
        job.start(ins, outs, scr)
        job.mid(ins, outs, scr)
        job.finish(ins, outs, scr)

    return pl.pallas_call(body, name=name, in_specs=[ANY] * n_in, out_specs=[ANY] * n_out, out_shape=job.out_shape,
                          scratch_shapes=job.scratch)(*job.operands)


def tiled_call(name, steps, body, args, in_specs, out_specs, out_shape, scratch=(), job=None, mid_step=None):
    if job is None:
        return pl.pallas_call(body, name=name, grid=(steps,), in_specs=in_specs, out_specs=out_specs, out_shape=out_shape,
                              scratch_shapes=list(scratch), compiler_params=_params(("arbitrary",)))(*args), []
    n_in, n_out, n_scr = len(args), len(out_shape), len(scratch)
    j_in, j_out = len(job.operands), len(job.out_shape)
    mid_step = steps // 2 if mid_step is None else mid_step

    def carried(*refs):
        a, ji = refs[:n_in], refs[n_in:n_in + j_in]
        o = refs[n_in + j_in:n_in + j_in + n_out]
        jo = refs[n_in + j_in + n_out:n_in + j_in + n_out + j_out]
        rest = refs[n_in + j_in + n_out + j_out:]
        sc, js = rest[:n_scr], rest[n_scr:]
        i = pl.program_id(0)

        @pl.when(i == 0)
        def _():
            job.start(ji, jo, js)

        body(*a, *o, *sc)

        @pl.when(i == mid_step)
        def _():
            job.mid(ji, jo, js)

        @pl.when(i == steps - 1)
        def _():
            job.finish(ji, jo, js)

    res = pl.pallas_call(
        carried, name=name, grid=(steps,), in_specs=list(in_specs) + [ANY] * j_in, out_specs=list(out_specs) + [ANY] * j_out,
        out_shape=list(out_shape) + list(job.out_shape), scratch_shapes=list(scratch) + list(job.scratch),
        compiler_params=_params(("arbitrary",)))(*args, *job.operands)
    return res[:n_out], res[n_out:]


def gather_small(shard):
    r, c = shard.shape

    def body(in_ref, out_ref, send, recv, lsem):
        x, y, cc = _place()
        own = pltpu.make_async_copy(in_ref, out_ref.at[2 * x + y], lsem)
        own.start()
        sends = []
        for j, chip in enumerate(_other_chips(x, y)):
            cp = pltpu.make_async_remote_copy(in_ref, out_ref.at[2 * x + y], send.at[j], recv.at[j],
                                              device_id=(chip[0], chip[1], cc), device_id_type=MESH)
            cp.start()
            sends.append(cp)
        for j, chip in enumerate(_other_chips(x, y)):
            slot = out_ref.at[2 * chip[0] + chip[1]]
            pltpu.make_async_remote_copy(slot, slot, send.at[j], recv.at[j],
                                         device_id=(chip[0], chip[1], cc), device_id_type=MESH).wait_recv()
        for cp in sends:
            cp.wait_send()
        own.wait()

    vm = pl.BlockSpec(memory_space=pltpu.VMEM)
    return pl.pallas_call(
        body, name="gather_small", in_specs=[vm], out_specs=vm,
        out_shape=jax.ShapeDtypeStruct((NQ, r, c), shard.dtype),
        scratch_shapes=[pltpu.SemaphoreType.DMA((3,)), pltpu.SemaphoreType.DMA((3,)), pltpu.SemaphoreType.DMA],
    )(shard)


D2D_CHUNKS = 4


class SwapJob:
    def __init__(self, arrs):
        self.arrs, self.n = list(arrs), len(arrs)
        self.operands = list(arrs)
        self.out_shape = [jax.ShapeDtypeStruct((a.shape[0], a.shape[1] // 2, a.shape[2]), a.dtype) for a in arrs]
        self.scratch = [pltpu.SemaphoreType.DMA((self.n,)), pltpu.SemaphoreType.DMA((self.n,))]

    def start(self, ins, got, scr):
        send, recv = scr
        x, y, c = _place()
        for k in range(self.n):
            q, r, _ = self.arrs[k].shape
            hr = r // 2
            for qi in range(q):
                _start_remote(lambda r0, nr: ins[k].at[qi, pl.ds((1 - c) * hr + r0, nr), :],
                              lambda r0, nr: got[k].at[qi, pl.ds(r0, nr), :], hr, D2D_CHUNKS,
                              send.at[k], recv.at[k], (x, y, 1 - c))

    def mid(self, ins, got, scr):
        pass

    def finish(self, ins, got, scr):
        send, recv = scr
        for k in range(self.n):
            hr = self.arrs[k].shape[1] // 2
            w = _waiter(ins[k].at[:, pl.ds(0, hr), :], got[k], send.at[k], recv.at[k])
            w.wait_send()
            w.wait_recv()


def swap_halves(arrs):
    return run_job(SwapJob(arrs), "swap_halves")


JOIN_CHUNKS = 8


def join_halves(arrs):
    n = len(arrs)

    def body(*refs):
        outs = refs[n:2 * n]
        send, recv = refs[2 * n:]
        x, y, c = _place()
        sib = (x, y, 1 - c)
        for k in range(n):
            hr = arrs[k].shape[0] // 2
            rc = hr // JOIN_CHUNKS
            for j in range(JOIN_CHUNKS):
                rows = outs[k].at[pl.ds(c * hr + j * rc, rc), :]
                pltpu.make_async_remote_copy(rows, rows, send.at[k], recv.at[k], device_id=sib, device_id_type=MESH).start()
        for k in range(n):
            hr = arrs[k].shape[0] // 2
            mine = outs[k].at[pl.ds(c * hr, hr), :]
            theirs = outs[k].at[pl.ds((1 - c) * hr, hr), :]
            whole = pltpu.make_async_remote_copy(mine, theirs, send.at[k], recv.at[k], device_id=sib, device_id_type=MESH)
            whole.wait_send()
            whole.wait_recv()

    out_shape = [jax.ShapeDtypeStruct(a.shape, a.dtype) for a in arrs]
    return pl.pallas_call(
        body, name="join_halves", in_specs=[ANY] * n, out_specs=[ANY] * n, out_shape=out_shape,
        input_output_aliases={k: k for k in range(n)},
        scratch_shapes=[pltpu.SemaphoreType.DMA((n,)), pltpu.SemaphoreType.DMA((n,))],
    )(*arrs)


def reduce_small(arr):
    pair = add_pairs([arr], swap_halves([arr]), F32)
    return join_halves(sum_chips(run_job(ExchangeJob(pair, [False]), "exchange_small")))[0]


def _block_diag(w):
    rows = [jnp.pad(w[hd], ((0, 0), (64 * (hd % 4), 192 - 64 * (hd % 4)))) for hd in range(8)]
    return jnp.stack([jnp.concatenate(rows[:4], axis=0), jnp.concatenate(rows[4:], axis=0)]).astype(BF16)


def _diag_blocks(b):
    return jnp.stack([b[hd // 4, 64 * (hd % 4):64 * (hd % 4) + 64, 64 * (hd % 4):64 * (hd % 4) + 64] for hd in range(8)])


def _vec_params(l, pool_b, pool_scale, conv_b, gate_r_b, gate_i_b, lru_lambda, conv_w_full, group_norm_g):
    rows = [pool_b[l], pool_scale[l], conv_b[l], gate_r_b[l], gate_i_b[l], lru_lambda[l],
            conv_w_full[l, 0], conv_w_full[l, 1], conv_w_full[l, 2], conv_w_full[l, 3],
            group_norm_g[l, :DP], group_norm_g[l, DP:]]
    return jnp.concatenate([jnp.stack(rows), jnp.zeros((4, 512), F32)], axis=0)


SMALL_ROWS_LAYER = 16 + 128 + 64 + 64 + 2 + 2
SMALL_ROWS = 1152


def _pack_small(layers, final_g, meta):
    rows = []
    for vec, pw, wr, wi, g1, g2 in layers:
        rows += [vec, pw.reshape(128, 512), wr.reshape(64, 512), wi.reshape(64, 512), g1.reshape(2, 512), g2.reshape(2, 512)]
    rows += [final_g.reshape(2, 512), meta.reshape(32, 512)]
    flat = jnp.concatenate(rows, axis=0)
    return jnp.concatenate([flat, jnp.zeros((SMALL_ROWS - flat.shape[0], 512), F32)], axis=0)


def _unpack_small(flat):
    layers, o = [], 0
    for _ in range(DEPTH):
        vec = flat[o:o + 16]; o += 16
        pw = flat[o:o + 128].reshape(4, 128, 128); o += 128
        wr = flat[o:o + 64].reshape(8, 64, 64); o += 64
        wi = flat[o:o + 64].reshape(8, 64, 64); o += 64
        g1 = flat[o:o + 2].reshape(1024); o += 2
        g2 = flat[o:o + 2].reshape(1024); o += 2
        layers.append((vec, pw, wr, wi, g1, g2))
    final_g = flat[o:o + 2].reshape(1024); o += 2
    meta = flat[o:o + 32].reshape(16, 1024)
    return layers, final_g, meta


def local_step(x2d, tgt2d, meta_full, conv_w_full, sp, shards=None, gathered=None):
    exchange = gathered is None
    if exchange:
        gathered = [None] * DEPTH
        first_in = run_job(GatherJob(shards[:1], 0), "gather_first")
    h = jnp.concatenate([meta_full, x2d], axis=0)
    tgt = jnp.concatenate([jnp.zeros((NMETA, D), F32), tgt2d], axis=0)
    saved = []
    for l in range(DEPTH):
        vp = _vec_params(l, sp["pool_b"], sp["pool_scale"], sp["conv_b"], sp["gate_r_b"], sp["gate_i_b"], sp["lru_lambda"],
                         conv_w_full, sp["group_norm_g"])
        pw = sp["pool_w"][l].astype(BF16)
        br, bi = _block_diag(sp["gate_r_w"][l]), _block_diag(sp["gate_i_w"][l])
        if exchange and l == 0:
            win = first_in[0]
            (proj,), (wo, wu) = fwd_in(h, sp["mix_norm_g"][l][None], win, GatherJob(shards[1:3], 0))
            (y, hs), (wd,) = fwd_mixer(proj, vp, pw, br, bi, GatherJob(shards[3:], 0))
        else:
            win, wo, wu, wd = gathered[l]
            (proj,), _ = fwd_in(h, sp["mix_norm_g"][l][None], win)
            (y, hs), _ = fwd_mixer(proj, vp, pw, br, bi)
        wo = wo.reshape(D, D)
        job = GatherJob(shards, l + 1) if exchange and l + 1 < DEPTH else None
        (h1, a, h2), fetched = fwd_post(h, y, wo, sp["mlp_norm_g"][l][None], wu, wd, job)
        if job is not None:
            gathered[l + 1] = fetched
        saved.append((h, proj, y, hs, h1, a, vp, pw, br, bi, win, wo, wu, wd))
        h = h2
    dh, dgf, loss_part = fwd_loss(h, sp["final_norm_g"][None], tgt)

    big = [None] * DEPTH
    pair_io = None
    small_layers = [None] * DEPTH
    for l in reversed(range(DEPTH)):
        h0, proj, y, hs, h1, a, vp, pw, br, bi, win, wo, wu, wd = saved[l]
        job = ExchangeJob(pair_io, [True] * 2) if exchange and pair_io is not None else None
        (dh1, da, xn2, dg2), parts_io = bwd_mlp_dx(dh, h1, a, sp["mlp_norm_g"][l][None], wu, wd, job)
        if job is not None:
            big[l + 1] = join_halves(sum_chips(parts_io)) + big[l + 1]
        dwu, dwd = bwd_mlp_dw(xn2, da, a, dh)
        (dy, dwo), got_ud = bwd_out(dh1, y, wo, SwapJob([dwu, dwd]) if exchange else None)
        job = ExchangeJob(add_pairs([dwu, dwd], got_ud, BF16), [True] * 2) if exchange else None
        (dproj, ga, dpw, dwr, dwi), parts_ud = bwd_mixer(dy, proj, hs, vp, pw, br, bi, job)
        dh, dwin, dg1 = bwd_in(dproj, h0, sp["mix_norm_g"][l][None], win, dh1)
        arrs_io = [dwin, dwo.reshape(NQ, D // NQ, D)]
        if exchange:
            big[l] = join_halves(sum_chips(parts_ud))
            pair_io = add_pairs(arrs_io, swap_halves(arrs_io), BF16)
        else:
            big[l] = arrs_io + [dwu, dwd]
        small_layers[l] = (ga, dpw, _diag_blocks(dwr), _diag_blocks(dwi), dg1[0], dg2[0])
    if exchange:
        big[0] = join_halves(sum_chips(run_job(ExchangeJob(pair_io, [True] * 2), "exchange_last"))) + big[0]
    return loss_part, dh, big, small_layers, dgf


def kernel(x, meta_tokens, mix_norm_g, w_in, pool_w, pool_b, pool_scale, conv_w, conv_b, gate_r_w, gate_r_b, gate_i_w, gate_i_b, lru_lambda, group_norm_g, w_out, mlp_norm_g, w_up, w_down, final_norm_g, loss_target, m_meta_tokens, m_mix_norm_g, m_w_in, m_pool_w, m_pool_b, m_pool_scale, m_conv_w, m_conv_b, m_gate_r_w, m_gate_r_b, m_gate_i_w, m_gate_i_b, m_lru_lambda, m_group_norm_g, m_w_out, m_mlp_norm_g, m_w_up, m_w_down, m_final_norm_g, v_meta_tokens, v_mix_norm_g, v_w_in, v_pool_w, v_pool_b, v_pool_scale, v_conv_w, v_conv_b, v_gate_r_w, v_gate_r_b, v_gate_i_w, v_gate_i_b, v_lru_lambda, v_group_norm_g, v_w_out, v_mlp_norm_g, v_w_up, v_w_down, v_final_norm_g):
    p = 2 * lax.axis_index("x") + lax.axis_index("y")

    shards = [w_in.astype(BF16), w_out.astype(BF16), w_up.astype(BF16), w_down.astype(BF16)]
    small = jnp.concatenate([meta_tokens, jnp.pad(conv_w.reshape(16, 128), ((0, 0), (0, 128)))], axis=0)
    small_g = gather_small(small)
    meta_full = jnp.transpose(small_g[:, :16, :], (1, 0, 2)).reshape(NMETA, D)
    conv_w_full = jnp.transpose(small_g[:, 16:, :128].reshape(NQ, DEPTH, 4, 128), (1, 2, 0, 3)).reshape(DEPTH, 4, DR)

    sp = dict(mix_norm_g=mix_norm_g, pool_w=pool_w, pool_b=pool_b, pool_scale=pool_scale, conv_b=conv_b, gate_r_w=gate_r_w,
              gate_r_b=gate_r_b, gate_i_w=gate_i_w, gate_i_b=gate_i_b, lru_lambda=lru_lambda, group_norm_g=group_norm_g,
              mlp_norm_g=mlp_norm_g, final_norm_g=final_norm_g)
    loss_part, dh, big, small_layers, dgf = local_step(x[0], loss_target[0], meta_full, conv_w_full, sp, shards=shards)
    loss = lax.psum(loss_part[0, 0], ("x", "y", "c"))
    grad_x = dh[NMETA:][None]

    small_sum = reduce_small(_pack_small(small_layers, dgf[0], dh[:NMETA])[None])
    g_layers, g_final, g_meta_full = _unpack_small(small_sum)

    g_vec = [gl[0] for gl in g_layers]
    grads = dict(
        meta_tokens=lax.dynamic_slice(g_meta_full, (0, p * (D // NQ)), (NMETA, D // NQ)),
        mix_norm_g=jnp.stack([gl[4] for gl in g_layers]),
        w_in=jnp.stack([big[l][0] for l in range(DEPTH)]),
        pool_w=jnp.stack([gl[1] for gl in g_layers]),
        pool_b=jnp.stack([gv[GA_PB] for gv in g_vec]),
        pool_scale=jnp.stack([gv[GA_PS] for gv in g_vec]),
        conv_w=lax.dynamic_slice(jnp.stack([gv[GA_CW:GA_CW + 4] for gv in g_vec]), (0, 0, p * 128), (DEPTH, 4, 128)),
        conv_b=jnp.stack([gv[GA_CB] for gv in g_vec]),
        gate_r_w=jnp.stack([gl[2] for gl in g_layers]),
        gate_r_b=jnp.stack([gv[GA_BR] for gv in g_vec]),
        gate_i_w=jnp.stack([gl[3] for gl in g_layers]),
        gate_i_b=jnp.stack([gv[GA_BI] for gv in g_vec]),
        lru_lambda=jnp.stack([gv[GA_LAM] for gv in g_vec]),
        group_norm_g=jnp.stack([jnp.concatenate([gv[GA_GNP], gv[GA_GNR]]) for gv in g_vec]),
        w_out=jnp.stack([big[l][1] for l in range(DEPTH)]),
        mlp_norm_g=jnp.stack([gl[5] for gl in g_layers]),
        w_up=jnp.stack([big[l][2] for l in range(DEPTH)]),
        w_down=jnp.stack([big[l][3] for l in range(DEPTH)]),
        final_norm_g=g_final,
    )
    weights = dict(meta_tokens=meta_tokens, mix_norm_g=mix_norm_g, w_in=w_in, pool_w=pool_w, pool_b=pool_b, pool_scale=pool_scale,
                   conv_w=conv_w, conv_b=conv_b, gate_r_w=gate_r_w, gate_r_b=gate_r_b, gate_i_w=gate_i_w, gate_i_b=gate_i_b,
                   lru_lambda=lru_lambda, group_norm_g=group_norm_g, w_out=w_out, mlp_norm_g=mlp_norm_g, w_up=w_up, w_down=w_down,
                   final_norm_g=final_norm_g)
    mom_m = dict(meta_tokens=m_meta_tokens, mix_norm_g=m_mix_norm_g, w_in=m_w_in, pool_w=m_pool_w, pool_b=m_pool_b,
                 pool_scale=m_pool_scale, conv_w=m_conv_w, conv_b=m_conv_b, gate_r_w=m_gate_r_w, gate_r_b=m_gate_r_b,
                 gate_i_w=m_gate_i_w, gate_i_b=m_gate_i_b, lru_lambda=m_lru_lambda, group_norm_g=m_group_norm_g, w_out=m_w_out,
                 mlp_norm_g=m_mlp_norm_g, w_up=m_w_up, w_down=m_w_down, final_norm_g=m_final_norm_g)
    mom_v = dict(meta_tokens=v_meta_tokens, mix_norm_g=v_mix_norm_g, w_in=v_w_in, pool_w=v_pool_w, pool_b=v_pool_b,
                 pool_scale=v_pool_scale, conv_w=v_conv_w, conv_b=v_conv_b, gate_r_w=v_gate_r_w, gate_r_b=v_gate_r_b,
                 gate_i_w=v_gate_i_w, gate_i_b=v_gate_i_b, lru_lambda=v_lru_lambda, group_norm_g=v_group_norm_g, w_out=v_w_out,
                 mlp_norm_g=v_mlp_norm_g, w_up=v_w_up, w_down=v_w_down, final_norm_g=v_final_norm_g)
    names = list(weights)

    delta, new_m, new_v = {}, {}, {}
    big_names = ("w_in", "w_out", "w_up", "w_down")
    for nm in big_names:
        shp = weights[nm].shape
        two_d = lambda t: t.reshape(shp[0] * shp[1], shp[2])
        d_, m_, v_ = adamw(two_d(weights[nm]), two_d(grads[nm]), two_d(mom_m[nm]), two_d(mom_v[nm]))
        delta[nm], new_m[nm], new_v[nm] = d_.reshape(shp), m_.reshape(shp), v_.reshape(shp)
    small_names = [nm for nm in names if nm not in big_names]
    sizes = [weights[nm].size for nm in small_names]
    total = sum(sizes)
    rows = -(-total // 512)
    rows = -(-rows // 16) * 16

    def flat(tree, fill):
        v_ = jnp.concatenate([tree[nm].reshape(-1) for nm in small_names])
        return jnp.concatenate([v_, jnp.full((rows * 512 - total,), fill, F32)]).reshape(rows, 512)

    d_, m_, v_ = adamw(flat(weights, 0.0), flat(grads, 0.0), flat(mom_m, 0.0), flat(mom_v, 1.0))
    o = 0
    for nm, sz in zip(small_names, sizes):
        shp = weights[nm].shape
        delta[nm] = d_.reshape(-1)[o:o + sz].reshape(shp)
        new_m[nm] = m_.reshape(-1)[o:o + sz].reshape(shp)
        new_v[nm] = v_.reshape(-1)[o:o + sz].reshape(shp)
        o += sz

    return (loss, grad_x, *[grads[nm] for nm in names], *[delta[nm] for nm in names],
            *[new_m[nm] for nm in names], *[new_v[nm] for nm in names])
```

```python
import jax
import jax.numpy as jnp
from jax import lax
from jax.experimental import pallas as pl
from jax.experimental.pallas import tpu as pltpu

F32 = jnp.float32
BF16 = jnp.bfloat16

D = 1024
DP = 512
DR = 512
DIN = 1536
DFF = 4096
DEPTH = 4
NMETA = 16
NQ = 4
WIN_S = DIN // NQ
FF_S = DFF // NQ
EPS = 1e-6
HALO = 16
TT = 432
TTW = 912
VMEM_LIMIT = 56 * 1024 * 1024

ADAM_LR = 0.001
ADAM_B1 = 0.9
ADAM_B2 = 0.999
ADAM_EPS = 1e-08
ADAM_WD = 0.01
ADAM_STEP = 10

MESH = pl.DeviceIdType.MESH
ANY = pl.BlockSpec(memory_space=pl.ANY)


def _params(sem=None, vmem=VMEM_LIMIT):
    return pltpu.CompilerParams(dimension_semantics=sem, vmem_limit_bytes=vmem)


def _resident(shape, index):
    return pl.BlockSpec(shape, index, pipeline_mode=pl.Buffered(1))


def _nt(x, w):
    return lax.dot_general(x, w, (((1,), (1,)), ((), ())), preferred_element_type=F32)


def _tn(a, b):
    return lax.dot_general(a, b, (((0,), (0,)), ((), ())), preferred_element_type=F32)


def _dot(x, w):
    return jnp.dot(x, w, preferred_element_type=F32)


def _rms(h):
    return lax.rsqrt(jnp.mean(h * h, axis=-1, keepdims=True) + EPS)


def _rms_bwd(n, r, dn):
    return r * (dn - n * jnp.mean(dn * n, axis=-1, keepdims=True))


def _down(x, s):
    return pltpu.roll(x, s, 0)


def _up(x, s):
    return pltpu.roll(x, x.shape[0] - s, 0)


GELU_C = 0.7978845608028654
GELU_K = 0.044715


def _gelu(x):
    t = jnp.tanh(GELU_C * (x + GELU_K * x * x * x))
    return 0.5 * x * (1.0 + t), t


def _softplus(x):
    return jnp.maximum(x, 0.0) + jnp.log1p(jnp.exp(-jnp.abs(x)))


VP_PB, VP_PS, VP_CB, VP_BR, VP_BI, VP_LAM, VP_CW, VP_GNP, VP_GNR = 0, 1, 2, 3, 4, 5, 6, 10, 11


def _row(vp, r):
    return vp[r:r + 1, :]


def _mixer_pre(ue, t0, vp, pw_ref, br_ref, bi_ref):
    tt = ue.shape[0] - HALO
    tf = (t0 + lax.broadcasted_iota(jnp.int32, (tt, 1), 0)).astype(F32) + 1.0
    pb, ps = _row(vp, VP_PB), _row(vp, VP_PS)
    pooled, mapped, inv_cnt = [], [], []
    for g in range(4):
        lanes = slice(128 * g, 128 * (g + 1))
        xe = ue[:, lanes]
        s = xe
        for j in range(g + 1):
            s = s + _down(s, 1 << j)
        inv = 1.0 / jnp.minimum(tf, float(2 << g))
        pg = s[HALO:] * inv - xe[HALO:]
        mg = _dot(pg.astype(BF16), pw_ref[g]) + pb[:, lanes]
        pooled.append(pg)
        mapped.append(mg)
        inv_cnt.append(inv)
    ypool = [mapped[g] * ps[:, 128 * g:128 * (g + 1)] for g in range(4)]

    xe = ue[:, DP:DP + DR]
    taps = [_down(xe, 3)[HALO:], _down(xe, 2)[HALO:], _down(xe, 1)[HALO:], xe[HALO:]]
    xc = _row(vp, VP_CB) + (taps[0] * _row(vp, VP_CW) + taps[1] * _row(vp, VP_CW + 1)
                            + taps[2] * _row(vp, VP_CW + 2) + taps[3] * _row(vp, VP_CW + 3))
    xcb = xc.astype(BF16)
    zr = jnp.concatenate([_dot(xcb[:, :256], br_ref[0]), _dot(xcb[:, 256:], br_ref[1])], axis=1) + _row(vp, VP_BR)
    zi = jnp.concatenate([_dot(xcb[:, :256], bi_ref[0]), _dot(xcb[:, 256:], bi_ref[1])], axis=1) + _row(vp, VP_BI)
    r = jax.nn.sigmoid(zr)
    ig = jax.nn.sigmoid(zi)
    sp = _softplus(-_row(vp, VP_LAM))
    la = (-8.0 * r) * sp
    a = jnp.exp(la)
    th = jnp.tanh(la)
    mult = jnp.sqrt((-2.0 * th) / (1.0 - th))
    gate = ue[HALO:, DP + DR:]
    gl, gt = _gelu(gate)
    return dict(pooled=pooled, mapped=mapped, ypool=ypool, inv_cnt=inv_cnt, taps=taps, xc=xc, xcb=xcb, r=r, ig=ig,
                sp=sp, a=a, mult=mult, gate=gate, gl=gl, gt=gt)


SUBLANES = 8


def _scan_fwd(a, b, h_in):
    tt = a.shape[0]
    sub = jnp.bitwise_and(lax.broadcasted_iota(jnp.int32, (tt, 1), 0), SUBLANES - 1)
    s = 1
    while s < SUBLANES:
        m = sub >= s
        a_s = jnp.where(m, _down(a, s), 1.0)
        b_s = jnp.where(m, _down(b, s), 0.0)
        b = a * b_s + b
        a = a * a_s
        s *= 2
    groups, h = [], h_in
    for g in range(tt // SUBLANES):
        rows = slice(SUBLANES * g, SUBLANES * (g + 1))
        hg = a[rows] * h + b[rows]
        groups.append(hg)
        h = hg[SUBLANES - 1:SUBLANES, :]
    return jnp.concatenate(groups, axis=0)


def _scan_rev(c, d, l_in):
    tt = c.shape[0]
    sub = jnp.bitwise_and(lax.broadcasted_iota(jnp.int32, (tt, 1), 0), SUBLANES - 1)
    s = 1
    while s < SUBLANES:
        m = sub < SUBLANES - s
        c_s = jnp.where(m, _up(c, s), 1.0)
        d_s = jnp.where(m, _up(d, s), 0.0)
        d = c * d_s + d
        c = c * c_s
        s *= 2
    groups, l = [], l_in
    for g in reversed(range(tt // SUBLANES)):
        rows = slice(SUBLANES * g, SUBLANES * (g + 1))
        lg = c[rows] * l + d[rows]
        groups.append(lg)
        l = lg[0:1, :]
    return jnp.concatenate(groups[::-1], axis=0)


def _halo_index(i, per_tile):
    return jnp.maximum(i * per_tile - 1, 0)


def _chip_slabs_to_columns(w_ref, wfull_ref):
    for q in range(NQ):
        wfull_ref[:, WIN_S * q:WIN_S * (q + 1)] = w_ref[q]


def fwd_in(h, g, w_g, job=None):
    T = h.shape[0]

    def body(h_ref, g_ref, w_ref, o_ref, wfull_ref):
        @pl.when(pl.program_id(0) == 0)
        def _():
            _chip_slabs_to_columns(w_ref, wfull_ref)

        hh = h_ref[...]
        xn = (hh * _rms(hh) * g_ref[...]).astype(BF16)
        o_ref[...] = _dot(xn, wfull_ref[...])

    return tiled_call(
        "fwd_in", T // TT, body, (h, g, w_g),
        in_specs=[pl.BlockSpec((TT, D), lambda i: (i, 0)), _resident((1, D), lambda i: (0, 0)),
                  _resident((NQ, D, WIN_S), lambda i: (0, 0, 0))],
        out_specs=[pl.BlockSpec((TT, DIN), lambda i: (i, 0))],
        out_shape=[jax.ShapeDtypeStruct((T, DIN), F32)], scratch=[pltpu.VMEM((D, DIN), BF16)],
        job=job, mid_step=T // TT - 1)


AUX_POOLED, AUX_MAPPED, AUX_XC, AUX_R, AUX_IG, AUX_A, AUX_MULT, AUX_GL, AUX_GT = range(9)
AUX_W = 9 * 512


def _aux(k):
    return slice(512 * k, 512 * (k + 1))


def fwd_mixer(proj, vp, pw, br, bi, job=None):
    T = proj.shape[0]
    per = TT // HALO

    def body(p_ref, ph_ref, vp_ref, pw_ref, br_ref, bi_ref, y_ref, hs_ref, aux_ref, carry_ref):
        i = pl.program_id(0)

        @pl.when(i == 0)
        def _():
            carry_ref[...] = jnp.zeros_like(carry_ref)

        halo = jnp.where(i > 0, ph_ref[...], 0.0)
        ue = jnp.concatenate([halo, p_ref[...]], axis=0)
        vp_v = vp_ref[...]
        m = _mixer_pre(ue, i * TT, vp_v, pw_ref, br_ref, bi_ref)
        for g in range(4):
            aux_ref[:, 512 * AUX_POOLED + 128 * g:512 * AUX_POOLED + 128 * (g + 1)] = m["pooled"][g]
            aux_ref[:, 512 * AUX_MAPPED + 128 * g:512 * AUX_MAPPED + 128 * (g + 1)] = m["mapped"][g]
        for k, name in ((AUX_XC, "xc"), (AUX_R, "r"), (AUX_IG, "ig"), (AUX_A, "a"), (AUX_MULT, "mult"), (AUX_GL, "gl"),
                        (AUX_GT, "gt")):
            aux_ref[:, _aux(k)] = m[name]
        b = m["mult"] * (m["ig"] * m["xc"])
        hs = _scan_fwd(m["a"], b, carry_ref[0:1, :])
        hs_ref[...] = hs
        carry_ref[0:1, :] = hs_ref[TT - 1:TT, :]
        yr = hs * m["gl"]
        ssq = sum(jnp.sum(yp * yp, axis=-1, keepdims=True) for yp in m["ypool"])
        rp = lax.rsqrt(ssq * (1.0 / DP) + EPS)
        gnp = _row(vp_v, VP_GNP)
        for g in range(4):
            lanes = slice(128 * g, 128 * (g + 1))
            y_ref[:, lanes] = (m["ypool"][g] * rp * gnp[:, lanes]).astype(BF16)
        y_ref[:, DP:] = (yr * _rms(yr) * _row(vp_v, VP_GNR)).astype(BF16)

    return tiled_call(
        "fwd_mixer", T // TT, body, (proj, proj, vp, pw, br, bi),
        in_specs=[pl.BlockSpec((TT, DIN), lambda i: (i, 0)),
                  pl.BlockSpec((HALO, DIN), lambda i: (_halo_index(i, per), 0)),
                  _resident((16, 512), lambda i: (0, 0)), _resident((4, 128, 128), lambda i: (0, 0, 0)),
                  _resident((2, 256, 256), lambda i: (0, 0, 0)), _resident((2, 256, 256), lambda i: (0, 0, 0))],
        out_specs=[pl.BlockSpec((TT, D), lambda i: (i, 0)), pl.BlockSpec((TT, DR), lambda i: (i, 0)),
                   pl.BlockSpec((TT, AUX_W), lambda i: (i, 0))],
        out_shape=[jax.ShapeDtypeStruct((T, D), BF16), jax.ShapeDtypeStruct((T, DR), F32),
                   jax.ShapeDtypeStruct((T, AUX_W), F32)],
        scratch=[pltpu.VMEM((8, DR), F32)], job=job, mid_step=(T // TT) * 3 // 4)


def fwd_post(h, y, wo_g, g2, wu_g, wd_g, job=None):
    T = h.shape[0]

    def body(h_ref, y_ref, wo_ref, g_ref, wu_ref, wd_ref, h1_ref, a_ref, h2_ref):
        h1 = h_ref[...] + _dot(y_ref[...], wo_ref[...])
        h1_ref[...] = h1
        xn = (h1 * _rms(h1) * g_ref[...]).astype(BF16)
        acc = h1
        for q in range(NQ):
            a = _dot(xn, wu_ref[q])
            a_ref[:, FF_S * q:FF_S * (q + 1)] = a.astype(BF16)
            ra = jnp.maximum(a, 0.0)
            acc = acc + _dot((ra * ra).astype(BF16), wd_ref[q])
        h2_ref[...] = acc

    return tiled_call(
        "fwd_post", T // TT, body, (h, y, wo_g, g2, wu_g, wd_g),
        in_specs=[pl.BlockSpec((TT, D), lambda i: (i, 0)), pl.BlockSpec((TT, D), lambda i: (i, 0)),
                  _resident((D, D), lambda i: (0, 0)), _resident((1, D), lambda i: (0, 0)),
                  _resident((NQ, D, FF_S), lambda i: (0, 0, 0)), _resident((NQ, FF_S, D), lambda i: (0, 0, 0))],
        out_specs=[pl.BlockSpec((TT, D), lambda i: (i, 0)), pl.BlockSpec((TT, DFF), lambda i: (i, 0)),
                   pl.BlockSpec((TT, D), lambda i: (i, 0))],
        out_shape=[jax.ShapeDtypeStruct((T, D), F32), jax.ShapeDtypeStruct((T, DFF), BF16),
                   jax.ShapeDtypeStruct((T, D), F32)],
        job=job, mid_step=(T // TT) * 3 // 4)


def fwd_loss(h, gf, tgt):
    T = h.shape[0]

    def body(h_ref, g_ref, t_ref, dh_ref, dg_ref, loss_ref):
        i = pl.program_id(0)

        @pl.when(i == 0)
        def _():
            dg_ref[...] = jnp.zeros_like(dg_ref)
            loss_ref[...] = jnp.zeros_like(loss_ref)

        hh = h_ref[...]
        r = _rms(hh)
        n = hh * r
        gfv = g_ref[...]
        row = i * TT + lax.broadcasted_iota(jnp.int32, (TT, 1), 0)
        e = jnp.where(row >= NMETA, n * gfv - t_ref[...], 0.0)
        loss_ref[...] += 0.5 * jnp.sum(jnp.sum(e * e, axis=-1, keepdims=True) * (1.0 / D), axis=0, keepdims=True)
        dy = e * (1.0 / D)
        dg_ref[...] += jnp.sum(dy * n, axis=0, keepdims=True)
        dh_ref[...] = _rms_bwd(n, r, dy * gfv)

    return tiled_call(
        "fwd_loss", T // TT, body, (h, gf, tgt),
        in_specs=[pl.BlockSpec((TT, D), lambda i: (i, 0)), _resident((1, D), lambda i: (0, 0)),
                  pl.BlockSpec((TT, D), lambda i: (i, 0))],
        out_specs=[pl.BlockSpec((TT, D), lambda i: (i, 0)), pl.BlockSpec((1, D), lambda i: (0, 0)),
                   pl.BlockSpec((1, 1), lambda i: (0, 0))],
        out_shape=[jax.ShapeDtypeStruct((T, D), F32), jax.ShapeDtypeStruct((1, D), F32),
                   jax.ShapeDtypeStruct((1, 1), F32)])[0]


def bwd_mlp_dx(dh2, h1, a, g2, wu_g, wd_g, job=None):
    T = dh2.shape[0]

    def body(dh2_ref, h1_ref, a_ref, g_ref, wu_ref, wd_ref, dh1_ref, da_ref, xn_ref, dg_ref):
        @pl.when(pl.program_id(0) == 0)
        def _():
            dg_ref[...] = jnp.zeros_like(dg_ref)

        h1v = h1_ref[...]
        r = _rms(h1v)
        n = h1v * r
        gv = g_ref[...]
        xn_ref[...] = (n * gv).astype(BF16)
        dh2v = dh2_ref[...]
        dh2b = dh2v.astype(BF16)
        dxn = jnp.zeros((TT, D), F32)
        for q in range(NQ):
            cols = slice(FF_S * q, FF_S * (q + 1))
            ra = jnp.maximum(a_ref[:, cols].astype(F32), 0.0)
            da = (_nt(dh2b, wd_ref[q]) * (2.0 * ra)).astype(BF16)
            da_ref[:, cols] = da
            dxn = dxn + _nt(da, wu_ref[q])
        dg_ref[...] += jnp.sum(dxn * n, axis=0, keepdims=True)
        dh1_ref[...] = dh2v + _rms_bwd(n, r, dxn * gv)

    return tiled_call(
        "bwd_mlp_dx", T // TT, body, (dh2, h1, a, g2, wu_g, wd_g),
        in_specs=[pl.BlockSpec((TT, D), lambda i: (i, 0)), pl.BlockSpec((TT, D), lambda i: (i, 0)),
                  pl.BlockSpec((TT, DFF), lambda i: (i, 0)), _resident((1, D), lambda i: (0, 0)),
                  _resident((NQ, D, FF_S), lambda i: (0, 0, 0)), _resident((NQ, FF_S, D), lambda i: (0, 0, 0))],
        out_specs=[pl.BlockSpec((TT, D), lambda i: (i, 0)), pl.BlockSpec((TT, DFF), lambda i: (i, 0)),
                   pl.BlockSpec((TT, D), lambda i: (i, 0)), pl.BlockSpec((1, D), lambda i: (0, 0))],
        out_shape=[jax.ShapeDtypeStruct((T, D), F32), jax.ShapeDtypeStruct((T, DFF), BF16),
                   jax.ShapeDtypeStruct((T, D), BF16), jax.ShapeDtypeStruct((1, D), F32)],
        job=job)


def bwd_mlp_dw(xn, da, a, dh2):
    T = xn.shape[0]

    def body(xn_ref, da_ref, a_ref, dh2_ref, dwu_ref, dwd_ref):
        @pl.when(pl.program_id(1) == 0)
        def _():
            dwu_ref[...] = jnp.zeros_like(dwu_ref)
            dwd_ref[...] = jnp.zeros_like(dwd_ref)

        dwu_ref[...] += _tn(xn_ref[...], da_ref[...])
        ra = jnp.maximum(a_ref[...].astype(F32), 0.0)
        dwd_ref[...] += _tn((ra * ra).astype(BF16), dh2_ref[...].astype(BF16))

    return pl.pallas_call(
        body, name="bwd_mlp_dw", grid=(NQ, T // TTW),
        in_specs=[pl.BlockSpec((TTW, D), lambda q, i: (i, 0)), pl.BlockSpec((TTW, FF_S), lambda q, i: (i, q)),
                  pl.BlockSpec((TTW, FF_S), lambda q, i: (i, q)), pl.BlockSpec((TTW, D), lambda q, i: (i, 0))],
        out_specs=[pl.BlockSpec((None, D, FF_S), lambda q, i: (q, 0, 0)),
                   pl.BlockSpec((None, FF_S, D), lambda q, i: (q, 0, 0))],
        out_shape=[jax.ShapeDtypeStruct((NQ, D, FF_S), F32), jax.ShapeDtypeStruct((NQ, FF_S, D), F32)],
        compiler_params=_params(("arbitrary", "arbitrary")),
    )(xn, da, a, dh2)


def bwd_out(dh1, y, wo_g, job=None):
    T = dh1.shape[0]

    def body(dh_ref, y_ref, wo_ref, dy_ref, dwo_ref):
        @pl.when(pl.program_id(0) == 0)
        def _():
            dwo_ref[...] = jnp.zeros_like(dwo_ref)

        dhb = dh_ref[...].astype(BF16)
        dy_ref[...] = _nt(dhb, wo_ref[...])
        dwo_ref[...] += _tn(y_ref[...], dhb)

    return tiled_call(
        "bwd_out", T // TT, body, (dh1, y, wo_g),
        in_specs=[pl.BlockSpec((TT, D), lambda i: (i, 0)), pl.BlockSpec((TT, D), lambda i: (i, 0)),
                  _resident((D, D), lambda i: (0, 0))],
        out_specs=[pl.BlockSpec((TT, D), lambda i: (i, 0)), pl.BlockSpec((D, D), lambda i: (0, 0))],
        out_shape=[jax.ShapeDtypeStruct((T, D), F32), jax.ShapeDtypeStruct((D, D), F32)], job=job)


GA_PS, GA_PB, GA_CB, GA_BR, GA_BI, GA_LAM, GA_CW, GA_GNP, GA_GNR = 0, 1, 2, 3, 4, 5, 6, 10, 11


def bwd_mixer(dy, proj, hs, aux, vp, pw, br, bi, job=None):
    T = proj.shape[0]
    tt = TT
    nt = T // tt
    per = tt // HALO

    def body(dy_ref, p_ref, ph_ref, hs_ref, hsh_ref, aux_ref, vp_ref, pw_ref, br_ref, bi_ref,
             dp_ref, ga_ref, dpw_ref, dwr_ref, dwi_ref, lam_ref, q_ref, dxc_ref):
        s = pl.program_id(0)
        ti = nt - 1 - s

        @pl.when(s == 0)
        def _():
            for ref in (ga_ref, dpw_ref, dwr_ref, dwi_ref, lam_ref, q_ref, dxc_ref):
                ref[...] = jnp.zeros_like(ref)

        dyv = dy_ref[...]
        lam_in, q_in, dxc_in = lam_ref[0:1, :], q_ref[...], dxc_ref[...]
        first = ti > 0
        vp_v = vp_ref[...]
        ur = jnp.concatenate([jnp.where(first, ph_ref[:, DP:DP + DR], 0.0), p_ref[:, DP:DP + DR]], axis=0)
        tf = (ti * tt + lax.broadcasted_iota(jnp.int32, (tt, 1), 0)).astype(F32) + 1.0
        mapped = [aux_ref[:, 512 * AUX_MAPPED + 128 * g:512 * AUX_MAPPED + 128 * (g + 1)] for g in range(4)]
        ps_row = _row(vp_v, VP_PS)
        xc = aux_ref[:, _aux(AUX_XC)]
        m = dict(pooled=[aux_ref[:, 512 * AUX_POOLED + 128 * g:512 * AUX_POOLED + 128 * (g + 1)] for g in range(4)],
                 mapped=mapped, ypool=[mapped[g] * ps_row[:, 128 * g:128 * (g + 1)] for g in range(4)],
                 inv_cnt=[1.0 / jnp.minimum(tf, float(2 << g)) for g in range(4)],
                 taps=[_down(ur, 3)[HALO:], _down(ur, 2)[HALO:], _down(ur, 1)[HALO:], ur[HALO:]],
                 xc=xc, xcb=xc.astype(BF16), r=aux_ref[:, _aux(AUX_R)], ig=aux_ref[:, _aux(AUX_IG)],
                 a=aux_ref[:, _aux(AUX_A)], mult=aux_ref[:, _aux(AUX_MULT)], gl=aux_ref[:, _aux(AUX_GL)],
                 gt=aux_ref[:, _aux(AUX_GT)], gate=p_ref[:, DP + DR:], sp=_softplus(-_row(vp_v, VP_LAM)))
        hs_v = hs_ref[...]
        hprev = _down(jnp.concatenate([jnp.where(first, hsh_ref[...], 0.0), hs_v], axis=0), 1)[HALO:]

        def acc(rw, v):
            ga_ref[rw:rw + 1, :] += jnp.sum(v, axis=0, keepdims=True)

        gnp = _row(vp_v, VP_GNP)
        ps = _row(vp_v, VP_PS)
        ssq = sum(jnp.sum(yp * yp, axis=-1, keepdims=True) for yp in m["ypool"])
        rp = lax.rsqrt(ssq * (1.0 / DP) + EPS)
        npool = [yp * rp for yp in m["ypool"]]
        dnp = [dyv[:, 128 * g:128 * (g + 1)] * gnp[:, 128 * g:128 * (g + 1)] for g in range(4)]
        mean_dn = sum(jnp.sum(dnp[g] * npool[g], axis=-1, keepdims=True) for g in range(4)) * (1.0 / DP)
        for g in range(4):
            lanes = slice(128 * g, 128 * (g + 1))
            ga_ref[GA_GNP:GA_GNP + 1, lanes] += jnp.sum(dyv[:, lanes] * npool[g], axis=0, keepdims=True)
            dyp = rp * (dnp[g] - npool[g] * mean_dn)
            ga_ref[GA_PS:GA_PS + 1, lanes] += jnp.sum(dyp * m["mapped"][g], axis=0, keepdims=True)
            dmap = dyp * ps[:, lanes]
            ga_ref[GA_PB:GA_PB + 1, lanes] += jnp.sum(dmap, axis=0, keepdims=True)
            dmb = dmap.astype(BF16)
            dpw_ref[g] += _tn(m["pooled"][g].astype(BF16), dmb)
            dpool = _nt(dmb, pw_ref[g])
            qv = dpool * m["inv_cnt"][g]
            win = jnp.concatenate([qv, q_in[:, lanes]], axis=0)
            for j in range(g + 1):
                win = win + _up(win, 1 << j)
            q_ref[:, lanes] = qv[:HALO]
            dp_ref[:, lanes] = (win[:tt] - dpool).astype(BF16)

        gnr = _row(vp_v, VP_GNR)
        yr = hs_v * m["gl"]
        rr = _rms(yr)
        nr = yr * rr
        dyr_out = dyv[:, DP:]
        acc(GA_GNR, dyr_out * nr)
        dyr = _rms_bwd(nr, rr, dyr_out * gnr)
        gate, gt = m["gate"], m["gt"]
        dgl = 0.5 * (1.0 + gt) + 0.5 * gate * (1.0 - gt * gt) * (GELU_C * (1.0 + 3.0 * GELU_K * gate * gate))
        dp_ref[:, DP + DR:] = (dyr * hs_v * dgl).astype(BF16)
        dhs = dyr * m["gl"]
        a = m["a"]
        row = lax.broadcasted_iota(jnp.int32, (tt, 1), 0)
        c_next = jnp.where(row < tt - 1, _up(a, 1), 1.0)
        lam = _scan_rev(c_next, dhs, lam_in)
        lam_ref[0:1, :] = a[0:1, :] * lam[0:1, :]
        xc, ig, r, mult = m["xc"], m["ig"], m["r"], m["mult"]
        dmult = lam * ig * xc
        dig = lam * mult * xc
        dxc = lam * mult * ig
        dla = lam * hprev * a - dmult * (a * a) / mult
        acc(GA_LAM, dla * r)
        dzr = (dla * (-8.0 * m["sp"])) * (r * (1.0 - r))
        dzi = dig * (ig * (1.0 - ig))
        acc(GA_BR, dzr)
        acc(GA_BI, dzi)
        dzrb, dzib = dzr.astype(BF16), dzi.astype(BF16)
        xcb = m["xcb"]
        halves = []
        for k in range(2):
            lanes = slice(256 * k, 256 * (k + 1))
            dwr_ref[k] += _tn(xcb[:, lanes], dzrb[:, lanes])
            dwi_ref[k] += _tn(xcb[:, lanes], dzib[:, lanes])
            halves.append(_nt(dzrb[:, lanes], br_ref[k]) + _nt(dzib[:, lanes], bi_ref[k]))
        dxc = dxc + jnp.concatenate(halves, axis=1)
        acc(GA_CB, dxc)
        for k in range(4):
            acc(GA_CW + k, dxc * m["taps"][k])
        dxe = jnp.concatenate([dxc, dxc_in], axis=0)
        du = (_up(dxe, 3)[:tt] * _row(vp_v, VP_CW) + _up(dxe, 2)[:tt] * _row(vp_v, VP_CW + 1)
              + _up(dxe, 1)[:tt] * _row(vp_v, VP_CW + 2) + dxc * _row(vp_v, VP_CW + 3))
        dxc_ref[...] = dxc[:HALO]
        dp_ref[:, DP:DP + DR] = du.astype(BF16)

        @pl.when(s == nt - 1)
        def _():
            lamp = _row(vp_v, VP_LAM)
            ga_ref[GA_LAM:GA_LAM + 1, :] = ga_ref[GA_LAM:GA_LAM + 1, :] * (8.0 * jax.nn.sigmoid(-lamp))

    rev = lambda i: (nt - 1 - i, 0)
    rev_halo = lambda i: (_halo_index(nt - 1 - i, per), 0)
    return tiled_call(
        "bwd_mixer", nt, body, (dy, proj, proj, hs, hs, aux, vp, pw, br, bi),
        in_specs=[pl.BlockSpec((tt, D), rev), pl.BlockSpec((tt, DIN), rev), pl.BlockSpec((HALO, DIN), rev_halo),
                  pl.BlockSpec((tt, DR), rev), pl.BlockSpec((HALO, DR), rev_halo), pl.BlockSpec((tt, AUX_W), rev),
                  _resident((16, 512), lambda i: (0, 0)), _resident((4, 128, 128), lambda i: (0, 0, 0)),
                  _resident((2, 256, 256), lambda i: (0, 0, 0)), _resident((2, 256, 256), lambda i: (0, 0, 0))],
        out_specs=[pl.BlockSpec((tt, DIN), rev), pl.BlockSpec((16, 512), lambda i: (0, 0)),
                   pl.BlockSpec((4, 128, 128), lambda i: (0, 0, 0)), pl.BlockSpec((2, 256, 256), lambda i: (0, 0, 0)),
                   pl.BlockSpec((2, 256, 256), lambda i: (0, 0, 0))],
        out_shape=[jax.ShapeDtypeStruct((T, DIN), BF16), jax.ShapeDtypeStruct((16, 512), F32),
                   jax.ShapeDtypeStruct((4, 128, 128), F32), jax.ShapeDtypeStruct((2, 256, 256), F32),
                   jax.ShapeDtypeStruct((2, 256, 256), F32)],
        scratch=[pltpu.VMEM((8, DR), F32), pltpu.VMEM((HALO, DP), F32), pltpu.VMEM((HALO, DR), F32)], job=job)


def bwd_in(dproj, h, g1, w_g, dh1):
    T = h.shape[0]

    def body(dp_ref, h_ref, g_ref, w_ref, dh1_ref, dh_ref, dw_ref, dg_ref, wfull_ref, acc_ref):
        i = pl.program_id(0)

        @pl.when(i == 0)
        def _():
            _chip_slabs_to_columns(w_ref, wfull_ref)
            acc_ref[...] = jnp.zeros_like(acc_ref)
            dg_ref[...] = jnp.zeros_like(dg_ref)

        hv = h_ref[...]
        r = _rms(hv)
        n = hv * r
        gv = g_ref[...]
        xn = (n * gv).astype(BF16)
        dpv = dp_ref[...]
        dxn = _nt(dpv, wfull_ref[...])
        acc_ref[...] += _tn(xn, dpv)
        dg_ref[...] += jnp.sum(dxn * n, axis=0, keepdims=True)
        dh_ref[...] = dh1_ref[...] + _rms_bwd(n, r, dxn * gv)

        @pl.when(i == T // TT - 1)
        def _():
            for q in range(NQ):
                dw_ref[q] = acc_ref[:, WIN_S * q:WIN_S * (q + 1)]

    return tiled_call(
        "bwd_in", T // TT, body, (dproj, h, g1, w_g, dh1),
        in_specs=[pl.BlockSpec((TT, DIN), lambda i: (i, 0)), pl.BlockSpec((TT, D), lambda i: (i, 0)),
                  _resident((1, D), lambda i: (0, 0)), _resident((NQ, D, WIN_S), lambda i: (0, 0, 0)),
                  pl.BlockSpec((TT, D), lambda i: (i, 0))],
        out_specs=[pl.BlockSpec((TT, D), lambda i: (i, 0)), pl.BlockSpec((NQ, D, WIN_S), lambda i: (0, 0, 0)),
                   pl.BlockSpec((1, D), lambda i: (0, 0))],
        out_shape=[jax.ShapeDtypeStruct((T, D), F32), jax.ShapeDtypeStruct((NQ, D, WIN_S), F32),
                   jax.ShapeDtypeStruct((1, D), F32)],
        scratch=[pltpu.VMEM((D, DIN), BF16), pltpu.VMEM((D, DIN), F32)])[0]


def _row_block(rows, cols, itemsize=4, budget=2 * 1024 * 1024):
    best = None
    for b in range(16, rows + 1, 16):
        if rows % b == 0 and b * cols * itemsize <= budget:
            best = b
    return best if best is not None else rows


def add_pairs(full, got, dtype):
    core = lax.axis_index("c").astype(jnp.int32).reshape(1)
    outs = []
    for k in range(len(full)):
        q, hr, c = got[k].shape
        rb = _row_block(hr, c)
        nb = hr // rb

        def body(c_ref, a_ref, b_ref, o_ref):
            o_ref[...] = (a_ref[...] + b_ref[...]).astype(dtype)

        outs.append(pl.pallas_call(
            body, name="add_pairs",
            grid_spec=pltpu.PrefetchScalarGridSpec(
                num_scalar_prefetch=1, grid=(q, nb),
                in_specs=[pl.BlockSpec((None, rb, c), lambda qi, i, c_ref, nb=nb: (qi, c_ref[0] * nb + i, 0)),
                          pl.BlockSpec((None, rb, c), lambda qi, i, c_ref: (qi, i, 0))],
                out_specs=pl.BlockSpec((None, rb, c), lambda qi, i, c_ref: (qi, i, 0))),
            out_shape=jax.ShapeDtypeStruct((q, hr, c), dtype),
            compiler_params=_params(("arbitrary", "arbitrary")),
        )(core, full[k], got[k]))
    return outs


def sum_chips(parts):
    core = lax.axis_index("c").astype(jnp.int32).reshape(1)
    outs = []
    for p in parts:
        _, hr, c = p.shape
        rb = _row_block(hr, c)
        nb = hr // rb

        def body(c_ref, p_ref, o_ref):
            s = p_ref[0].astype(F32) + p_ref[1].astype(F32)
            s = s + p_ref[2].astype(F32)
            o_ref[...] = s + p_ref[3].astype(F32)

        outs.append(pl.pallas_call(
            body, name="sum_chips",
            grid_spec=pltpu.PrefetchScalarGridSpec(
                num_scalar_prefetch=1, grid=(nb,),
                in_specs=[pl.BlockSpec((NQ, rb, c), lambda i, c_ref: (0, i, 0))],
                out_specs=pl.BlockSpec((rb, c), lambda i, c_ref, nb=nb: (c_ref[0] * nb + i, 0))),
            out_shape=jax.ShapeDtypeStruct((2 * hr, c), F32),
            compiler_params=_params(("arbitrary",)),
        )(core, p))
    return outs


def adamw(w, g, m, v):
    r, c = w.shape
    rb = _row_block(r, c, budget=1024 * 1024)
    c1 = 1.0 / (1.0 - ADAM_B1 ** ADAM_STEP)
    c2 = 1.0 / (1.0 - ADAM_B2 ** ADAM_STEP)

    def body(w_ref, g_ref, m_ref, v_ref, d_ref, nm_ref, nv_ref):
        gv = g_ref[...]
        nm = ADAM_B1 * m_ref[...] + (1.0 - ADAM_B1) * gv
        nv = ADAM_B2 * v_ref[...] + (1.0 - ADAM_B2) * (gv * gv)
        nm_ref[...] = nm
        nv_ref[...] = nv
        d_ref[...] = -ADAM_LR * ((nm * c1) / (jnp.sqrt(nv * c2) + ADAM_EPS) + ADAM_WD * w_ref[...])

    spec = pl.BlockSpec((rb, c), lambda i: (i, 0))
    return pl.pallas_call(
        body, name="adamw", grid=(r // rb,), in_specs=[spec] * 4, out_specs=[spec] * 3,
        out_shape=[jax.ShapeDtypeStruct((r, c), F32)] * 3,
        compiler_params=_params(("arbitrary",)),
    )(w, g, m, v)


def _place():
    return lax.axis_index("x"), lax.axis_index("y"), lax.axis_index("c")


def _other_chips(x, y):
    return [(1 - x, y), (x, 1 - y), (1 - x, 1 - y)]


LOCAL_CHUNKS = 4
ICI_CHUNKS = 2
FWD_CHUNKS = 4


def _start_remote(src_rows, dst_rows, rows, chunks, send_sem, recv_sem, dev):
    rc = rows // chunks
    for j in range(chunks):
        pltpu.make_async_remote_copy(src_rows(j * rc, rc), dst_rows(j * rc, rc), send_sem, recv_sem,
                                     device_id=dev, device_id_type=MESH).start()


def _waiter(src, dst, send_sem, recv_sem):
    x, y, c = _place()
    return pltpu.make_async_remote_copy(src, dst, send_sem, recv_sem, device_id=(x, y, c), device_id_type=MESH)


class GatherJob:
    def __init__(self, shards, layer):
        self.shards, self.layer, self.n = list(shards), layer, len(shards)
        self.operands = list(shards)
        self.out_shape = [jax.ShapeDtypeStruct((NQ,) + s.shape[1:], s.dtype) for s in shards]
        sems = pltpu.SemaphoreType.DMA((self.n, 3))
        self.scratch = [sems, sems, sems, sems, pltpu.SemaphoreType.DMA((self.n,))]

    def _half(self, k):
        return self.shards[k].shape[1] // 2

    def _src(self, ins, k, half):
        hr = self._half(k)
        return lambda r0, nr: ins[k].at[self.layer, pl.ds(half * hr + r0, nr), :]

    def _dst(self, outs, k, chip, half):
        hr = self._half(k)
        return lambda r0, nr: outs[k].at[2 * chip[0] + chip[1], pl.ds(half * hr + r0, nr), :]

    def start(self, ins, outs, scr):
        send, recv, fsend, frecv, lsem = scr
        x, y, c = _place()
        for k in range(self.n):
            rows = self.shards[k].shape[1]
            rc = rows // LOCAL_CHUNKS
            for j in range(LOCAL_CHUNKS):
                pltpu.make_async_copy(ins[k].at[self.layer, pl.ds(j * rc, rc), :],
                                      outs[k].at[2 * x + y, pl.ds(j * rc, rc), :], lsem.at[k]).start()
        for k in range(self.n):
            for j, chip in enumerate(_other_chips(x, y)):
                _start_remote(self._src(ins, k, c), self._dst(outs, k, (x, y), c), self._half(k), ICI_CHUNKS,
                              send.at[k, j], recv.at[k, j], (chip[0], chip[1], c))

    def mid(self, ins, outs, scr):
        send, recv, fsend, frecv, lsem = scr
        x, y, c = _place()
        for k in range(self.n):
            hr = self._half(k)
            for j, chip in enumerate(_other_chips(x, y)):
                got = self._dst(outs, k, chip, c)
                _waiter(got(0, hr), got(0, hr), send.at[k, j], recv.at[k, j]).wait_recv()
                _start_remote(got, got, hr, FWD_CHUNKS, fsend.at[k, j], frecv.at[k, j], (x, y, 1 - c))

    def finish(self, ins, outs, scr):
        send, recv, fsend, frecv, lsem = scr
        x, y, c = _place()
        for k in range(self.n):
            hr = self._half(k)
            for j, chip in enumerate(_other_chips(x, y)):
                theirs = self._dst(outs, k, chip, 1 - c)(0, hr)
                w = _waiter(theirs, theirs, fsend.at[k, j], frecv.at[k, j])
                w.wait_recv()
                w.wait_send()
                _waiter(theirs, theirs, send.at[k, j], recv.at[k, j]).wait_send()
            pltpu.make_async_copy(ins[k].at[self.layer], outs[k].at[2 * x + y], lsem.at[k]).wait()


class ExchangeJob:
    def __init__(self, arrs, scatter):
        self.arrs, self.scatter, self.n = list(arrs), list(scatter), len(arrs)
        self.operands = list(arrs)
        self.out_shape = [jax.ShapeDtypeStruct((NQ,) + a.shape[1:], a.dtype) for a in arrs]
        sems = pltpu.SemaphoreType.DMA((self.n, 3))
        self.scratch = [sems, sems, pltpu.SemaphoreType.DMA((self.n,))]

    def _slot(self, ref, s):
        return lambda r0, nr: ref.at[s, pl.ds(r0, nr), :]

    def start(self, ins, outs, scr):
        send, recv, lsem = scr
        x, y, c = _place()
        p = 2 * x + y
        for k in range(self.n):
            rows = self.arrs[k].shape[1]
            rc = rows // ICI_CHUNKS
            for j in range(ICI_CHUNKS):
                pltpu.make_async_copy(ins[k].at[p if self.scatter[k] else 0, pl.ds(j * rc, rc), :],
                                      outs[k].at[p, pl.ds(j * rc, rc), :], lsem.at[k]).start()
            for j, chip in enumerate(_other_chips(x, y)):
                q = 2 * chip[0] + chip[1]
                _start_remote(self._slot(ins[k], q if self.scatter[k] else 0), self._slot(outs[k], p), rows, ICI_CHUNKS,
                              send.at[k, j], recv.at[k, j], (chip[0], chip[1], c))

    def mid(self, ins, outs, scr):
        pass

    def finish(self, ins, outs, scr):
        send, recv, lsem = scr
        x, y, c = _place()
        for k in range(self.n):
            for j, chip in enumerate(_other_chips(x, y)):
                slot = outs[k].at[2 * chip[0] + chip[1]]
                w = _waiter(slot, slot, send.at[k, j], recv.at[k, j])
                w.wait_recv()
                w.wait_send()
            pltpu.make_async_copy(ins[k].at[0], outs[k].at[0], lsem.at[k]).wait()


def run_job(job, name):
    n_in, n_out = len(job.operands), len(job.out_shape)

    def body(*refs):
        ins, outs, scr = refs[:n_in], refs[n_in:n_in + n_out], refs[n_in + n_out:]
        job.start(ins, outs, scr)
        job.mid(ins, outs, scr)
        job.finish(ins, outs, scr)

    return pl.pallas_call(body, name=name, in_specs=[ANY] * n_in, out_specs=[ANY] * n_out, out_shape=job.out_shape,
                          scratch_shapes=job.scratch)(*job.operands)


def tiled_call(name, steps, body, args, in_specs, out_specs, out_shape, scratch=(), job=None, mid_step=None):
    if job is None:
        return pl.pallas_call(body, name=name, grid=(steps,), in_specs=in_specs, out_specs=out_specs, out_shape=out_shape,
                              scratch_shapes=list(scratch), compiler_params=_params(("arbitrary",)))(*args), []
    n_in, n_out, n_scr = len(args), len(out_shape), len(scratch)
    j_in, j_out = len(job.operands), len(job.out_shape)
    mid_step = steps // 2 if mid_step is None else mid_step

    def carried(*refs):
        a, ji = refs[:n_in], refs[n_in:n_in + j_in]
        o = refs[n_in + j_in:n_in + j_in + n_out]
        jo = refs[n_in + j_in + n_out:n_in + j_in + n_out + j_out]
        rest = refs[n_in + j_in + n_out + j_out:]
        sc, js = rest[:n_scr], rest[n_scr:]
        i = pl.program_id(0)

        @pl.when(i == 0)
        def _():
            job.start(ji, jo, js)

        body(*a, *o, *sc)

        @pl.when(i == mid_step)
        def _():
            job.mid(ji, jo, js)

        @pl.when(i == steps - 1)
        def _():
            job.finish(ji, jo, js)

    res = pl.pallas_call(
        carried, name=name, grid=(steps,), in_specs=list(in_specs) + [ANY] * j_in, out_specs=list(out_specs) + [ANY] * j_out,
        out_shape=list(out_shape) + list(job.out_shape), scratch_shapes=list(scratch) + list(job.scratch),
        compiler_params=_params(("arbitrary",)))(*args, *job.operands)
    return res[:n_out], res[n_out:]


def gather_small(shard):
    r, c = shard.shape

    def body(in_ref, out_ref, send, recv, lsem):
        x, y, cc = _place()
        own = pltpu.make_async_copy(in_ref, out_ref.at[2 * x + y], lsem)
        own.start()
        sends = []
        for j, chip in enumerate(_other_chips(x, y)):
            cp = pltpu.make_async_remote_copy(in_ref, out_ref.at[2 * x + y], send.at[j], recv.at[j],
                                              device_id=(chip[0], chip[1], cc), device_id_type=MESH)
            cp.start()
            sends.append(cp)
        for j, chip in enumerate(_other_chips(x, y)):
            slot = out_ref.at[2 * chip[0] + chip[1]]
            pltpu.make_async_remote_copy(slot, slot, send.at[j], recv.at[j],
                                         device_id=(chip[0], chip[1], cc), device_id_type=MESH).wait_recv()
        for cp in sends:
            cp.wait_send()
        own.wait()

    vm = pl.BlockSpec(memory_space=pltpu.VMEM)
    return pl.pallas_call(
        body, name="gather_small", in_specs=[vm], out_specs=vm,
        out_shape=jax.ShapeDtypeStruct((NQ, r, c), shard.dtype),
        scratch_shapes=[pltpu.SemaphoreType.DMA((3,)), pltpu.SemaphoreType.DMA((3,)), pltpu.SemaphoreType.DMA],
    )(shard)


D2D_CHUNKS = 4


class SwapJob:
    def __init__(self, arrs):
        self.arrs, self.n = list(arrs), len(arrs)
        self.operands = list(arrs)
        self.out_shape = [jax.ShapeDtypeStruct((a.shape[0], a.shape[1] // 2, a.shape[2]), a.dtype) for a in arrs]
        self.scratch = [pltpu.SemaphoreType.DMA((self.n,)), pltpu.SemaphoreType.DMA((self.n,))]

    def start(self, ins, got, scr):
        send, recv = scr
        x, y, c = _place()
        for k in range(self.n):
            q, r, _ = self.arrs[k].shape
            hr = r // 2
            for qi in range(q):
                _start_remote(lambda r0, nr: ins[k].at[qi, pl.ds((1 - c) * hr + r0, nr), :],
                              lambda r0, nr: got[k].at[qi, pl.ds(r0, nr), :], hr, D2D_CHUNKS,
                              send.at[k], recv.at[k], (x, y, 1 - c))

    def mid(self, ins, got, scr):
        pass

    def finish(self, ins, got, scr):
        send, recv = scr
        for k in range(self.n):
            hr = self.arrs[k].shape[1] // 2
            w = _waiter(ins[k].at[:, pl.ds(0, hr), :], got[k], send.at[k], recv.at[k])
            w.wait_send()
            w.wait_recv()


def swap_halves(arrs):
    return run_job(SwapJob(arrs), "swap_halves")


JOIN_CHUNKS = 8


def join_halves(arrs):
    n = len(arrs)

    def body(*refs):
        outs = refs[n:2 * n]
        send, recv = refs[2 * n:]
        x, y, c = _place()
        sib = (x, y, 1 - c)
        for k in range(n):
            hr = arrs[k].shape[0] // 2
            rc = hr // JOIN_CHUNKS
            for j in range(JOIN_CHUNKS):
                rows = outs[k].at[pl.ds(c * hr + j * rc, rc), :]
                pltpu.make_async_remote_copy(rows, rows, send.at[k], recv.at[k], device_id=sib, device_id_type=MESH).start()
        for k in range(n):
            hr = arrs[k].shape[0] // 2
            mine = outs[k].at[pl.ds(c * hr, hr), :]
            theirs = outs[k].at[pl.ds((1 - c) * hr, hr), :]
            whole = pltpu.make_async_remote_copy(mine, theirs, send.at[k], recv.at[k], device_id=sib, device_id_type=MESH)
            whole.wait_send()
            whole.wait_recv()

    out_shape = [jax.ShapeDtypeStruct(a.shape, a.dtype) for a in arrs]
    return pl.pallas_call(
        body, name="join_halves", in_specs=[ANY] * n, out_specs=[ANY] * n, out_shape=out_shape,
        input_output_aliases={k: k for k in range(n)},
        scratch_shapes=[pltpu.SemaphoreType.DMA((n,)), pltpu.SemaphoreType.DMA((n,))],
    )(*arrs)


def reduce_small(arr):
    pair = add_pairs([arr], swap_halves([arr]), F32)
    return join_halves(sum_chips(run_job(ExchangeJob(pair, [False]), "exchange_small")))[0]


def _block_diag(w):
    eye = jnp.eye(4, dtype=F32)[None, :, None, :, None]
    return (w.reshape(2, 4, 64, 1, 64) * eye).reshape(2, 256, 256).astype(BF16)


def _diag_blocks(b):
    eye = jnp.eye(4, dtype=F32)[None, :, None, :, None]
    return (b.reshape(2, 4, 64, 4, 64) * eye).sum(axis=3).reshape(8, 64, 64)


def _vec_params(l, pool_b, pool_scale, conv_b, gate_r_b, gate_i_b, lru_lambda, conv_w_full, group_norm_g):
    rows = [pool_b[l], pool_scale[l], conv_b[l], gate_r_b[l], gate_i_b[l], lru_lambda[l],
            conv_w_full[l, 0], conv_w_full[l, 1], conv_w_full[l, 2], conv_w_full[l, 3],
            group_norm_g[l, :DP], group_norm_g[l, DP:]]
    return jnp.concatenate([jnp.stack(rows), jnp.zeros((4, 512), F32)], axis=0)


SMALL_ROWS_LAYER = 16 + 128 + 64 + 64 + 2 + 2
SMALL_ROWS = 1152


def _pack_small(layers, final_g, meta):
    rows = []
    for vec, pw, wr, wi, g1, g2 in layers:
        rows += [vec, pw.reshape(128, 512), wr.reshape(64, 512), wi.reshape(64, 512), g1.reshape(2, 512), g2.reshape(2, 512)]
    rows += [final_g.reshape(2, 512), meta.reshape(32, 512)]
    flat = jnp.concatenate(rows, axis=0)
    return jnp.concatenate([flat, jnp.zeros((SMALL_ROWS - flat.shape[0], 512), F32)], axis=0)


def _unpack_small(flat):
    layers, o = [], 0
    for _ in range(DEPTH):
        vec = flat[o:o + 16]; o += 16
        pw = flat[o:o + 128].reshape(4, 128, 128); o += 128
        wr = flat[o:o + 64].reshape(8, 64, 64); o += 64
        wi = flat[o:o + 64].reshape(8, 64, 64); o += 64
        g1 = flat[o:o + 2].reshape(1024); o += 2
        g2 = flat[o:o + 2].reshape(1024); o += 2
        layers.append((vec, pw, wr, wi, g1, g2))
    final_g = flat[o:o + 2].reshape(1024); o += 2
    meta = flat[o:o + 32].reshape(16, 1024)
    return layers, final_g, meta


def local_step(x2d, tgt2d, meta_full, conv_w_full, sp, shards=None, gathered=None):
    exchange = gathered is None
    if exchange:
        gathered = [None] * DEPTH
        first_in = run_job(GatherJob(shards[:1], 0), "gather_first")
    h = jnp.concatenate([meta_full, x2d], axis=0)
    tgt = jnp.concatenate([jnp.zeros((NMETA, D), F32), tgt2d], axis=0)
    saved = []
    for l in range(DEPTH):
        vp = _vec_params(l, sp["pool_b"], sp["pool_scale"], sp["conv_b"], sp["gate_r_b"], sp["gate_i_b"], sp["lru_lambda"],
                         conv_w_full, sp["group_norm_g"])
        pw = sp["pool_w"][l].astype(BF16)
        br, bi = _block_diag(sp["gate_r_w"][l]), _block_diag(sp["gate_i_w"][l])
        if exchange and l == 0:
            win = first_in[0]
            (proj,), (wo, wu) = fwd_in(h, sp["mix_norm_g"][l][None], win, GatherJob(shards[1:3], 0))
            (y, hs, aux), (wd,) = fwd_mixer(proj, vp, pw, br, bi, GatherJob(shards[3:], 0))
        else:
            win, wo, wu, wd = gathered[l]
            (proj,), _ = fwd_in(h, sp["mix_norm_g"][l][None], win)
            (y, hs, aux), _ = fwd_mixer(proj, vp, pw, br, bi)
        wo = wo.reshape(D, D)
        job = GatherJob(shards, l + 1) if exchange and l + 1 < DEPTH else None
        (h1, a, h2), fetched = fwd_post(h, y, wo, sp["mlp_norm_g"][l][None], wu, wd, job)
        if job is not None:
            gathered[l + 1] = fetched
        saved.append((h, proj, y, hs, aux, h1, a, vp, pw, br, bi, win, wo, wu, wd))
        h = h2
    dh, dgf, loss_part = fwd_loss(h, sp["final_norm_g"][None], tgt)

    big = [None] * DEPTH
    pair_io = None
    small_layers = [None] * DEPTH
    for l in reversed(range(DEPTH)):
        h0, proj, y, hs, aux, h1, a, vp, pw, br, bi, win, wo, wu, wd = saved[l]
        job = ExchangeJob(pair_io, [True] * 2) if exchange and pair_io is not None else None
        (dh1, da, xn2, dg2), parts_io = bwd_mlp_dx(dh, h1, a, sp["mlp_norm_g"][l][None], wu, wd, job)
        if job is not None:
            big[l + 1] = join_halves(sum_chips(parts_io)) + big[l + 1]
        dwu, dwd = bwd_mlp_dw(xn2, da, a, dh)
        (dy, dwo), got_ud = bwd_out(dh1, y, wo, SwapJob([dwu, dwd]) if exchange else None)
        job = ExchangeJob(add_pairs([dwu, dwd], got_ud, BF16), [True] * 2) if exchange else None
        (dproj, ga, dpw, dwr, dwi), parts_ud = bwd_mixer(dy, proj, hs, aux, vp, pw, br, bi, job)
        dh, dwin, dg1 = bwd_in(dproj, h0, sp["mix_norm_g"][l][None], win, dh1)
        arrs_io = [dwin, dwo.reshape(NQ, D // NQ, D)]
        if exchange:
            big[l] = join_halves(sum_chips(parts_ud))
            pair_io = add_pairs(arrs_io, swap_halves(arrs_io), BF16)
        else:
            big[l] = arrs_io + [dwu, dwd]
        small_layers[l] = (ga, dpw, _diag_blocks(dwr), _diag_blocks(dwi), dg1[0], dg2[0])
    if exchange:
        big[0] = join_halves(sum_chips(run_job(ExchangeJob(pair_io, [True] * 2), "exchange_last"))) + big[0]
    return loss_part, dh, big, small_layers, dgf


def kernel(x, meta_tokens, mix_norm_g, w_in, pool_w, pool_b, pool_scale, conv_w, conv_b, gate_r_w, gate_r_b, gate_i_w, gate_i_b, lru_lambda, group_norm_g, w_out, mlp_norm_g, w_up, w_down, final_norm_g, loss_target, m_meta_tokens, m_mix_norm_g, m_w_in, m_pool_w, m_pool_b, m_pool_scale, m_conv_w, m_conv_b, m_gate_r_w, m_gate_r_b, m_gate_i_w, m_gate_i_b, m_lru_lambda, m_group_norm_g, m_w_out, m_mlp_norm_g, m_w_up, m_w_down, m_final_norm_g, v_meta_tokens, v_mix_norm_g, v_w_in, v_pool_w, v_pool_b, v_pool_scale, v_conv_w, v_conv_b, v_gate_r_w, v_gate_r_b, v_gate_i_w, v_gate_i_b, v_lru_lambda, v_group_norm_g, v_w_out, v_mlp_norm_g, v_w_up, v_w_down, v_final_norm_g):
    p = 2 * lax.axis_index("x") + lax.axis_index("y")

    shards = [w_in.astype(BF16), w_out.astype(BF16), w_up.astype(BF16), w_down.astype(BF16)]
    small = jnp.concatenate([meta_tokens, jnp.pad(conv_w.reshape(16, 128), ((0, 0), (0, 128)))], axis=0)
    small_g = gather_small(small)
    meta_full = jnp.transpose(small_g[:, :16, :], (1, 0, 2)).reshape(NMETA, D)
    conv_w_full = jnp.transpose(small_g[:, 16:, :128].reshape(NQ, DEPTH, 4, 128), (1, 2, 0, 3)).reshape(DEPTH, 4, DR)

    sp = dict(mix_norm_g=mix_norm_g, pool_w=pool_w, pool_b=pool_b, pool_scale=pool_scale, conv_b=conv_b, gate_r_w=gate_r_w,
              gate_r_b=gate_r_b, gate_i_w=gate_i_w, gate_i_b=gate_i_b, lru_lambda=lru_lambda, group_norm_g=group_norm_g,
              mlp_norm_g=mlp_norm_g, final_norm_g=final_norm_g)
    loss_part, dh, big, small_layers, dgf = local_step(x[0], loss_target[0], meta_full, conv_w_full, sp, shards=shards)
    loss = lax.psum(loss_part[0, 0], ("x", "y", "c"))
    grad_x = dh[NMETA:][None]

    small_sum = reduce_small(_pack_small(small_layers, dgf[0], dh[:NMETA])[None])
    g_layers, g_final, g_meta_full = _unpack_small(small_sum)

    g_vec = [gl[0] for gl in g_layers]
    grads = dict(
        meta_tokens=lax.dynamic_slice(g_meta_full, (0, p * (D // NQ)), (NMETA, D // NQ)),
        mix_norm_g=jnp.stack([gl[4] for gl in g_layers]),
        w_in=jnp.stack([big[l][0] for l in range(DEPTH)]),
        pool_w=jnp.stack([gl[1] for gl in g_layers]),
        pool_b=jnp.stack([gv[GA_PB] for gv in g_vec]),
        pool_scale=jnp.stack([gv[GA_PS] for gv in g_vec]),
        conv_w=lax.dynamic_slice(jnp.stack([gv[GA_CW:GA_CW + 4] for gv in g_vec]), (0, 0, p * 128), (DEPTH, 4, 128)),
        conv_b=jnp.stack([gv[GA_CB] for gv in g_vec]),
        gate_r_w=jnp.stack([gl[2] for gl in g_layers]),
        gate_r_b=jnp.stack([gv[GA_BR] for gv in g_vec]),
        gate_i_w=jnp.stack([gl[3] for gl in g_layers]),
        gate_i_b=jnp.stack([gv[GA_BI] for gv in g_vec]),
        lru_lambda=jnp.stack([gv[GA_LAM] for gv in g_vec]),
        group_norm_g=jnp.stack([jnp.concatenate([gv[GA_GNP], gv[GA_GNR]]) for gv in g_vec]),
        w_out=jnp.stack([big[l][1] for l in range(DEPTH)]),
        mlp_norm_g=jnp.stack([gl[5] for gl in g_layers]),
        w_up=jnp.stack([big[l][2] for l in range(DEPTH)]),
        w_down=jnp.stack([big[l][3] for l in range(DEPTH)]),
        final_norm_g=g_final,
    )
    weights = dict(meta_tokens=meta_tokens, mix_norm_g=mix_norm_g, w_in=w_in, pool_w=pool_w, pool_b=pool_b, pool_scale=pool_scale,
                   conv_w=conv_w, conv_b=conv_b, gate_r_w=gate_r_w, gate_r_b=gate_r_b, gate_i_w=gate_i_w, gate_i_b=gate_i_b,
                   lru_lambda=lru_lambda, group_norm_g=group_norm_g, w_out=w_out, mlp_norm_g=mlp_norm_g, w_up=w_up, w_down=w_down,
                   final_norm_g=final_norm_g)
    mom_m = dict(meta_tokens=m_meta_tokens, mix_norm_g=m_mix_norm_g, w_in=m_w_in, pool_w=m_pool_w, pool_b=m_pool_b,
                 pool_scale=m_pool_scale, conv_w=m_conv_w, conv_b=m_conv_b, gate_r_w=m_gate_r_w, gate_r_b=m_gate_r_b,
                 gate_i_w=m_gate_i_w, gate_i_b=m_gate_i_b, lru_lambda=m_lru_lambda, group_norm_g=m_group_norm_g, w_out=m_w_out,
                 mlp_norm_g=m_mlp_norm_g, w_up=m_w_up, w_down=m_w_down, final_norm_g=m_final_norm_g)
    mom_v = dict(meta_tokens=v_meta_tokens, mix_norm_g=v_mix_norm_g, w_in=v_w_in, pool_w=v_pool_w, pool_b=v_pool_b,
                 pool_scale=v_pool_scale, conv_w=v_conv_w, conv_b=v_conv_b, gate_r_w=v_gate_r_w, gate_r_b=v_gate_r_b,
                 gate_i_w=v_gate_i_w, gate_i_b=v_gate_i_b, lru_lambda=v_lru_lambda, group_norm_g=v_group_norm_g, w_out=v_w_out,
                 mlp_norm_g=v_mlp_norm_g, w_up=v_w_up, w_down=v_w_down, final_norm_g=v_final_norm_g)
    names = list(weights)

    delta, new_m, new_v = {}, {}, {}
    big_names = ("w_in", "w_out", "w_up", "w_down")
    for nm in big_names:
        shp = weights[nm].shape
        two_d = lambda t: t.reshape(shp[0] * shp[1], shp[2])
        d_, m_, v_ = adamw(two_d(weights[nm]), two_d(grads[nm]), two_d(mom_m[nm]), two_d(mom_v[nm]))
        delta[nm], new_m[nm], new_v[nm] = d_.reshape(shp), m_.reshape(shp), v_.reshape(shp)
    small_names = [nm for nm in names if nm not in big_names]
    sizes = [weights[nm].size for nm in small_names]
    total = sum(sizes)
    rows = -(-total // 512)
    rows = -(-rows // 16) * 16

    def flat(tree, fill):
        v_ = jnp.concatenate([tree[nm].reshape(-1) for nm in small_names])
        return jnp.concatenate([v_, jnp.full((rows * 512 - total,), fill, F32)]).reshape(rows, 512)

    d_, m_, v_ = adamw(flat(weights, 0.0), flat(grads, 0.0), flat(mom_m, 0.0), flat(mom_v, 1.0))
    o = 0
    for nm, sz in zip(small_names, sizes):
        shp = weights[nm].shape
        delta[nm] = d_.reshape(-1)[o:o + sz].reshape(shp)
        new_m[nm] = m_.reshape(-1)[o:o + sz].reshape(shp)
        new_v[nm] = v_.reshape(-1)[o:o + sz].reshape(shp)
        o += sz

    return (loss, grad_x, *[grads[nm] for nm in names], *[delta[nm] for nm in names],
            *[new_m[nm] for nm in names], *[new_v[nm] for nm in names])
```

```python
import jax
import jax.numpy as jnp
from jax import lax
from jax.experimental import pallas as pl
from jax.experimental.pallas import tpu as pltpu

F32 = jnp.float32
BF16 = jnp.bfloat16

D = 1024
DP = 512
DR = 512
DIN = 1536
DFF = 4096
DEPTH = 4
NMETA = 16
NQ = 4
WIN_S = DIN // NQ
FF_S = DFF // NQ
EPS = 1e-6
HALO = 16
TT = 432
TTW = 912
VMEM_LIMIT = 56 * 1024 * 1024

ADAM_LR = 0.001
ADAM_B1 = 0.9
ADAM_B2 = 0.999
ADAM_EPS = 1e-08
ADAM_WD = 0.01
ADAM_STEP = 10

MESH = pl.DeviceIdType.MESH
ANY = pl.BlockSpec(memory_space=pl.ANY)


def _params(sem=None, vmem=VMEM_LIMIT):
    return pltpu.CompilerParams(dimension_semantics=sem, vmem_limit_bytes=vmem)


def _resident(shape, index):
    return pl.BlockSpec(shape, index, pipeline_mode=pl.Buffered(1))


def _nt(x, w):
    return lax.dot_general(x, w, (((1,), (1,)), ((), ())), preferred_element_type=F32)


def _tn(a, b):
    return lax.dot_general(a, b, (((0,), (0,)), ((), ())), preferred_element_type=F32)


def _dot(x, w):
    return jnp.dot(x, w, preferred_element_type=F32)


def _rms(h):
    return lax.rsqrt(jnp.mean(h * h, axis=-1, keepdims=True) + EPS)


def _rms_bwd(n, r, dn):
    return r * (dn - n * jnp.mean(dn * n, axis=-1, keepdims=True))


def _down(x, s):
    return pltpu.roll(x, s, 0)


def _up(x, s):
    return pltpu.roll(x, x.shape[0] - s, 0)


GELU_C = 0.7978845608028654
GELU_K = 0.044715


def _gelu(x):
    t = jnp.tanh(GELU_C * (x + GELU_K * x * x * x))
    return 0.5 * x * (1.0 + t), t


def _softplus(x):
    return jnp.maximum(x, 0.0) + jnp.log1p(jnp.exp(-jnp.abs(x)))


VP_PB, VP_PS, VP_CB, VP_BR, VP_BI, VP_LAM, VP_CW, VP_GNP, VP_GNR = 0, 1, 2, 3, 4, 5, 6, 10, 11


def _row(vp, r):
    return vp[r:r + 1, :]


def _mixer_pre(ue, t0, vp, pw_ref, br_ref, bi_ref):
    tt = ue.shape[0] - HALO
    tf = (t0 + lax.broadcasted_iota(jnp.int32, (tt, 1), 0)).astype(F32) + 1.0
    pb, ps = _row(vp, VP_PB), _row(vp, VP_PS)
    pooled, mapped, inv_cnt = [], [], []
    for g in range(4):
        lanes = slice(128 * g, 128 * (g + 1))
        xe = ue[:, lanes]
        s = xe
        for j in range(g + 1):
            s = s + _down(s, 1 << j)
        inv = 1.0 / jnp.minimum(tf, float(2 << g))
        pg = s[HALO:] * inv - xe[HALO:]
        mg = _dot(pg.astype(BF16), pw_ref[g]) + pb[:, lanes]
        pooled.append(pg)
        mapped.append(mg)
        inv_cnt.append(inv)
    ypool = [mapped[g] * ps[:, 128 * g:128 * (g + 1)] for g in range(4)]

    xe = ue[:, DP:DP + DR]
    taps = [_down(xe, 3)[HALO:], _down(xe, 2)[HALO:], _down(xe, 1)[HALO:], xe[HALO:]]
    xc = _row(vp, VP_CB) + (taps[0] * _row(vp, VP_CW) + taps[1] * _row(vp, VP_CW + 1)
                            + taps[2] * _row(vp, VP_CW + 2) + taps[3] * _row(vp, VP_CW + 3))
    xcb = xc.astype(BF16)
    zr = jnp.concatenate([_dot(xcb[:, :256], br_ref[0]), _dot(xcb[:, 256:], br_ref[1])], axis=1) + _row(vp, VP_BR)
    zi = jnp.concatenate([_dot(xcb[:, :256], bi_ref[0]), _dot(xcb[:, 256:], bi_ref[1])], axis=1) + _row(vp, VP_BI)
    r = jax.nn.sigmoid(zr)
    ig = jax.nn.sigmoid(zi)
    sp = _softplus(-_row(vp, VP_LAM))
    la = (-8.0 * r) * sp
    a = jnp.exp(la)
    th = jnp.tanh(la)
    mult = jnp.sqrt((-2.0 * th) / (1.0 - th))
    gate = ue[HALO:, DP + DR:]
    gl, gt = _gelu(gate)
    return dict(pooled=pooled, mapped=mapped, ypool=ypool, inv_cnt=inv_cnt, taps=taps, xc=xc, xcb=xcb, r=r, ig=ig,
                sp=sp, a=a, mult=mult, gate=gate, gl=gl, gt=gt)


SUBLANES = 8


def _scan_fwd(a, b, h_in):
    tt = a.shape[0]
    sub = jnp.bitwise_and(lax.broadcasted_iota(jnp.int32, (tt, 1), 0), SUBLANES - 1)
    s = 1
    while s < SUBLANES:
        m = sub >= s
        a_s = jnp.where(m, _down(a, s), 1.0)
        b_s = jnp.where(m, _down(b, s), 0.0)
        b = a * b_s + b
        a = a * a_s
        s *= 2
    groups, h = [], h_in
    for g in range(tt // SUBLANES):
        rows = slice(SUBLANES * g, SUBLANES * (g + 1))
        hg = a[rows] * h + b[rows]
        groups.append(hg)
        h = hg[SUBLANES - 1:SUBLANES, :]
    return jnp.concatenate(groups, axis=0)


def _scan_rev(c, d, l_in):
    tt = c.shape[0]
    sub = jnp.bitwise_and(lax.broadcasted_iota(jnp.int32, (tt, 1), 0), SUBLANES - 1)
    s = 1
    while s < SUBLANES:
        m = sub < SUBLANES - s
        c_s = jnp.where(m, _up(c, s), 1.0)
        d_s = jnp.where(m, _up(d, s), 0.0)
        d = c * d_s + d
        c = c * c_s
        s *= 2
    groups, l = [], l_in
    for g in reversed(range(tt // SUBLANES)):
        rows = slice(SUBLANES * g, SUBLANES * (g + 1))
        lg = c[rows] * l + d[rows]
        groups.append(lg)
        l = lg[0:1, :]
    return jnp.concatenate(groups[::-1], axis=0)


def _halo_index(i, per_tile):
    return jnp.maximum(i * per_tile - 1, 0)


def _chip_slabs_to_columns(w_ref, wfull_ref):
    for q in range(NQ):
        wfull_ref[:, WIN_S * q:WIN_S * (q + 1)] = w_ref[q]


AUX_POOLED, AUX_MAPPED, AUX_XC, AUX_R, AUX_IG, AUX_A, AUX_MULT, AUX_GL, AUX_GT = range(9)
AUX_W = 9 * 512


def _aux(k):
    return slice(512 * k, 512 * (k + 1))


def fwd_mix(h, g, w_g, vp, pw, br, bi, job=None):
    T = h.shape[0]

    def body(h_ref, g_ref, w_ref, vp_ref, pw_ref, br_ref, bi_ref, p_ref, y_ref, hs_ref, aux_ref,
             wfull_ref, halo_ref, carry_ref):
        i = pl.program_id(0)

        @pl.when(i == 0)
        def _():
            _chip_slabs_to_columns(w_ref, wfull_ref)
            carry_ref[...] = jnp.zeros_like(carry_ref)
            halo_ref[...] = jnp.zeros_like(halo_ref)

        hh = h_ref[...]
        xn = (hh * _rms(hh) * g_ref[...]).astype(BF16)
        proj = _dot(xn, wfull_ref[...])
        p_ref[...] = proj
        ue = jnp.concatenate([halo_ref[...], proj], axis=0)
        halo_ref[...] = p_ref[TT - HALO:TT, :]
        vp_v = vp_ref[...]
        m = _mixer_pre(ue, i * TT, vp_v, pw_ref, br_ref, bi_ref)
        for g in range(4):
            aux_ref[:, 512 * AUX_POOLED + 128 * g:512 * AUX_POOLED + 128 * (g + 1)] = m["pooled"][g]
            aux_ref[:, 512 * AUX_MAPPED + 128 * g:512 * AUX_MAPPED + 128 * (g + 1)] = m["mapped"][g]
        for k, name in ((AUX_XC, "xc"), (AUX_R, "r"), (AUX_IG, "ig"), (AUX_A, "a"), (AUX_MULT, "mult"), (AUX_GL, "gl"),
                        (AUX_GT, "gt")):
            aux_ref[:, _aux(k)] = m[name]
        b = m["mult"] * (m["ig"] * m["xc"])
        hs = _scan_fwd(m["a"], b, carry_ref[0:1, :])
        hs_ref[...] = hs
        carry_ref[0:1, :] = hs_ref[TT - 1:TT, :]
        yr = hs * m["gl"]
        ssq = sum(jnp.sum(yp * yp, axis=-1, keepdims=True) for yp in m["ypool"])
        rp = lax.rsqrt(ssq * (1.0 / DP) + EPS)
        gnp = _row(vp_v, VP_GNP)
        for g in range(4):
            lanes = slice(128 * g, 128 * (g + 1))
            y_ref[:, lanes] = (m["ypool"][g] * rp * gnp[:, lanes]).astype(BF16)
        y_ref[:, DP:] = (yr * _rms(yr) * _row(vp_v, VP_GNR)).astype(BF16)

    return tiled_call(
        "fwd_mix", T // TT, body, (h, g, w_g, vp, pw, br, bi),
        in_specs=[pl.BlockSpec((TT, D), lambda i: (i, 0)), _resident((1, D), lambda i: (0, 0)),
                  _resident((NQ, D, WIN_S), lambda i: (0, 0, 0)),
                  _resident((16, 512), lambda i: (0, 0)), _resident((4, 128, 128), lambda i: (0, 0, 0)),
                  _resident((2, 256, 256), lambda i: (0, 0, 0)), _resident((2, 256, 256), lambda i: (0, 0, 0))],
        out_specs=[pl.BlockSpec((TT, DIN), lambda i: (i, 0)), pl.BlockSpec((TT, D), lambda i: (i, 0)),
                   pl.BlockSpec((TT, DR), lambda i: (i, 0)), pl.BlockSpec((TT, AUX_W), lambda i: (i, 0))],
        out_shape=[jax.ShapeDtypeStruct((T, DIN), F32), jax.ShapeDtypeStruct((T, D), BF16),
                   jax.ShapeDtypeStruct((T, DR), F32), jax.ShapeDtypeStruct((T, AUX_W), F32)],
        scratch=[pltpu.VMEM((D, DIN), BF16), pltpu.VMEM((HALO, DIN), F32), pltpu.VMEM((8, DR), F32)],
        job=job, mid_step=(T // TT) * 3 // 4)


def fwd_post(h, y, wo_g, g2, wu_g, wd_g, job=None):
    T = h.shape[0]

    def body(h_ref, y_ref, wo_ref, g_ref, wu_ref, wd_ref, h1_ref, a_ref, h2_ref):
        h1 = h_ref[...] + _dot(y_ref[...], wo_ref[...])
        h1_ref[...] = h1
        xn = (h1 * _rms(h1) * g_ref[...]).astype(BF16)
        acc = h1
        for q in range(NQ):
            a = _dot(xn, wu_ref[q])
            a_ref[:, FF_S * q:FF_S * (q + 1)] = a.astype(BF16)
            ra = jnp.maximum(a, 0.0)
            acc = acc + _dot((ra * ra).astype(BF16), wd_ref[q])
        h2_ref[...] = acc

    return tiled_call(
        "fwd_post", T // TT, body, (h, y, wo_g, g2, wu_g, wd_g),
        in_specs=[pl.BlockSpec((TT, D), lambda i: (i, 0)), pl.BlockSpec((TT, D), lambda i: (i, 0)),
                  _resident((D, D), lambda i: (0, 0)), _resident((1, D), lambda i: (0, 0)),
                  _resident((NQ, D, FF_S), lambda i: (0, 0, 0)), _resident((NQ, FF_S, D), lambda i: (0, 0, 0))],
        out_specs=[pl.BlockSpec((TT, D), lambda i: (i, 0)), pl.BlockSpec((TT, DFF), lambda i: (i, 0)),
                   pl.BlockSpec((TT, D), lambda i: (i, 0))],
        out_shape=[jax.ShapeDtypeStruct((T, D), F32), jax.ShapeDtypeStruct((T, DFF), BF16),
                   jax.ShapeDtypeStruct((T, D), F32)],
        job=job, mid_step=(T // TT) * 3 // 4)


def fwd_loss(h, gf, tgt):
    T = h.shape[0]

    def body(h_ref, g_ref, t_ref, dh_ref, dg_ref, loss_ref):
        i = pl.program_id(0)

        @pl.when(i == 0)
        def _():
            dg_ref[...] = jnp.zeros_like(dg_ref)
            loss_ref[...] = jnp.zeros_like(loss_ref)

        hh = h_ref[...]
        r = _rms(hh)
        n = hh * r
        gfv = g_ref[...]
        row = i * TT + lax.broadcasted_iota(jnp.int32, (TT, 1), 0)
        e = jnp.where(row >= NMETA, n * gfv - t_ref[...], 0.0)
        loss_ref[...] += 0.5 * jnp.sum(jnp.sum(e * e, axis=-1, keepdims=True) * (1.0 / D), axis=0, keepdims=True)
        dy = e * (1.0 / D)
        dg_ref[...] += jnp.sum(dy * n, axis=0, keepdims=True)
        dh_ref[...] = _rms_bwd(n, r, dy * gfv)

    return tiled_call(
        "fwd_loss", T // TT, body, (h, gf, tgt),
        in_specs=[pl.BlockSpec((TT, D), lambda i: (i, 0)), _resident((1, D), lambda i: (0, 0)),
                  pl.BlockSpec((TT, D), lambda i: (i, 0))],
        out_specs=[pl.BlockSpec((TT, D), lambda i: (i, 0)), pl.BlockSpec((1, D), lambda i: (0, 0)),
                   pl.BlockSpec((1, 1), lambda i: (0, 0))],
        out_shape=[jax.ShapeDtypeStruct((T, D), F32), jax.ShapeDtypeStruct((1, D), F32),
                   jax.ShapeDtypeStruct((1, 1), F32)])[0]


def bwd_mlp_dx(dh2, h1, a, g2, wu_g, wd_g, job=None):
    T = dh2.shape[0]

    def body(dh2_ref, h1_ref, a_ref, g_ref, wu_ref, wd_ref, dh1_ref, da_ref, xn_ref, dg_ref):
        @pl.when(pl.program_id(0) == 0)
        def _():
            dg_ref[...] = jnp.zeros_like(dg_ref)

        h1v = h1_ref[...]
        r = _rms(h1v)
        n = h1v * r
        gv = g_ref[...]
        xn_ref[...] = (n * gv).astype(BF16)
        dh2v = dh2_ref[...]
        dh2b = dh2v.astype(BF16)
        dxn = jnp.zeros((TT, D), F32)
        for q in range(NQ):
            cols = slice(FF_S * q, FF_S * (q + 1))
            ra = jnp.maximum(a_ref[:, cols].astype(F32), 0.0)
            da = (_nt(dh2b, wd_ref[q]) * (2.0 * ra)).astype(BF16)
            da_ref[:, cols] = da
            dxn = dxn + _nt(da, wu_ref[q])
        dg_ref[...] += jnp.sum(dxn * n, axis=0, keepdims=True)
        dh1_ref[...] = dh2v + _rms_bwd(n, r, dxn * gv)

    return tiled_call(
        "bwd_mlp_dx", T // TT, body, (dh2, h1, a, g2, wu_g, wd_g),
        in_specs=[pl.BlockSpec((TT, D), lambda i: (i, 0)), pl.BlockSpec((TT, D), lambda i: (i, 0)),
                  pl.BlockSpec((TT, DFF), lambda i: (i, 0)), _resident((1, D), lambda i: (0, 0)),
                  _resident((NQ, D, FF_S), lambda i: (0, 0, 0)), _resident((NQ, FF_S, D), lambda i: (0, 0, 0))],
        out_specs=[pl.BlockSpec((TT, D), lambda i: (i, 0)), pl.BlockSpec((TT, DFF), lambda i: (i, 0)),
                   pl.BlockSpec((TT, D), lambda i: (i, 0)), pl.BlockSpec((1, D), lambda i: (0, 0))],
        out_shape=[jax.ShapeDtypeStruct((T, D), F32), jax.ShapeDtypeStruct((T, DFF), BF16),
                   jax.ShapeDtypeStruct((T, D), BF16), jax.ShapeDtypeStruct((1, D), F32)],
        job=job)


def bwd_mlp_dw(xn, da, a, dh2):
    T = xn.shape[0]

    def body(xn_ref, da_ref, a_ref, dh2_ref, dwu_ref, dwd_ref):
        @pl.when(pl.program_id(1) == 0)
        def _():
            dwu_ref[...] = jnp.zeros_like(dwu_ref)
            dwd_ref[...] = jnp.zeros_like(dwd_ref)

        dwu_ref[...] += _tn(xn_ref[...], da_ref[...])
        ra = jnp.maximum(a_ref[...].astype(F32), 0.0)
        dwd_ref[...] += _tn((ra * ra).astype(BF16), dh2_ref[...].astype(BF16))

    return pl.pallas_call(
        body, name="bwd_mlp_dw", grid=(NQ, T // TTW),
        in_specs=[pl.BlockSpec((TTW, D), lambda q, i: (i, 0)), pl.BlockSpec((TTW, FF_S), lambda q, i: (i, q)),
                  pl.BlockSpec((TTW, FF_S), lambda q, i: (i, q)), pl.BlockSpec((TTW, D), lambda q, i: (i, 0))],
        out_specs=[pl.BlockSpec((None, D, FF_S), lambda q, i: (q, 0, 0)),
                   pl.BlockSpec((None, FF_S, D), lambda q, i: (q, 0, 0))],
        out_shape=[jax.ShapeDtypeStruct((NQ, D, FF_S), F32), jax.ShapeDtypeStruct((NQ, FF_S, D), F32)],
        compiler_params=_params(("arbitrary", "arbitrary")),
    )(xn, da, a, dh2)


def bwd_out(dh1, y, wo_g, job=None):
    T = dh1.shape[0]

    def body(dh_ref, y_ref, wo_ref, dy_ref, dwo_ref):
        @pl.when(pl.program_id(0) == 0)
        def _():
            dwo_ref[...] = jnp.zeros_like(dwo_ref)

        dhb = dh_ref[...].astype(BF16)
        dy_ref[...] = _nt(dhb, wo_ref[...])
        dwo_ref[...] += _tn(y_ref[...], dhb)

    return tiled_call(
        "bwd_out", T // TT, body, (dh1, y, wo_g),
        in_specs=[pl.BlockSpec((TT, D), lambda i: (i, 0)), pl.BlockSpec((TT, D), lambda i: (i, 0)),
                  _resident((D, D), lambda i: (0, 0))],
        out_specs=[pl.BlockSpec((TT, D), lambda i: (i, 0)), pl.BlockSpec((D, D), lambda i: (0, 0))],
        out_shape=[jax.ShapeDtypeStruct((T, D), F32), jax.ShapeDtypeStruct((D, D), F32)], job=job)


GA_PS, GA_PB, GA_CB, GA_BR, GA_BI, GA_LAM, GA_CW, GA_GNP, GA_GNR = 0, 1, 2, 3, 4, 5, 6, 10, 11


def bwd_mixer(dy, proj, hs, aux, vp, pw, br, bi, job=None):
    T = proj.shape[0]
    tt = TT
    nt = T // tt
    per = tt // HALO

    def body(dy_ref, p_ref, ph_ref, hs_ref, hsh_ref, aux_ref, vp_ref, pw_ref, br_ref, bi_ref,
             dp_ref, ga_ref, dpw_ref, dwr_ref, dwi_ref, lam_ref, q_ref, dxc_ref):
        s = pl.program_id(0)
        ti = nt - 1 - s

        @pl.when(s == 0)
        def _():
            for ref in (ga_ref, dpw_ref, dwr_ref, dwi_ref, lam_ref, q_ref, dxc_ref):
                ref[...] = jnp.zeros_like(ref)

        dyv = dy_ref[...]
        lam_in, q_in, dxc_in = lam_ref[0:1, :], q_ref[...], dxc_ref[...]
        first = ti > 0
        vp_v = vp_ref[...]
        ur = jnp.concatenate([jnp.where(first, ph_ref[:, DP:DP + DR], 0.0), p_ref[:, DP:DP + DR]], axis=0)
        tf = (ti * tt + lax.broadcasted_iota(jnp.int32, (tt, 1), 0)).astype(F32) + 1.0
        mapped = [aux_ref[:, 512 * AUX_MAPPED + 128 * g:512 * AUX_MAPPED + 128 * (g + 1)] for g in range(4)]
        ps_row = _row(vp_v, VP_PS)
        xc = aux_ref[:, _aux(AUX_XC)]
        m = dict(pooled=[aux_ref[:, 512 * AUX_POOLED + 128 * g:512 * AUX_POOLED + 128 * (g + 1)] for g in range(4)],
                 mapped=mapped, ypool=[mapped[g] * ps_row[:, 128 * g:128 * (g + 1)] for g in range(4)],
                 inv_cnt=[1.0 / jnp.minimum(tf, float(2 << g)) for g in range(4)],
                 taps=[_down(ur, 3)[HALO:], _down(ur, 2)[HALO:], _down(ur, 1)[HALO:], ur[HALO:]],
                 xc=xc, xcb=xc.astype(BF16), r=aux_ref[:, _aux(AUX_R)], ig=aux_ref[:, _aux(AUX_IG)],
                 a=aux_ref[:, _aux(AUX_A)], mult=aux_ref[:, _aux(AUX_MULT)], gl=aux_ref[:, _aux(AUX_GL)],
                 gt=aux_ref[:, _aux(AUX_GT)], gate=p_ref[:, DP + DR:], sp=_softplus(-_row(vp_v, VP_LAM)))
        hs_v = hs_ref[...]
        hprev = _down(jnp.concatenate([jnp.where(first, hsh_ref[...], 0.0), hs_v], axis=0), 1)[HALO:]

        def acc(rw, v):
            ga_ref[rw:rw + 1, :] += jnp.sum(v, axis=0, keepdims=True)

        gnp = _row(vp_v, VP_GNP)
        ps = _row(vp_v, VP_PS)
        ssq = sum(jnp.sum(yp * yp, axis=-1, keepdims=True) for yp in m["ypool"])
        rp = lax.rsqrt(ssq * (1.0 / DP) + EPS)
        npool = [yp * rp for yp in m["ypool"]]
        dnp = [dyv[:, 128 * g:128 * (g + 1)] * gnp[:, 128 * g:128 * (g + 1)] for g in range(4)]
        mean_dn = sum(jnp.sum(dnp[g] * npool[g], axis=-1, keepdims=True) for g in range(4)) * (1.0 / DP)
        for g in range(4):
            lanes = slice(128 * g, 128 * (g + 1))
            ga_ref[GA_GNP:GA_GNP + 1, lanes] += jnp.sum(dyv[:, lanes] * npool[g], axis=0, keepdims=True)
            dyp = rp * (dnp[g] - npool[g] * mean_dn)
            ga_ref[GA_PS:GA_PS + 1, lanes] += jnp.sum(dyp * m["mapped"][g], axis=0, keepdims=True)
            dmap = dyp * ps[:, lanes]
            ga_ref[GA_PB:GA_PB + 1, lanes] += jnp.sum(dmap, axis=0, keepdims=True)
            dmb = dmap.astype(BF16)
            dpw_ref[g] += _tn(m["pooled"][g].astype(BF16), dmb)
            dpool = _nt(dmb, pw_ref[g])
            qv = dpool * m["inv_cnt"][g]
            win = jnp.concatenate([qv, q_in[:, lanes]], axis=0)
            for j in range(g + 1):
                win = win + _up(win, 1 << j)
            q_ref[:, lanes] = qv[:HALO]
            dp_ref[:, lanes] = (win[:tt] - dpool).astype(BF16)

        gnr = _row(vp_v, VP_GNR)
        yr = hs_v * m["gl"]
        rr = _rms(yr)
        nr = yr * rr
        dyr_out = dyv[:, DP:]
        acc(GA_GNR, dyr_out * nr)
        dyr = _rms_bwd(nr, rr, dyr_out * gnr)
        gate, gt = m["gate"], m["gt"]
        dgl = 0.5 * (1.0 + gt) + 0.5 * gate * (1.0 - gt * gt) * (GELU_C * (1.0 + 3.0 * GELU_K * gate * gate))
        dp_ref[:, DP + DR:] = (dyr * hs_v * dgl).astype(BF16)
        dhs = dyr * m["gl"]
        a = m["a"]
        row = lax.broadcasted_iota(jnp.int32, (tt, 1), 0)
        c_next = jnp.where(row < tt - 1, _up(a, 1), 1.0)
        lam = _scan_rev(c_next, dhs, lam_in)
        lam_ref[0:1, :] = a[0:1, :] * lam[0:1, :]
        xc, ig, r, mult = m["xc"], m["ig"], m["r"], m["mult"]
        dmult = lam * ig * xc
        dig = lam * mult * xc
        dxc = lam * mult * ig
        dla = lam * hprev * a - dmult * (a * a) / mult
        acc(GA_LAM, dla * r)
        dzr = (dla * (-8.0 * m["sp"])) * (r * (1.0 - r))
        dzi = dig * (ig * (1.0 - ig))
        acc(GA_BR, dzr)
        acc(GA_BI, dzi)
        dzrb, dzib = dzr.astype(BF16), dzi.astype(BF16)
        xcb = m["xcb"]
        halves = []
        for k in range(2):
            lanes = slice(256 * k, 256 * (k + 1))
            dwr_ref[k] += _tn(xcb[:, lanes], dzrb[:, lanes])
            dwi_ref[k] += _tn(xcb[:, lanes], dzib[:, lanes])
            halves.append(_nt(dzrb[:, lanes], br_ref[k]) + _nt(dzib[:, lanes], bi_ref[k]))
        dxc = dxc + jnp.concatenate(halves, axis=1)
        acc(GA_CB, dxc)
        for k in range(4):
            acc(GA_CW + k, dxc * m["taps"][k])
        dxe = jnp.concatenate([dxc, dxc_in], axis=0)
        du = (_up(dxe, 3)[:tt] * _row(vp_v, VP_CW) + _up(dxe, 2)[:tt] * _row(vp_v, VP_CW + 1)
              + _up(dxe, 1)[:tt] * _row(vp_v, VP_CW + 2) + dxc * _row(vp_v, VP_CW + 3))
        dxc_ref[...] = dxc[:HALO]
        dp_ref[:, DP:DP + DR] = du.astype(BF16)

        @pl.when(s == nt - 1)
        def _():
            lamp = _row(vp_v, VP_LAM)
            ga_ref[GA_LAM:GA_LAM + 1, :] = ga_ref[GA_LAM:GA_LAM + 1, :] * (8.0 * jax.nn.sigmoid(-lamp))

    rev = lambda i: (nt - 1 - i, 0)
    rev_halo = lambda i: (_halo_index(nt - 1 - i, per), 0)
    return tiled_call(
        "bwd_mixer", nt, body, (dy, proj, proj, hs, hs, aux, vp, pw, br, bi),
        in_specs=[pl.BlockSpec((tt, D), rev), pl.BlockSpec((tt, DIN), rev), pl.BlockSpec((HALO, DIN), rev_halo),
                  pl.BlockSpec((tt, DR), rev), pl.BlockSpec((HALO, DR), rev_halo), pl.BlockSpec((tt, AUX_W), rev),
                  _resident((16, 512), lambda i: (0, 0)), _resident((4, 128, 128), lambda i: (0, 0, 0)),
                  _resident((2, 256, 256), lambda i: (0, 0, 0)), _resident((2, 256, 256), lambda i: (0, 0, 0))],
        out_specs=[pl.BlockSpec((tt, DIN), rev), pl.BlockSpec((16, 512), lambda i: (0, 0)),
                   pl.BlockSpec((4, 128, 128), lambda i: (0, 0, 0)), pl.BlockSpec((2, 256, 256), lambda i: (0, 0, 0)),
                   pl.BlockSpec((2, 256, 256), lambda i: (0, 0, 0))],
        out_shape=[jax.ShapeDtypeStruct((T, DIN), BF16), jax.ShapeDtypeStruct((16, 512), F32),
                   jax.ShapeDtypeStruct((4, 128, 128), F32), jax.ShapeDtypeStruct((2, 256, 256), F32),
                   jax.ShapeDtypeStruct((2, 256, 256), F32)],
        scratch=[pltpu.VMEM((8, DR), F32), pltpu.VMEM((HALO, DP), F32), pltpu.VMEM((HALO, DR), F32)], job=job)


def bwd_in(dproj, h, g1, w_g, dh1, job=None):
    T = h.shape[0]

    def body(dp_ref, h_ref, g_ref, w_ref, dh1_ref, dh_ref, dw_ref, dg_ref, wfull_ref, acc_ref):
        i = pl.program_id(0)

        @pl.when(i == 0)
        def _():
            _chip_slabs_to_columns(w_ref, wfull_ref)
            acc_ref[...] = jnp.zeros_like(acc_ref)
            dg_ref[...] = jnp.zeros_like(dg_ref)

        hv = h_ref[...]
        r = _rms(hv)
        n = hv * r
        gv = g_ref[...]
        xn = (n * gv).astype(BF16)
        dpv = dp_ref[...]
        dxn = _nt(dpv, wfull_ref[...])
        acc_ref[...] += _tn(xn, dpv)
        dg_ref[...] += jnp.sum(dxn * n, axis=0, keepdims=True)
        dh_ref[...] = dh1_ref[...] + _rms_bwd(n, r, dxn * gv)

        @pl.when(i == T // TT - 1)
        def _():
            for q in range(NQ):
                dw_ref[q] = acc_ref[:, WIN_S * q:WIN_S * (q + 1)]

    return tiled_call(
        "bwd_in", T // TT, body, (dproj, h, g1, w_g, dh1),
        in_specs=[pl.BlockSpec((TT, DIN), lambda i: (i, 0)), pl.BlockSpec((TT, D), lambda i: (i, 0)),
                  _resident((1, D), lambda i: (0, 0)), _resident((NQ, D, WIN_S), lambda i: (0, 0, 0)),
                  pl.BlockSpec((TT, D), lambda i: (i, 0))],
        out_specs=[pl.BlockSpec((TT, D), lambda i: (i, 0)), pl.BlockSpec((NQ, D, WIN_S), lambda i: (0, 0, 0)),
                   pl.BlockSpec((1, D), lambda i: (0, 0))],
        out_shape=[jax.ShapeDtypeStruct((T, D), F32), jax.ShapeDtypeStruct((NQ, D, WIN_S), F32),
                   jax.ShapeDtypeStruct((1, D), F32)],
        scratch=[pltpu.VMEM((D, DIN), BF16), pltpu.VMEM((D, DIN), F32)], job=job)


def _row_block(rows, cols, itemsize=4, budget=2 * 1024 * 1024):
    best = None
    for b in range(16, rows + 1, 16):
        if rows % b == 0 and b * cols * itemsize <= budget:
            best = b
    return best if best is not None else rows


def add_pairs(full, got, dtype):
    core = lax.axis_index("c").astype(jnp.int32).reshape(1)
    outs = []
    for k in range(len(full)):
        q, hr, c = got[k].shape
        rb = _row_block(hr, c)
        nb = hr // rb

        def body(c_ref, a_ref, b_ref, o_ref):
            o_ref[...] = (a_ref[...] + b_ref[...]).astype(dtype)

        outs.append(pl.pallas_call(
            body, name="add_pairs",
            grid_spec=pltpu.PrefetchScalarGridSpec(
                num_scalar_prefetch=1, grid=(q, nb),
                in_specs=[pl.BlockSpec((None, rb, c), lambda qi, i, c_ref, nb=nb: (qi, c_ref[0] * nb + i, 0)),
                          pl.BlockSpec((None, rb, c), lambda qi, i, c_ref: (qi, i, 0))],
                out_specs=pl.BlockSpec((None, rb, c), lambda qi, i, c_ref: (qi, i, 0))),
            out_shape=jax.ShapeDtypeStruct((q, hr, c), dtype),
            compiler_params=_params(("arbitrary", "arbitrary")),
        )(core, full[k], got[k]))
    return outs


def sum_chips(parts):
    core = lax.axis_index("c").astype(jnp.int32).reshape(1)
    outs = []
    for p in parts:
        _, hr, c = p.shape
        rb = _row_block(hr, c)
        nb = hr // rb

        def body(c_ref, p_ref, o_ref):
            s = p_ref[0].astype(F32) + p_ref[1].astype(F32)
            s = s + p_ref[2].astype(F32)
            o_ref[...] = s + p_ref[3].astype(F32)

        outs.append(pl.pallas_call(
            body, name="sum_chips",
            grid_spec=pltpu.PrefetchScalarGridSpec(
                num_scalar_prefetch=1, grid=(nb,),
                in_specs=[pl.BlockSpec((NQ, rb, c), lambda i, c_ref: (0, i, 0))],
                out_specs=pl.BlockSpec((rb, c), lambda i, c_ref, nb=nb: (c_ref[0] * nb + i, 0))),
            out_shape=jax.ShapeDtypeStruct((2 * hr, c), F32),
            compiler_params=_params(("arbitrary",)),
        )(core, p))
    return outs


def adamw(w, g, m, v):
    r, c = w.shape
    rb = _row_block(r, c, budget=1024 * 1024)
    c1 = 1.0 / (1.0 - ADAM_B1 ** ADAM_STEP)
    c2 = 1.0 / (1.0 - ADAM_B2 ** ADAM_STEP)

    def body(w_ref, g_ref, m_ref, v_ref, d_ref, nm_ref, nv_ref):
        gv = g_ref[...]
        nm = ADAM_B1 * m_ref[...] + (1.0 - ADAM_B1) * gv
        nv = ADAM_B2 * v_ref[...] + (1.0 - ADAM_B2) * (gv * gv)
        nm_ref[...] = nm
        nv_ref[...] = nv
        d_ref[...] = -ADAM_LR * ((nm * c1) / (jnp.sqrt(nv * c2) + ADAM_EPS) + ADAM_WD * w_ref[...])

    spec = pl.BlockSpec((rb, c), lambda i: (i, 0))
    return pl.pallas_call(
        body, name="adamw", grid=(r // rb,), in_specs=[spec] * 4, out_specs=[spec] * 3,
        out_shape=[jax.ShapeDtypeStruct((r, c), F32)] * 3,
        compiler_params=_params(("arbitrary",)),
    )(w, g, m, v)


def _place():
    return lax.axis_index("x"), lax.axis_index("y"), lax.axis_index("c")


def _other_chips(x, y):
    return [(1 - x, y), (x, 1 - y), (1 - x, 1 - y)]


LOCAL_CHUNKS = 4
ICI_CHUNKS = 2
FWD_CHUNKS = 4


def _start_remote(src_rows, dst_rows, rows, chunks, send_sem, recv_sem, dev):
    rc = rows // chunks
    for j in range(chunks):
        pltpu.make_async_remote_copy(src_rows(j * rc, rc), dst_rows(j * rc, rc), send_sem, recv_sem,
                                     device_id=dev, device_id_type=MESH).start()


def _waiter(src, dst, send_sem, recv_sem):
    x, y, c = _place()
    return pltpu.make_async_remote_copy(src, dst, send_sem, recv_sem, device_id=(x, y, c), device_id_type=MESH)


class GatherJob:
    def __init__(self, shards, layer):
        self.shards, self.layer, self.n = list(shards), layer, len(shards)
        self.operands = list(shards)
        self.out_shape = [jax.ShapeDtypeStruct((NQ,) + s.shape[1:], s.dtype) for s in shards]
        sems = pltpu.SemaphoreType.DMA((self.n, 3))
        self.scratch = [sems, sems, sems, sems, pltpu.SemaphoreType.DMA((self.n,))]

    def _half(self, k):
        return self.shards[k].shape[1] // 2

    def _src(self, ins, k, half):
        hr = self._half(k)
        return lambda r0, nr: ins[k].at[self.layer, pl.ds(half * hr + r0, nr), :]

    def _dst(self, outs, k, chip, half):
        hr = self._half(k)
        return lambda r0, nr: outs[k].at[2 * chip[0] + chip[1], pl.ds(half * hr + r0, nr), :]

    def start(self, ins, outs, scr):
        send, recv, fsend, frecv, lsem = scr
        x, y, c = _place()
        for k in range(self.n):
            rows = self.shards[k].shape[1]
            rc = rows // LOCAL_CHUNKS
            for j in range(LOCAL_CHUNKS):
                pltpu.make_async_copy(ins[k].at[self.layer, pl.ds(j * rc, rc), :],
                                      outs[k].at[2 * x + y, pl.ds(j * rc, rc), :], lsem.at[k]).start()
        for k in range(self.n):
            for j, chip in enumerate(_other_chips(x, y)):
                _start_remote(self._src(ins, k, c), self._dst(outs, k, (x, y), c), self._half(k), ICI_CHUNKS,
                              send.at[k, j], recv.at[k, j], (chip[0], chip[1], c))

    def mid(self, ins, outs, scr):
        send, recv, fsend, frecv, lsem = scr
        x, y, c = _place()
        for k in range(self.n):
            hr = self._half(k)
            for j, chip in enumerate(_other_chips(x, y)):
                got = self._dst(outs, k, chip, c)
                _waiter(got(0, hr), got(0, hr), send.at[k, j], recv.at[k, j]).wait_recv()
                _start_remote(got, got, hr, FWD_CHUNKS, fsend.at[k, j], frecv.at[k, j], (x, y, 1 - c))

    def finish(self, ins, outs, scr):
        send, recv, fsend, frecv, lsem = scr
        x, y, c = _place()
        for k in range(self.n):
            hr = self._half(k)
            for j, chip in enumerate(_other_chips(x, y)):
                theirs = self._dst(outs, k, chip, 1 - c)(0, hr)
                w = _waiter(theirs, theirs, fsend.at[k, j], frecv.at[k, j])
                w.wait_recv()
                w.wait_send()
                _waiter(theirs, theirs, send.at[k, j], recv.at[k, j]).wait_send()
            pltpu.make_async_copy(ins[k].at[self.layer], outs[k].at[2 * x + y], lsem.at[k]).wait()


class ExchangeJob:
    def __init__(self, arrs, scatter):
        self.arrs, self.scatter, self.n = list(arrs), list(scatter), len(arrs)
        self.operands = list(arrs)
        self.out_shape = [jax.ShapeDtypeStruct((NQ,) + a.shape[1:], a.dtype) for a in arrs]
        sems = pltpu.SemaphoreType.DMA((self.n, 3))
        self.scratch = [sems, sems, pltpu.SemaphoreType.DMA((self.n,))]

    def _slot(self, ref, s):
        return lambda r0, nr: ref.at[s, pl.ds(r0, nr), :]

    def start(self, ins, outs, scr):
        send, recv, lsem = scr
        x, y, c = _place()
        p = 2 * x + y
        for k in range(self.n):
            rows = self.arrs[k].shape[1]
            rc = rows // ICI_CHUNKS
            for j in range(ICI_CHUNKS):
                pltpu.make_async_copy(ins[k].at[p if self.scatter[k] else 0, pl.ds(j * rc, rc), :],
                                      outs[k].at[p, pl.ds(j * rc, rc), :], lsem.at[k]).start()
            for j, chip in enumerate(_other_chips(x, y)):
                q = 2 * chip[0] + chip[1]
                _start_remote(self._slot(ins[k], q if self.scatter[k] else 0), self._slot(outs[k], p), rows, ICI_CHUNKS,
                              send.at[k, j], recv.at[k, j], (chip[0], chip[1], c))

    def mid(self, ins, outs, scr):
        pass

    def finish(self, ins, outs, scr):
        send, recv, lsem = scr
        x, y, c = _place()
        for k in range(self.n):
            for j, chip in enumerate(_other_chips(x, y)):
                slot = outs[k].at[2 * chip[0] + chip[1]]
                w = _waiter(slot, slot, send.at[k, j], recv.at[k, j])
                w.wait_recv()
                w.wait_send()
            pltpu.make_async_copy(ins[k].at[0], outs[k].at[0], lsem.at[k]).wait()


def run_job(job, name):
    n_in, n_out = len(job.operands), len(job.out_shape)

    def body(*refs):
        ins, outs, scr = refs[:n_in], refs[n_in:n_in + n_out], refs[n_in + n_out:]
        job.start(ins, outs, scr)
        job.mid(ins, outs, scr)
        job.finish(ins, outs, scr)

    return pl.pallas_call(body, name=name, in_specs=[ANY] * n_in, out_specs=[ANY] * n_out, out_shape=job.out_shape,
                          input_output_aliases=dict(getattr(job, "aliases", {})),
                          scratch_shapes=job.scratch)(*job.operands)


def tiled_call(name, steps, body, args, in_specs, out_specs, out_shape, scratch=(), job=None, mid_step=None):
    if job is None:
        return pl.pallas_call(body, name=name, grid=(steps,), in_specs=in_specs, out_specs=out_specs, out_shape=out_shape,
                              scratch_shapes=list(scratch), compiler_params=_params(("arbitrary",)))(*args), []
    n_in, n_out, n_scr = len(args), len(out_shape), len(scratch)
    j_in, j_out = len(job.operands), len(job.out_shape)
    mid_step = steps // 2 if mid_step is None else mid_step

    def carried(*refs):
        a, ji = refs[:n_in], refs[n_in:n_in + j_in]
        o = refs[n_in + j_in:n_in + j_in + n_out]
        jo = refs[n_in + j_in + n_out:n_in + j_in + n_out + j_out]
        rest = refs[n_in + j_in + n_out + j_out:]
        sc, js = rest[:n_scr], rest[n_scr:]
        i = pl.program_id(0)

        @pl.when(i == 0)
        def _():
            job.start(ji, jo, js)

        body(*a, *o, *sc)

        @pl.when(i == mid_step)
        def _():
            job.mid(ji, jo, js)

        @pl.when(i == steps - 1)
        def _():
            job.finish(ji, jo, js)

    res = pl.pallas_call(
        carried, name=name, grid=(steps,), in_specs=list(in_specs) + [ANY] * j_in, out_specs=list(out_specs) + [ANY] * j_out,
        out_shape=list(out_shape) + list(job.out_shape), scratch_shapes=list(scratch) + list(job.scratch),
        input_output_aliases={n_in + i: n_out + o for i, o in getattr(job, "aliases", {}).items()},
        compiler_params=_params(("arbitrary",)))(*args, *job.operands)
    return res[:n_out], res[n_out:]


def gather_small(shard):
    r, c = shard.shape

    def body(in_ref, out_ref, send, recv, lsem):
        x, y, cc = _place()
        own = pltpu.make_async_copy(in_ref, out_ref.at[2 * x + y], lsem)
        own.start()
        sends = []
        for j, chip in enumerate(_other_chips(x, y)):
            cp = pltpu.make_async_remote_copy(in_ref, out_ref.at[2 * x + y], send.at[j], recv.at[j],
                                              device_id=(chip[0], chip[1], cc), device_id_type=MESH)
            cp.start()
            sends.append(cp)
        for j, chip in enumerate(_other_chips(x, y)):
            slot = out_ref.at[2 * chip[0] + chip[1]]
            pltpu.make_async_remote_copy(slot, slot, send.at[j], recv.at[j],
                                         device_id=(chip[0], chip[1], cc), device_id_type=MESH).wait_recv()
        for cp in sends:
            cp.wait_send()
        own.wait()

    vm = pl.BlockSpec(memory_space=pltpu.VMEM)
    return pl.pallas_call(
        body, name="gather_small", in_specs=[vm], out_specs=vm,
        out_shape=jax.ShapeDtypeStruct((NQ, r, c), shard.dtype),
        scratch_shapes=[pltpu.SemaphoreType.DMA((3,)), pltpu.SemaphoreType.DMA((3,)), pltpu.SemaphoreType.DMA],
    )(shard)


D2D_CHUNKS = 4


class SwapJob:
    def __init__(self, arrs):
        self.arrs, self.n = list(arrs), len(arrs)
        self.operands = list(arrs)
        self.out_shape = [jax.ShapeDtypeStruct((a.shape[0], a.shape[1] // 2, a.shape[2]), a.dtype) for a in arrs]
        self.scratch = [pltpu.SemaphoreType.DMA((self.n,)), pltpu.SemaphoreType.DMA((self.n,))]

    def start(self, ins, got, scr):
        send, recv = scr
        x, y, c = _place()
        for k in range(self.n):
            q, r, _ = self.arrs[k].shape
            hr = r // 2
            for qi in range(q):
                _start_remote(lambda r0, nr: ins[k].at[qi, pl.ds((1 - c) * hr + r0, nr), :],
                              lambda r0, nr: got[k].at[qi, pl.ds(r0, nr), :], hr, D2D_CHUNKS,
                              send.at[k], recv.at[k], (x, y, 1 - c))

    def mid(self, ins, got, scr):
        pass

    def finish(self, ins, got, scr):
        send, recv = scr
        for k in range(self.n):
            hr = self.arrs[k].shape[1] // 2
            w = _waiter(ins[k].at[:, pl.ds(0, hr), :], got[k], send.at[k], recv.at[k])
            w.wait_send()
            w.wait_recv()


def swap_halves(arrs):
    return run_job(SwapJob(arrs), "swap_halves")


JOIN_CHUNKS = 8


class JoinJob:
    def __init__(self, arrs):
        self.arrs, self.n = list(arrs), len(arrs)
        self.operands = list(arrs)
        self.out_shape = [jax.ShapeDtypeStruct(a.shape, a.dtype) for a in arrs]
        self.scratch = [pltpu.SemaphoreType.DMA((self.n,)), pltpu.SemaphoreType.DMA((self.n,))]
        self.aliases = {k: k for k in range(self.n)}

    def start(self, ins, outs, scr):
        send, recv = scr
        x, y, c = _place()
        for k in range(self.n):
            hr = self.arrs[k].shape[0] // 2
            rows = lambda r0, nr: outs[k].at[pl.ds(c * hr + r0, nr), :]
            _start_remote(rows, rows, hr, JOIN_CHUNKS, send.at[k], recv.at[k], (x, y, 1 - c))

    def mid(self, ins, outs, scr):
        pass

    def finish(self, ins, outs, scr):
        send, recv = scr
        x, y, c = _place()
        for k in range(self.n):
            hr = self.arrs[k].shape[0] // 2
            w = _waiter(outs[k].at[pl.ds(c * hr, hr), :], outs[k].at[pl.ds((1 - c) * hr, hr), :], send.at[k], recv.at[k])
            w.wait_send()
            w.wait_recv()


class Jobs:
    def __init__(self, jobs):
        self.jobs = list(jobs)
        self.operands = [a for j in self.jobs for a in j.operands]
        self.out_shape = [s for j in self.jobs for s in j.out_shape]
        self.scratch = [s for j in self.jobs for s in j.scratch]
        self.aliases = {}
        i0 = o0 = 0
        for j in self.jobs:
            self.aliases.update({i0 + i: o0 + o for i, o in getattr(j, "aliases", {}).items()})
            i0, o0 = i0 + len(j.operands), o0 + len(j.out_shape)

    def _each(self, ins, outs, scr):
        i0 = o0 = s0 = 0
        for j in self.jobs:
            ni, no, ns = len(j.operands), len(j.out_shape), len(j.scratch)
            yield j, ins[i0:i0 + ni], outs[o0:o0 + no], scr[s0:s0 + ns]
            i0, o0, s0 = i0 + ni, o0 + no, s0 + ns

    def start(self, ins, outs, scr):
        for j, i, o, s in self._each(ins, outs, scr):
            j.start(i, o, s)

    def mid(self, ins, outs, scr):
        for j, i, o, s in self._each(ins, outs, scr):
            j.mid(i, o, s)

    def finish(self, ins, outs, scr):
        for j, i, o, s in self._each(ins, outs, scr):
            j.finish(i, o, s)

    def split(self, results):
        out, o0 = [], 0
        for j in self.jobs:
            out.append(results[o0:o0 + len(j.out_shape)])
            o0 += len(j.out_shape)
        return out


def join_halves(arrs):
    return run_job(JoinJob(arrs), "join_halves")


def reduce_small(arr):
    pair = add_pairs([arr], swap_halves([arr]), F32)
    return join_halves(sum_chips(run_job(ExchangeJob(pair, [False]), "exchange_small")))[0]


def _block_diag(w):
    eye = jnp.eye(4, dtype=F32)[None, :, None, :, None]
    return (w.reshape(2, 4, 64, 1, 64) * eye).reshape(2, 256, 256).astype(BF16)


def _diag_blocks(b):
    eye = jnp.eye(4, dtype=F32)[None, :, None, :, None]
    return (b.reshape(2, 4, 64, 4, 64) * eye).sum(axis=3).reshape(8, 64, 64)


def _vec_params(l, pool_b, pool_scale, conv_b, gate_r_b, gate_i_b, lru_lambda, conv_w_full, group_norm_g):
    rows = [pool_b[l], pool_scale[l], conv_b[l], gate_r_b[l], gate_i_b[l], lru_lambda[l],
            conv_w_full[l, 0], conv_w_full[l, 1], conv_w_full[l, 2], conv_w_full[l, 3],
            group_norm_g[l, :DP], group_norm_g[l, DP:]]
    return jnp.concatenate([jnp.stack(rows), jnp.zeros((4, 512), F32)], axis=0)


SMALL_ROWS_LAYER = 16 + 128 + 64 + 64 + 2 + 2
SMALL_ROWS = 1152


def _pack_small(layers, final_g, meta):
    rows = []
    for vec, pw, wr, wi, g1, g2 in layers:
        rows += [vec, pw.reshape(128, 512), wr.reshape(64, 512), wi.reshape(64, 512), g1.reshape(2, 512), g2.reshape(2, 512)]
    rows += [final_g.reshape(2, 512), meta.reshape(32, 512)]
    flat = jnp.concatenate(rows, axis=0)
    return jnp.concatenate([flat, jnp.zeros((SMALL_ROWS - flat.shape[0], 512), F32)], axis=0)


def _unpack_small(flat):
    layers, o = [], 0
    for _ in range(DEPTH):
        vec = flat[o:o + 16]; o += 16
        pw = flat[o:o + 128].reshape(4, 128, 128); o += 128
        wr = flat[o:o + 64].reshape(8, 64, 64); o += 64
        wi = flat[o:o + 64].reshape(8, 64, 64); o += 64
        g1 = flat[o:o + 2].reshape(1024); o += 2
        g2 = flat[o:o + 2].reshape(1024); o += 2
        layers.append((vec, pw, wr, wi, g1, g2))
    final_g = flat[o:o + 2].reshape(1024); o += 2
    meta = flat[o:o + 32].reshape(16, 1024)
    return layers, final_g, meta


def local_step(x2d, tgt2d, meta_full, conv_w_full, sp, shards=None, gathered=None):
    exchange = gathered is None
    if exchange:
        gathered = [None] * DEPTH
        first_in = run_job(GatherJob(shards[:1], 0), "gather_first")
    h = jnp.concatenate([meta_full, x2d], axis=0)
    tgt = jnp.concatenate([jnp.zeros((NMETA, D), F32), tgt2d], axis=0)
    saved = []
    for l in range(DEPTH):
        vp = _vec_params(l, sp["pool_b"], sp["pool_scale"], sp["conv_b"], sp["gate_r_b"], sp["gate_i_b"], sp["lru_lambda"],
                         conv_w_full, sp["group_norm_g"])
        pw = sp["pool_w"][l].astype(BF16)
        br, bi = _block_diag(sp["gate_r_w"][l]), _block_diag(sp["gate_i_w"][l])
        if exchange and l == 0:
            win = first_in[0]
            (proj, y, hs, aux), (wo, wu, wd) = fwd_mix(h, sp["mix_norm_g"][l][None], win, vp, pw, br, bi,
                                                       GatherJob(shards[1:], 0))
        else:
            win, wo, wu, wd = gathered[l]
            (proj, y, hs, aux), _ = fwd_mix(h, sp["mix_norm_g"][l][None], win, vp, pw, br, bi)
        wo = wo.reshape(D, D)
        job = GatherJob(shards, l + 1) if exchange and l + 1 < DEPTH else None
        (h1, a, h2), fetched = fwd_post(h, y, wo, sp["mlp_norm_g"][l][None], wu, wd, job)
        if job is not None:
            gathered[l + 1] = fetched
        saved.append((h, proj, y, hs, aux, h1, a, vp, pw, br, bi, win, wo, wu, wd))
        h = h2
    dh, dgf, loss_part = fwd_loss(h, sp["final_norm_g"][None], tgt)

    big = [None] * DEPTH
    pair_io = None
    small_layers = [None] * DEPTH
    for l in reversed(range(DEPTH)):
        h0, proj, y, hs, aux, h1, a, vp, pw, br, bi, win, wo, wu, wd = saved[l]
        g1 = sp["mix_norm_g"][l][None]
        job = ExchangeJob(pair_io, [True] * 2) if exchange and pair_io is not None else None
        (dh1, da, xn2, dg2), parts_io = bwd_mlp_dx(dh, h1, a, sp["mlp_norm_g"][l][None], wu, wd, job)
        dwu, dwd = bwd_mlp_dw(xn2, da, a, dh)
        if not exchange:
            (dy, dwo), _ = bwd_out(dh1, y, wo)
            (dproj, ga, dpw, dwr, dwi), _ = bwd_mixer(dy, proj, hs, aux, vp, pw, br, bi)
            (dh, dwin, dg1), _ = bwd_in(dproj, h0, g1, win, dh1)
            big[l] = [dwin, dwo.reshape(NQ, D // NQ, D), dwu, dwd]
        else:
            jobs = Jobs([SwapJob([dwu, dwd])] + ([JoinJob(sum_chips(parts_io))] if job is not None else []))
            (dy, dwo), carried = bwd_out(dh1, y, wo, jobs)
            got_ud = jobs.split(carried)[0]
            if job is not None:
                big[l + 1] = jobs.split(carried)[1] + big[l + 1]
            (dproj, ga, dpw, dwr, dwi), parts_ud = bwd_mixer(dy, proj, hs, aux, vp, pw, br, bi,
                                                             ExchangeJob(add_pairs([dwu, dwd], got_ud, BF16), [True] * 2))
            (dh, dwin, dg1), big[l] = bwd_in(dproj, h0, g1, win, dh1, JoinJob(sum_chips(parts_ud)))
            arrs_io = [dwin, dwo.reshape(NQ, D // NQ, D)]
            pair_io = add_pairs(arrs_io, swap_halves(arrs_io), BF16)
        small_layers[l] = (ga, dpw, _diag_blocks(dwr), _diag_blocks(dwi), dg1[0], dg2[0])
    if exchange:
        big[0] = join_halves(sum_chips(run_job(ExchangeJob(pair_io, [True] * 2), "exchange_last"))) + big[0]
    return loss_part, dh, big, small_layers, dgf


def kernel(x, meta_tokens, mix_norm_g, w_in, pool_w, pool_b, pool_scale, conv_w, conv_b, gate_r_w, gate_r_b, gate_i_w, gate_i_b, lru_lambda, group_norm_g, w_out, mlp_norm_g, w_up, w_down, final_norm_g, loss_target, m_meta_tokens, m_mix_norm_g, m_w_in, m_pool_w, m_pool_b, m_pool_scale, m_conv_w, m_conv_b, m_gate_r_w, m_gate_r_b, m_gate_i_w, m_gate_i_b, m_lru_lambda, m_group_norm_g, m_w_out, m_mlp_norm_g, m_w_up, m_w_down, m_final_norm_g, v_meta_tokens, v_mix_norm_g, v_w_in, v_pool_w, v_pool_b, v_pool_scale, v_conv_w, v_conv_b, v_gate_r_w, v_gate_r_b, v_gate_i_w, v_gate_i_b, v_lru_lambda, v_group_norm_g, v_w_out, v_mlp_norm_g, v_w_up, v_w_down, v_final_norm_g):
    p = 2 * lax.axis_index("x") + lax.axis_index("y")

    shards = [w_in.astype(BF16), w_out.astype(BF16), w_up.astype(BF16), w_down.astype(BF16)]
    small = jnp.concatenate([meta_tokens, jnp.pad(conv_w.reshape(16, 128), ((0, 0), (0, 128)))], axis=0)
    small_g = gather_small(small)
    meta_full = jnp.transpose(small_g[:, :16, :], (1, 0, 2)).reshape(NMETA, D)
    conv_w_full = jnp.transpose(small_g[:, 16:, :128].reshape(NQ, DEPTH, 4, 128), (1, 2, 0, 3)).reshape(DEPTH, 4, DR)

    sp = dict(mix_norm_g=mix_norm_g, pool_w=pool_w, pool_b=pool_b, pool_scale=pool_scale, conv_b=conv_b, gate_r_w=gate_r_w,
              gate_r_b=gate_r_b, gate_i_w=gate_i_w, gate_i_b=gate_i_b, lru_lambda=lru_lambda, group_norm_g=group_norm_g,
              mlp_norm_g=mlp_norm_g, final_norm_g=final_norm_g)
    loss_part, dh, big, small_layers, dgf = local_step(x[0], loss_target[0], meta_full, conv_w_full, sp, shards=shards)
    loss = lax.psum(loss_part[0, 0], ("x", "y", "c"))
    grad_x = dh[NMETA:][None]

    small_sum = reduce_small(_pack_small(small_layers, dgf[0], dh[:NMETA])[None])
    g_layers, g_final, g_meta_full = _unpack_small(small_sum)

    g_vec = [gl[0] for gl in g_layers]
    grads = dict(
        meta_tokens=lax.dynamic_slice(g_meta_full, (0, p * (D // NQ)), (NMETA, D // NQ)),
        mix_norm_g=jnp.stack([gl[4] for gl in g_layers]),
        w_in=jnp.stack([big[l][0] for l in range(DEPTH)]),
        pool_w=jnp.stack([gl[1] for gl in g_layers]),
        pool_b=jnp.stack([gv[GA_PB] for gv in g_vec]),
        pool_scale=jnp.stack([gv[GA_PS] for gv in g_vec]),
        conv_w=lax.dynamic_slice(jnp.stack([gv[GA_CW:GA_CW + 4] for gv in g_vec]), (0, 0, p * 128), (DEPTH, 4, 128)),
        conv_b=jnp.stack([gv[GA_CB] for gv in g_vec]),
        gate_r_w=jnp.stack([gl[2] for gl in g_layers]),
        gate_r_b=jnp.stack([gv[GA_BR] for gv in g_vec]),
        gate_i_w=jnp.stack([gl[3] for gl in g_layers]),
        gate_i_b=jnp.stack([gv[GA_BI] for gv in g_vec]),
        lru_lambda=jnp.stack([gv[GA_LAM] for gv in g_vec]),
        group_norm_g=jnp.stack([jnp.concatenate([gv[GA_GNP], gv[GA_GNR]]) for gv in g_vec]),
        w_out=jnp.stack([big[l][1] for l in range(DEPTH)]),
        mlp_norm_g=jnp.stack([gl[5] for gl in g_layers]),
        w_up=jnp.stack([big[l][2] for l in range(DEPTH)]),
        w_down=jnp.stack([big[l][3] for l in range(DEPTH)]),
        final_norm_g=g_final,
    )
    weights = dict(meta_tokens=meta_tokens, mix_norm_g=mix_norm_g, w_in=w_in, pool_w=pool_w, pool_b=pool_b, pool_scale=pool_scale,
                   conv_w=conv_w, conv_b=conv_b, gate_r_w=gate_r_w, gate_r_b=gate_r_b, gate_i_w=gate_i_w, gate_i_b=gate_i_b,
                   lru_lambda=lru_lambda, group_norm_g=group_norm_g, w_out=w_out, mlp_norm_g=mlp_norm_g, w_up=w_up, w_down=w_down,
                   final_norm_g=final_norm_g)
    mom_m = dict(meta_tokens=m_meta_tokens, mix_norm_g=m_mix_norm_g, w_in=m_w_in, pool_w=m_pool_w, pool_b=m_pool_b,
                 pool_scale=m_pool_scale, conv_w=m_conv_w, conv_b=m_conv_b, gate_r_w=m_gate_r_w, gate_r_b=m_gate_r_b,
                 gate_i_w=m_gate_i_w, gate_i_b=m_gate_i_b, lru_lambda=m_lru_lambda, group_norm_g=m_group_norm_g, w_out=m_w_out,
                 mlp_norm_g=m_mlp_norm_g, w_up=m_w_up, w_down=m_w_down, final_norm_g=m_final_norm_g)
    mom_v = dict(meta_tokens=v_meta_tokens, mix_norm_g=v_mix_norm_g, w_in=v_w_in, pool_w=v_pool_w, pool_b=v_pool_b,
                 pool_scale=v_pool_scale, conv_w=v_conv_w, conv_b=v_conv_b, gate_r_w=v_gate_r_w, gate_r_b=v_gate_r_b,
                 gate_i_w=v_gate_i_w, gate_i_b=v_gate_i_b, lru_lambda=v_lru_lambda, group_norm_g=v_group_norm_g, w_out=v_w_out,
                 mlp_norm_g=v_mlp_norm_g, w_up=v_w_up, w_down=v_w_down, final_norm_g=v_final_norm_g)
    names = list(weights)

    delta, new_m, new_v = {}, {}, {}
    big_names = ("w_in", "w_out", "w_up", "w_down")
    for nm in big_names:
        shp = weights[nm].shape
        two_d = lambda t: t.reshape(shp[0] * shp[1], shp[2])
        d_, m_, v_ = adamw(two_d(weights[nm]), two_d(grads[nm]), two_d(mom_m[nm]), two_d(mom_v[nm]))
        delta[nm], new_m[nm], new_v[nm] = d_.reshape(shp), m_.reshape(shp), v_.reshape(shp)
    small_names = [nm for nm in names if nm not in big_names]
    sizes = [weights[nm].size for nm in small_names]
    total = sum(sizes)
    rows = -(-total // 512)
    rows = -(-rows // 16) * 16

    def flat(tree, fill):
        v_ = jnp.concatenate([tree[nm].reshape(-1) for nm in small_names])
        return jnp.concatenate([v_, jnp.full((rows * 512 - total,), fill, F32)]).reshape(rows, 512)

    d_, m_, v_ = adamw(flat(weights, 0.0), flat(grads, 0.0), flat(mom_m, 0.0), flat(mom_v, 1.0))
    o = 0
    for nm, sz in zip(small_names, sizes):
        shp = weights[nm].shape
        delta[nm] = d_.reshape(-1)[o:o + sz].reshape(shp)
        new_m[nm] = m_.reshape(-1)[o:o + sz].reshape(shp)
        new_v[nm] = v_.reshape(-1)[o:o + sz].reshape(shp)
        o += sz

    return (loss, grad_x, *[grads[nm] for nm in names], *[delta[nm] for nm in names],
            *[new_m[nm] for nm in names], *[new_v[nm] for nm in names])
```

```python
import jax
import jax.numpy as jnp
from jax import lax
from jax.experimental import pallas as pl
from jax.experimental.pallas import tpu as pltpu

F32 = jnp.float32
BF16 = jnp.bfloat16

D = 1024
DP = 512
DR = 512
DIN = 1536
DFF = 4096
DEPTH = 4
NMETA = 16
NQ = 4
WIN_S = DIN // NQ
FF_S = DFF // NQ
EPS = 1e-6
HALO = 16
TT = 432
TTW = 912
VMEM_LIMIT = 56 * 1024 * 1024

ADAM_LR = 0.001
ADAM_B1 = 0.9
ADAM_B2 = 0.999
ADAM_EPS = 1e-08
ADAM_WD = 0.01
ADAM_STEP = 10

MESH = pl.DeviceIdType.MESH
ANY = pl.BlockSpec(memory_space=pl.ANY)


def _params(sem=None, vmem=VMEM_LIMIT):
    return pltpu.CompilerParams(dimension_semantics=sem, vmem_limit_bytes=vmem)


def _resident(shape, index):
    return pl.BlockSpec(shape, index, pipeline_mode=pl.Buffered(1))


def _nt(x, w):
    return lax.dot_general(x, w, (((1,), (1,)), ((), ())), preferred_element_type=F32)


def _tn(a, b):
    return lax.dot_general(a, b, (((0,), (0,)), ((), ())), preferred_element_type=F32)


def _dot(x, w):
    return jnp.dot(x, w, preferred_element_type=F32)


def _rms(h):
    return lax.rsqrt(jnp.mean(h * h, axis=-1, keepdims=True) + EPS)


def _rms_bwd(n, r, dn):
    return r * (dn - n * jnp.mean(dn * n, axis=-1, keepdims=True))


def _down(x, s):
    return pltpu.roll(x, s, 0)


def _up(x, s):
    return pltpu.roll(x, x.shape[0] - s, 0)


GELU_C = 0.7978845608028654
GELU_K = 0.044715


def _gelu(x):
    t = jnp.tanh(GELU_C * (x + GELU_K * x * x * x))
    return 0.5 * x * (1.0 + t), t


def _softplus(x):
    return jnp.maximum(x, 0.0) + jnp.log1p(jnp.exp(-jnp.abs(x)))


VP_PB, VP_PS, VP_CB, VP_BR, VP_BI, VP_LAM, VP_CW, VP_GNP, VP_GNR = 0, 1, 2, 3, 4, 5, 6, 10, 11


def _row(vp, r):
    return vp[r:r + 1, :]


def _mixer_pre(ue, t0, vp, pw_ref, br_ref, bi_ref):
    tt = ue.shape[0] - HALO
    tf = (t0 + lax.broadcasted_iota(jnp.int32, (tt, 1), 0)).astype(F32) + 1.0
    pb, ps = _row(vp, VP_PB), _row(vp, VP_PS)
    pooled, mapped, inv_cnt = [], [], []
    for g in range(4):
        lanes = slice(128 * g, 128 * (g + 1))
        xe = ue[:, lanes]
        s = xe
        for j in range(g + 1):
            s = s + _down(s, 1 << j)
        inv = 1.0 / jnp.minimum(tf, float(2 << g))
        pg = s[HALO:] * inv - xe[HALO:]
        mg = _dot(pg.astype(BF16), pw_ref[g]) + pb[:, lanes]
        pooled.append(pg)
        mapped.append(mg)
        inv_cnt.append(inv)
    ypool = [mapped[g] * ps[:, 128 * g:128 * (g + 1)] for g in range(4)]

    xe = ue[:, DP:DP + DR]
    taps = [_down(xe, 3)[HALO:], _down(xe, 2)[HALO:], _down(xe, 1)[HALO:], xe[HALO:]]
    xc = _row(vp, VP_CB) + (taps[0] * _row(vp, VP_CW) + taps[1] * _row(vp, VP_CW + 1)
                            + taps[2] * _row(vp, VP_CW + 2) + taps[3] * _row(vp, VP_CW + 3))
    xcb = xc.astype(BF16)
    zr = jnp.concatenate([_dot(xcb[:, :256], br_ref[0]), _dot(xcb[:, 256:], br_ref[1])], axis=1) + _row(vp, VP_BR)
    zi = jnp.concatenate([_dot(xcb[:, :256], bi_ref[0]), _dot(xcb[:, 256:], bi_ref[1])], axis=1) + _row(vp, VP_BI)
    r = jax.nn.sigmoid(zr)
    ig = jax.nn.sigmoid(zi)
    sp = _softplus(-_row(vp, VP_LAM))
    la = (-8.0 * r) * sp
    a = jnp.exp(la)
    th = jnp.tanh(la)
    mult = jnp.sqrt((-2.0 * th) / (1.0 - th))
    gate = ue[HALO:, DP + DR:]
    gl, gt = _gelu(gate)
    return dict(pooled=pooled, mapped=mapped, ypool=ypool, inv_cnt=inv_cnt, taps=taps, xc=xc, xcb=xcb, r=r, ig=ig,
                sp=sp, a=a, mult=mult, gate=gate, gl=gl, gt=gt)


SUBLANES = 8


def _scan_fwd(a, b, h_in):
    tt = a.shape[0]
    sub = jnp.bitwise_and(lax.broadcasted_iota(jnp.int32, (tt, 1), 0), SUBLANES - 1)
    s = 1
    while s < SUBLANES:
        m = sub >= s
        a_s = jnp.where(m, _down(a, s), 1.0)
        b_s = jnp.where(m, _down(b, s), 0.0)
        b = a * b_s + b
        a = a * a_s
        s *= 2
    groups, h = [], h_in
    for g in range(tt // SUBLANES):
        rows = slice(SUBLANES * g, SUBLANES * (g + 1))
        hg = a[rows] * h + b[rows]
        groups.append(hg)
        h = hg[SUBLANES - 1:SUBLANES, :]
    return jnp.concatenate(groups, axis=0)


def _scan_rev(c, d, l_in):
    tt = c.shape[0]
    sub = jnp.bitwise_and(lax.broadcasted_iota(jnp.int32, (tt, 1), 0), SUBLANES - 1)
    s = 1
    while s < SUBLANES:
        m = sub < SUBLANES - s
        c_s = jnp.where(m, _up(c, s), 1.0)
        d_s = jnp.where(m, _up(d, s), 0.0)
        d = c * d_s + d
        c = c * c_s
        s *= 2
    groups, l = [], l_in
    for g in reversed(range(tt // SUBLANES)):
        rows = slice(SUBLANES * g, SUBLANES * (g + 1))
        lg = c[rows] * l + d[rows]
        groups.append(lg)
        l = lg[0:1, :]
    return jnp.concatenate(groups[::-1], axis=0)


def _halo_index(i, per_tile):
    return jnp.maximum(i * per_tile - 1, 0)


def _chip_slabs_to_columns(w_ref, wfull_ref):
    for q in range(NQ):
        wfull_ref[:, WIN_S * q:WIN_S * (q + 1)] = w_ref[q]


AUX_POOLED, AUX_MAPPED, AUX_XC, AUX_R, AUX_IG, AUX_A, AUX_MULT, AUX_GL, AUX_GT = range(9)
AUX_W = 9 * 512


def _aux(k):
    return slice(512 * k, 512 * (k + 1))


def fwd_mix(h, g, w_g, vp, pw, br, bi, job=None, meta=None):
    T = h.shape[0] + (NMETA if meta is not None else 0)
    n_h = 3 if meta is not None else 1

    def body(*refs):
        g_ref, w_ref, vp_ref, pw_ref, br_ref, bi_ref, p_ref, y_ref, hs_ref, aux_ref = refs[n_h:n_h + 10]
        wfull_ref, halo_ref, carry_ref = refs[-3:]
        i = pl.program_id(0)

        @pl.when(i == 0)
        def _():
            _chip_slabs_to_columns(w_ref, wfull_ref)
            carry_ref[...] = jnp.zeros_like(carry_ref)
            halo_ref[...] = jnp.zeros_like(halo_ref)

        if meta is not None:
            xp_ref, x_ref, meta_ref = refs[:n_h]
            hh = jnp.concatenate([jnp.where(i == 0, meta_ref[...], xp_ref[...]), x_ref[0:TT - HALO, :]], axis=0)
            refs[n_h + 10][...] = hh
        else:
            hh = refs[0][...]
        xn = (hh * _rms(hh) * g_ref[...]).astype(BF16)
        proj = _dot(xn, wfull_ref[...])
        p_ref[...] = proj
        ue = jnp.concatenate([halo_ref[...], proj], axis=0)
        halo_ref[...] = p_ref[TT - HALO:TT, :]
        vp_v = vp_ref[...]
        m = _mixer_pre(ue, i * TT, vp_v, pw_ref, br_ref, bi_ref)
        for g in range(4):
            aux_ref[:, 512 * AUX_POOLED + 128 * g:512 * AUX_POOLED + 128 * (g + 1)] = m["pooled"][g]
            aux_ref[:, 512 * AUX_MAPPED + 128 * g:512 * AUX_MAPPED + 128 * (g + 1)] = m["mapped"][g]
        for k, name in ((AUX_XC, "xc"), (AUX_R, "r"), (AUX_IG, "ig"), (AUX_A, "a"), (AUX_MULT, "mult"), (AUX_GL, "gl"),
                        (AUX_GT, "gt")):
            aux_ref[:, _aux(k)] = m[name]
        b = m["mult"] * (m["ig"] * m["xc"])
        hs = _scan_fwd(m["a"], b, carry_ref[0:1, :])
        hs_ref[...] = hs
        carry_ref[0:1, :] = hs_ref[TT - 1:TT, :]
        yr = hs * m["gl"]
        ssq = sum(jnp.sum(yp * yp, axis=-1, keepdims=True) for yp in m["ypool"])
        rp = lax.rsqrt(ssq * (1.0 / DP) + EPS)
        gnp = _row(vp_v, VP_GNP)
        for g in range(4):
            lanes = slice(128 * g, 128 * (g + 1))
            y_ref[:, lanes] = (m["ypool"][g] * rp * gnp[:, lanes]).astype(BF16)
        y_ref[:, DP:] = (yr * _rms(yr) * _row(vp_v, VP_GNR)).astype(BF16)

    if meta is not None:
        h_args, h_specs = (h, h, meta), _shifted_specs() + [_resident((NMETA, D), lambda i: (0, 0))]
    else:
        h_args, h_specs = (h,), [pl.BlockSpec((TT, D), lambda i: (i, 0))]
    full = meta is not None
    return tiled_call(
        "fwd_mix", T // TT, body, h_args + (g, w_g, vp, pw, br, bi),
        in_specs=h_specs + [_resident((1, D), lambda i: (0, 0)), _resident((NQ, D, WIN_S), lambda i: (0, 0, 0)),
                            _resident((16, 512), lambda i: (0, 0)), _resident((4, 128, 128), lambda i: (0, 0, 0)),
                            _resident((2, 256, 256), lambda i: (0, 0, 0)), _resident((2, 256, 256), lambda i: (0, 0, 0))],
        out_specs=[pl.BlockSpec((TT, DIN), lambda i: (i, 0)), pl.BlockSpec((TT, D), lambda i: (i, 0)),
                   pl.BlockSpec((TT, DR), lambda i: (i, 0)), pl.BlockSpec((TT, AUX_W), lambda i: (i, 0))]
        + [pl.BlockSpec((TT, D), lambda i: (i, 0))] * full,
        out_shape=[jax.ShapeDtypeStruct((T, DIN), F32), jax.ShapeDtypeStruct((T, D), BF16),
                   jax.ShapeDtypeStruct((T, DR), F32), jax.ShapeDtypeStruct((T, AUX_W), F32)]
        + [jax.ShapeDtypeStruct((T, D), F32)] * full,
        scratch=[pltpu.VMEM((D, DIN), BF16), pltpu.VMEM((HALO, DIN), F32), pltpu.VMEM((8, DR), F32)],
        job=job, mid_step=T // TT - 1)


def fwd_post(h, y, wo_g, g2, wu_g, wd_g, job=None):
    T = h.shape[0]

    def body(h_ref, y_ref, wo_ref, g_ref, wu_ref, wd_ref, h1_ref, a_ref, h2_ref):
        h1 = h_ref[...] + _dot(y_ref[...], wo_ref[...])
        h1_ref[...] = h1
        xn = (h1 * _rms(h1) * g_ref[...]).astype(BF16)
        acc = h1
        for q in range(NQ):
            a = _dot(xn, wu_ref[q])
            a_ref[:, FF_S * q:FF_S * (q + 1)] = a.astype(BF16)
            ra = jnp.maximum(a, 0.0)
            acc = acc + _dot((ra * ra).astype(BF16), wd_ref[q])
        h2_ref[...] = acc

    return tiled_call(
        "fwd_post", T // TT, body, (h, y, wo_g, g2, wu_g, wd_g),
        in_specs=[pl.BlockSpec((TT, D), lambda i: (i, 0)), pl.BlockSpec((TT, D), lambda i: (i, 0)),
                  _resident((D, D), lambda i: (0, 0)), _resident((1, D), lambda i: (0, 0)),
                  _resident((NQ, D, FF_S), lambda i: (0, 0, 0)), _resident((NQ, FF_S, D), lambda i: (0, 0, 0))],
        out_specs=[pl.BlockSpec((TT, D), lambda i: (i, 0)), pl.BlockSpec((TT, DFF), lambda i: (i, 0)),
                   pl.BlockSpec((TT, D), lambda i: (i, 0))],
        out_shape=[jax.ShapeDtypeStruct((T, D), F32), jax.ShapeDtypeStruct((T, DFF), BF16),
                   jax.ShapeDtypeStruct((T, D), F32)],
        job=job, mid_step=(T // TT) * 3 // 4)


def _shifted_tile(prev_ref, cur_ref):
    return jnp.concatenate([prev_ref[...], cur_ref[0:TT - HALO, :]], axis=0)


def _shifted_specs():
    per = TT // HALO
    return [pl.BlockSpec((HALO, D), lambda i: (_halo_index(i, per), 0)), pl.BlockSpec((TT, D), lambda i: (i, 0))]


def fwd_loss(h, gf, tgt):
    T = h.shape[0]

    def body(h_ref, g_ref, tp_ref, t_ref, dh_ref, dg_ref, loss_ref):
        i = pl.program_id(0)

        @pl.when(i == 0)
        def _():
            dg_ref[...] = jnp.zeros_like(dg_ref)
            loss_ref[...] = jnp.zeros_like(loss_ref)

        hh = h_ref[...]
        r = _rms(hh)
        n = hh * r
        gfv = g_ref[...]
        row = i * TT + lax.broadcasted_iota(jnp.int32, (TT, 1), 0)
        e = jnp.where(row >= NMETA, n * gfv - _shifted_tile(tp_ref, t_ref), 0.0)
        loss_ref[...] += 0.5 * jnp.sum(jnp.sum(e * e, axis=-1, keepdims=True) * (1.0 / D), axis=0, keepdims=True)
        dy = e * (1.0 / D)
        dg_ref[...] += jnp.sum(dy * n, axis=0, keepdims=True)
        dh_ref[...] = _rms_bwd(n, r, dy * gfv)

    return tiled_call(
        "fwd_loss", T // TT, body, (h, gf, tgt, tgt),
        in_specs=[pl.BlockSpec((TT, D), lambda i: (i, 0)), _resident((1, D), lambda i: (0, 0))] + _shifted_specs(),
        out_specs=[pl.BlockSpec((TT, D), lambda i: (i, 0)), pl.BlockSpec((1, D), lambda i: (0, 0)),
                   pl.BlockSpec((1, 1), lambda i: (0, 0))],
        out_shape=[jax.ShapeDtypeStruct((T, D), F32), jax.ShapeDtypeStruct((1, D), F32),
                   jax.ShapeDtypeStruct((1, 1), F32)])[0]


def bwd_mlp_dx(dh2, h1, a, g2, wu_g, wd_g, job=None):
    T = dh2.shape[0]

    def body(dh2_ref, h1_ref, a_ref, g_ref, wu_ref, wd_ref, dh1_ref, da_ref, xn_ref, dg_ref):
        @pl.when(pl.program_id(0) == 0)
        def _():
            dg_ref[...] = jnp.zeros_like(dg_ref)

        h1v = h1_ref[...]
        r = _rms(h1v)
        n = h1v * r
        gv = g_ref[...]
        xn_ref[...] = (n * gv).astype(BF16)
        dh2v = dh2_ref[...]
        dh2b = dh2v.astype(BF16)
        dxn = jnp.zeros((TT, D), F32)
        for q in range(NQ):
            cols = slice(FF_S * q, FF_S * (q + 1))
            ra = jnp.maximum(a_ref[:, cols].astype(F32), 0.0)
            da = (_nt(dh2b, wd_ref[q]) * (2.0 * ra)).astype(BF16)
            da_ref[:, cols] = da
            dxn = dxn + _nt(da, wu_ref[q])
        dg_ref[...] += jnp.sum(dxn * n, axis=0, keepdims=True)
        dh1_ref[...] = dh2v + _rms_bwd(n, r, dxn * gv)

    return tiled_call(
        "bwd_mlp_dx", T // TT, body, (dh2, h1, a, g2, wu_g, wd_g),
        in_specs=[pl.BlockSpec((TT, D), lambda i: (i, 0)), pl.BlockSpec((TT, D), lambda i: (i, 0)),
                  pl.BlockSpec((TT, DFF), lambda i: (i, 0)), _resident((1, D), lambda i: (0, 0)),
                  _resident((NQ, D, FF_S), lambda i: (0, 0, 0)), _resident((NQ, FF_S, D), lambda i: (0, 0, 0))],
        out_specs=[pl.BlockSpec((TT, D), lambda i: (i, 0)), pl.BlockSpec((TT, DFF), lambda i: (i, 0)),
                   pl.BlockSpec((TT, D), lambda i: (i, 0)), pl.BlockSpec((1, D), lambda i: (0, 0))],
        out_shape=[jax.ShapeDtypeStruct((T, D), F32), jax.ShapeDtypeStruct((T, DFF), BF16),
                   jax.ShapeDtypeStruct((T, D), BF16), jax.ShapeDtypeStruct((1, D), F32)],
        job=job)


def bwd_mlp_dw(xn, da, a, dh2):
    T = xn.shape[0]

    def body(xn_ref, da_ref, a_ref, dh2_ref, dwu_ref, dwd_ref):
        @pl.when(pl.program_id(1) == 0)
        def _():
            dwu_ref[...] = jnp.zeros_like(dwu_ref)
            dwd_ref[...] = jnp.zeros_like(dwd_ref)

        dwu_ref[...] += _tn(xn_ref[...], da_ref[...])
        ra = jnp.maximum(a_ref[...].astype(F32), 0.0)
        dwd_ref[...] += _tn((ra * ra).astype(BF16), dh2_ref[...].astype(BF16))

    return pl.pallas_call(
        body, name="bwd_mlp_dw", grid=(NQ, T // TTW),
        in_specs=[pl.BlockSpec((TTW, D), lambda q, i: (i, 0)), pl.BlockSpec((TTW, FF_S), lambda q, i: (i, q)),
                  pl.BlockSpec((TTW, FF_S), lambda q, i: (i, q)), pl.BlockSpec((TTW, D), lambda q, i: (i, 0))],
        out_specs=[pl.BlockSpec((None, D, FF_S), lambda q, i: (q, 0, 0)),
                   pl.BlockSpec((None, FF_S, D), lambda q, i: (q, 0, 0))],
        out_shape=[jax.ShapeDtypeStruct((NQ, D, FF_S), F32), jax.ShapeDtypeStruct((NQ, FF_S, D), F32)],
        compiler_params=_params(("arbitrary", "arbitrary")),
    )(xn, da, a, dh2)


def bwd_out(dh1, y, wo_g, job=None):
    T = dh1.shape[0]

    def body(dh_ref, y_ref, wo_ref, dy_ref, dwo_ref):
        @pl.when(pl.program_id(0) == 0)
        def _():
            dwo_ref[...] = jnp.zeros_like(dwo_ref)

        dhb = dh_ref[...].astype(BF16)
        dy_ref[...] = _nt(dhb, wo_ref[...])
        dwo_ref[...] += _tn(y_ref[...], dhb)

    return tiled_call(
        "bwd_out", T // TT, body, (dh1, y, wo_g),
        in_specs=[pl.BlockSpec((TT, D), lambda i: (i, 0)), pl.BlockSpec((TT, D), lambda i: (i, 0)),
                  _resident((D, D), lambda i: (0, 0))],
        out_specs=[pl.BlockSpec((TT, D), lambda i: (i, 0)), pl.BlockSpec((D, D), lambda i: (0, 0))],
        out_shape=[jax.ShapeDtypeStruct((T, D), F32), jax.ShapeDtypeStruct((D, D), F32)], job=job)


GA_PS, GA_PB, GA_CB, GA_BR, GA_BI, GA_LAM, GA_CW, GA_GNP, GA_GNR = 0, 1, 2, 3, 4, 5, 6, 10, 11


def bwd_mixer(dy, proj, hs, aux, vp, pw, br, bi, job=None):
    T = proj.shape[0]
    tt = TT
    nt = T // tt
    per = tt // HALO

    def body(dy_ref, p_ref, ph_ref, hs_ref, hsh_ref, aux_ref, vp_ref, pw_ref, br_ref, bi_ref,
             dp_ref, ga_ref, dpw_ref, dwr_ref, dwi_ref, lam_ref, q_ref, dxc_ref):
        s = pl.program_id(0)
        ti = nt - 1 - s

        @pl.when(s == 0)
        def _():
            for ref in (ga_ref, dpw_ref, dwr_ref, dwi_ref, lam_ref, q_ref, dxc_ref):
                ref[...] = jnp.zeros_like(ref)

        dyv = dy_ref[...]
        lam_in, q_in, dxc_in = lam_ref[0:1, :], q_ref[...], dxc_ref[...]
        first = ti > 0
        vp_v = vp_ref[...]
        ur = jnp.concatenate([jnp.where(first, ph_ref[:, DP:DP + DR], 0.0), p_ref[:, DP:DP + DR]], axis=0)
        tf = (ti * tt + lax.broadcasted_iota(jnp.int32, (tt, 1), 0)).astype(F32) + 1.0
        mapped = [aux_ref[:, 512 * AUX_MAPPED + 128 * g:512 * AUX_MAPPED + 128 * (g + 1)] for g in range(4)]
        ps_row = _row(vp_v, VP_PS)
        xc = aux_ref[:, _aux(AUX_XC)]
        m = dict(pooled=[aux_ref[:, 512 * AUX_POOLED + 128 * g:512 * AUX_POOLED + 128 * (g + 1)] for g in range(4)],
                 mapped=mapped, ypool=[mapped[g] * ps_row[:, 128 * g:128 * (g + 1)] for g in range(4)],
                 inv_cnt=[1.0 / jnp.minimum(tf, float(2 << g)) for g in range(4)],
                 taps=[_down(ur, 3)[HALO:], _down(ur, 2)[HALO:], _down(ur, 1)[HALO:], ur[HALO:]],
                 xc=xc, xcb=xc.astype(BF16), r=aux_ref[:, _aux(AUX_R)], ig=aux_ref[:, _aux(AUX_IG)],
                 a=aux_ref[:, _aux(AUX_A)], mult=aux_ref[:, _aux(AUX_MULT)], gl=aux_ref[:, _aux(AUX_GL)],
                 gt=aux_ref[:, _aux(AUX_GT)], gate=p_ref[:, DP + DR:], sp=_softplus(-_row(vp_v, VP_LAM)))
        hs_v = hs_ref[...]
        hprev = _down(jnp.concatenate([jnp.where(first, hsh_ref[...], 0.0), hs_v], axis=0), 1)[HALO:]

        def acc(rw, v):
            ga_ref[rw:rw + 1, :] += jnp.sum(v, axis=0, keepdims=True)

        gnp = _row(vp_v, VP_GNP)
        ps = _row(vp_v, VP_PS)
        ssq = sum(jnp.sum(yp * yp, axis=-1, keepdims=True) for yp in m["ypool"])
        rp = lax.rsqrt(ssq * (1.0 / DP) + EPS)
        npool = [yp * rp for yp in m["ypool"]]
        dnp = [dyv[:, 128 * g:128 * (g + 1)] * gnp[:, 128 * g:128 * (g + 1)] for g in range(4)]
        mean_dn = sum(jnp.sum(dnp[g] * npool[g], axis=-1, keepdims=True) for g in range(4)) * (1.0 / DP)
        for g in range(4):
            lanes = slice(128 * g, 128 * (g + 1))
            ga_ref[GA_GNP:GA_GNP + 1, lanes] += jnp.sum(dyv[:, lanes] * npool[g], axis=0, keepdims=True)
            dyp = rp * (dnp[g] - npool[g] * mean_dn)
            ga_ref[GA_PS:GA_PS + 1, lanes] += jnp.sum(dyp * m["mapped"][g], axis=0, keepdims=True)
            dmap = dyp * ps[:, lanes]
            ga_ref[GA_PB:GA_PB + 1, lanes] += jnp.sum(dmap, axis=0, keepdims=True)
            dmb = dmap.astype(BF16)
            dpw_ref[g] += _tn(m["pooled"][g].astype(BF16), dmb)
            dpool = _nt(dmb, pw_ref[g])
            qv = dpool * m["inv_cnt"][g]
            win = jnp.concatenate([qv, q_in[:, lanes]], axis=0)
            for j in range(g + 1):
                win = win + _up(win, 1 << j)
            q_ref[:, lanes] = qv[:HALO]
            dp_ref[:, lanes] = (win[:tt] - dpool).astype(BF16)

        gnr = _row(vp_v, VP_GNR)
        yr = hs_v * m["gl"]
        rr = _rms(yr)
        nr = yr * rr
        dyr_out = dyv[:, DP:]
        acc(GA_GNR, dyr_out * nr)
        dyr = _rms_bwd(nr, rr, dyr_out * gnr)
        gate, gt = m["gate"], m["gt"]
        dgl = 0.5 * (1.0 + gt) + 0.5 * gate * (1.0 - gt * gt) * (GELU_C * (1.0 + 3.0 * GELU_K * gate * gate))
        dp_ref[:, DP + DR:] = (dyr * hs_v * dgl).astype(BF16)
        dhs = dyr * m["gl"]
        a = m["a"]
        row = lax.broadcasted_iota(jnp.int32, (tt, 1), 0)
        c_next = jnp.where(row < tt - 1, _up(a, 1), 1.0)
        lam = _scan_rev(c_next, dhs, lam_in)
        lam_ref[0:1, :] = a[0:1, :] * lam[0:1, :]
        xc, ig, r, mult = m["xc"], m["ig"], m["r"], m["mult"]
        dmult = lam * ig * xc
        dig = lam * mult * xc
        dxc = lam * mult * ig
        dla = lam * hprev * a - dmult * (a * a) / mult
        acc(GA_LAM, dla * r)
        dzr = (dla * (-8.0 * m["sp"])) * (r * (1.0 - r))
        dzi = dig * (ig * (1.0 - ig))
        acc(GA_BR, dzr)
        acc(GA_BI, dzi)
        dzrb, dzib = dzr.astype(BF16), dzi.astype(BF16)
        xcb = m["xcb"]
        halves = []
        for k in range(2):
            lanes = slice(256 * k, 256 * (k + 1))
            dwr_ref[k] += _tn(xcb[:, lanes], dzrb[:, lanes])
            dwi_ref[k] += _tn(xcb[:, lanes], dzib[:, lanes])
            halves.append(_nt(dzrb[:, lanes], br_ref[k]) + _nt(dzib[:, lanes], bi_ref[k]))
        dxc = dxc + jnp.concatenate(halves, axis=1)
        acc(GA_CB, dxc)
        for k in range(4):
            acc(GA_CW + k, dxc * m["taps"][k])
        dxe = jnp.concatenate([dxc, dxc_in], axis=0)
        du = (_up(dxe, 3)[:tt] * _row(vp_v, VP_CW) + _up(dxe, 2)[:tt] * _row(vp_v, VP_CW + 1)
              + _up(dxe, 1)[:tt] * _row(vp_v, VP_CW + 2) + dxc * _row(vp_v, VP_CW + 3))
        dxc_ref[...] = dxc[:HALO]
        dp_ref[:, DP:DP + DR] = du.astype(BF16)

        @pl.when(s == nt - 1)
        def _():
            lamp = _row(vp_v, VP_LAM)
            ga_ref[GA_LAM:GA_LAM + 1, :] = ga_ref[GA_LAM:GA_LAM + 1, :] * (8.0 * jax.nn.sigmoid(-lamp))

    rev = lambda i: (nt - 1 - i, 0)
    rev_halo = lambda i: (_halo_index(nt - 1 - i, per), 0)
    return tiled_call(
        "bwd_mixer", nt, body, (dy, proj, proj, hs, hs, aux, vp, pw, br, bi),
        in_specs=[pl.BlockSpec((tt, D), rev), pl.BlockSpec((tt, DIN), rev), pl.BlockSpec((HALO, DIN), rev_halo),
                  pl.BlockSpec((tt, DR), rev), pl.BlockSpec((HALO, DR), rev_halo), pl.BlockSpec((tt, AUX_W), rev),
                  _resident((16, 512), lambda i: (0, 0)), _resident((4, 128, 128), lambda i: (0, 0, 0)),
                  _resident((2, 256, 256), lambda i: (0, 0, 0)), _resident((2, 256, 256), lambda i: (0, 0, 0))],
        out_specs=[pl.BlockSpec((tt, DIN), rev), pl.BlockSpec((16, 512), lambda i: (0, 0)),
                   pl.BlockSpec((4, 128, 128), lambda i: (0, 0, 0)), pl.BlockSpec((2, 256, 256), lambda i: (0, 0, 0)),
                   pl.BlockSpec((2, 256, 256), lambda i: (0, 0, 0))],
        out_shape=[jax.ShapeDtypeStruct((T, DIN), BF16), jax.ShapeDtypeStruct((16, 512), F32),
                   jax.ShapeDtypeStruct((4, 128, 128), F32), jax.ShapeDtypeStruct((2, 256, 256), F32),
                   jax.ShapeDtypeStruct((2, 256, 256), F32)],
        scratch=[pltpu.VMEM((8, DR), F32), pltpu.VMEM((HALO, DP), F32), pltpu.VMEM((HALO, DR), F32)], job=job)


def bwd_in(dproj, h, g1, w_g, dh1, job=None):
    T = h.shape[0]

    def body(dp_ref, h_ref, g_ref, w_ref, dh1_ref, dh_ref, dw_ref, dg_ref, wfull_ref, acc_ref):
        i = pl.program_id(0)

        @pl.when(i == 0)
        def _():
            _chip_slabs_to_columns(w_ref, wfull_ref)
            acc_ref[...] = jnp.zeros_like(acc_ref)
            dg_ref[...] = jnp.zeros_like(dg_ref)

        hv = h_ref[...]
        r = _rms(hv)
        n = hv * r
        gv = g_ref[...]
        xn = (n * gv).astype(BF16)
        dpv = dp_ref[...]
        dxn = _nt(dpv, wfull_ref[...])
        acc_ref[...] += _tn(xn, dpv)
        dg_ref[...] += jnp.sum(dxn * n, axis=0, keepdims=True)
        dh_ref[...] = dh1_ref[...] + _rms_bwd(n, r, dxn * gv)

        @pl.when(i == T // TT - 1)
        def _():
            for q in range(NQ):
                dw_ref[q] = acc_ref[:, WIN_S * q:WIN_S * (q + 1)]

    return tiled_call(
        "bwd_in", T // TT, body, (dproj, h, g1, w_g, dh1),
        in_specs=[pl.BlockSpec((TT, DIN), lambda i: (i, 0)), pl.BlockSpec((TT, D), lambda i: (i, 0)),
                  _resident((1, D), lambda i: (0, 0)), _resident((NQ, D, WIN_S), lambda i: (0, 0, 0)),
                  pl.BlockSpec((TT, D), lambda i: (i, 0))],
        out_specs=[pl.BlockSpec((TT, D), lambda i: (i, 0)), pl.BlockSpec((NQ, D, WIN_S), lambda i: (0, 0, 0)),
                   pl.BlockSpec((1, D), lambda i: (0, 0))],
        out_shape=[jax.ShapeDtypeStruct((T, D), F32), jax.ShapeDtypeStruct((NQ, D, WIN_S), F32),
                   jax.ShapeDtypeStruct((1, D), F32)],
        scratch=[pltpu.VMEM((D, DIN), BF16), pltpu.VMEM((D, DIN), F32)], job=job)


def _row_block(rows, cols, itemsize=4, budget=2 * 1024 * 1024):
    best = None
    for b in range(16, rows + 1, 16):
        if rows % b == 0 and b * cols * itemsize <= budget:
            best = b
    return best if best is not None else rows


def add_pairs(full, got, dtype):
    core = lax.axis_index("c").astype(jnp.int32).reshape(1)
    outs = []
    for k in range(len(full)):
        q, hr, c = got[k].shape
        rb = _row_block(hr, c)
        nb = hr // rb

        def body(c_ref, a_ref, b_ref, o_ref):
            o_ref[...] = (a_ref[...] + b_ref[...]).astype(dtype)

        outs.append(pl.pallas_call(
            body, name="add_pairs",
            grid_spec=pltpu.PrefetchScalarGridSpec(
                num_scalar_prefetch=1, grid=(q, nb),
                in_specs=[pl.BlockSpec((None, rb, c), lambda qi, i, c_ref, nb=nb: (qi, c_ref[0] * nb + i, 0)),
                          pl.BlockSpec((None, rb, c), lambda qi, i, c_ref: (qi, i, 0))],
                out_specs=pl.BlockSpec((None, rb, c), lambda qi, i, c_ref: (qi, i, 0))),
            out_shape=jax.ShapeDtypeStruct((q, hr, c), dtype),
            compiler_params=_params(("arbitrary", "arbitrary")),
        )(core, full[k], got[k]))
    return outs


def sum_chips(parts):
    core = lax.axis_index("c").astype(jnp.int32).reshape(1)
    outs = []
    for p in parts:
        _, hr, c = p.shape
        rb = _row_block(hr, c)
        nb = hr // rb

        def body(c_ref, p_ref, o_ref):
            s = p_ref[0].astype(F32) + p_ref[1].astype(F32)
            s = s + p_ref[2].astype(F32)
            o_ref[...] = s + p_ref[3].astype(F32)

        outs.append(pl.pallas_call(
            body, name="sum_chips",
            grid_spec=pltpu.PrefetchScalarGridSpec(
                num_scalar_prefetch=1, grid=(nb,),
                in_specs=[pl.BlockSpec((NQ, rb, c), lambda i, c_ref: (0, i, 0))],
                out_specs=pl.BlockSpec((rb, c), lambda i, c_ref, nb=nb: (c_ref[0] * nb + i, 0))),
            out_shape=jax.ShapeDtypeStruct((2 * hr, c), F32),
            compiler_params=_params(("arbitrary",)),
        )(core, p))
    return outs


def adamw(w, g, m, v):
    r, c = w.shape
    rb = _row_block(r, c, budget=1024 * 1024)
    c1 = 1.0 / (1.0 - ADAM_B1 ** ADAM_STEP)
    c2 = 1.0 / (1.0 - ADAM_B2 ** ADAM_STEP)

    def body(w_ref, g_ref, m_ref, v_ref, d_ref, nm_ref, nv_ref):
        gv = g_ref[...]
        nm = ADAM_B1 * m_ref[...] + (1.0 - ADAM_B1) * gv
        nv = ADAM_B2 * v_ref[...] + (1.0 - ADAM_B2) * (gv * gv)
        nm_ref[...] = nm
        nv_ref[...] = nv
        d_ref[...] = -ADAM_LR * ((nm * c1) / (jnp.sqrt(nv * c2) + ADAM_EPS) + ADAM_WD * w_ref[...])

    spec = pl.BlockSpec((rb, c), lambda i: (i, 0))
    return pl.pallas_call(
        body, name="adamw", grid=(r // rb,), in_specs=[spec] * 4, out_specs=[spec] * 3,
        out_shape=[jax.ShapeDtypeStruct((r, c), F32)] * 3,
        compiler_params=_params(("arbitrary",)),
    )(w, g, m, v)


def _place():
    return lax.axis_index("x"), lax.axis_index("y"), lax.axis_index("c")


def _other_chips(x, y):
    return [(1 - x, y), (x, 1 - y), (1 - x, 1 - y)]


LOCAL_CHUNKS = 4
ICI_CHUNKS = 2
FWD_CHUNKS = 8


def _start_remote(src_rows, dst_rows, rows, chunks, send_sem, recv_sem, dev):
    rc = rows // chunks
    for j in range(chunks):
        pltpu.make_async_remote_copy(src_rows(j * rc, rc), dst_rows(j * rc, rc), send_sem, recv_sem,
                                     device_id=dev, device_id_type=MESH).start()


def _waiter(src, dst, send_sem, recv_sem):
    x, y, c = _place()
    return pltpu.make_async_remote_copy(src, dst, send_sem, recv_sem, device_id=(x, y, c), device_id_type=MESH)


class GatherJob:
    def __init__(self, shards, layer):
        self.shards, self.layer, self.n = list(shards), layer, len(shards)
        self.operands = list(shards)
        self.out_shape = [jax.ShapeDtypeStruct((NQ,) + s.shape[1:], s.dtype) for s in shards]
        sems = pltpu.SemaphoreType.DMA((self.n, 3))
        self.scratch = [sems, sems, sems, sems, pltpu.SemaphoreType.DMA((self.n,))]

    def _half(self, k):
        return self.shards[k].shape[1] // 2

    def _src(self, ins, k, half):
        hr = self._half(k)
        return lambda r0, nr: ins[k].at[self.layer, pl.ds(half * hr + r0, nr), :]

    def _dst(self, outs, k, chip, half):
        hr = self._half(k)
        return lambda r0, nr: outs[k].at[2 * chip[0] + chip[1], pl.ds(half * hr + r0, nr), :]

    def start(self, ins, outs, scr):
        send, recv, fsend, frecv, lsem = scr
        x, y, c = _place()
        for k in range(self.n):
            rows = self.shards[k].shape[1]
            rc = rows // LOCAL_CHUNKS
            for j in range(LOCAL_CHUNKS):
                pltpu.make_async_copy(ins[k].at[self.layer, pl.ds(j * rc, rc), :],
                                      outs[k].at[2 * x + y, pl.ds(j * rc, rc), :], lsem.at[k]).start()
        for k in range(self.n):
            for j, chip in enumerate(_other_chips(x, y)):
                _start_remote(self._src(ins, k, c), self._dst(outs, k, (x, y), c), self._half(k), ICI_CHUNKS,
                              send.at[k, j], recv.at[k, j], (chip[0], chip[1], c))

    def mid(self, ins, outs, scr):
        send, recv, fsend, frecv, lsem = scr
        x, y, c = _place()
        for k in range(self.n):
            hr = self._half(k)
            for j, chip in enumerate(_other_chips(x, y)):
                got = self._dst(outs, k, chip, c)
                _waiter(got(0, hr), got(0, hr), send.at[k, j], recv.at[k, j]).wait_recv()
                _start_remote(got, got, hr, FWD_CHUNKS, fsend.at[k, j], frecv.at[k, j], (x, y, 1 - c))

    def finish(self, ins, outs, scr):
        send, recv, fsend, frecv, lsem = scr
        x, y, c = _place()
        for k in range(self.n):
            hr = self._half(k)
            for j, chip in enumerate(_other_chips(x, y)):
                theirs = self._dst(outs, k, chip, 1 - c)(0, hr)
                w = _waiter(theirs, theirs, fsend.at[k, j], frecv.at[k, j])
                w.wait_recv()
                w.wait_send()
                _waiter(theirs, theirs, send.at[k, j], recv.at[k, j]).wait_send()
            pltpu.make_async_copy(ins[k].at[self.layer], outs[k].at[2 * x + y], lsem.at[k]).wait()


class ExchangeJob:
    def __init__(self, arrs, scatter):
        self.arrs, self.scatter, self.n = list(arrs), list(scatter), len(arrs)
        self.operands = list(arrs)
        self.out_shape = [jax.ShapeDtypeStruct((NQ,) + a.shape[1:], a.dtype) for a in arrs]
        sems = pltpu.SemaphoreType.DMA((self.n, 3))
        self.scratch = [sems, sems, pltpu.SemaphoreType.DMA((self.n,))]

    def _slot(self, ref, s):
        return lambda r0, nr: ref.at[s, pl.ds(r0, nr), :]

    def start(self, ins, outs, scr):
        send, recv, lsem = scr
        x, y, c = _place()
        p = 2 * x + y
        for k in range(self.n):
            rows = self.arrs[k].shape[1]
            rc = rows // ICI_CHUNKS
            for j in range(ICI_CHUNKS):
                pltpu.make_async_copy(ins[k].at[p if self.scatter[k] else 0, pl.ds(j * rc, rc), :],
                                      outs[k].at[p, pl.ds(j * rc, rc), :], lsem.at[k]).start()
            for j, chip in enumerate(_other_chips(x, y)):
                q = 2 * chip[0] + chip[1]
                _start_remote(self._slot(ins[k], q if self.scatter[k] else 0), self._slot(outs[k], p), rows, ICI_CHUNKS,
                              send.at[k, j], recv.at[k, j], (chip[0], chip[1], c))

    def mid(self, ins, outs, scr):
        pass

    def finish(self, ins, outs, scr):
        send, recv, lsem = scr
        x, y, c = _place()
        for k in range(self.n):
            for j, chip in enumerate(_other_chips(x, y)):
                slot = outs[k].at[2 * chip[0] + chip[1]]
                w = _waiter(slot, slot, send.at[k, j], recv.at[k, j])
                w.wait_recv()
                w.wait_send()
            pltpu.make_async_copy(ins[k].at[0], outs[k].at[0], lsem.at[k]).wait()


def run_job(job, name):
    n_in, n_out = len(job.operands), len(job.out_shape)

    def body(*refs):
        ins, outs, scr = refs[:n_in], refs[n_in:n_in + n_out], refs[n_in + n_out:]
        job.start(ins, outs, scr)
        job.mid(ins, outs, scr)
        job.finish(ins, outs, scr)

    return pl.pallas_call(body, name=name, in_specs=[ANY] * n_in, out_specs=[ANY] * n_out, out_shape=job.out_shape,
                          input_output_aliases=dict(getattr(job, "aliases", {})),
                          scratch_shapes=job.scratch)(*job.operands)


def tiled_call(name, steps, body, args, in_specs, out_specs, out_shape, scratch=(), job=None, mid_step=None):
    if job is None:
        return pl.pallas_call(body, name=name, grid=(steps,), in_specs=in_specs, out_specs=out_specs, out_shape=out_shape,
                              scratch_shapes=list(scratch), compiler_params=_params(("arbitrary",)))(*args), []
    n_in, n_out, n_scr = len(args), len(out_shape), len(scratch)
    j_in, j_out = len(job.operands), len(job.out_shape)
    mid_step = steps // 2 if mid_step is None else mid_step

    def carried(*refs):
        a, ji = refs[:n_in], refs[n_in:n_in + j_in]
        o = refs[n_in + j_in:n_in + j_in + n_out]
        jo = refs[n_in + j_in + n_out:n_in + j_in + n_out + j_out]
        rest = refs[n_in + j_in + n_out + j_out:]
        sc, js = rest[:n_scr], rest[n_scr:]
        i = pl.program_id(0)

        @pl.when(i == 0)
        def _():
            job.start(ji, jo, js)

        body(*a, *o, *sc)

        @pl.when(i == mid_step)
        def _():
            job.mid(ji, jo, js)

        @pl.when(i == steps - 1)
        def _():
            job.finish(ji, jo, js)

    res = pl.pallas_call(
        carried, name=name, grid=(steps,), in_specs=list(in_specs) + [ANY] * j_in, out_specs=list(out_specs) + [ANY] * j_out,
        out_shape=list(out_shape) + list(job.out_shape), scratch_shapes=list(scratch) + list(job.scratch),
        input_output_aliases={n_in + i: n_out + o for i, o in getattr(job, "aliases", {}).items()},
        compiler_params=_params(("arbitrary",)))(*args, *job.operands)
    return res[:n_out], res[n_out:]


def gather_small(shard):
    r, c = shard.shape

    def body(in_ref, out_ref, send, recv, lsem):
        x, y, cc = _place()
        own = pltpu.make_async_copy(in_ref, out_ref.at[2 * x + y], lsem)
        own.start()
        sends = []
        for j, chip in enumerate(_other_chips(x, y)):
            cp = pltpu.make_async_remote_copy(in_ref, out_ref.at[2 * x + y], send.at[j], recv.at[j],
                                              device_id=(chip[0], chip[1], cc), device_id_type=MESH)
            cp.start()
            sends.append(cp)
        for j, chip in enumerate(_other_chips(x, y)):
            slot = out_ref.at[2 * chip[0] + chip[1]]
            pltpu.make_async_remote_copy(slot, slot, send.at[j], recv.at[j],
                                         device_id=(chip[0], chip[1], cc), device_id_type=MESH).wait_recv()
        for cp in sends:
            cp.wait_send()
        own.wait()

    vm = pl.BlockSpec(memory_space=pltpu.VMEM)
    return pl.pallas_call(
        body, name="gather_small", in_specs=[vm], out_specs=vm,
        out_shape=jax.ShapeDtypeStruct((NQ, r, c), shard.dtype),
        scratch_shapes=[pltpu.SemaphoreType.DMA((3,)), pltpu.SemaphoreType.DMA((3,)), pltpu.SemaphoreType.DMA],
    )(shard)


D2D_CHUNKS = 4


class SwapJob:
    def __init__(self, arrs):
        self.arrs, self.n = list(arrs), len(arrs)
        self.operands = list(arrs)
        self.out_shape = [jax.ShapeDtypeStruct((a.shape[0], a.shape[1] // 2, a.shape[2]), a.dtype) for a in arrs]
        self.scratch = [pltpu.SemaphoreType.DMA((self.n,)), pltpu.SemaphoreType.DMA((self.n,))]

    def start(self, ins, got, scr):
        send, recv = scr
        x, y, c = _place()
        for k in range(self.n):
            q, r, _ = self.arrs[k].shape
            hr = r // 2
            for qi in range(q):
                _start_remote(lambda r0, nr: ins[k].at[qi, pl.ds((1 - c) * hr + r0, nr), :],
                              lambda r0, nr: got[k].at[qi, pl.ds(r0, nr), :], hr, D2D_CHUNKS,
                              send.at[k], recv.at[k], (x, y, 1 - c))

    def mid(self, ins, got, scr):
        pass

    def finish(self, ins, got, scr):
        send, recv = scr
        for k in range(self.n):
            hr = self.arrs[k].shape[1] // 2
            w = _waiter(ins[k].at[:, pl.ds(0, hr), :], got[k], send.at[k], recv.at[k])
            w.wait_send()
            w.wait_recv()


def swap_halves(arrs):
    return run_job(SwapJob(arrs), "swap_halves")


JOIN_CHUNKS = 8


class JoinJob:
    def __init__(self, arrs):
        self.arrs, self.n = list(arrs), len(arrs)
        self.operands = list(arrs)
        self.out_shape = [jax.ShapeDtypeStruct(a.shape, a.dtype) for a in arrs]
        self.scratch = [pltpu.SemaphoreType.DMA((self.n,)), pltpu.SemaphoreType.DMA((self.n,))]
        self.aliases = {k: k for k in range(self.n)}

    def start(self, ins, outs, scr):
        send, recv = scr
        x, y, c = _place()
        for k in range(self.n):
            hr = self.arrs[k].shape[0] // 2
            rows = lambda r0, nr: outs[k].at[pl.ds(c * hr + r0, nr), :]
            _start_remote(rows, rows, hr, JOIN_CHUNKS, send.at[k], recv.at[k], (x, y, 1 - c))

    def mid(self, ins, outs, scr):
        pass

    def finish(self, ins, outs, scr):
        send, recv = scr
        x, y, c = _place()
        for k in range(self.n):
            hr = self.arrs[k].shape[0] // 2
            w = _waiter(outs[k].at[pl.ds(c * hr, hr), :], outs[k].at[pl.ds((1 - c) * hr, hr), :], send.at[k], recv.at[k])
            w.wait_send()
            w.wait_recv()


class Jobs:
    def __init__(self, jobs):
        self.jobs = list(jobs)
        self.operands = [a for j in self.jobs for a in j.operands]
        self.out_shape = [s for j in self.jobs for s in j.out_shape]
        self.scratch = [s for j in self.jobs for s in j.scratch]
        self.aliases = {}
        i0 = o0 = 0
        for j in self.jobs:
            self.aliases.update({i0 + i: o0 + o for i, o in getattr(j, "aliases", {}).items()})
            i0, o0 = i0 + len(j.operands), o0 + len(j.out_shape)

    def _each(self, ins, outs, scr):
        i0 = o0 = s0 = 0
        for j in self.jobs:
            ni, no, ns = len(j.operands), len(j.out_shape), len(j.scratch)
            yield j, ins[i0:i0 + ni], outs[o0:o0 + no], scr[s0:s0 + ns]
            i0, o0, s0 = i0 + ni, o0 + no, s0 + ns

    def start(self, ins, outs, scr):
        for j, i, o, s in self._each(ins, outs, scr):
            j.start(i, o, s)

    def mid(self, ins, outs, scr):
        for j, i, o, s in self._each(ins, outs, scr):
            j.mid(i, o, s)

    def finish(self, ins, outs, scr):
        for j, i, o, s in self._each(ins, outs, scr):
            j.finish(i, o, s)

    def split(self, results):
        out, o0 = [], 0
        for j in self.jobs:
            out.append(results[o0:o0 + len(j.out_shape)])
            o0 += len(j.out_shape)
        return out


def join_halves(arrs):
    return run_job(JoinJob(arrs), "join_halves")


def reduce_small(arr):
    pair = add_pairs([arr], swap_halves([arr]), F32)
    return join_halves(sum_chips(run_job(ExchangeJob(pair, [False]), "exchange_small")))[0]


def _block_diag(w):
    eye = jnp.eye(4, dtype=F32)[None, :, None, :, None]
    return (w.reshape(-1, 4, 64, 1, 64) * eye).reshape(-1, 256, 256).astype(BF16)


def _diag_blocks(b):
    eye = jnp.eye(4, dtype=F32)[None, :, None, :, None]
    return (b.reshape(2, 4, 64, 4, 64) * eye).sum(axis=3).reshape(8, 64, 64)


def _vec_params(pool_b, pool_scale, conv_b, gate_r_b, gate_i_b, lru_lambda, conv_w_full, group_norm_g):
    rows = [pool_b, pool_scale, conv_b, gate_r_b, gate_i_b, lru_lambda,
            conv_w_full[:, 0], conv_w_full[:, 1], conv_w_full[:, 2], conv_w_full[:, 3],
            group_norm_g[:, :DP], group_norm_g[:, DP:]]
    return jnp.pad(jnp.stack(rows, axis=1), ((0, 0), (0, 4), (0, 0)))


SMALL_ROWS_LAYER = 16 + 128 + 64 + 64 + 2 + 2
SMALL_ROWS = 1152


def _pack_small(layers, final_g, meta):
    rows = []
    for vec, pw, wr, wi, g1, g2 in layers:
        rows += [vec, pw.reshape(128, 512), wr.reshape(64, 512), wi.reshape(64, 512), g1.reshape(2, 512), g2.reshape(2, 512)]
    rows += [final_g.reshape(2, 512), meta.reshape(32, 512)]
    flat = jnp.concatenate(rows, axis=0)
    return jnp.concatenate([flat, jnp.zeros((SMALL_ROWS - flat.shape[0], 512), F32)], axis=0)


def _unpack_small(flat):
    layers, o = [], 0
    for _ in range(DEPTH):
        vec = flat[o:o + 16]; o += 16
        pw = flat[o:o + 128].reshape(4, 128, 128); o += 128
        wr = flat[o:o + 64].reshape(8, 64, 64); o += 64
        wi = flat[o:o + 64].reshape(8, 64, 64); o += 64
        g1 = flat[o:o + 2].reshape(1024); o += 2
        g2 = flat[o:o + 2].reshape(1024); o += 2
        layers.append((vec, pw, wr, wi, g1, g2))
    final_g = flat[o:o + 2].reshape(1024); o += 2
    meta = flat[o:o + 32].reshape(16, 1024)
    return layers, final_g, meta


def local_step(x2d, tgt2d, meta_full, conv_w_full, sp, shards=None, gathered=None):
    exchange = gathered is None
    if exchange:
        gathered = [None] * DEPTH
        first_in = run_job(GatherJob(shards[:1], 0), "gather_first")
    h = None
    saved = []
    vp_all = _vec_params(sp["pool_b"], sp["pool_scale"], sp["conv_b"], sp["gate_r_b"], sp["gate_i_b"], sp["lru_lambda"],
                         conv_w_full, sp["group_norm_g"])
    pw_all = sp["pool_w"].astype(BF16)
    br_all = _block_diag(sp["gate_r_w"].reshape(DEPTH * 8, 64, 64)).reshape(DEPTH, 2, 256, 256)
    bi_all = _block_diag(sp["gate_i_w"].reshape(DEPTH * 8, 64, 64)).reshape(DEPTH, 2, 256, 256)
    for l in range(DEPTH):
        vp, pw, br, bi = vp_all[l], pw_all[l], br_all[l], bi_all[l]
        if l == 0:
            job = GatherJob(shards[1:], 0) if exchange else None
            win = first_in[0] if exchange else gathered[0][0]
            (proj, y, hs, aux, h), rest = fwd_mix(x2d, sp["mix_norm_g"][l][None], win, vp, pw, br, bi, job, meta=meta_full)
            wo, wu, wd = rest if exchange else gathered[0][1:]
        else:
            win, wo, wu, wd = gathered[l]
            (proj, y, hs, aux), _ = fwd_mix(h, sp["mix_norm_g"][l][None], win, vp, pw, br, bi)
        wo = wo.reshape(D, D)
        job = GatherJob(shards, l + 1) if exchange and l + 1 < DEPTH else None
        (h1, a, h2), fetched = fwd_post(h, y, wo, sp["mlp_norm_g"][l][None], wu, wd, job)
        if job is not None:
            gathered[l + 1] = fetched
        saved.append((h, proj, y, hs, aux, h1, a, vp, pw, br, bi, win, wo, wu, wd))
        h = h2
    dh, dgf, loss_part = fwd_loss(h, sp["final_norm_g"][None], tgt2d)

    big = [None] * DEPTH
    pair_io = None
    small_layers = [None] * DEPTH
    for l in reversed(range(DEPTH)):
        h0, proj, y, hs, aux, h1, a, vp, pw, br, bi, win, wo, wu, wd = saved[l]
        g1 = sp["mix_norm_g"][l][None]
        job = ExchangeJob(pair_io, [True] * 2) if exchange and pair_io is not None else None
        (dh1, da, xn2, dg2), parts_io = bwd_mlp_dx(dh, h1, a, sp["mlp_norm_g"][l][None], wu, wd, job)
        dwu, dwd = bwd_mlp_dw(xn2, da, a, dh)
        if not exchange:
            (dy, dwo), _ = bwd_out(dh1, y, wo)
            (dproj, ga, dpw, dwr, dwi), _ = bwd_mixer(dy, proj, hs, aux, vp, pw, br, bi)
            (dh, dwin, dg1), _ = bwd_in(dproj, h0, g1, win, dh1)
            big[l] = [dwin, dwo.reshape(NQ, D // NQ, D), dwu, dwd]
        else:
            jobs = Jobs([SwapJob([dwu, dwd])] + ([JoinJob(sum_chips(parts_io))] if job is not None else []))
            (dy, dwo), carried = bwd_out(dh1, y, wo, jobs)
            got_ud = jobs.split(carried)[0]
            if job is not None:
                big[l + 1] = jobs.split(carried)[1] + big[l + 1]
            (dproj, ga, dpw, dwr, dwi), parts_ud = bwd_mixer(dy, proj, hs, aux, vp, pw, br, bi,
                                                             ExchangeJob(add_pairs([dwu, dwd], got_ud, BF16), [True] * 2))
            (dh, dwin, dg1), big[l] = bwd_in(dproj, h0, g1, win, dh1, JoinJob(sum_chips(parts_ud)))
            arrs_io = [dwin, dwo.reshape(NQ, D // NQ, D)]
            pair_io = add_pairs(arrs_io, swap_halves(arrs_io), BF16)
        small_layers[l] = (ga, dpw, _diag_blocks(dwr), _diag_blocks(dwi), dg1[0], dg2[0])
    if exchange:
        big[0] = join_halves(sum_chips(run_job(ExchangeJob(pair_io, [True] * 2), "exchange_last"))) + big[0]
    return loss_part, dh, big, small_layers, dgf


def kernel(x, meta_tokens, mix_norm_g, w_in, pool_w, pool_b, pool_scale, conv_w, conv_b, gate_r_w, gate_r_b, gate_i_w, gate_i_b, lru_lambda, group_norm_g, w_out, mlp_norm_g, w_up, w_down, final_norm_g, loss_target, m_meta_tokens, m_mix_norm_g, m_w_in, m_pool_w, m_pool_b, m_pool_scale, m_conv_w, m_conv_b, m_gate_r_w, m_gate_r_b, m_gate_i_w, m_gate_i_b, m_lru_lambda, m_group_norm_g, m_w_out, m_mlp_norm_g, m_w_up, m_w_down, m_final_norm_g, v_meta_tokens, v_mix_norm_g, v_w_in, v_pool_w, v_pool_b, v_pool_scale, v_conv_w, v_conv_b, v_gate_r_w, v_gate_r_b, v_gate_i_w, v_gate_i_b, v_lru_lambda, v_group_norm_g, v_w_out, v_mlp_norm_g, v_w_up, v_w_down, v_final_norm_g):
    p = 2 * lax.axis_index("x") + lax.axis_index("y")

    shards = [w_in.astype(BF16), w_out.astype(BF16), w_up.astype(BF16), w_down.astype(BF16)]
    small = jnp.concatenate([meta_tokens, jnp.pad(conv_w.reshape(16, 128), ((0, 0), (0, 128)))], axis=0)
    small_g = gather_small(small)
    meta_full = jnp.transpose(small_g[:, :16, :], (1, 0, 2)).reshape(NMETA, D)
    conv_w_full = jnp.transpose(small_g[:, 16:, :128].reshape(NQ, DEPTH, 4, 128), (1, 2, 0, 3)).reshape(DEPTH, 4, DR)

    sp = dict(mix_norm_g=mix_norm_g, pool_w=pool_w, pool_b=pool_b, pool_scale=pool_scale, conv_b=conv_b, gate_r_w=gate_r_w,
              gate_r_b=gate_r_b, gate_i_w=gate_i_w, gate_i_b=gate_i_b, lru_lambda=lru_lambda, group_norm_g=group_norm_g,
              mlp_norm_g=mlp_norm_g, final_norm_g=final_norm_g)
    loss_part, dh, big, small_layers, dgf = local_step(x[0], loss_target[0], meta_full, conv_w_full, sp, shards=shards)
    loss = lax.psum(loss_part[0, 0], ("x", "y", "c"))
    grad_x = dh[NMETA:][None]

    small_sum = reduce_small(_pack_small(small_layers, dgf[0], dh[:NMETA])[None])
    g_layers, g_final, g_meta_full = _unpack_small(small_sum)

    g_vec = [gl[0] for gl in g_layers]
    grads = dict(
        meta_tokens=lax.dynamic_slice(g_meta_full, (0, p * (D // NQ)), (NMETA, D // NQ)),
        mix_norm_g=jnp.stack([gl[4] for gl in g_layers]),
        w_in=jnp.stack([big[l][0] for l in range(DEPTH)]),
        pool_w=jnp.stack([gl[1] for gl in g_layers]),
        pool_b=jnp.stack([gv[GA_PB] for gv in g_vec]),
        pool_scale=jnp.stack([gv[GA_PS] for gv in g_vec]),
        conv_w=lax.dynamic_slice(jnp.stack([gv[GA_CW:GA_CW + 4] for gv in g_vec]), (0, 0, p * 128), (DEPTH, 4, 128)),
        conv_b=jnp.stack([gv[GA_CB] for gv in g_vec]),
        gate_r_w=jnp.stack([gl[2] for gl in g_layers]),
        gate_r_b=jnp.stack([gv[GA_BR] for gv in g_vec]),
        gate_i_w=jnp.stack([gl[3] for gl in g_layers]),
        gate_i_b=jnp.stack([gv[GA_BI] for gv in g_vec]),
        lru_lambda=jnp.stack([gv[GA_LAM] for gv in g_vec]),
        group_norm_g=jnp.stack([jnp.concatenate([gv[GA_GNP], gv[GA_GNR]]) for gv in g_vec]),
        w_out=jnp.stack([big[l][1] for l in range(DEPTH)]),
        mlp_norm_g=jnp.stack([gl[5] for gl in g_layers]),
        w_up=jnp.stack([big[l][2] for l in range(DEPTH)]),
        w_down=jnp.stack([big[l][3] for l in range(DEPTH)]),
        final_norm_g=g_final,
    )
    weights = dict(meta_tokens=meta_tokens, mix_norm_g=mix_norm_g, w_in=w_in, pool_w=pool_w, pool_b=pool_b, pool_scale=pool_scale,
                   conv_w=conv_w, conv_b=conv_b, gate_r_w=gate_r_w, gate_r_b=gate_r_b, gate_i_w=gate_i_w, gate_i_b=gate_i_b,
                   lru_lambda=lru_lambda, group_norm_g=group_norm_g, w_out=w_out, mlp_norm_g=mlp_norm_g, w_up=w_up, w_down=w_down,
                   final_norm_g=final_norm_g)
    mom_m = dict(meta_tokens=m_meta_tokens, mix_norm_g=m_mix_norm_g, w_in=m_w_in, pool_w=m_pool_w, pool_b=m_pool_b,
                 pool_scale=m_pool_scale, conv_w=m_conv_w, conv_b=m_conv_b, gate_r_w=m_gate_r_w, gate_r_b=m_gate_r_b,
                 gate_i_w=m_gate_i_w, gate_i_b=m_gate_i_b, lru_lambda=m_lru_lambda, group_norm_g=m_group_norm_g, w_out=m_w_out,
                 mlp_norm_g=m_mlp_norm_g, w_up=m_w_up, w_down=m_w_down, final_norm_g=m_final_norm_g)
    mom_v = dict(meta_tokens=v_meta_tokens, mix_norm_g=v_mix_norm_g, w_in=v_w_in, pool_w=v_pool_w, pool_b=v_pool_b,
                 pool_scale=v_pool_scale, conv_w=v_conv_w, conv_b=v_conv_b, gate_r_w=v_gate_r_w, gate_r_b=v_gate_r_b,
                 gate_i_w=v_gate_i_w, gate_i_b=v_gate_i_b, lru_lambda=v_lru_lambda, group_norm_g=v_group_norm_g, w_out=v_w_out,
                 mlp_norm_g=v_mlp_norm_g, w_up=v_w_up, w_down=v_w_down, final_norm_g=v_final_norm_g)
    names = list(weights)

    delta, new_m, new_v = {}, {}, {}
    big_names = ("w_in", "w_out", "w_up", "w_down")
    for nm in big_names:
        shp = weights[nm].shape
        two_d = lambda t: t.reshape(shp[0] * shp[1], shp[2])
        d_, m_, v_ = adamw(two_d(weights[nm]), two_d(grads[nm]), two_d(mom_m[nm]), two_d(mom_v[nm]))
        delta[nm], new_m[nm], new_v[nm] = d_.reshape(shp), m_.reshape(shp), v_.reshape(shp)
    small_names = [nm for nm in names if nm not in big_names]
    sizes = [weights[nm].size for nm in small_names]
    total = sum(sizes)
    rows = -(-total // 512)
    rows = -(-rows // 16) * 16

    def flat(tree, fill):
        v_ = jnp.concatenate([tree[nm].reshape(-1) for nm in small_names])
        return jnp.concatenate([v_, jnp.full((rows * 512 - total,), fill, F32)]).reshape(rows, 512)

    d_, m_, v_ = adamw(flat(weights, 0.0), flat(grads, 0.0), flat(mom_m, 0.0), flat(mom_v, 1.0))
    o = 0
    for nm, sz in zip(small_names, sizes):
        shp = weights[nm].shape
        delta[nm] = d_.reshape(-1)[o:o + sz].reshape(shp)
        new_m[nm] = m_.reshape(-1)[o:o + sz].reshape(shp)
        new_v[nm] = v_.reshape(-1)[o:o + sz].reshape(shp)
        o += sz

    return (loss, grad_x, *[grads[nm] for nm in names], *[delta[nm] for nm in names],
            *[new_m[nm] for nm in names], *[new_v[nm] for nm in names])
```

```python
import jax
import jax.numpy as jnp
from jax import lax
from jax.experimental import pallas as pl
from jax.experimental.pallas import tpu as pltpu

F32 = jnp.float32
BF16 = jnp.bfloat16

D = 1024
DP = 512
DR = 512
DIN = 1536
DFF = 4096
DEPTH = 4
NMETA = 16
NQ = 4
WIN_S = DIN // NQ
FF_S = DFF // NQ
EPS = 1e-6
HALO = 16
TT = 432
TTW = 912
VMEM_LIMIT = 56 * 1024 * 1024

ADAM_LR = 0.001
ADAM_B1 = 0.9
ADAM_B2 = 0.999
ADAM_EPS = 1e-08
ADAM_WD = 0.01
ADAM_STEP = 10

MESH = pl.DeviceIdType.MESH
ANY = pl.BlockSpec(memory_space=pl.ANY)


def _params(sem=None, vmem=VMEM_LIMIT):
    return pltpu.CompilerParams(dimension_semantics=sem, vmem_limit_bytes=vmem)


def _resident(shape, index):
    return pl.BlockSpec(shape, index, pipeline_mode=pl.Buffered(1))


def _nt(x, w):
    return lax.dot_general(x, w, (((1,), (1,)), ((), ())), preferred_element_type=F32)


def _tn(a, b):
    return lax.dot_general(a, b, (((0,), (0,)), ((), ())), preferred_element_type=F32)


def _dot(x, w):
    return jnp.dot(x, w, preferred_element_type=F32)


def _rms(h):
    return lax.rsqrt(jnp.mean(h * h, axis=-1, keepdims=True) + EPS)


def _rms_bwd(n, r, dn):
    return r * (dn - n * jnp.mean(dn * n, axis=-1, keepdims=True))


def _down(x, s):
    return pltpu.roll(x, s, 0)


def _up(x, s):
    return pltpu.roll(x, x.shape[0] - s, 0)


GELU_C = 0.7978845608028654
GELU_K = 0.044715


def _gelu(x):
    t = jnp.tanh(GELU_C * (x + GELU_K * x * x * x))
    return 0.5 * x * (1.0 + t), t


def _softplus(x):
    return jnp.maximum(x, 0.0) + jnp.log1p(jnp.exp(-jnp.abs(x)))


VP_PB, VP_PS, VP_CB, VP_BR, VP_BI, VP_LAM, VP_CW, VP_GNP, VP_GNR = 0, 1, 2, 3, 4, 5, 6, 10, 11


def _row(vp, r):
    return vp[r:r + 1, :]


def _mixer_pre(ue, t0, vp, pw_ref, br_ref, bi_ref):
    tt = ue.shape[0] - HALO
    tf = (t0 + lax.broadcasted_iota(jnp.int32, (tt, 1), 0)).astype(F32) + 1.0
    pb, ps = _row(vp, VP_PB), _row(vp, VP_PS)
    pooled, mapped, inv_cnt = [], [], []
    for g in range(4):
        lanes = slice(128 * g, 128 * (g + 1))
        xe = ue[:, lanes]
        s = xe
        for j in range(g + 1):
            s = s + _down(s, 1 << j)
        inv = 1.0 / jnp.minimum(tf, float(2 << g))
        pg = s[HALO:] * inv - xe[HALO:]
        mg = _dot(pg.astype(BF16), pw_ref[g]) + pb[:, lanes]
        pooled.append(pg)
        mapped.append(mg)
        inv_cnt.append(inv)
    ypool = [mapped[g] * ps[:, 128 * g:128 * (g + 1)] for g in range(4)]

    xe = ue[:, DP:DP + DR]
    taps = [_down(xe, 3)[HALO:], _down(xe, 2)[HALO:], _down(xe, 1)[HALO:], xe[HALO:]]
    xc = _row(vp, VP_CB) + (taps[0] * _row(vp, VP_CW) + taps[1] * _row(vp, VP_CW + 1)
                            + taps[2] * _row(vp, VP_CW + 2) + taps[3] * _row(vp, VP_CW + 3))
    xcb = xc.astype(BF16)
    zr = jnp.concatenate([_dot(xcb[:, :256], br_ref[0]), _dot(xcb[:, 256:], br_ref[1])], axis=1) + _row(vp, VP_BR)
    zi = jnp.concatenate([_dot(xcb[:, :256], bi_ref[0]), _dot(xcb[:, 256:], bi_ref[1])], axis=1) + _row(vp, VP_BI)
    r = jax.nn.sigmoid(zr)
    ig = jax.nn.sigmoid(zi)
    sp = _softplus(-_row(vp, VP_LAM))
    la = (-8.0 * r) * sp
    a = jnp.exp(la)
    th = jnp.tanh(la)
    mult = jnp.sqrt((-2.0 * th) / (1.0 - th))
    gate = ue[HALO:, DP + DR:]
    gl, gt = _gelu(gate)
    return dict(pooled=pooled, mapped=mapped, ypool=ypool, inv_cnt=inv_cnt, taps=taps, xc=xc, xcb=xcb, r=r, ig=ig,
                sp=sp, a=a, mult=mult, gate=gate, gl=gl, gt=gt)


SUBLANES = 8


def _scan_fwd(a, b, h_in):
    tt = a.shape[0]
    sub = jnp.bitwise_and(lax.broadcasted_iota(jnp.int32, (tt, 1), 0), SUBLANES - 1)
    s = 1
    while s < SUBLANES:
        m = sub >= s
        a_s = jnp.where(m, _down(a, s), 1.0)
        b_s = jnp.where(m, _down(b, s), 0.0)
        b = a * b_s + b
        a = a * a_s
        s *= 2
    groups, h = [], h_in
    for g in range(tt // SUBLANES):
        rows = slice(SUBLANES * g, SUBLANES * (g + 1))
        hg = a[rows] * h + b[rows]
        groups.append(hg)
        h = hg[SUBLANES - 1:SUBLANES, :]
    return jnp.concatenate(groups, axis=0)


def _scan_rev(c, d, l_in):
    tt = c.shape[0]
    sub = jnp.bitwise_and(lax.broadcasted_iota(jnp.int32, (tt, 1), 0), SUBLANES - 1)
    s = 1
    while s < SUBLANES:
        m = sub < SUBLANES - s
        c_s = jnp.where(m, _up(c, s), 1.0)
        d_s = jnp.where(m, _up(d, s), 0.0)
        d = c * d_s + d
        c = c * c_s
        s *= 2
    groups, l = [], l_in
    for g in reversed(range(tt // SUBLANES)):
        rows = slice(SUBLANES * g, SUBLANES * (g + 1))
        lg = c[rows] * l + d[rows]
        groups.append(lg)
        l = lg[0:1, :]
    return jnp.concatenate(groups[::-1], axis=0)


def _halo_index(i, per_tile):
    return jnp.maximum(i * per_tile - 1, 0)


def _chip_slabs_to_columns(w_ref, wfull_ref):
    for q in range(NQ):
        wfull_ref[:, WIN_S * q:WIN_S * (q + 1)] = w_ref[q]


AUX_POOLED, AUX_MAPPED, AUX_XC, AUX_R, AUX_IG, AUX_A, AUX_MULT, AUX_GL, AUX_GT = range(9)
AUX_W = 9 * 512


def _aux(k):
    return slice(512 * k, 512 * (k + 1))


def fwd_mix(h, g, w_g, vp, pw, br, bi, job=None, meta=None):
    T = h.shape[0] + (NMETA if meta is not None else 0)
    n_h = 3 if meta is not None else 1

    def body(*refs):
        g_ref, w_ref, vp_ref, pw_ref, br_ref, bi_ref, p_ref, y_ref, hs_ref, aux_ref = refs[n_h:n_h + 10]
        wfull_ref, halo_ref, carry_ref = refs[-3:]
        i = pl.program_id(0)

        @pl.when(i == 0)
        def _():
            _chip_slabs_to_columns(w_ref, wfull_ref)
            carry_ref[...] = jnp.zeros_like(carry_ref)
            halo_ref[...] = jnp.zeros_like(halo_ref)

        if meta is not None:
            xp_ref, x_ref, meta_ref = refs[:n_h]
            hh = jnp.concatenate([jnp.where(i == 0, meta_ref[...], xp_ref[...]), x_ref[0:TT - HALO, :]], axis=0)
            refs[n_h + 10][...] = hh
        else:
            hh = refs[0][...]
        xn = (hh * _rms(hh) * g_ref[...]).astype(BF16)
        proj = _dot(xn, wfull_ref[...])
        p_ref[...] = proj
        ue = jnp.concatenate([halo_ref[...], proj], axis=0)
        halo_ref[...] = p_ref[TT - HALO:TT, :]
        vp_v = vp_ref[...]
        m = _mixer_pre(ue, i * TT, vp_v, pw_ref, br_ref, bi_ref)
        for g in range(4):
            aux_ref[:, 512 * AUX_POOLED + 128 * g:512 * AUX_POOLED + 128 * (g + 1)] = m["pooled"][g]
            aux_ref[:, 512 * AUX_MAPPED + 128 * g:512 * AUX_MAPPED + 128 * (g + 1)] = m["mapped"][g]
        for k, name in ((AUX_XC, "xc"), (AUX_R, "r"), (AUX_IG, "ig"), (AUX_A, "a"), (AUX_MULT, "mult"), (AUX_GL, "gl"),
                        (AUX_GT, "gt")):
            aux_ref[:, _aux(k)] = m[name]
        b = m["mult"] * (m["ig"] * m["xc"])
        hs = _scan_fwd(m["a"], b, carry_ref[0:1, :])
        hs_ref[...] = hs
        carry_ref[0:1, :] = hs_ref[TT - 1:TT, :]
        yr = hs * m["gl"]
        ssq = sum(jnp.sum(yp * yp, axis=-1, keepdims=True) for yp in m["ypool"])
        rp = lax.rsqrt(ssq * (1.0 / DP) + EPS)
        gnp = _row(vp_v, VP_GNP)
        for g in range(4):
            lanes = slice(128 * g, 128 * (g + 1))
            y_ref[:, lanes] = (m["ypool"][g] * rp * gnp[:, lanes]).astype(BF16)
        y_ref[:, DP:] = (yr * _rms(yr) * _row(vp_v, VP_GNR)).astype(BF16)

    if meta is not None:
        h_args, h_specs = (h, h, meta), _shifted_specs() + [_resident((NMETA, D), lambda i: (0, 0))]
    else:
        h_args, h_specs = (h,), [pl.BlockSpec((TT, D), lambda i: (i, 0))]
    full = meta is not None
    return tiled_call(
        "fwd_mix", T // TT, body, h_args + (g, w_g, vp, pw, br, bi),
        in_specs=h_specs + [_resident((1, D), lambda i: (0, 0)), _resident((NQ, D, WIN_S), lambda i: (0, 0, 0)),
                            _resident((16, 512), lambda i: (0, 0)), _resident((4, 128, 128), lambda i: (0, 0, 0)),
                            _resident((2, 256, 256), lambda i: (0, 0, 0)), _resident((2, 256, 256), lambda i: (0, 0, 0))],
        out_specs=[pl.BlockSpec((TT, DIN), lambda i: (i, 0)), pl.BlockSpec((TT, D), lambda i: (i, 0)),
                   pl.BlockSpec((TT, DR), lambda i: (i, 0)), pl.BlockSpec((TT, AUX_W), lambda i: (i, 0))]
        + [pl.BlockSpec((TT, D), lambda i: (i, 0))] * full,
        out_shape=[jax.ShapeDtypeStruct((T, DIN), F32), jax.ShapeDtypeStruct((T, D), BF16),
                   jax.ShapeDtypeStruct((T, DR), F32), jax.ShapeDtypeStruct((T, AUX_W), F32)]
        + [jax.ShapeDtypeStruct((T, D), F32)] * full,
        scratch=[pltpu.VMEM((D, DIN), BF16), pltpu.VMEM((HALO, DIN), F32), pltpu.VMEM((8, DR), F32)],
        job=job, mid_step=T // TT - 1)


def fwd_post(h, y, wo_g, g2, wu_g, wd_g, job=None, loss=None):
    T = h.shape[0]

    def body(*refs):
        h_ref, y_ref, wo_ref, g_ref, wu_ref, wd_ref = refs[:6]
        h1_ref, a_ref = refs[-5:-3] if loss is not None else refs[-3:-1]
        h1 = h_ref[...] + _dot(y_ref[...], wo_ref[...])
        h1_ref[...] = h1
        xn = (h1 * _rms(h1) * g_ref[...]).astype(BF16)
        acc = h1
        for q in range(NQ):
            a = _dot(xn, wu_ref[q])
            a_ref[:, FF_S * q:FF_S * (q + 1)] = a.astype(BF16)
            ra = jnp.maximum(a, 0.0)
            acc = acc + _dot((ra * ra).astype(BF16), wd_ref[q])
        if loss is None:
            refs[-1][...] = acc
        else:
            _loss_tile(acc, *refs[6:9], *refs[-3:])

    row = pl.BlockSpec((TT, D), lambda i: (i, 0))
    in_specs = [row, row, _resident((D, D), lambda i: (0, 0)), _resident((1, D), lambda i: (0, 0)),
                _resident((NQ, D, FF_S), lambda i: (0, 0, 0)), _resident((NQ, FF_S, D), lambda i: (0, 0, 0))]
    out_specs = [row, pl.BlockSpec((TT, DFF), lambda i: (i, 0)), row]
    out_shape = [jax.ShapeDtypeStruct((T, D), F32), jax.ShapeDtypeStruct((T, DFF), BF16), jax.ShapeDtypeStruct((T, D), F32)]
    args = (h, y, wo_g, g2, wu_g, wd_g)
    if loss is not None:
        args += (loss[0], loss[1], loss[1])
        in_specs += [_resident((1, D), lambda i: (0, 0))] + _shifted_specs()
        out_specs += [pl.BlockSpec((1, D), lambda i: (0, 0)), pl.BlockSpec((1, 1), lambda i: (0, 0))]
        out_shape += [jax.ShapeDtypeStruct((1, D), F32), jax.ShapeDtypeStruct((1, 1), F32)]
    return tiled_call("fwd_post", T // TT, body, args, in_specs=in_specs, out_specs=out_specs, out_shape=out_shape,
                      job=job, mid_step=(T // TT) * 3 // 4)


def _shifted_tile(prev_ref, cur_ref):
    return jnp.concatenate([prev_ref[...], cur_ref[0:TT - HALO, :]], axis=0)


def _shifted_specs():
    per = TT // HALO
    return [pl.BlockSpec((HALO, D), lambda i: (_halo_index(i, per), 0)), pl.BlockSpec((TT, D), lambda i: (i, 0))]


def _loss_tile(hh, g_ref, tp_ref, t_ref, dh_ref, dg_ref, loss_ref):
    i = pl.program_id(0)

    @pl.when(i == 0)
    def _():
        dg_ref[...] = jnp.zeros_like(dg_ref)
        loss_ref[...] = jnp.zeros_like(loss_ref)

    r = _rms(hh)
    n = hh * r
    gfv = g_ref[...]
    row = i * TT + lax.broadcasted_iota(jnp.int32, (TT, 1), 0)
    e = jnp.where(row >= NMETA, n * gfv - _shifted_tile(tp_ref, t_ref), 0.0)
    loss_ref[...] += 0.5 * jnp.sum(jnp.sum(e * e, axis=-1, keepdims=True) * (1.0 / D), axis=0, keepdims=True)
    dy = e * (1.0 / D)
    dg_ref[...] += jnp.sum(dy * n, axis=0, keepdims=True)
    dh_ref[...] = _rms_bwd(n, r, dy * gfv)


def bwd_mlp_dx(dh2, h1, a, g2, wu_g, wd_g, job=None):
    T = dh2.shape[0]

    def body(dh2_ref, h1_ref, a_ref, g_ref, wu_ref, wd_ref, dh1_ref, da_ref, xn_ref, dg_ref):
        @pl.when(pl.program_id(0) == 0)
        def _():
            dg_ref[...] = jnp.zeros_like(dg_ref)

        h1v = h1_ref[...]
        r = _rms(h1v)
        n = h1v * r
        gv = g_ref[...]
        xn_ref[...] = (n * gv).astype(BF16)
        dh2v = dh2_ref[...]
        dh2b = dh2v.astype(BF16)
        dxn = jnp.zeros((TT, D), F32)
        for q in range(NQ):
            cols = slice(FF_S * q, FF_S * (q + 1))
            ra = jnp.maximum(a_ref[:, cols].astype(F32), 0.0)
            da = (_nt(dh2b, wd_ref[q]) * (2.0 * ra)).astype(BF16)
            da_ref[:, cols] = da
            dxn = dxn + _nt(da, wu_ref[q])
        dg_ref[...] += jnp.sum(dxn * n, axis=0, keepdims=True)
        dh1_ref[...] = dh2v + _rms_bwd(n, r, dxn * gv)

    return tiled_call(
        "bwd_mlp_dx", T // TT, body, (dh2, h1, a, g2, wu_g, wd_g),
        in_specs=[pl.BlockSpec((TT, D), lambda i: (i, 0)), pl.BlockSpec((TT, D), lambda i: (i, 0)),
                  pl.BlockSpec((TT, DFF), lambda i: (i, 0)), _resident((1, D), lambda i: (0, 0)),
                  _resident((NQ, D, FF_S), lambda i: (0, 0, 0)), _resident((NQ, FF_S, D), lambda i: (0, 0, 0))],
        out_specs=[pl.BlockSpec((TT, D), lambda i: (i, 0)), pl.BlockSpec((TT, DFF), lambda i: (i, 0)),
                   pl.BlockSpec((TT, D), lambda i: (i, 0)), pl.BlockSpec((1, D), lambda i: (0, 0))],
        out_shape=[jax.ShapeDtypeStruct((T, D), F32), jax.ShapeDtypeStruct((T, DFF), BF16),
                   jax.ShapeDtypeStruct((T, D), BF16), jax.ShapeDtypeStruct((1, D), F32)],
        job=job)


def bwd_mlp_dw(xn, da, a, dh2):
    T = xn.shape[0]

    def body(xn_ref, da_ref, a_ref, dh2_ref, dwu_ref, dwd_ref):
        @pl.when(pl.program_id(1) == 0)
        def _():
            dwu_ref[...] = jnp.zeros_like(dwu_ref)
            dwd_ref[...] = jnp.zeros_like(dwd_ref)

        dwu_ref[...] += _tn(xn_ref[...], da_ref[...])
        ra = jnp.maximum(a_ref[...].astype(F32), 0.0)
        dwd_ref[...] += _tn((ra * ra).astype(BF16), dh2_ref[...].astype(BF16))

    return pl.pallas_call(
        body, name="bwd_mlp_dw", grid=(NQ, T // TTW),
        in_specs=[pl.BlockSpec((TTW, D), lambda q, i: (i, 0)), pl.BlockSpec((TTW, FF_S), lambda q, i: (i, q)),
                  pl.BlockSpec((TTW, FF_S), lambda q, i: (i, q)), pl.BlockSpec((TTW, D), lambda q, i: (i, 0))],
        out_specs=[pl.BlockSpec((None, D, FF_S), lambda q, i: (q, 0, 0)),
                   pl.BlockSpec((None, FF_S, D), lambda q, i: (q, 0, 0))],
        out_shape=[jax.ShapeDtypeStruct((NQ, D, FF_S), F32), jax.ShapeDtypeStruct((NQ, FF_S, D), F32)],
        compiler_params=_params(("arbitrary", "arbitrary")),
    )(xn, da, a, dh2)


def bwd_out(dh1, y, wo_g, job=None):
    T = dh1.shape[0]

    def body(dh_ref, y_ref, wo_ref, dy_ref, dwo_ref):
        @pl.when(pl.program_id(0) == 0)
        def _():
            dwo_ref[...] = jnp.zeros_like(dwo_ref)

        dhb = dh_ref[...].astype(BF16)
        dy_ref[...] = _nt(dhb, wo_ref[...])
        dwo_ref[...] += _tn(y_ref[...], dhb)

    return tiled_call(
        "bwd_out", T // TT, body, (dh1, y, wo_g),
        in_specs=[pl.BlockSpec((TT, D), lambda i: (i, 0)), pl.BlockSpec((TT, D), lambda i: (i, 0)),
                  _resident((D, D), lambda i: (0, 0))],
        out_specs=[pl.BlockSpec((TT, D), lambda i: (i, 0)), pl.BlockSpec((D, D), lambda i: (0, 0))],
        out_shape=[jax.ShapeDtypeStruct((T, D), F32), jax.ShapeDtypeStruct((D, D), F32)], job=job)


GA_PS, GA_PB, GA_CB, GA_BR, GA_BI, GA_LAM, GA_CW, GA_GNP, GA_GNR = 0, 1, 2, 3, 4, 5, 6, 10, 11


def bwd_mixer(dy, proj, hs, aux, vp, pw, br, bi, job=None):
    T = proj.shape[0]
    tt = TT
    nt = T // tt
    per = tt // HALO

    def body(dy_ref, p_ref, ph_ref, hs_ref, hsh_ref, aux_ref, vp_ref, pw_ref, br_ref, bi_ref,
             dp_ref, ga_ref, dpw_ref, dwr_ref, dwi_ref, lam_ref, q_ref, dxc_ref):
        s = pl.program_id(0)
        ti = nt - 1 - s

        @pl.when(s == 0)
        def _():
            for ref in (ga_ref, dpw_ref, dwr_ref, dwi_ref, lam_ref, q_ref, dxc_ref):
                ref[...] = jnp.zeros_like(ref)

        dyv = dy_ref[...]
        lam_in, q_in, dxc_in = lam_ref[0:1, :], q_ref[...], dxc_ref[...]
        first = ti > 0
        vp_v = vp_ref[...]
        ur = jnp.concatenate([jnp.where(first, ph_ref[:, DP:DP + DR], 0.0), p_ref[:, DP:DP + DR]], axis=0)
        tf = (ti * tt + lax.broadcasted_iota(jnp.int32, (tt, 1), 0)).astype(F32) + 1.0
        mapped = [aux_ref[:, 512 * AUX_MAPPED + 128 * g:512 * AUX_MAPPED + 128 * (g + 1)] for g in range(4)]
        ps_row = _row(vp_v, VP_PS)
        xc = aux_ref[:, _aux(AUX_XC)]
        m = dict(pooled=[aux_ref[:, 512 * AUX_POOLED + 128 * g:512 * AUX_POOLED + 128 * (g + 1)] for g in range(4)],
                 mapped=mapped, ypool=[mapped[g] * ps_row[:, 128 * g:128 * (g + 1)] for g in range(4)],
                 inv_cnt=[1.0 / jnp.minimum(tf, float(2 << g)) for g in range(4)],
                 taps=[_down(ur, 3)[HALO:], _down(ur, 2)[HALO:], _down(ur, 1)[HALO:], ur[HALO:]],
                 xc=xc, xcb=xc.astype(BF16), r=aux_ref[:, _aux(AUX_R)], ig=aux_ref[:, _aux(AUX_IG)],
                 a=aux_ref[:, _aux(AUX_A)], mult=aux_ref[:, _aux(AUX_MULT)], gl=aux_ref[:, _aux(AUX_GL)],
                 gt=aux_ref[:, _aux(AUX_GT)], gate=p_ref[:, DP + DR:], sp=_softplus(-_row(vp_v, VP_LAM)))
        hs_v = hs_ref[...]
        hprev = _down(jnp.concatenate([jnp.where(first, hsh_ref[...], 0.0), hs_v], axis=0), 1)[HALO:]

        def acc(rw, v):
            ga_ref[rw:rw + 1, :] += jnp.sum(v, axis=0, keepdims=True)

        gnp = _row(vp_v, VP_GNP)
        ps = _row(vp_v, VP_PS)
        ssq = sum(jnp.sum(yp * yp, axis=-1, keepdims=True) for yp in m["ypool"])
        rp = lax.rsqrt(ssq * (1.0 / DP) + EPS)
        npool = [yp * rp for yp in m["ypool"]]
        dnp = [dyv[:, 128 * g:128 * (g + 1)] * gnp[:, 128 * g:128 * (g + 1)] for g in range(4)]
        mean_dn = sum(jnp.sum(dnp[g] * npool[g], axis=-1, keepdims=True) for g in range(4)) * (1.0 / DP)
        for g in range(4):
            lanes = slice(128 * g, 128 * (g + 1))
            ga_ref[GA_GNP:GA_GNP + 1, lanes] += jnp.sum(dyv[:, lanes] * npool[g], axis=0, keepdims=True)
            dyp = rp * (dnp[g] - npool[g] * mean_dn)
            ga_ref[GA_PS:GA_PS + 1, lanes] += jnp.sum(dyp * m["mapped"][g], axis=0, keepdims=True)
            dmap = dyp * ps[:, lanes]
            ga_ref[GA_PB:GA_PB + 1, lanes] += jnp.sum(dmap, axis=0, keepdims=True)
            dmb = dmap.astype(BF16)
            dpw_ref[g] += _tn(m["pooled"][g].astype(BF16), dmb)
            dpool = _nt(dmb, pw_ref[g])
            qv = dpool * m["inv_cnt"][g]
            win = jnp.concatenate([qv, q_in[:, lanes]], axis=0)
            for j in range(g + 1):
                win = win + _up(win, 1 << j)
            q_ref[:, lanes] = qv[:HALO]
            dp_ref[:, lanes] = (win[:tt] - dpool).astype(BF16)

        gnr = _row(vp_v, VP_GNR)
        yr = hs_v * m["gl"]
        rr = _rms(yr)
        nr = yr * rr
        dyr_out = dyv[:, DP:]
        acc(GA_GNR, dyr_out * nr)
        dyr = _rms_bwd(nr, rr, dyr_out * gnr)
        gate, gt = m["gate"], m["gt"]
        dgl = 0.5 * (1.0 + gt) + 0.5 * gate * (1.0 - gt * gt) * (GELU_C * (1.0 + 3.0 * GELU_K * gate * gate))
        dp_ref[:, DP + DR:] = (dyr * hs_v * dgl).astype(BF16)
        dhs = dyr * m["gl"]
        a = m["a"]
        row = lax.broadcasted_iota(jnp.int32, (tt, 1), 0)
        c_next = jnp.where(row < tt - 1, _up(a, 1), 1.0)
        lam = _scan_rev(c_next, dhs, lam_in)
        lam_ref[0:1, :] = a[0:1, :] * lam[0:1, :]
        xc, ig, r, mult = m["xc"], m["ig"], m["r"], m["mult"]
        dmult = lam * ig * xc
        dig = lam * mult * xc
        dxc = lam * mult * ig
        dla = lam * hprev * a - dmult * (a * a) / mult
        acc(GA_LAM, dla * r)
        dzr = (dla * (-8.0 * m["sp"])) * (r * (1.0 - r))
        dzi = dig * (ig * (1.0 - ig))
        acc(GA_BR, dzr)
        acc(GA_BI, dzi)
        dzrb, dzib = dzr.astype(BF16), dzi.astype(BF16)
        xcb = m["xcb"]
        halves = []
        for k in range(2):
            lanes = slice(256 * k, 256 * (k + 1))
            dwr_ref[k] += _tn(xcb[:, lanes], dzrb[:, lanes])
            dwi_ref[k] += _tn(xcb[:, lanes], dzib[:, lanes])
            halves.append(_nt(dzrb[:, lanes], br_ref[k]) + _nt(dzib[:, lanes], bi_ref[k]))
        dxc = dxc + jnp.concatenate(halves, axis=1)
        acc(GA_CB, dxc)
        for k in range(4):
            acc(GA_CW + k, dxc * m["taps"][k])
        dxe = jnp.concatenate([dxc, dxc_in], axis=0)
        du = (_up(dxe, 3)[:tt] * _row(vp_v, VP_CW) + _up(dxe, 2)[:tt] * _row(vp_v, VP_CW + 1)
              + _up(dxe, 1)[:tt] * _row(vp_v, VP_CW + 2) + dxc * _row(vp_v, VP_CW + 3))
        dxc_ref[...] = dxc[:HALO]
        dp_ref[:, DP:DP + DR] = du.astype(BF16)

        @pl.when(s == nt - 1)
        def _():
            lamp = _row(vp_v, VP_LAM)
            ga_ref[GA_LAM:GA_LAM + 1, :] = ga_ref[GA_LAM:GA_LAM + 1, :] * (8.0 * jax.nn.sigmoid(-lamp))

    rev = lambda i: (nt - 1 - i, 0)
    rev_halo = lambda i: (_halo_index(nt - 1 - i, per), 0)
    return tiled_call(
        "bwd_mixer", nt, body, (dy, proj, proj, hs, hs, aux, vp, pw, br, bi),
        in_specs=[pl.BlockSpec((tt, D), rev), pl.BlockSpec((tt, DIN), rev), pl.BlockSpec((HALO, DIN), rev_halo),
                  pl.BlockSpec((tt, DR), rev), pl.BlockSpec((HALO, DR), rev_halo), pl.BlockSpec((tt, AUX_W), rev),
                  _resident((16, 512), lambda i: (0, 0)), _resident((4, 128, 128), lambda i: (0, 0, 0)),
                  _resident((2, 256, 256), lambda i: (0, 0, 0)), _resident((2, 256, 256), lambda i: (0, 0, 0))],
        out_specs=[pl.BlockSpec((tt, DIN), rev), pl.BlockSpec((16, 512), lambda i: (0, 0)),
                   pl.BlockSpec((4, 128, 128), lambda i: (0, 0, 0)), pl.BlockSpec((2, 256, 256), lambda i: (0, 0, 0)),
                   pl.BlockSpec((2, 256, 256), lambda i: (0, 0, 0))],
        out_shape=[jax.ShapeDtypeStruct((T, DIN), BF16), jax.ShapeDtypeStruct((16, 512), F32),
                   jax.ShapeDtypeStruct((4, 128, 128), F32), jax.ShapeDtypeStruct((2, 256, 256), F32),
                   jax.ShapeDtypeStruct((2, 256, 256), F32)],
        scratch=[pltpu.VMEM((8, DR), F32), pltpu.VMEM((HALO, DP), F32), pltpu.VMEM((HALO, DR), F32)], job=job)


def bwd_in(dproj, h, g1, w_g, dh1, job=None):
    T = h.shape[0]

    def body(dp_ref, h_ref, g_ref, w_ref, dh1_ref, dh_ref, dw_ref, dg_ref, wfull_ref, acc_ref):
        i = pl.program_id(0)

        @pl.when(i == 0)
        def _():
            _chip_slabs_to_columns(w_ref, wfull_ref)
            acc_ref[...] = jnp.zeros_like(acc_ref)
            dg_ref[...] = jnp.zeros_like(dg_ref)

        hv = h_ref[...]
        r = _rms(hv)
        n = hv * r
        gv = g_ref[...]
        xn = (n * gv).astype(BF16)
        dpv = dp_ref[...]
        dxn = _nt(dpv, wfull_ref[...])
        acc_ref[...] += _tn(xn, dpv)
        dg_ref[...] += jnp.sum(dxn * n, axis=0, keepdims=True)
        dh_ref[...] = dh1_ref[...] + _rms_bwd(n, r, dxn * gv)

        @pl.when(i == T // TT - 1)
        def _():
            for q in range(NQ):
                dw_ref[q] = acc_ref[:, WIN_S * q:WIN_S * (q + 1)]

    return tiled_call(
        "bwd_in", T // TT, body, (dproj, h, g1, w_g, dh1),
        in_specs=[pl.BlockSpec((TT, DIN), lambda i: (i, 0)), pl.BlockSpec((TT, D), lambda i: (i, 0)),
                  _resident((1, D), lambda i: (0, 0)), _resident((NQ, D, WIN_S), lambda i: (0, 0, 0)),
                  pl.BlockSpec((TT, D), lambda i: (i, 0))],
        out_specs=[pl.BlockSpec((TT, D), lambda i: (i, 0)), pl.BlockSpec((NQ, D, WIN_S), lambda i: (0, 0, 0)),
                   pl.BlockSpec((1, D), lambda i: (0, 0))],
        out_shape=[jax.ShapeDtypeStruct((T, D), F32), jax.ShapeDtypeStruct((NQ, D, WIN_S), F32),
                   jax.ShapeDtypeStruct((1, D), F32)],
        scratch=[pltpu.VMEM((D, DIN), BF16), pltpu.VMEM((D, DIN), F32)], job=job)


def _row_block(rows, cols, itemsize=4, budget=2 * 1024 * 1024):
    best = None
    for b in range(16, rows + 1, 16):
        if rows % b == 0 and b * cols * itemsize <= budget:
            best = b
    return best if best is not None else rows


def add_pairs(full, got, dtype):
    core = lax.axis_index("c").astype(jnp.int32).reshape(1)
    outs = []
    for k in range(len(full)):
        q, hr, c = got[k].shape
        rb = _row_block(hr, c)
        nb = hr // rb

        def body(c_ref, a_ref, b_ref, o_ref):
            o_ref[...] = (a_ref[...] + b_ref[...]).astype(dtype)

        outs.append(pl.pallas_call(
            body, name="add_pairs",
            grid_spec=pltpu.PrefetchScalarGridSpec(
                num_scalar_prefetch=1, grid=(q, nb),
                in_specs=[pl.BlockSpec((None, rb, c), lambda qi, i, c_ref, nb=nb: (qi, c_ref[0] * nb + i, 0)),
                          pl.BlockSpec((None, rb, c), lambda qi, i, c_ref: (qi, i, 0))],
                out_specs=pl.BlockSpec((None, rb, c), lambda qi, i, c_ref: (qi, i, 0))),
            out_shape=jax.ShapeDtypeStruct((q, hr, c), dtype),
            compiler_params=_params(("arbitrary", "arbitrary")),
        )(core, full[k], got[k]))
    return outs


def sum_chips(parts):
    core = lax.axis_index("c").astype(jnp.int32).reshape(1)
    outs = []
    for p in parts:
        _, hr, c = p.shape
        rb = _row_block(hr, c)
        nb = hr // rb

        def body(c_ref, p_ref, o_ref):
            s = p_ref[0].astype(F32) + p_ref[1].astype(F32)
            s = s + p_ref[2].astype(F32)
            o_ref[...] = s + p_ref[3].astype(F32)

        outs.append(pl.pallas_call(
            body, name="sum_chips",
            grid_spec=pltpu.PrefetchScalarGridSpec(
                num_scalar_prefetch=1, grid=(nb,),
                in_specs=[pl.BlockSpec((NQ, rb, c), lambda i, c_ref: (0, i, 0))],
                out_specs=pl.BlockSpec((rb, c), lambda i, c_ref, nb=nb: (c_ref[0] * nb + i, 0))),
            out_shape=jax.ShapeDtypeStruct((2 * hr, c), F32),
            compiler_params=_params(("arbitrary",)),
        )(core, p))
    return outs


def adamw(w, g, m, v, job=None):
    r, c = w.shape
    rb = _row_block(r, c, budget=1024 * 1024)
    c1 = 1.0 / (1.0 - ADAM_B1 ** ADAM_STEP)
    c2 = 1.0 / (1.0 - ADAM_B2 ** ADAM_STEP)

    def body(w_ref, g_ref, m_ref, v_ref, d_ref, nm_ref, nv_ref):
        gv = g_ref[...]
        nm = ADAM_B1 * m_ref[...] + (1.0 - ADAM_B1) * gv
        nv = ADAM_B2 * v_ref[...] + (1.0 - ADAM_B2) * (gv * gv)
        nm_ref[...] = nm
        nv_ref[...] = nv
        d_ref[...] = -ADAM_LR * ((nm * c1) / (jnp.sqrt(nv * c2) + ADAM_EPS) + ADAM_WD * w_ref[...])

    spec = pl.BlockSpec((rb, c), lambda i: (i, 0))
    return tiled_call("adamw", r // rb, body, (w, g, m, v), in_specs=[spec] * 4, out_specs=[spec] * 3,
                      out_shape=[jax.ShapeDtypeStruct((r, c), F32)] * 3, job=job)


def _place():
    return lax.axis_index("x"), lax.axis_index("y"), lax.axis_index("c")


def _other_chips(x, y):
    return [(1 - x, y), (x, 1 - y), (1 - x, 1 - y)]


LOCAL_CHUNKS = 4
ICI_CHUNKS = 2
FWD_CHUNKS = 8


def _start_remote(src_rows, dst_rows, rows, chunks, send_sem, recv_sem, dev):
    rc = rows // chunks
    for j in range(chunks):
        pltpu.make_async_remote_copy(src_rows(j * rc, rc), dst_rows(j * rc, rc), send_sem, recv_sem,
                                     device_id=dev, device_id_type=MESH).start()


def _waiter(src, dst, send_sem, recv_sem):
    x, y, c = _place()
    return pltpu.make_async_remote_copy(src, dst, send_sem, recv_sem, device_id=(x, y, c), device_id_type=MESH)


class GatherJob:
    def __init__(self, shards, layer):
        self.shards, self.layer, self.n = list(shards), layer, len(shards)
        self.operands = list(shards)
        self.out_shape = [jax.ShapeDtypeStruct((NQ,) + s.shape[1:], s.dtype) for s in shards]
        sems = pltpu.SemaphoreType.DMA((self.n, 3))
        self.scratch = [sems, sems, sems, sems, pltpu.SemaphoreType.DMA((self.n,))]

    def _half(self, k):
        return self.shards[k].shape[1] // 2

    def _src(self, ins, k, half):
        hr = self._half(k)
        return lambda r0, nr: ins[k].at[self.layer, pl.ds(half * hr + r0, nr), :]

    def _dst(self, outs, k, chip, half):
        hr = self._half(k)
        return lambda r0, nr: outs[k].at[2 * chip[0] + chip[1], pl.ds(half * hr + r0, nr), :]

    def start(self, ins, outs, scr):
        send, recv, fsend, frecv, lsem = scr
        x, y, c = _place()
        for k in range(self.n):
            rows = self.shards[k].shape[1]
            rc = rows // LOCAL_CHUNKS
            for j in range(LOCAL_CHUNKS):
                pltpu.make_async_copy(ins[k].at[self.layer, pl.ds(j * rc, rc), :],
                                      outs[k].at[2 * x + y, pl.ds(j * rc, rc), :], lsem.at[k]).start()
        for k in range(self.n):
            for j, chip in enumerate(_other_chips(x, y)):
                _start_remote(self._src(ins, k, c), self._dst(outs, k, (x, y), c), self._half(k), ICI_CHUNKS,
                              send.at[k, j], recv.at[k, j], (chip[0], chip[1], c))

    def mid(self, ins, outs, scr):
        send, recv, fsend, frecv, lsem = scr
        x, y, c = _place()
        for k in range(self.n):
            hr = self._half(k)
            for j, chip in enumerate(_other_chips(x, y)):
                got = self._dst(outs, k, chip, c)
                _waiter(got(0, hr), got(0, hr), send.at[k, j], recv.at[k, j]).wait_recv()
                _start_remote(got, got, hr, FWD_CHUNKS, fsend.at[k, j], frecv.at[k, j], (x, y, 1 - c))

    def finish(self, ins, outs, scr):
        send, recv, fsend, frecv, lsem = scr
        x, y, c = _place()
        for k in range(self.n):
            hr = self._half(k)
            for j, chip in enumerate(_other_chips(x, y)):
                theirs = self._dst(outs, k, chip, 1 - c)(0, hr)
                w = _waiter(theirs, theirs, fsend.at[k, j], frecv.at[k, j])
                w.wait_recv()
                w.wait_send()
                _waiter(theirs, theirs, send.at[k, j], recv.at[k, j]).wait_send()
            pltpu.make_async_copy(ins[k].at[self.layer], outs[k].at[2 * x + y], lsem.at[k]).wait()


class ExchangeJob:
    def __init__(self, arrs, scatter):
        self.arrs, self.scatter, self.n = list(arrs), list(scatter), len(arrs)
        self.operands = list(arrs)
        self.out_shape = [jax.ShapeDtypeStruct((NQ,) + a.shape[1:], a.dtype) for a in arrs]
        sems = pltpu.SemaphoreType.DMA((self.n, 3))
        self.scratch = [sems, sems, pltpu.SemaphoreType.DMA((self.n,))]

    def _slot(self, ref, s):
        return lambda r0, nr: ref.at[s, pl.ds(r0, nr), :]

    def start(self, ins, outs, scr):
        send, recv, lsem = scr
        x, y, c = _place()
        p = 2 * x + y
        for k in range(self.n):
            rows = self.arrs[k].shape[1]
            rc = rows // ICI_CHUNKS
            for j in range(ICI_CHUNKS):
                pltpu.make_async_copy(ins[k].at[p if self.scatter[k] else 0, pl.ds(j * rc, rc), :],
                                      outs[k].at[p, pl.ds(j * rc, rc), :], lsem.at[k]).start()
            for j, chip in enumerate(_other_chips(x, y)):
                q = 2 * chip[0] + chip[1]
                _start_remote(self._slot(ins[k], q if self.scatter[k] else 0), self._slot(outs[k], p), rows, ICI_CHUNKS,
                              send.at[k, j], recv.at[k, j], (chip[0], chip[1], c))

    def mid(self, ins, outs, scr):
        pass

    def finish(self, ins, outs, scr):
        send, recv, lsem = scr
        x, y, c = _place()
        for k in range(self.n):
            for j, chip in enumerate(_other_chips(x, y)):
                slot = outs[k].at[2 * chip[0] + chip[1]]
                w = _waiter(slot, slot, send.at[k, j], recv.at[k, j])
                w.wait_recv()
                w.wait_send()
            pltpu.make_async_copy(ins[k].at[0], outs[k].at[0], lsem.at[k]).wait()


def run_job(job, name):
    n_in, n_out = len(job.operands), len(job.out_shape)

    def body(*refs):
        ins, outs, scr = refs[:n_in], refs[n_in:n_in + n_out], refs[n_in + n_out:]
        job.start(ins, outs, scr)
        job.mid(ins, outs, scr)
        job.finish(ins, outs, scr)

    return pl.pallas_call(body, name=name, in_specs=[ANY] * n_in, out_specs=[ANY] * n_out, out_shape=job.out_shape,
                          input_output_aliases=dict(getattr(job, "aliases", {})),
                          scratch_shapes=job.scratch)(*job.operands)


def tiled_call(name, steps, body, args, in_specs, out_specs, out_shape, scratch=(), job=None, mid_step=None):
    if job is None:
        return pl.pallas_call(body, name=name, grid=(steps,), in_specs=in_specs, out_specs=out_specs, out_shape=out_shape,
                              scratch_shapes=list(scratch), compiler_params=_params(("arbitrary",)))(*args), []
    n_in, n_out, n_scr = len(args), len(out_shape), len(scratch)
    j_in, j_out = len(job.operands), len(job.out_shape)
    mid_step = steps // 2 if mid_step is None else mid_step

    def carried(*refs):
        a, ji = refs[:n_in], refs[n_in:n_in + j_in]
        o = refs[n_in + j_in:n_in + j_in + n_out]
        jo = refs[n_in + j_in + n_out:n_in + j_in + n_out + j_out]
        rest = refs[n_in + j_in + n_out + j_out:]
        sc, js = rest[:n_scr], rest[n_scr:]
        i = pl.program_id(0)

        @pl.when(i == 0)
        def _():
            job.start(ji, jo, js)

        body(*a, *o, *sc)

        @pl.when(i == mid_step)
        def _():
            job.mid(ji, jo, js)

        @pl.when(i == steps - 1)
        def _():
            job.finish(ji, jo, js)

    res = pl.pallas_call(
        carried, name=name, grid=(steps,), in_specs=list(in_specs) + [ANY] * j_in, out_specs=list(out_specs) + [ANY] * j_out,
        out_shape=list(out_shape) + list(job.out_shape), scratch_shapes=list(scratch) + list(job.scratch),
        input_output_aliases={n_in + i: n_out + o for i, o in getattr(job, "aliases", {}).items()},
        compiler_params=_params(("arbitrary",)))(*args, *job.operands)
    return res[:n_out], res[n_out:]


def gather_small(shard):
    r, c = shard.shape

    def body(in_ref, out_ref, send, recv, lsem):
        x, y, cc = _place()
        own = pltpu.make_async_copy(in_ref, out_ref.at[2 * x + y], lsem)
        own.start()
        sends = []
        for j, chip in enumerate(_other_chips(x, y)):
            cp = pltpu.make_async_remote_copy(in_ref, out_ref.at[2 * x + y], send.at[j], recv.at[j],
                                              device_id=(chip[0], chip[1], cc), device_id_type=MESH)
            cp.start()
            sends.append(cp)
        for j, chip in enumerate(_other_chips(x, y)):
            slot = out_ref.at[2 * chip[0] + chip[1]]
            pltpu.make_async_remote_copy(slot, slot, send.at[j], recv.at[j],
                                         device_id=(chip[0], chip[1], cc), device_id_type=MESH).wait_recv()
        for cp in sends:
            cp.wait_send()
        own.wait()

    vm = pl.BlockSpec(memory_space=pltpu.VMEM)
    return pl.pallas_call(
        body, name="gather_small", in_specs=[vm], out_specs=vm,
        out_shape=jax.ShapeDtypeStruct((NQ, r, c), shard.dtype),
        scratch_shapes=[pltpu.SemaphoreType.DMA((3,)), pltpu.SemaphoreType.DMA((3,)), pltpu.SemaphoreType.DMA],
    )(shard)


D2D_CHUNKS = 4


class SwapJob:
    def __init__(self, arrs):
        self.arrs, self.n = list(arrs), len(arrs)
        self.operands = list(arrs)
        self.out_shape = [jax.ShapeDtypeStruct((a.shape[0], a.shape[1] // 2, a.shape[2]), a.dtype) for a in arrs]
        self.scratch = [pltpu.SemaphoreType.DMA((self.n,)), pltpu.SemaphoreType.DMA((self.n,))]

    def start(self, ins, got, scr):
        send, recv = scr
        x, y, c = _place()
        for k in range(self.n):
            q, r, _ = self.arrs[k].shape
            hr = r // 2
            for qi in range(q):
                _start_remote(lambda r0, nr: ins[k].at[qi, pl.ds((1 - c) * hr + r0, nr), :],
                              lambda r0, nr: got[k].at[qi, pl.ds(r0, nr), :], hr, D2D_CHUNKS,
                              send.at[k], recv.at[k], (x, y, 1 - c))

    def mid(self, ins, got, scr):
        pass

    def finish(self, ins, got, scr):
        send, recv = scr
        for k in range(self.n):
            hr = self.arrs[k].shape[1] // 2
            w = _waiter(ins[k].at[:, pl.ds(0, hr), :], got[k], send.at[k], recv.at[k])
            w.wait_send()
            w.wait_recv()


def swap_halves(arrs):
    return run_job(SwapJob(arrs), "swap_halves")


JOIN_CHUNKS = 8


class JoinJob:
    def __init__(self, arrs):
        self.arrs, self.n = list(arrs), len(arrs)
        self.operands = list(arrs)
        self.out_shape = [jax.ShapeDtypeStruct(a.shape, a.dtype) for a in arrs]
        self.scratch = [pltpu.SemaphoreType.DMA((self.n,)), pltpu.SemaphoreType.DMA((self.n,))]
        self.aliases = {k: k for k in range(self.n)}

    def start(self, ins, outs, scr):
        send, recv = scr
        x, y, c = _place()
        for k in range(self.n):
            hr = self.arrs[k].shape[0] // 2
            rows = lambda r0, nr: outs[k].at[pl.ds(c * hr + r0, nr), :]
            _start_remote(rows, rows, hr, JOIN_CHUNKS, send.at[k], recv.at[k], (x, y, 1 - c))

    def mid(self, ins, outs, scr):
        pass

    def finish(self, ins, outs, scr):
        send, recv = scr
        x, y, c = _place()
        for k in range(self.n):
            hr = self.arrs[k].shape[0] // 2
            w = _waiter(outs[k].at[pl.ds(c * hr, hr), :], outs[k].at[pl.ds((1 - c) * hr, hr), :], send.at[k], recv.at[k])
            w.wait_send()
            w.wait_recv()


class Jobs:
    def __init__(self, jobs):
        self.jobs = list(jobs)
        self.operands = [a for j in self.jobs for a in j.operands]
        self.out_shape = [s for j in self.jobs for s in j.out_shape]
        self.scratch = [s for j in self.jobs for s in j.scratch]
        self.aliases = {}
        i0 = o0 = 0
        for j in self.jobs:
            self.aliases.update({i0 + i: o0 + o for i, o in getattr(j, "aliases", {}).items()})
            i0, o0 = i0 + len(j.operands), o0 + len(j.out_shape)

    def _each(self, ins, outs, scr):
        i0 = o0 = s0 = 0
        for j in self.jobs:
            ni, no, ns = len(j.operands), len(j.out_shape), len(j.scratch)
            yield j, ins[i0:i0 + ni], outs[o0:o0 + no], scr[s0:s0 + ns]
            i0, o0, s0 = i0 + ni, o0 + no, s0 + ns

    def start(self, ins, outs, scr):
        for j, i, o, s in self._each(ins, outs, scr):
            j.start(i, o, s)

    def mid(self, ins, outs, scr):
        for j, i, o, s in self._each(ins, outs, scr):
            j.mid(i, o, s)

    def finish(self, ins, outs, scr):
        for j, i, o, s in self._each(ins, outs, scr):
            j.finish(i, o, s)

    def split(self, results):
        out, o0 = [], 0
        for j in self.jobs:
            out.append(results[o0:o0 + len(j.out_shape)])
            o0 += len(j.out_shape)
        return out


def join_halves(arrs):
    return run_job(JoinJob(arrs), "join_halves")


def _block_diag(w):
    eye = jnp.eye(4, dtype=F32)[None, :, None, :, None]
    return (w.reshape(-1, 4, 64, 1, 64) * eye).reshape(-1, 256, 256).astype(BF16)


def _diag_blocks(b):
    eye = jnp.eye(4, dtype=F32)[None, :, None, :, None]
    return (b.reshape(2, 4, 64, 4, 64) * eye).sum(axis=3).reshape(8, 64, 64)


def _vec_params(pool_b, pool_scale, conv_b, gate_r_b, gate_i_b, lru_lambda, conv_w_full, group_norm_g):
    rows = [pool_b, pool_scale, conv_b, gate_r_b, gate_i_b, lru_lambda,
            conv_w_full[:, 0], conv_w_full[:, 1], conv_w_full[:, 2], conv_w_full[:, 3],
            group_norm_g[:, :DP], group_norm_g[:, DP:]]
    return jnp.pad(jnp.stack(rows, axis=1), ((0, 0), (0, 4), (0, 0)))


SMALL_ROWS_LAYER = 16 + 128 + 64 + 64 + 2 + 2
SMALL_ROWS = 1152


def _pack_small(layers, final_g, meta):
    rows = []
    for vec, pw, wr, wi, g1, g2 in layers:
        rows += [vec, pw.reshape(128, 512), wr.reshape(64, 512), wi.reshape(64, 512), g1.reshape(2, 512), g2.reshape(2, 512)]
    rows += [final_g.reshape(2, 512), meta.reshape(32, 512)]
    flat = jnp.concatenate(rows, axis=0)
    return jnp.concatenate([flat, jnp.zeros((SMALL_ROWS - flat.shape[0], 512), F32)], axis=0)


def _unpack_small(flat):
    layers, o = [], 0
    for _ in range(DEPTH):
        vec = flat[o:o + 16]; o += 16
        pw = flat[o:o + 128].reshape(4, 128, 128); o += 128
        wr = flat[o:o + 64].reshape(8, 64, 64); o += 64
        wi = flat[o:o + 64].reshape(8, 64, 64); o += 64
        g1 = flat[o:o + 2].reshape(1024); o += 2
        g2 = flat[o:o + 2].reshape(1024); o += 2
        layers.append((vec, pw, wr, wi, g1, g2))
    final_g = flat[o:o + 2].reshape(1024); o += 2
    meta = flat[o:o + 32].reshape(16, 1024)
    return layers, final_g, meta


def local_step(x2d, tgt2d, meta_full, conv_w_full, sp, shards=None, gathered=None):
    exchange = gathered is None
    if exchange:
        gathered = [None] * DEPTH
        first_in = run_job(GatherJob(shards[:1], 0), "gather_first")
    h = None
    saved = []
    vp_all = _vec_params(sp["pool_b"], sp["pool_scale"], sp["conv_b"], sp["gate_r_b"], sp["gate_i_b"], sp["lru_lambda"],
                         conv_w_full, sp["group_norm_g"])
    pw_all = sp["pool_w"].astype(BF16)
    br_all = _block_diag(sp["gate_r_w"].reshape(DEPTH * 8, 64, 64)).reshape(DEPTH, 2, 256, 256)
    bi_all = _block_diag(sp["gate_i_w"].reshape(DEPTH * 8, 64, 64)).reshape(DEPTH, 2, 256, 256)
    for l in range(DEPTH):
        vp, pw, br, bi = vp_all[l], pw_all[l], br_all[l], bi_all[l]
        if l == 0:
            job = GatherJob(shards[1:], 0) if exchange else None
            win = first_in[0] if exchange else gathered[0][0]
            (proj, y, hs, aux, h), rest = fwd_mix(x2d, sp["mix_norm_g"][l][None], win, vp, pw, br, bi, job, meta=meta_full)
            wo, wu, wd = rest if exchange else gathered[0][1:]
        else:
            win, wo, wu, wd = gathered[l]
            (proj, y, hs, aux), _ = fwd_mix(h, sp["mix_norm_g"][l][None], win, vp, pw, br, bi)
        wo = wo.reshape(D, D)
        if l + 1 < DEPTH:
            job = GatherJob(shards, l + 1) if exchange else None
            (h1, a, h2), fetched = fwd_post(h, y, wo, sp["mlp_norm_g"][l][None], wu, wd, job)
            if exchange:
                gathered[l + 1] = fetched
        else:
            (h1, a, dh, dgf, loss_part), _ = fwd_post(h, y, wo, sp["mlp_norm_g"][l][None], wu, wd,
                                                      loss=(sp["final_norm_g"][None], tgt2d))
        saved.append((h, proj, y, hs, aux, h1, a, vp, pw, br, bi, win, wo, wu, wd))
        h = h2

    big = [None] * DEPTH
    pair_io = None
    small_layers = [None] * DEPTH
    for l in reversed(range(DEPTH)):
        h0, proj, y, hs, aux, h1, a, vp, pw, br, bi, win, wo, wu, wd = saved[l]
        g1 = sp["mix_norm_g"][l][None]
        job = ExchangeJob(pair_io, [True] * 2) if exchange and pair_io is not None else None
        (dh1, da, xn2, dg2), parts_io = bwd_mlp_dx(dh, h1, a, sp["mlp_norm_g"][l][None], wu, wd, job)
        dwu, dwd = bwd_mlp_dw(xn2, da, a, dh)
        if not exchange:
            (dy, dwo), _ = bwd_out(dh1, y, wo)
            (dproj, ga, dpw, dwr, dwi), _ = bwd_mixer(dy, proj, hs, aux, vp, pw, br, bi)
            (dh, dwin, dg1), _ = bwd_in(dproj, h0, g1, win, dh1)
            big[l] = [dwin, dwo.reshape(NQ, D // NQ, D), dwu, dwd]
        else:
            jobs = Jobs([SwapJob([dwu, dwd])] + ([JoinJob(sum_chips(parts_io))] if job is not None else []))
            (dy, dwo), carried = bwd_out(dh1, y, wo, jobs)
            got_ud = jobs.split(carried)[0]
            if job is not None:
                big[l + 1] = jobs.split(carried)[1] + big[l + 1]
            (dproj, ga, dpw, dwr, dwi), parts_ud = bwd_mixer(dy, proj, hs, aux, vp, pw, br, bi,
                                                             ExchangeJob(add_pairs([dwu, dwd], got_ud, BF16), [True] * 2))
            (dh, dwin, dg1), big[l] = bwd_in(dproj, h0, g1, win, dh1, JoinJob(sum_chips(parts_ud)))
            arrs_io = [dwin, dwo.reshape(NQ, D // NQ, D)]
            pair_io = add_pairs(arrs_io, swap_halves(arrs_io), BF16)
        small_layers[l] = (ga, dpw, _diag_blocks(dwr), _diag_blocks(dwi), dg1[0], dg2[0])
    return loss_part, dh, big, small_layers, dgf, pair_io


def kernel(x, meta_tokens, mix_norm_g, w_in, pool_w, pool_b, pool_scale, conv_w, conv_b, gate_r_w, gate_r_b, gate_i_w, gate_i_b, lru_lambda, group_norm_g, w_out, mlp_norm_g, w_up, w_down, final_norm_g, loss_target, m_meta_tokens, m_mix_norm_g, m_w_in, m_pool_w, m_pool_b, m_pool_scale, m_conv_w, m_conv_b, m_gate_r_w, m_gate_r_b, m_gate_i_w, m_gate_i_b, m_lru_lambda, m_group_norm_g, m_w_out, m_mlp_norm_g, m_w_up, m_w_down, m_final_norm_g, v_meta_tokens, v_mix_norm_g, v_w_in, v_pool_w, v_pool_b, v_pool_scale, v_conv_w, v_conv_b, v_gate_r_w, v_gate_r_b, v_gate_i_w, v_gate_i_b, v_lru_lambda, v_group_norm_g, v_w_out, v_mlp_norm_g, v_w_up, v_w_down, v_final_norm_g):
    p = 2 * lax.axis_index("x") + lax.axis_index("y")

    shards = [w_in.astype(BF16), w_out.astype(BF16), w_up.astype(BF16), w_down.astype(BF16)]
    small = jnp.concatenate([meta_tokens, jnp.pad(conv_w.reshape(16, 128), ((0, 0), (0, 128)))], axis=0)
    small_g = gather_small(small)
    meta_full = jnp.transpose(small_g[:, :16, :], (1, 0, 2)).reshape(NMETA, D)
    conv_w_full = jnp.transpose(small_g[:, 16:, :128].reshape(NQ, DEPTH, 4, 128), (1, 2, 0, 3)).reshape(DEPTH, 4, DR)

    sp = dict(mix_norm_g=mix_norm_g, pool_w=pool_w, pool_b=pool_b, pool_scale=pool_scale, conv_b=conv_b, gate_r_w=gate_r_w,
              gate_r_b=gate_r_b, gate_i_w=gate_i_w, gate_i_b=gate_i_b, lru_lambda=lru_lambda, group_norm_g=group_norm_g,
              mlp_norm_g=mlp_norm_g, final_norm_g=final_norm_g)
    loss_part, dh, big, small_layers, dgf, pair_io0 = local_step(x[0], loss_target[0], meta_full, conv_w_full, sp,
                                                                 shards=shards)
    loss = lax.psum(loss_part[0, 0], ("x", "y", "c"))
    grad_x = dh[NMETA:][None]

    weights = dict(meta_tokens=meta_tokens, mix_norm_g=mix_norm_g, w_in=w_in, pool_w=pool_w, pool_b=pool_b, pool_scale=pool_scale,
                   conv_w=conv_w, conv_b=conv_b, gate_r_w=gate_r_w, gate_r_b=gate_r_b, gate_i_w=gate_i_w, gate_i_b=gate_i_b,
                   lru_lambda=lru_lambda, group_norm_g=group_norm_g, w_out=w_out, mlp_norm_g=mlp_norm_g, w_up=w_up, w_down=w_down,
                   final_norm_g=final_norm_g)
    mom_m = dict(meta_tokens=m_meta_tokens, mix_norm_g=m_mix_norm_g, w_in=m_w_in, pool_w=m_pool_w, pool_b=m_pool_b,
                 pool_scale=m_pool_scale, conv_w=m_conv_w, conv_b=m_conv_b, gate_r_w=m_gate_r_w, gate_r_b=m_gate_r_b,
                 gate_i_w=m_gate_i_w, gate_i_b=m_gate_i_b, lru_lambda=m_lru_lambda, group_norm_g=m_group_norm_g, w_out=m_w_out,
                 mlp_norm_g=m_mlp_norm_g, w_up=m_w_up, w_down=m_w_down, final_norm_g=m_final_norm_g)
    mom_v = dict(meta_tokens=v_meta_tokens, mix_norm_g=v_mix_norm_g, w_in=v_w_in, pool_w=v_pool_w, pool_b=v_pool_b,
                 pool_scale=v_pool_scale, conv_w=v_conv_w, conv_b=v_conv_b, gate_r_w=v_gate_r_w, gate_r_b=v_gate_r_b,
                 gate_i_w=v_gate_i_w, gate_i_b=v_gate_i_b, lru_lambda=v_lru_lambda, group_norm_g=v_group_norm_g, w_out=v_w_out,
                 mlp_norm_g=v_mlp_norm_g, w_up=v_w_up, w_down=v_w_down, final_norm_g=v_final_norm_g)
    names = list(weights)
    delta, new_m, new_v = {}, {}, {}

    def adam_big(nm, g, job=None):
        shp = weights[nm].shape
        two_d = lambda t: t.reshape(shp[0] * shp[1], shp[2])
        (d_, m_, v_), got = adamw(two_d(weights[nm]), two_d(g), two_d(mom_m[nm]), two_d(mom_v[nm]), job)
        delta[nm], new_m[nm], new_v[nm] = d_.reshape(shp), m_.reshape(shp), v_.reshape(shp)
        return got

    small_part = _pack_small(small_layers, dgf[0], dh[:NMETA])[None]
    pair_small = add_pairs([small_part], swap_halves([small_part]), F32)
    g_up = jnp.stack([big[l][-2] for l in range(DEPTH)])
    g_down = jnp.stack([big[l][-1] for l in range(DEPTH)])
    io0 = join_halves(sum_chips(run_job(ExchangeJob(pair_io0, [True] * 2), "exchange_last")))
    small_sum = join_halves(sum_chips(run_job(ExchangeJob(pair_small, [False]), "exchange_small")))[0]
    adam_big("w_up", g_up)
    adam_big("w_down", g_down)
    g_in = jnp.stack([io0[0]] + [big[l][0] for l in range(1, DEPTH)])
    g_out = jnp.stack([io0[1]] + [big[l][1] for l in range(1, DEPTH)])
    adam_big("w_in", g_in)
    adam_big("w_out", g_out)
    g_layers, g_final, g_meta_full = _unpack_small(small_sum)

    g_vec = [gl[0] for gl in g_layers]
    grads = dict(
        meta_tokens=lax.dynamic_slice(g_meta_full, (0, p * (D // NQ)), (NMETA, D // NQ)),
        mix_norm_g=jnp.stack([gl[4] for gl in g_layers]),
        w_in=g_in,
        pool_w=jnp.stack([gl[1] for gl in g_layers]),
        pool_b=jnp.stack([gv[GA_PB] for gv in g_vec]),
        pool_scale=jnp.stack([gv[GA_PS] for gv in g_vec]),
        conv_w=lax.dynamic_slice(jnp.stack([gv[GA_CW:GA_CW + 4] for gv in g_vec]), (0, 0, p * 128), (DEPTH, 4, 128)),
        conv_b=jnp.stack([gv[GA_CB] for gv in g_vec]),
        gate_r_w=jnp.stack([gl[2] for gl in g_layers]),
        gate_r_b=jnp.stack([gv[GA_BR] for gv in g_vec]),
        gate_i_w=jnp.stack([gl[3] for gl in g_layers]),
        gate_i_b=jnp.stack([gv[GA_BI] for gv in g_vec]),
        lru_lambda=jnp.stack([gv[GA_LAM] for gv in g_vec]),
        group_norm_g=jnp.stack([jnp.concatenate([gv[GA_GNP], gv[GA_GNR]]) for gv in g_vec]),
        w_out=g_out,
        mlp_norm_g=jnp.stack([gl[5] for gl in g_layers]),
        w_up=g_up,
        w_down=g_down,
        final_norm_g=g_final,
    )

    small_names = [nm for nm in names if nm not in ("w_in", "w_out", "w_up", "w_down")]
    sizes = [weights[nm].size for nm in small_names]
    total = sum(sizes)
    rows = -(-total // 512)
    rows = -(-rows // 16) * 16

    def flat(tree, fill):
        v_ = jnp.concatenate([tree[nm].reshape(-1) for nm in small_names])
        return jnp.concatenate([v_, jnp.full((rows * 512 - total,), fill, F32)]).reshape(rows, 512)

    (d_, m_, v_), _ = adamw(flat(weights, 0.0), flat(grads, 0.0), flat(mom_m, 0.0), flat(mom_v, 1.0))
    o = 0
    for nm, sz in zip(small_names, sizes):
        shp = weights[nm].shape
        delta[nm] = d_.reshape(-1)[o:o + sz].reshape(shp)
        new_m[nm] = m_.reshape(-1)[o:o + sz].reshape(shp)
        new_v[nm] = v_.reshape(-1)[o:o + sz].reshape(shp)
        o += sz

    return (loss, grad_x, *[grads[nm] for nm in names], *[delta[nm] for nm in names],
            *[new_m[nm] for nm in names], *[new_v[nm] for nm in names])
```

```python
import jax
import jax.numpy as jnp
from jax import lax
from jax.experimental import pallas as pl
from jax.experimental.pallas import tpu as pltpu

F32 = jnp.float32
BF16 = jnp.bfloat16

D = 1024
DP = 512
DR = 512
DIN = 1536
DFF = 4096
DEPTH = 4
NMETA = 16
NQ = 4
WIN_S = DIN // NQ
FF_S = DFF // NQ
EPS = 1e-6
HALO = 16
TT = 432
TTW = 912
VMEM_LIMIT = 56 * 1024 * 1024

ADAM_LR = 0.001
ADAM_B1 = 0.9
ADAM_B2 = 0.999
ADAM_EPS = 1e-08
ADAM_WD = 0.01
ADAM_STEP = 10

MESH = pl.DeviceIdType.MESH
ANY = pl.BlockSpec(memory_space=pl.ANY)


def _params(sem=None, vmem=VMEM_LIMIT):
    return pltpu.CompilerParams(dimension_semantics=sem, vmem_limit_bytes=vmem)


def _resident(shape, index):
    return pl.BlockSpec(shape, index, pipeline_mode=pl.Buffered(1))


def _nt(x, w):
    return lax.dot_general(x, w, (((1,), (1,)), ((), ())), preferred_element_type=F32)


def _tn(a, b):
    return lax.dot_general(a, b, (((0,), (0,)), ((), ())), preferred_element_type=F32)


def _dot(x, w):
    return jnp.dot(x, w, preferred_element_type=F32)


def _rms(h):
    return lax.rsqrt(jnp.mean(h * h, axis=-1, keepdims=True) + EPS)


def _rms_bwd(n, r, dn):
    return r * (dn - n * jnp.mean(dn * n, axis=-1, keepdims=True))


def _down(x, s):
    return pltpu.roll(x, s, 0)


def _up(x, s):
    return pltpu.roll(x, x.shape[0] - s, 0)


GELU_C = 0.7978845608028654
GELU_K = 0.044715


def _gelu(x):
    t = jnp.tanh(GELU_C * (x + GELU_K * x * x * x))
    return 0.5 * x * (1.0 + t), t


def _softplus(x):
    return jnp.maximum(x, 0.0) + jnp.log1p(jnp.exp(-jnp.abs(x)))


VP_PB, VP_PS, VP_CB, VP_BR, VP_BI, VP_LAM, VP_CW, VP_GNP, VP_GNR = 0, 1, 2, 3, 4, 5, 6, 10, 11


def _row(vp, r):
    return vp[r:r + 1, :]


def _mixer_pre(ue, t0, vp, pw_ref, br_ref, bi_ref):
    tt = ue.shape[0] - HALO
    tf = (t0 + lax.broadcasted_iota(jnp.int32, (tt, 1), 0)).astype(F32) + 1.0
    pb, ps = _row(vp, VP_PB), _row(vp, VP_PS)
    pooled, mapped, inv_cnt = [], [], []
    for g in range(4):
        lanes = slice(128 * g, 128 * (g + 1))
        xe = ue[:, lanes]
        s = xe
        for j in range(g + 1):
            s = s + _down(s, 1 << j)
        inv = 1.0 / jnp.minimum(tf, float(2 << g))
        pg = s[HALO:] * inv - xe[HALO:]
        mg = _dot(pg.astype(BF16), pw_ref[g]) + pb[:, lanes]
        pooled.append(pg)
        mapped.append(mg)
        inv_cnt.append(inv)
    ypool = [mapped[g] * ps[:, 128 * g:128 * (g + 1)] for g in range(4)]

    xe = ue[:, DP:DP + DR]
    taps = [_down(xe, 3)[HALO:], _down(xe, 2)[HALO:], _down(xe, 1)[HALO:], xe[HALO:]]
    xc = _row(vp, VP_CB) + (taps[0] * _row(vp, VP_CW) + taps[1] * _row(vp, VP_CW + 1)
                            + taps[2] * _row(vp, VP_CW + 2) + taps[3] * _row(vp, VP_CW + 3))
    xcb = xc.astype(BF16)
    zr = jnp.concatenate([_dot(xcb[:, :256], br_ref[0]), _dot(xcb[:, 256:], br_ref[1])], axis=1) + _row(vp, VP_BR)
    zi = jnp.concatenate([_dot(xcb[:, :256], bi_ref[0]), _dot(xcb[:, 256:], bi_ref[1])], axis=1) + _row(vp, VP_BI)
    r = jax.nn.sigmoid(zr)
    ig = jax.nn.sigmoid(zi)
    sp = _softplus(-_row(vp, VP_LAM))
    la = (-8.0 * r) * sp
    a = jnp.exp(la)
    th = jnp.tanh(la)
    mult = jnp.sqrt((-2.0 * th) / (1.0 - th))
    gate = ue[HALO:, DP + DR:]
    gl, gt = _gelu(gate)
    return dict(pooled=pooled, mapped=mapped, ypool=ypool, inv_cnt=inv_cnt, taps=taps, xc=xc, xcb=xcb, r=r, ig=ig,
                sp=sp, a=a, mult=mult, gate=gate, gl=gl, gt=gt)


SUBLANES = 8


def _scan_fwd(a, b, h_in):
    tt = a.shape[0]
    sub = jnp.bitwise_and(lax.broadcasted_iota(jnp.int32, (tt, 1), 0), SUBLANES - 1)
    s = 1
    while s < SUBLANES:
        m = sub >= s
        a_s = jnp.where(m, _down(a, s), 1.0)
        b_s = jnp.where(m, _down(b, s), 0.0)
        b = a * b_s + b
        a = a * a_s
        s *= 2
    groups, h = [], h_in
    for g in range(tt // SUBLANES):
        rows = slice(SUBLANES * g, SUBLANES * (g + 1))
        hg = a[rows] * h + b[rows]
        groups.append(hg)
        h = hg[SUBLANES - 1:SUBLANES, :]
    return jnp.concatenate(groups, axis=0)


def _scan_rev(c, d, l_in):
    tt = c.shape[0]
    sub = jnp.bitwise_and(lax.broadcasted_iota(jnp.int32, (tt, 1), 0), SUBLANES - 1)
    s = 1
    while s < SUBLANES:
        m = sub < SUBLANES - s
        c_s = jnp.where(m, _up(c, s), 1.0)
        d_s = jnp.where(m, _up(d, s), 0.0)
        d = c * d_s + d
        c = c * c_s
        s *= 2
    groups, l = [], l_in
    for g in reversed(range(tt // SUBLANES)):
        rows = slice(SUBLANES * g, SUBLANES * (g + 1))
        lg = c[rows] * l + d[rows]
        groups.append(lg)
        l = lg[0:1, :]
    return jnp.concatenate(groups[::-1], axis=0)


def _halo_index(i, per_tile):
    return jnp.maximum(i * per_tile - 1, 0)


def _chip_slabs_to_columns(w_ref, wfull_ref):
    for q in range(NQ):
        wfull_ref[:, WIN_S * q:WIN_S * (q + 1)] = w_ref[q]


AUX_POOLED, AUX_MAPPED, AUX_XC, AUX_R, AUX_IG, AUX_A, AUX_MULT, AUX_GL, AUX_GT = range(9)
AUX_W = 9 * 512


def _aux(k):
    return slice(512 * k, 512 * (k + 1))


def fwd_mix(h, g, w_g, vp, pw, br, bi, job=None, meta=None):
    T = h.shape[0] + (NMETA if meta is not None else 0)
    n_h = 3 if meta is not None else 1

    def body(*refs):
        g_ref, w_ref, vp_ref, pw_ref, br_ref, bi_ref, p_ref, y_ref, hs_ref, aux_ref = refs[n_h:n_h + 10]
        wfull_ref, halo_ref, carry_ref = refs[-3:]
        i = pl.program_id(0)

        @pl.when(i == 0)
        def _():
            _chip_slabs_to_columns(w_ref, wfull_ref)
            carry_ref[...] = jnp.zeros_like(carry_ref)
            halo_ref[...] = jnp.zeros_like(halo_ref)

        if meta is not None:
            xp_ref, x_ref, meta_ref = refs[:n_h]
            hh = jnp.concatenate([jnp.where(i == 0, meta_ref[...], xp_ref[...]), x_ref[0:TT - HALO, :]], axis=0)
            refs[n_h + 10][...] = hh
        else:
            hh = refs[0][...]
        xn = (hh * _rms(hh) * g_ref[...]).astype(BF16)
        proj = _dot(xn, wfull_ref[...])
        p_ref[...] = proj
        ue = jnp.concatenate([halo_ref[...], proj], axis=0)
        halo_ref[...] = p_ref[TT - HALO:TT, :]
        vp_v = vp_ref[...]
        m = _mixer_pre(ue, i * TT, vp_v, pw_ref, br_ref, bi_ref)
        for g in range(4):
            aux_ref[:, 512 * AUX_POOLED + 128 * g:512 * AUX_POOLED + 128 * (g + 1)] = m["pooled"][g]
            aux_ref[:, 512 * AUX_MAPPED + 128 * g:512 * AUX_MAPPED + 128 * (g + 1)] = m["mapped"][g]
        for k, name in ((AUX_XC, "xc"), (AUX_R, "r"), (AUX_IG, "ig"), (AUX_A, "a"), (AUX_MULT, "mult"), (AUX_GL, "gl"),
                        (AUX_GT, "gt")):
            aux_ref[:, _aux(k)] = m[name]
        b = m["mult"] * (m["ig"] * m["xc"])
        hs = _scan_fwd(m["a"], b, carry_ref[0:1, :])
        hs_ref[...] = hs
        carry_ref[0:1, :] = hs_ref[TT - 1:TT, :]
        yr = hs * m["gl"]
        ssq = sum(jnp.sum(yp * yp, axis=-1, keepdims=True) for yp in m["ypool"])
        rp = lax.rsqrt(ssq * (1.0 / DP) + EPS)
        gnp = _row(vp_v, VP_GNP)
        for g in range(4):
            lanes = slice(128 * g, 128 * (g + 1))
            y_ref[:, lanes] = (m["ypool"][g] * rp * gnp[:, lanes]).astype(BF16)
        y_ref[:, DP:] = (yr * _rms(yr) * _row(vp_v, VP_GNR)).astype(BF16)

    if meta is not None:
        h_args, h_specs = (h, h, meta), _shifted_specs() + [_resident((NMETA, D), lambda i: (0, 0))]
    else:
        h_args, h_specs = (h,), [pl.BlockSpec((TT, D), lambda i: (i, 0))]
    full = meta is not None
    return tiled_call(
        "fwd_mix", T // TT, body, h_args + (g, w_g, vp, pw, br, bi),
        in_specs=h_specs + [_resident((1, D), lambda i: (0, 0)), _resident((NQ, D, WIN_S), lambda i: (0, 0, 0)),
                            _resident((16, 512), lambda i: (0, 0)), _resident((4, 128, 128), lambda i: (0, 0, 0)),
                            _resident((2, 256, 256), lambda i: (0, 0, 0)), _resident((2, 256, 256), lambda i: (0, 0, 0))],
        out_specs=[pl.BlockSpec((TT, DIN), lambda i: (i, 0)), pl.BlockSpec((TT, D), lambda i: (i, 0)),
                   pl.BlockSpec((TT, DR), lambda i: (i, 0)), pl.BlockSpec((TT, AUX_W), lambda i: (i, 0))]
        + [pl.BlockSpec((TT, D), lambda i: (i, 0))] * full,
        out_shape=[jax.ShapeDtypeStruct((T, DIN), F32), jax.ShapeDtypeStruct((T, D), BF16),
                   jax.ShapeDtypeStruct((T, DR), F32), jax.ShapeDtypeStruct((T, AUX_W), F32)]
        + [jax.ShapeDtypeStruct((T, D), F32)] * full,
        scratch=[pltpu.VMEM((D, DIN), BF16), pltpu.VMEM((HALO, DIN), F32), pltpu.VMEM((8, DR), F32)],
        job=job, mid_step=T // TT - 1)


def fwd_post(h, y, wo_g, g2, wu_g, wd_g, job=None, loss=None):
    T = h.shape[0]

    def body(*refs):
        h_ref, y_ref, wo_ref, g_ref, wu_ref, wd_ref = refs[:6]
        h1_ref, a_ref = refs[-5:-3] if loss is not None else refs[-3:-1]
        h1 = h_ref[...] + _dot(y_ref[...], wo_ref[...])
        h1_ref[...] = h1
        xn = (h1 * _rms(h1) * g_ref[...]).astype(BF16)
        acc = h1
        for q in range(NQ):
            a = _dot(xn, wu_ref[q])
            a_ref[:, FF_S * q:FF_S * (q + 1)] = a.astype(BF16)
            ra = jnp.maximum(a, 0.0)
            acc = acc + _dot((ra * ra).astype(BF16), wd_ref[q])
        if loss is None:
            refs[-1][...] = acc
        else:
            _loss_tile(acc, *refs[6:9], *refs[-3:])

    row = pl.BlockSpec((TT, D), lambda i: (i, 0))
    in_specs = [row, row, _resident((D, D), lambda i: (0, 0)), _resident((1, D), lambda i: (0, 0)),
                _resident((NQ, D, FF_S), lambda i: (0, 0, 0)), _resident((NQ, FF_S, D), lambda i: (0, 0, 0))]
    out_specs = [row, pl.BlockSpec((TT, DFF), lambda i: (i, 0)), row]
    out_shape = [jax.ShapeDtypeStruct((T, D), F32), jax.ShapeDtypeStruct((T, DFF), BF16), jax.ShapeDtypeStruct((T, D), F32)]
    args = (h, y, wo_g, g2, wu_g, wd_g)
    if loss is not None:
        args += (loss[0], loss[1], loss[1])
        in_specs += [_resident((1, D), lambda i: (0, 0))] + _shifted_specs()
        out_specs += [pl.BlockSpec((1, D), lambda i: (0, 0)), pl.BlockSpec((1, 1), lambda i: (0, 0))]
        out_shape += [jax.ShapeDtypeStruct((1, D), F32), jax.ShapeDtypeStruct((1, 1), F32)]
    return tiled_call("fwd_post", T // TT, body, args, in_specs=in_specs, out_specs=out_specs, out_shape=out_shape,
                      job=job, mid_step=(T // TT) * 3 // 4)


def _shifted_tile(prev_ref, cur_ref):
    return jnp.concatenate([prev_ref[...], cur_ref[0:TT - HALO, :]], axis=0)


def _shifted_specs():
    per = TT // HALO
    return [pl.BlockSpec((HALO, D), lambda i: (_halo_index(i, per), 0)), pl.BlockSpec((TT, D), lambda i: (i, 0))]


def _loss_tile(hh, g_ref, tp_ref, t_ref, dh_ref, dg_ref, loss_ref):
    i = pl.program_id(0)

    @pl.when(i == 0)
    def _():
        dg_ref[...] = jnp.zeros_like(dg_ref)
        loss_ref[...] = jnp.zeros_like(loss_ref)

    r = _rms(hh)
    n = hh * r
    gfv = g_ref[...]
    row = i * TT + lax.broadcasted_iota(jnp.int32, (TT, 1), 0)
    e = jnp.where(row >= NMETA, n * gfv - _shifted_tile(tp_ref, t_ref), 0.0)
    loss_ref[...] += 0.5 * jnp.sum(jnp.sum(e * e, axis=-1, keepdims=True) * (1.0 / D), axis=0, keepdims=True)
    dy = e * (1.0 / D)
    dg_ref[...] += jnp.sum(dy * n, axis=0, keepdims=True)
    dh_ref[...] = _rms_bwd(n, r, dy * gfv)


def bwd_mlp_dx(dh2, h1, a, g2, wu_g, wd_g, job=None):
    T = dh2.shape[0]

    def body(dh2_ref, h1_ref, a_ref, g_ref, wu_ref, wd_ref, dh1_ref, da_ref, xn_ref, dg_ref):
        @pl.when(pl.program_id(0) == 0)
        def _():
            dg_ref[...] = jnp.zeros_like(dg_ref)

        h1v = h1_ref[...]
        r = _rms(h1v)
        n = h1v * r
        gv = g_ref[...]
        xn_ref[...] = (n * gv).astype(BF16)
        dh2v = dh2_ref[...]
        dh2b = dh2v.astype(BF16)
        dxn = jnp.zeros((TT, D), F32)
        for q in range(NQ):
            cols = slice(FF_S * q, FF_S * (q + 1))
            ra = jnp.maximum(a_ref[:, cols].astype(F32), 0.0)
            da = (_nt(dh2b, wd_ref[q]) * (2.0 * ra)).astype(BF16)
            da_ref[:, cols] = da
            dxn = dxn + _nt(da, wu_ref[q])
        dg_ref[...] += jnp.sum(dxn * n, axis=0, keepdims=True)
        dh1_ref[...] = dh2v + _rms_bwd(n, r, dxn * gv)

    return tiled_call(
        "bwd_mlp_dx", T // TT, body, (dh2, h1, a, g2, wu_g, wd_g),
        in_specs=[pl.BlockSpec((TT, D), lambda i: (i, 0)), pl.BlockSpec((TT, D), lambda i: (i, 0)),
                  pl.BlockSpec((TT, DFF), lambda i: (i, 0)), _resident((1, D), lambda i: (0, 0)),
                  _resident((NQ, D, FF_S), lambda i: (0, 0, 0)), _resident((NQ, FF_S, D), lambda i: (0, 0, 0))],
        out_specs=[pl.BlockSpec((TT, D), lambda i: (i, 0)), pl.BlockSpec((TT, DFF), lambda i: (i, 0)),
                   pl.BlockSpec((TT, D), lambda i: (i, 0)), pl.BlockSpec((1, D), lambda i: (0, 0))],
        out_shape=[jax.ShapeDtypeStruct((T, D), F32), jax.ShapeDtypeStruct((T, DFF), BF16),
                   jax.ShapeDtypeStruct((T, D), BF16), jax.ShapeDtypeStruct((1, D), F32)],
        job=job)


def bwd_mlp_dw(xn, da, a, dh2):
    T = xn.shape[0]

    def body(xn_ref, da_ref, a_ref, dh2_ref, dwu_ref, dwd_ref):
        @pl.when(pl.program_id(1) == 0)
        def _():
            dwu_ref[...] = jnp.zeros_like(dwu_ref)
            dwd_ref[...] = jnp.zeros_like(dwd_ref)

        dwu_ref[...] += _tn(xn_ref[...], da_ref[...])
        ra = jnp.maximum(a_ref[...].astype(F32), 0.0)
        dwd_ref[...] += _tn((ra * ra).astype(BF16), dh2_ref[...].astype(BF16))

    return pl.pallas_call(
        body, name="bwd_mlp_dw", grid=(NQ, T // TTW),
        in_specs=[pl.BlockSpec((TTW, D), lambda q, i: (i, 0)), pl.BlockSpec((TTW, FF_S), lambda q, i: (i, q)),
                  pl.BlockSpec((TTW, FF_S), lambda q, i: (i, q)), pl.BlockSpec((TTW, D), lambda q, i: (i, 0))],
        out_specs=[pl.BlockSpec((None, D, FF_S), lambda q, i: (q, 0, 0)),
                   pl.BlockSpec((None, FF_S, D), lambda q, i: (q, 0, 0))],
        out_shape=[jax.ShapeDtypeStruct((NQ, D, FF_S), F32), jax.ShapeDtypeStruct((NQ, FF_S, D), F32)],
        compiler_params=_params(("arbitrary", "arbitrary")),
    )(xn, da, a, dh2)


def bwd_out(dh1, y, wo_g, job=None):
    T = dh1.shape[0]

    def body(dh_ref, y_ref, wo_ref, dy_ref, dwo_ref):
        @pl.when(pl.program_id(0) == 0)
        def _():
            dwo_ref[...] = jnp.zeros_like(dwo_ref)

        dhb = dh_ref[...].astype(BF16)
        dy_ref[...] = _nt(dhb, wo_ref[...])
        dwo_ref[...] += _tn(y_ref[...], dhb)

    return tiled_call(
        "bwd_out", T // TT, body, (dh1, y, wo_g),
        in_specs=[pl.BlockSpec((TT, D), lambda i: (i, 0)), pl.BlockSpec((TT, D), lambda i: (i, 0)),
                  _resident((D, D), lambda i: (0, 0))],
        out_specs=[pl.BlockSpec((TT, D), lambda i: (i, 0)), pl.BlockSpec((D, D), lambda i: (0, 0))],
        out_shape=[jax.ShapeDtypeStruct((T, D), F32), jax.ShapeDtypeStruct((D, D), F32)], job=job)


GA_PS, GA_PB, GA_CB, GA_BR, GA_BI, GA_LAM, GA_CW, GA_GNP, GA_GNR = 0, 1, 2, 3, 4, 5, 6, 10, 11


def bwd_mixer(dy, proj, hs, aux, vp, pw, br, bi, job=None):
    T = proj.shape[0]
    tt = TT
    nt = T // tt
    per = tt // HALO

    def body(dy_ref, p_ref, ph_ref, hs_ref, hsh_ref, aux_ref, vp_ref, pw_ref, br_ref, bi_ref,
             dp_ref, ga_ref, dpw_ref, dwr_ref, dwi_ref, lam_ref, q_ref, dxc_ref):
        s = pl.program_id(0)
        ti = nt - 1 - s

        @pl.when(s == 0)
        def _():
            for ref in (ga_ref, dpw_ref, dwr_ref, dwi_ref, lam_ref, q_ref, dxc_ref):
                ref[...] = jnp.zeros_like(ref)

        dyv = dy_ref[...]
        lam_in, q_in, dxc_in = lam_ref[0:1, :], q_ref[...], dxc_ref[...]
        first = ti > 0
        vp_v = vp_ref[...]
        ur = jnp.concatenate([jnp.where(first, ph_ref[:, DP:DP + DR], 0.0), p_ref[:, DP:DP + DR]], axis=0)
        tf = (ti * tt + lax.broadcasted_iota(jnp.int32, (tt, 1), 0)).astype(F32) + 1.0
        mapped = [aux_ref[:, 512 * AUX_MAPPED + 128 * g:512 * AUX_MAPPED + 128 * (g + 1)] for g in range(4)]
        ps_row = _row(vp_v, VP_PS)
        xc = aux_ref[:, _aux(AUX_XC)]
        m = dict(pooled=[aux_ref[:, 512 * AUX_POOLED + 128 * g:512 * AUX_POOLED + 128 * (g + 1)] for g in range(4)],
                 mapped=mapped, ypool=[mapped[g] * ps_row[:, 128 * g:128 * (g + 1)] for g in range(4)],
                 inv_cnt=[1.0 / jnp.minimum(tf, float(2 << g)) for g in range(4)],
                 taps=[_down(ur, 3)[HALO:], _down(ur, 2)[HALO:], _down(ur, 1)[HALO:], ur[HALO:]],
                 xc=xc, xcb=xc.astype(BF16), r=aux_ref[:, _aux(AUX_R)], ig=aux_ref[:, _aux(AUX_IG)],
                 a=aux_ref[:, _aux(AUX_A)], mult=aux_ref[:, _aux(AUX_MULT)], gl=aux_ref[:, _aux(AUX_GL)],
                 gt=aux_ref[:, _aux(AUX_GT)], gate=p_ref[:, DP + DR:], sp=_softplus(-_row(vp_v, VP_LAM)))
        hs_v = hs_ref[...]
        hprev = _down(jnp.concatenate([jnp.where(first, hsh_ref[...], 0.0), hs_v], axis=0), 1)[HALO:]

        def acc(rw, v):
            ga_ref[rw:rw + 1, :] += jnp.sum(v, axis=0, keepdims=True)

        gnp = _row(vp_v, VP_GNP)
        ps = _row(vp_v, VP_PS)
        ssq = sum(jnp.sum(yp * yp, axis=-1, keepdims=True) for yp in m["ypool"])
        rp = lax.rsqrt(ssq * (1.0 / DP) + EPS)
        npool = [yp * rp for yp in m["ypool"]]
        dnp = [dyv[:, 128 * g:128 * (g + 1)] * gnp[:, 128 * g:128 * (g + 1)] for g in range(4)]
        mean_dn = sum(jnp.sum(dnp[g] * npool[g], axis=-1, keepdims=True) for g in range(4)) * (1.0 / DP)
        for g in range(4):
            lanes = slice(128 * g, 128 * (g + 1))
            ga_ref[GA_GNP:GA_GNP + 1, lanes] += jnp.sum(dyv[:, lanes] * npool[g], axis=0, keepdims=True)
            dyp = rp * (dnp[g] - npool[g] * mean_dn)
            ga_ref[GA_PS:GA_PS + 1, lanes] += jnp.sum(dyp * m["mapped"][g], axis=0, keepdims=True)
            dmap = dyp * ps[:, lanes]
            ga_ref[GA_PB:GA_PB + 1, lanes] += jnp.sum(dmap, axis=0, keepdims=True)
            dmb = dmap.astype(BF16)
            dpw_ref[g] += _tn(m["pooled"][g].astype(BF16), dmb)
            dpool = _nt(dmb, pw_ref[g])
            qv = dpool * m["inv_cnt"][g]
            win = jnp.concatenate([qv, q_in[:, lanes]], axis=0)
            for j in range(g + 1):
                win = win + _up(win, 1 << j)
            q_ref[:, lanes] = qv[:HALO]
            dp_ref[:, lanes] = (win[:tt] - dpool).astype(BF16)

        gnr = _row(vp_v, VP_GNR)
        yr = hs_v * m["gl"]
        rr = _rms(yr)
        nr = yr * rr
        dyr_out = dyv[:, DP:]
        acc(GA_GNR, dyr_out * nr)
        dyr = _rms_bwd(nr, rr, dyr_out * gnr)
        gate, gt = m["gate"], m["gt"]
        dgl = 0.5 * (1.0 + gt) + 0.5 * gate * (1.0 - gt * gt) * (GELU_C * (1.0 + 3.0 * GELU_K * gate * gate))
        dp_ref[:, DP + DR:] = (dyr * hs_v * dgl).astype(BF16)
        dhs = dyr * m["gl"]
        a = m["a"]
        row = lax.broadcasted_iota(jnp.int32, (tt, 1), 0)
        c_next = jnp.where(row < tt - 1, _up(a, 1), 1.0)
        lam = _scan_rev(c_next, dhs, lam_in)
        lam_ref[0:1, :] = a[0:1, :] * lam[0:1, :]
        xc, ig, r, mult = m["xc"], m["ig"], m["r"], m["mult"]
        dmult = lam * ig * xc
        dig = lam * mult * xc
        dxc = lam * mult * ig
        dla = lam * hprev * a - dmult * (a * a) / mult
        acc(GA_LAM, dla * r)
        dzr = (dla * (-8.0 * m["sp"])) * (r * (1.0 - r))
        dzi = dig * (ig * (1.0 - ig))
        acc(GA_BR, dzr)
        acc(GA_BI, dzi)
        dzrb, dzib = dzr.astype(BF16), dzi.astype(BF16)
        xcb = m["xcb"]
        halves = []
        for k in range(2):
            lanes = slice(256 * k, 256 * (k + 1))
            dwr_ref[k] += _tn(xcb[:, lanes], dzrb[:, lanes])
            dwi_ref[k] += _tn(xcb[:, lanes], dzib[:, lanes])
            halves.append(_nt(dzrb[:, lanes], br_ref[k]) + _nt(dzib[:, lanes], bi_ref[k]))
        dxc = dxc + jnp.concatenate(halves, axis=1)
        acc(GA_CB, dxc)
        for k in range(4):
            acc(GA_CW + k, dxc * m["taps"][k])
        dxe = jnp.concatenate([dxc, dxc_in], axis=0)
        du = (_up(dxe, 3)[:tt] * _row(vp_v, VP_CW) + _up(dxe, 2)[:tt] * _row(vp_v, VP_CW + 1)
              + _up(dxe, 1)[:tt] * _row(vp_v, VP_CW + 2) + dxc * _row(vp_v, VP_CW + 3))
        dxc_ref[...] = dxc[:HALO]
        dp_ref[:, DP:DP + DR] = du.astype(BF16)

        @pl.when(s == nt - 1)
        def _():
            lamp = _row(vp_v, VP_LAM)
            ga_ref[GA_LAM:GA_LAM + 1, :] = ga_ref[GA_LAM:GA_LAM + 1, :] * (8.0 * jax.nn.sigmoid(-lamp))

    rev = lambda i: (nt - 1 - i, 0)
    rev_halo = lambda i: (_halo_index(nt - 1 - i, per), 0)
    return tiled_call(
        "bwd_mixer", nt, body, (dy, proj, proj, hs, hs, aux, vp, pw, br, bi),
        in_specs=[pl.BlockSpec((tt, D), rev), pl.BlockSpec((tt, DIN), rev), pl.BlockSpec((HALO, DIN), rev_halo),
                  pl.BlockSpec((tt, DR), rev), pl.BlockSpec((HALO, DR), rev_halo), pl.BlockSpec((tt, AUX_W), rev),
                  _resident((16, 512), lambda i: (0, 0)), _resident((4, 128, 128), lambda i: (0, 0, 0)),
                  _resident((2, 256, 256), lambda i: (0, 0, 0)), _resident((2, 256, 256), lambda i: (0, 0, 0))],
        out_specs=[pl.BlockSpec((tt, DIN), rev), pl.BlockSpec((16, 512), lambda i: (0, 0)),
                   pl.BlockSpec((4, 128, 128), lambda i: (0, 0, 0)), pl.BlockSpec((2, 256, 256), lambda i: (0, 0, 0)),
                   pl.BlockSpec((2, 256, 256), lambda i: (0, 0, 0))],
        out_shape=[jax.ShapeDtypeStruct((T, DIN), BF16), jax.ShapeDtypeStruct((16, 512), F32),
                   jax.ShapeDtypeStruct((4, 128, 128), F32), jax.ShapeDtypeStruct((2, 256, 256), F32),
                   jax.ShapeDtypeStruct((2, 256, 256), F32)],
        scratch=[pltpu.VMEM((8, DR), F32), pltpu.VMEM((HALO, DP), F32), pltpu.VMEM((HALO, DR), F32)], job=job)


def bwd_in(dproj, h, g1, w_g, dh1, job=None):
    T = h.shape[0]

    def body(dp_ref, h_ref, g_ref, w_ref, dh1_ref, dh_ref, dw_ref, dg_ref, wfull_ref, acc_ref):
        i = pl.program_id(0)

        @pl.when(i == 0)
        def _():
            _chip_slabs_to_columns(w_ref, wfull_ref)
            acc_ref[...] = jnp.zeros_like(acc_ref)
            dg_ref[...] = jnp.zeros_like(dg_ref)

        hv = h_ref[...]
        r = _rms(hv)
        n = hv * r
        gv = g_ref[...]
        xn = (n * gv).astype(BF16)
        dpv = dp_ref[...]
        dxn = _nt(dpv, wfull_ref[...])
        acc_ref[...] += _tn(xn, dpv)
        dg_ref[...] += jnp.sum(dxn * n, axis=0, keepdims=True)
        dh_ref[...] = dh1_ref[...] + _rms_bwd(n, r, dxn * gv)

        @pl.when(i == T // TT - 1)
        def _():
            for q in range(NQ):
                dw_ref[q] = acc_ref[:, WIN_S * q:WIN_S * (q + 1)]

    return tiled_call(
        "bwd_in", T // TT, body, (dproj, h, g1, w_g, dh1),
        in_specs=[pl.BlockSpec((TT, DIN), lambda i: (i, 0)), pl.BlockSpec((TT, D), lambda i: (i, 0)),
                  _resident((1, D), lambda i: (0, 0)), _resident((NQ, D, WIN_S), lambda i: (0, 0, 0)),
                  pl.BlockSpec((TT, D), lambda i: (i, 0))],
        out_specs=[pl.BlockSpec((TT, D), lambda i: (i, 0)), pl.BlockSpec((NQ, D, WIN_S), lambda i: (0, 0, 0)),
                   pl.BlockSpec((1, D), lambda i: (0, 0))],
        out_shape=[jax.ShapeDtypeStruct((T, D), F32), jax.ShapeDtypeStruct((NQ, D, WIN_S), F32),
                   jax.ShapeDtypeStruct((1, D), F32)],
        scratch=[pltpu.VMEM((D, DIN), BF16), pltpu.VMEM((D, DIN), F32)], job=job)


def _row_block(rows, cols, itemsize=4, budget=2 * 1024 * 1024):
    best = None
    for b in range(16, rows + 1, 16):
        if rows % b == 0 and b * cols * itemsize <= budget:
            best = b
    return best if best is not None else rows


def add_pairs(full, got, dtype):
    core = lax.axis_index("c").astype(jnp.int32).reshape(1)
    outs = []
    for k in range(len(full)):
        q, hr, c = got[k].shape
        rb = _row_block(hr, c)
        nb = hr // rb

        def body(c_ref, a_ref, b_ref, o_ref):
            o_ref[...] = (a_ref[...] + b_ref[...]).astype(dtype)

        outs.append(pl.pallas_call(
            body, name="add_pairs",
            grid_spec=pltpu.PrefetchScalarGridSpec(
                num_scalar_prefetch=1, grid=(q, nb),
                in_specs=[pl.BlockSpec((None, rb, c), lambda qi, i, c_ref, nb=nb: (qi, c_ref[0] * nb + i, 0)),
                          pl.BlockSpec((None, rb, c), lambda qi, i, c_ref: (qi, i, 0))],
                out_specs=pl.BlockSpec((None, rb, c), lambda qi, i, c_ref: (qi, i, 0))),
            out_shape=jax.ShapeDtypeStruct((q, hr, c), dtype),
            compiler_params=_params(("arbitrary", "arbitrary")),
        )(core, full[k], got[k]))
    return outs


def sum_chips(parts):
    core = lax.axis_index("c").astype(jnp.int32).reshape(1)
    outs = []
    for p in parts:
        _, hr, c = p.shape
        rb = _row_block(hr, c)
        nb = hr // rb

        def body(c_ref, p_ref, o_ref):
            s = p_ref[0].astype(F32) + p_ref[1].astype(F32)
            s = s + p_ref[2].astype(F32)
            o_ref[...] = s + p_ref[3].astype(F32)

        outs.append(pl.pallas_call(
            body, name="sum_chips",
            grid_spec=pltpu.PrefetchScalarGridSpec(
                num_scalar_prefetch=1, grid=(nb,),
                in_specs=[pl.BlockSpec((NQ, rb, c), lambda i, c_ref: (0, i, 0))],
                out_specs=pl.BlockSpec((rb, c), lambda i, c_ref, nb=nb: (c_ref[0] * nb + i, 0))),
            out_shape=jax.ShapeDtypeStruct((2 * hr, c), F32),
            compiler_params=_params(("arbitrary",)),
        )(core, p))
    return outs


def adamw(w, g, m, v, job=None):
    r, c = w.shape
    rb = _row_block(r, c, budget=1024 * 1024)
    c1 = 1.0 / (1.0 - ADAM_B1 ** ADAM_STEP)
    c2 = 1.0 / (1.0 - ADAM_B2 ** ADAM_STEP)

    def body(w_ref, g_ref, m_ref, v_ref, d_ref, nm_ref, nv_ref):
        gv = g_ref[...]
        nm = ADAM_B1 * m_ref[...] + (1.0 - ADAM_B1) * gv
        nv = ADAM_B2 * v_ref[...] + (1.0 - ADAM_B2) * (gv * gv)
        nm_ref[...] = nm
        nv_ref[...] = nv
        d_ref[...] = -ADAM_LR * ((nm * c1) / (jnp.sqrt(nv * c2) + ADAM_EPS) + ADAM_WD * w_ref[...])

    spec = pl.BlockSpec((rb, c), lambda i: (i, 0))
    return tiled_call("adamw", r // rb, body, (w, g, m, v), in_specs=[spec] * 4, out_specs=[spec] * 3,
                      out_shape=[jax.ShapeDtypeStruct((r, c), F32)] * 3, job=job)


def _place():
    return lax.axis_index("x"), lax.axis_index("y"), lax.axis_index("c")


def _other_chips(x, y):
    return [(1 - x, y), (x, 1 - y), (1 - x, 1 - y)]


LOCAL_CHUNKS = 4
ICI_CHUNKS = 2
FWD_CHUNKS = 8


def _start_remote(src_rows, dst_rows, rows, chunks, send_sem, recv_sem, dev):
    rc = rows // chunks
    for j in range(chunks):
        pltpu.make_async_remote_copy(src_rows(j * rc, rc), dst_rows(j * rc, rc), send_sem, recv_sem,
                                     device_id=dev, device_id_type=MESH).start()


def _waiter(src, dst, send_sem, recv_sem):
    x, y, c = _place()
    return pltpu.make_async_remote_copy(src, dst, send_sem, recv_sem, device_id=(x, y, c), device_id_type=MESH)


class GatherJob:
    def __init__(self, shards, layer):
        self.shards, self.layer, self.n = list(shards), layer, len(shards)
        self.operands = list(shards)
        self.out_shape = [jax.ShapeDtypeStruct((NQ,) + s.shape[1:], s.dtype) for s in shards]
        sems = pltpu.SemaphoreType.DMA((self.n, 3))
        self.scratch = [sems, sems, sems, sems, pltpu.SemaphoreType.DMA((self.n,))]

    def _half(self, k):
        return self.shards[k].shape[1] // 2

    def _src(self, ins, k, half):
        hr = self._half(k)
        return lambda r0, nr: ins[k].at[self.layer, pl.ds(half * hr + r0, nr), :]

    def _dst(self, outs, k, chip, half):
        hr = self._half(k)
        return lambda r0, nr: outs[k].at[2 * chip[0] + chip[1], pl.ds(half * hr + r0, nr), :]

    def start(self, ins, outs, scr):
        send, recv, fsend, frecv, lsem = scr
        x, y, c = _place()
        for k in range(self.n):
            rows = self.shards[k].shape[1]
            rc = rows // LOCAL_CHUNKS
            for j in range(LOCAL_CHUNKS):
                pltpu.make_async_copy(ins[k].at[self.layer, pl.ds(j * rc, rc), :],
                                      outs[k].at[2 * x + y, pl.ds(j * rc, rc), :], lsem.at[k]).start()
        for k in range(self.n):
            for j, chip in enumerate(_other_chips(x, y)):
                _start_remote(self._src(ins, k, c), self._dst(outs, k, (x, y), c), self._half(k), ICI_CHUNKS,
                              send.at[k, j], recv.at[k, j], (chip[0], chip[1], c))

    def mid(self, ins, outs, scr):
        send, recv, fsend, frecv, lsem = scr
        x, y, c = _place()
        for k in range(self.n):
            hr = self._half(k)
            for j, chip in enumerate(_other_chips(x, y)):
                got = self._dst(outs, k, chip, c)
                _waiter(got(0, hr), got(0, hr), send.at[k, j], recv.at[k, j]).wait_recv()
                _start_remote(got, got, hr, FWD_CHUNKS, fsend.at[k, j], frecv.at[k, j], (x, y, 1 - c))

    def finish(self, ins, outs, scr):
        send, recv, fsend, frecv, lsem = scr
        x, y, c = _place()
        for k in range(self.n):
            hr = self._half(k)
            for j, chip in enumerate(_other_chips(x, y)):
                theirs = self._dst(outs, k, chip, 1 - c)(0, hr)
                w = _waiter(theirs, theirs, fsend.at[k, j], frecv.at[k, j])
                w.wait_recv()
                w.wait_send()
                _waiter(theirs, theirs, send.at[k, j], recv.at[k, j]).wait_send()
            pltpu.make_async_copy(ins[k].at[self.layer], outs[k].at[2 * x + y], lsem.at[k]).wait()


class ExchangeJob:
    def __init__(self, arrs, scatter):
        self.arrs, self.scatter, self.n = list(arrs), list(scatter), len(arrs)
        self.operands = list(arrs)
        self.out_shape = [jax.ShapeDtypeStruct((NQ,) + a.shape[1:], a.dtype) for a in arrs]
        sems = pltpu.SemaphoreType.DMA((self.n, 3))
        self.scratch = [sems, sems, pltpu.SemaphoreType.DMA((self.n,))]

    def _slot(self, ref, s):
        return lambda r0, nr: ref.at[s, pl.ds(r0, nr), :]

    def start(self, ins, outs, scr):
        send, recv, lsem = scr
        x, y, c = _place()
        p = 2 * x + y
        for k in range(self.n):
            rows = self.arrs[k].shape[1]
            rc = rows // ICI_CHUNKS
            for j in range(ICI_CHUNKS):
                pltpu.make_async_copy(ins[k].at[p if self.scatter[k] else 0, pl.ds(j * rc, rc), :],
                                      outs[k].at[p, pl.ds(j * rc, rc), :], lsem.at[k]).start()
            for j, chip in enumerate(_other_chips(x, y)):
                q = 2 * chip[0] + chip[1]
                _start_remote(self._slot(ins[k], q if self.scatter[k] else 0), self._slot(outs[k], p), rows, ICI_CHUNKS,
                              send.at[k, j], recv.at[k, j], (chip[0], chip[1], c))

    def mid(self, ins, outs, scr):
        pass

    def finish(self, ins, outs, scr):
        send, recv, lsem = scr
        x, y, c = _place()
        for k in range(self.n):
            for j, chip in enumerate(_other_chips(x, y)):
                slot = outs[k].at[2 * chip[0] + chip[1]]
                w = _waiter(slot, slot, send.at[k, j], recv.at[k, j])
                w.wait_recv()
                w.wait_send()
            pltpu.make_async_copy(ins[k].at[0], outs[k].at[0], lsem.at[k]).wait()


def run_job(job, name):
    n_in, n_out = len(job.operands), len(job.out_shape)

    def body(*refs):
        ins, outs, scr = refs[:n_in], refs[n_in:n_in + n_out], refs[n_in + n_out:]
        job.start(ins, outs, scr)
        job.mid(ins, outs, scr)
        job.finish(ins, outs, scr)

    return pl.pallas_call(body, name=name, in_specs=[ANY] * n_in, out_specs=[ANY] * n_out, out_shape=job.out_shape,
                          input_output_aliases=dict(getattr(job, "aliases", {})),
                          scratch_shapes=job.scratch)(*job.operands)


def tiled_call(name, steps, body, args, in_specs, out_specs, out_shape, scratch=(), job=None, mid_step=None):
    if job is None:
        return pl.pallas_call(body, name=name, grid=(steps,), in_specs=in_specs, out_specs=out_specs, out_shape=out_shape,
                              scratch_shapes=list(scratch), compiler_params=_params(("arbitrary",)))(*args), []
    n_in, n_out, n_scr = len(args), len(out_shape), len(scratch)
    j_in, j_out = len(job.operands), len(job.out_shape)
    mid_step = steps // 2 if mid_step is None else mid_step

    def carried(*refs):
        a, ji = refs[:n_in], refs[n_in:n_in + j_in]
        o = refs[n_in + j_in:n_in + j_in + n_out]
        jo = refs[n_in + j_in + n_out:n_in + j_in + n_out + j_out]
        rest = refs[n_in + j_in + n_out + j_out:]
        sc, js = rest[:n_scr], rest[n_scr:]
        i = pl.program_id(0)

        @pl.when(i == 0)
        def _():
            job.start(ji, jo, js)

        body(*a, *o, *sc)

        @pl.when(i == mid_step)
        def _():
            job.mid(ji, jo, js)

        @pl.when(i == steps - 1)
        def _():
            job.finish(ji, jo, js)

    res = pl.pallas_call(
        carried, name=name, grid=(steps,), in_specs=list(in_specs) + [ANY] * j_in, out_specs=list(out_specs) + [ANY] * j_out,
        out_shape=list(out_shape) + list(job.out_shape), scratch_shapes=list(scratch) + list(job.scratch),
        input_output_aliases={n_in + i: n_out + o for i, o in getattr(job, "aliases", {}).items()},
        compiler_params=_params(("arbitrary",)))(*args, *job.operands)
    return res[:n_out], res[n_out:]


def gather_small(shard):
    r, c = shard.shape

    def body(in_ref, out_ref, send, recv, lsem):
        x, y, cc = _place()
        own = pltpu.make_async_copy(in_ref, out_ref.at[2 * x + y], lsem)
        own.start()
        sends = []
        for j, chip in enumerate(_other_chips(x, y)):
            cp = pltpu.make_async_remote_copy(in_ref, out_ref.at[2 * x + y], send.at[j], recv.at[j],
                                              device_id=(chip[0], chip[1], cc), device_id_type=MESH)
            cp.start()
            sends.append(cp)
        for j, chip in enumerate(_other_chips(x, y)):
            slot = out_ref.at[2 * chip[0] + chip[1]]
            pltpu.make_async_remote_copy(slot, slot, send.at[j], recv.at[j],
                                         device_id=(chip[0], chip[1], cc), device_id_type=MESH).wait_recv()
        for cp in sends:
            cp.wait_send()
        own.wait()

    vm = pl.BlockSpec(memory_space=pltpu.VMEM)
    return pl.pallas_call(
        body, name="gather_small", in_specs=[vm], out_specs=vm,
        out_shape=jax.ShapeDtypeStruct((NQ, r, c), shard.dtype),
        scratch_shapes=[pltpu.SemaphoreType.DMA((3,)), pltpu.SemaphoreType.DMA((3,)), pltpu.SemaphoreType.DMA],
    )(shard)


D2D_CHUNKS = 4


class SwapJob:
    def __init__(self, arrs):
        self.arrs, self.n = list(arrs), len(arrs)
        self.operands = list(arrs)
        self.out_shape = [jax.ShapeDtypeStruct((a.shape[0], a.shape[1] // 2, a.shape[2]), a.dtype) for a in arrs]
        self.scratch = [pltpu.SemaphoreType.DMA((self.n,)), pltpu.SemaphoreType.DMA((self.n,))]

    def start(self, ins, got, scr):
        send, recv = scr
        x, y, c = _place()
        for k in range(self.n):
            q, r, _ = self.arrs[k].shape
            hr = r // 2
            for qi in range(q):
                _start_remote(lambda r0, nr: ins[k].at[qi, pl.ds((1 - c) * hr + r0, nr), :],
                              lambda r0, nr: got[k].at[qi, pl.ds(r0, nr), :], hr, D2D_CHUNKS,
                              send.at[k], recv.at[k], (x, y, 1 - c))

    def mid(self, ins, got, scr):
        pass

    def finish(self, ins, got, scr):
        send, recv = scr
        for k in range(self.n):
            hr = self.arrs[k].shape[1] // 2
            w = _waiter(ins[k].at[:, pl.ds(0, hr), :], got[k], send.at[k], recv.at[k])
            w.wait_send()
            w.wait_recv()


def swap_halves(arrs):
    return run_job(SwapJob(arrs), "swap_halves")


JOIN_CHUNKS = 8


class JoinJob:
    def __init__(self, arrs):
        self.arrs, self.n = list(arrs), len(arrs)
        self.operands = list(arrs)
        self.out_shape = [jax.ShapeDtypeStruct(a.shape, a.dtype) for a in arrs]
        self.scratch = [pltpu.SemaphoreType.DMA((self.n,)), pltpu.SemaphoreType.DMA((self.n,))]
        self.aliases = {k: k for k in range(self.n)}

    def start(self, ins, outs, scr):
        send, recv = scr
        x, y, c = _place()
        for k in range(self.n):
            hr = self.arrs[k].shape[0] // 2
            rows = lambda r0, nr: outs[k].at[pl.ds(c * hr + r0, nr), :]
            _start_remote(rows, rows, hr, JOIN_CHUNKS, send.at[k], recv.at[k], (x, y, 1 - c))

    def mid(self, ins, outs, scr):
        pass

    def finish(self, ins, outs, scr):
        send, recv = scr
        x, y, c = _place()
        for k in range(self.n):
            hr = self.arrs[k].shape[0] // 2
            w = _waiter(outs[k].at[pl.ds(c * hr, hr), :], outs[k].at[pl.ds((1 - c) * hr, hr), :], send.at[k], recv.at[k])
            w.wait_send()
            w.wait_recv()


def join_halves(arrs):
    return run_job(JoinJob(arrs), "join_halves")


def _block_diag(w):
    eye = jnp.eye(4, dtype=F32)[None, :, None, :, None]
    return (w.reshape(-1, 4, 64, 1, 64) * eye).reshape(-1, 256, 256).astype(BF16)


def _diag_blocks(b):
    eye = jnp.eye(4, dtype=F32)[None, :, None, :, None]
    return (b.reshape(2, 4, 64, 4, 64) * eye).sum(axis=3).reshape(8, 64, 64)


def _vec_params(pool_b, pool_scale, conv_b, gate_r_b, gate_i_b, lru_lambda, conv_w_full, group_norm_g):
    rows = [pool_b, pool_scale, conv_b, gate_r_b, gate_i_b, lru_lambda,
            conv_w_full[:, 0], conv_w_full[:, 1], conv_w_full[:, 2], conv_w_full[:, 3],
            group_norm_g[:, :DP], group_norm_g[:, DP:]]
    return jnp.pad(jnp.stack(rows, axis=1), ((0, 0), (0, 4), (0, 0)))


SMALL_ROWS = 1152


def _pack_small(layers, final_g, meta):
    rows = []
    for vec, pw, wr, wi, g1, g2 in layers:
        rows += [vec, pw.reshape(128, 512), wr.reshape(64, 512), wi.reshape(64, 512), g1.reshape(2, 512), g2.reshape(2, 512)]
    rows += [final_g.reshape(2, 512), meta.reshape(32, 512)]
    flat = jnp.concatenate(rows, axis=0)
    return jnp.concatenate([flat, jnp.zeros((SMALL_ROWS - flat.shape[0], 512), F32)], axis=0)


def _unpack_small(flat):
    layers, o = [], 0
    for _ in range(DEPTH):
        vec = flat[o:o + 16]; o += 16
        pw = flat[o:o + 128].reshape(4, 128, 128); o += 128
        wr = flat[o:o + 64].reshape(8, 64, 64); o += 64
        wi = flat[o:o + 64].reshape(8, 64, 64); o += 64
        g1 = flat[o:o + 2].reshape(1024); o += 2
        g2 = flat[o:o + 2].reshape(1024); o += 2
        layers.append((vec, pw, wr, wi, g1, g2))
    final_g = flat[o:o + 2].reshape(1024); o += 2
    meta = flat[o:o + 32].reshape(16, 1024)
    return layers, final_g, meta


def local_step(x2d, tgt2d, meta_full, conv_w_full, sp, shards=None, gathered=None):
    exchange = gathered is None
    if exchange:
        gathered = [None] * DEPTH
        first_in = run_job(GatherJob(shards[:1], 0), "gather_first")
    h = None
    saved = []
    vp_all = _vec_params(sp["pool_b"], sp["pool_scale"], sp["conv_b"], sp["gate_r_b"], sp["gate_i_b"], sp["lru_lambda"],
                         conv_w_full, sp["group_norm_g"])
    pw_all = sp["pool_w"].astype(BF16)
    br_all = _block_diag(sp["gate_r_w"].reshape(DEPTH * 8, 64, 64)).reshape(DEPTH, 2, 256, 256)
    bi_all = _block_diag(sp["gate_i_w"].reshape(DEPTH * 8, 64, 64)).reshape(DEPTH, 2, 256, 256)
    for l in range(DEPTH):
        vp, pw, br, bi = vp_all[l], pw_all[l], br_all[l], bi_all[l]
        if l == 0:
            job = GatherJob(shards[1:], 0) if exchange else None
            win = first_in[0] if exchange else gathered[0][0]
            (proj, y, hs, aux, h), rest = fwd_mix(x2d, sp["mix_norm_g"][l][None], win, vp, pw, br, bi, job, meta=meta_full)
            wo, wu, wd = rest if exchange else gathered[0][1:]
        else:
            win, wo, wu, wd = gathered[l]
            (proj, y, hs, aux), _ = fwd_mix(h, sp["mix_norm_g"][l][None], win, vp, pw, br, bi)
        wo = wo.reshape(D, D)
        if l + 1 < DEPTH:
            job = GatherJob(shards, l + 1) if exchange else None
            (h1, a, h_next), fetched = fwd_post(h, y, wo, sp["mlp_norm_g"][l][None], wu, wd, job)
            if exchange:
                gathered[l + 1] = fetched
        else:
            (h1, a, dh, dgf, loss_part), _ = fwd_post(h, y, wo, sp["mlp_norm_g"][l][None], wu, wd,
                                                      loss=(sp["final_norm_g"][None], tgt2d))
            h_next = None
        saved.append((h, proj, y, hs, aux, h1, a, vp, pw, br, bi, win, wo, wu, wd))
        h = h_next

    big = [[None, None, None, None] for _ in range(DEPTH)]
    io_raw, io_pairs = [], None
    small_layers = [None] * DEPTH
    for l in reversed(range(DEPTH)):
        h0, proj, y, hs, aux, h1, a, vp, pw, br, bi, win, wo, wu, wd = saved[l]
        g1 = sp["mix_norm_g"][l][None]
        job = ExchangeJob(io_pairs, [True] * 2) if io_pairs is not None else None
        (dh1, da, xn2, dg2), io_parts = bwd_mlp_dx(dh, h1, a, sp["mlp_norm_g"][l][None], wu, wd, job)
        io_sums = sum_chips(io_parts) if job is not None else []
        dwu, dwd = bwd_mlp_dw(xn2, da, a, dh)
        if not exchange:
            (dy, dwo), _ = bwd_out(dh1, y, wo)
            (dproj, ga, dpw, dwr, dwi), _ = bwd_mixer(dy, proj, hs, aux, vp, pw, br, bi)
            (dh, dwin, dg1), _ = bwd_in(dproj, h0, g1, win, dh1)
            big[l] = [dwin, dwo.reshape(NQ, D // NQ, D), dwu, dwd]
        else:
            arrs = [dwu, dwd] + io_raw
            (dy, dwo), got = bwd_out(dh1, y, wo, SwapJob(arrs))
            pairs = add_pairs(arrs, got, BF16)
            crossing = pairs if l == 0 else pairs[:2]
            io_pairs = None if l == 0 or not io_raw else pairs[2:]
            (dproj, ga, dpw, dwr, dwi), parts = bwd_mixer(dy, proj, hs, aux, vp, pw, br, bi,
                                                          ExchangeJob(crossing, [True] * len(crossing)))
            (dh, dwin, dg1), sums = bwd_in(dproj, h0, g1, win, dh1, JoinJob(list(sum_chips(parts)) + list(io_sums)))
            big[l][2:] = sums[:2]
            if l == 0:
                big[1][:2] = sums[2:4]
            if io_sums:
                big[l + 2][:2] = sums[-2:]
            io_raw = [dwin, dwo.reshape(NQ, D // NQ, D)]
        small_layers[l] = (ga, dpw, _diag_blocks(dwr), _diag_blocks(dwi), dg1[0], dg2[0])
    return loss_part, dh, big, small_layers, dgf, io_raw


def kernel(x, meta_tokens, mix_norm_g, w_in, pool_w, pool_b, pool_scale, conv_w, conv_b, gate_r_w, gate_r_b, gate_i_w, gate_i_b, lru_lambda, group_norm_g, w_out, mlp_norm_g, w_up, w_down, final_norm_g, loss_target, m_meta_tokens, m_mix_norm_g, m_w_in, m_pool_w, m_pool_b, m_pool_scale, m_conv_w, m_conv_b, m_gate_r_w, m_gate_r_b, m_gate_i_w, m_gate_i_b, m_lru_lambda, m_group_norm_g, m_w_out, m_mlp_norm_g, m_w_up, m_w_down, m_final_norm_g, v_meta_tokens, v_mix_norm_g, v_w_in, v_pool_w, v_pool_b, v_pool_scale, v_conv_w, v_conv_b, v_gate_r_w, v_gate_r_b, v_gate_i_w, v_gate_i_b, v_lru_lambda, v_group_norm_g, v_w_out, v_mlp_norm_g, v_w_up, v_w_down, v_final_norm_g):
    p = 2 * lax.axis_index("x") + lax.axis_index("y")

    shards = [w_in.astype(BF16), w_out.astype(BF16), w_up.astype(BF16), w_down.astype(BF16)]
    small = jnp.concatenate([meta_tokens, jnp.pad(conv_w.reshape(16, 128), ((0, 0), (0, 128)))], axis=0)
    small_g = gather_small(small)
    meta_full = jnp.transpose(small_g[:, :16, :], (1, 0, 2)).reshape(NMETA, D)
    conv_w_full = jnp.transpose(small_g[:, 16:, :128].reshape(NQ, DEPTH, 4, 128), (1, 2, 0, 3)).reshape(DEPTH, 4, DR)

    sp = dict(mix_norm_g=mix_norm_g, pool_w=pool_w, pool_b=pool_b, pool_scale=pool_scale, conv_b=conv_b, gate_r_w=gate_r_w,
              gate_r_b=gate_r_b, gate_i_w=gate_i_w, gate_i_b=gate_i_b, lru_lambda=lru_lambda, group_norm_g=group_norm_g,
              mlp_norm_g=mlp_norm_g, final_norm_g=final_norm_g)
    loss_part, dh, big, small_layers, dgf, io0_parts = local_step(x[0], loss_target[0], meta_full, conv_w_full, sp,
                                                                  shards=shards)
    loss = lax.psum(loss_part[0, 0], ("x", "y", "c"))
    grad_x = dh[NMETA:][None]

    weights = dict(meta_tokens=meta_tokens, mix_norm_g=mix_norm_g, w_in=w_in, pool_w=pool_w, pool_b=pool_b, pool_scale=pool_scale,
                   conv_w=conv_w, conv_b=conv_b, gate_r_w=gate_r_w, gate_r_b=gate_r_b, gate_i_w=gate_i_w, gate_i_b=gate_i_b,
                   lru_lambda=lru_lambda, group_norm_g=group_norm_g, w_out=w_out, mlp_norm_g=mlp_norm_g, w_up=w_up, w_down=w_down,
                   final_norm_g=final_norm_g)
    mom_m = dict(meta_tokens=m_meta_tokens, mix_norm_g=m_mix_norm_g, w_in=m_w_in, pool_w=m_pool_w, pool_b=m_pool_b,
                 pool_scale=m_pool_scale, conv_w=m_conv_w, conv_b=m_conv_b, gate_r_w=m_gate_r_w, gate_r_b=m_gate_r_b,
                 gate_i_w=m_gate_i_w, gate_i_b=m_gate_i_b, lru_lambda=m_lru_lambda, group_norm_g=m_group_norm_g, w_out=m_w_out,
                 mlp_norm_g=m_mlp_norm_g, w_up=m_w_up, w_down=m_w_down, final_norm_g=m_final_norm_g)
    mom_v = dict(meta_tokens=v_meta_tokens, mix_norm_g=v_mix_norm_g, w_in=v_w_in, pool_w=v_pool_w, pool_b=v_pool_b,
                 pool_scale=v_pool_scale, conv_w=v_conv_w, conv_b=v_conv_b, gate_r_w=v_gate_r_w, gate_r_b=v_gate_r_b,
                 gate_i_w=v_gate_i_w, gate_i_b=v_gate_i_b, lru_lambda=v_lru_lambda, group_norm_g=v_group_norm_g, w_out=v_w_out,
                 mlp_norm_g=v_mlp_norm_g, w_up=v_w_up, w_down=v_w_down, final_norm_g=v_final_norm_g)
    names = list(weights)
    delta, new_m, new_v = {}, {}, {}

    def adam_big(nm, g, job=None):
        shp = weights[nm].shape
        two_d = lambda t: t.reshape(shp[0] * shp[1], shp[2])
        (d_, m_, v_), got = adamw(two_d(weights[nm]), two_d(g), two_d(mom_m[nm]), two_d(mom_v[nm]), job)
        delta[nm], new_m[nm], new_v[nm] = d_.reshape(shp), m_.reshape(shp), v_.reshape(shp)
        return got

    last = [_pack_small(small_layers, dgf[0], dh[:NMETA])[None]] + io0_parts
    got = swap_halves(last)
    pairs = add_pairs(last[:1], got[:1], F32) + add_pairs(last[1:], got[1:], BF16)
    sums = join_halves(sum_chips(run_job(ExchangeJob(pairs, [False, True, True]), "exchange_last")))
    small_sum, io0 = sums[0], sums[1:]
    g_up = jnp.stack([big[l][-2] for l in range(DEPTH)])
    g_down = jnp.stack([big[l][-1] for l in range(DEPTH)])
    adam_big("w_up", g_up)
    adam_big("w_down", g_down)
    g_in = jnp.stack([io0[0]] + [big[l][0] for l in range(1, DEPTH)])
    g_out = jnp.stack([io0[1]] + [big[l][1] for l in range(1, DEPTH)])
    adam_big("w_in", g_in)
    adam_big("w_out", g_out)
    g_layers, g_final, g_meta_full = _unpack_small(small_sum)

    g_vec = [gl[0] for gl in g_layers]
    grads = dict(
        meta_tokens=lax.dynamic_slice(g_meta_full, (0, p * (D // NQ)), (NMETA, D // NQ)),
        mix_norm_g=jnp.stack([gl[4] for gl in g_layers]),
        w_in=g_in,
        pool_w=jnp.stack([gl[1] for gl in g_layers]),
        pool_b=jnp.stack([gv[GA_PB] for gv in g_vec]),
        pool_scale=jnp.stack([gv[GA_PS] for gv in g_vec]),
        conv_w=lax.dynamic_slice(jnp.stack([gv[GA_CW:GA_CW + 4] for gv in g_vec]), (0, 0, p * 128), (DEPTH, 4, 128)),
        conv_b=jnp.stack([gv[GA_CB] for gv in g_vec]),
        gate_r_w=jnp.stack([gl[2] for gl in g_layers]),
        gate_r_b=jnp.stack([gv[GA_BR] for gv in g_vec]),
        gate_i_w=jnp.stack([gl[3] for gl in g_layers]),
        gate_i_b=jnp.stack([gv[GA_BI] for gv in g_vec]),
        lru_lambda=jnp.stack([gv[GA_LAM] for gv in g_vec]),
        group_norm_g=jnp.stack([jnp.concatenate([gv[GA_GNP], gv[GA_GNR]]) for gv in g_vec]),
        w_out=g_out,
        mlp_norm_g=jnp.stack([gl[5] for gl in g_layers]),
        w_up=g_up,
        w_down=g_down,
        final_norm_g=g_final,
    )

    small_names = [nm for nm in names if nm not in ("w_in", "w_out", "w_up", "w_down")]
    sizes = [weights[nm].size for nm in small_names]
    total = sum(sizes)
    rows = -(-total // 512)
    rows = -(-rows // 16) * 16

    def flat(tree, fill):
        v_ = jnp.concatenate([tree[nm].reshape(-1) for nm in small_names])
        return jnp.concatenate([v_, jnp.full((rows * 512 - total,), fill, F32)]).reshape(rows, 512)

    (d_, m_, v_), _ = adamw(flat(weights, 0.0), flat(grads, 0.0), flat(mom_m, 0.0), flat(mom_v, 1.0))
    o = 0
    for nm, sz in zip(small_names, sizes):
        shp = weights[nm].shape
        delta[nm] = d_.reshape(-1)[o:o + sz].reshape(shp)
        new_m[nm] = m_.reshape(-1)[o:o + sz].reshape(shp)
        new_v[nm] = v_.reshape(-1)[o:o + sz].reshape(shp)
        o += sz

    return (loss, grad_x, *[grads[nm] for nm in names], *[delta[nm] for nm in names],
            *[new_m[nm] for nm in names], *[new_v[nm] for nm in names])
```

```python
import jax
import jax.numpy as jnp
from jax import lax
from jax.experimental import pallas as pl
from jax.experimental.pallas import tpu as pltpu

F32 = jnp.float32
BF16 = jnp.bfloat16

D = 1024
DP = 512
DR = 512
DIN = 1536
DFF = 4096
DEPTH = 4
NMETA = 16
NQ = 4
WIN_S = DIN // NQ
FF_S = DFF // NQ
EPS = 1e-6
HALO = 16
TT = 432
TTW = 912
VMEM_LIMIT = 56 * 1024 * 1024

ADAM_LR = 0.001
ADAM_B1 = 0.9
ADAM_B2 = 0.999
ADAM_EPS = 1e-08
ADAM_WD = 0.01
ADAM_STEP = 10

MESH = pl.DeviceIdType.MESH
ANY = pl.BlockSpec(memory_space=pl.ANY)


def _params(sem=None, vmem=VMEM_LIMIT):
    return pltpu.CompilerParams(dimension_semantics=sem, vmem_limit_bytes=vmem)


def _resident(shape, index):
    return pl.BlockSpec(shape, index, pipeline_mode=pl.Buffered(1))


def _nt(x, w):
    return lax.dot_general(x, w, (((1,), (1,)), ((), ())), preferred_element_type=F32)


def _tn(a, b):
    return lax.dot_general(a, b, (((0,), (0,)), ((), ())), preferred_element_type=F32)


def _dot(x, w):
    return jnp.dot(x, w, preferred_element_type=F32)


def _rms(h):
    return lax.rsqrt(jnp.mean(h * h, axis=-1, keepdims=True) + EPS)


def _rms_bwd(n, r, dn):
    return r * (dn - n * jnp.mean(dn * n, axis=-1, keepdims=True))


def _down(x, s):
    return pltpu.roll(x, s, 0)


def _up(x, s):
    return pltpu.roll(x, x.shape[0] - s, 0)


GELU_C = 0.7978845608028654
GELU_K = 0.044715


def _gelu(x):
    t = jnp.tanh(GELU_C * (x + GELU_K * x * x * x))
    return 0.5 * x * (1.0 + t), t


def _softplus(x):
    return jnp.maximum(x, 0.0) + jnp.log1p(jnp.exp(-jnp.abs(x)))


VP_PB, VP_PS, VP_CB, VP_BR, VP_BI, VP_LAM, VP_CW, VP_GNP, VP_GNR = 0, 1, 2, 3, 4, 5, 6, 10, 11


def _row(vp, r):
    return vp[r:r + 1, :]


def _mixer_pre(ue, t0, vp, pw_ref, br_ref, bi_ref):
    tt = ue.shape[0] - HALO
    tf = (t0 + lax.broadcasted_iota(jnp.int32, (tt, 1), 0)).astype(F32) + 1.0
    pb, ps = _row(vp, VP_PB), _row(vp, VP_PS)
    pooled, mapped, inv_cnt = [], [], []
    for g in range(4):
        lanes = slice(128 * g, 128 * (g + 1))
        xe = ue[:, lanes]
        s = xe
        for j in range(g + 1):
            s = s + _down(s, 1 << j)
        inv = 1.0 / jnp.minimum(tf, float(2 << g))
        pg = s[HALO:] * inv - xe[HALO:]
        mg = _dot(pg.astype(BF16), pw_ref[g]) + pb[:, lanes]
        pooled.append(pg)
        mapped.append(mg)
        inv_cnt.append(inv)
    ypool = [mapped[g] * ps[:, 128 * g:128 * (g + 1)] for g in range(4)]

    xe = ue[:, DP:DP + DR]
    taps = [_down(xe, 3)[HALO:], _down(xe, 2)[HALO:], _down(xe, 1)[HALO:], xe[HALO:]]
    xc = _row(vp, VP_CB) + (taps[0] * _row(vp, VP_CW) + taps[1] * _row(vp, VP_CW + 1)
                            + taps[2] * _row(vp, VP_CW + 2) + taps[3] * _row(vp, VP_CW + 3))
    xcb = xc.astype(BF16)
    zr = jnp.concatenate([_dot(xcb[:, :256], br_ref[0]), _dot(xcb[:, 256:], br_ref[1])], axis=1) + _row(vp, VP_BR)
    zi = jnp.concatenate([_dot(xcb[:, :256], bi_ref[0]), _dot(xcb[:, 256:], bi_ref[1])], axis=1) + _row(vp, VP_BI)
    r = jax.nn.sigmoid(zr)
    ig = jax.nn.sigmoid(zi)
    sp = _softplus(-_row(vp, VP_LAM))
    la = (-8.0 * r) * sp
    a = jnp.exp(la)
    th = jnp.tanh(la)
    mult = jnp.sqrt((-2.0 * th) / (1.0 - th))
    gate = ue[HALO:, DP + DR:]
    gl, gt = _gelu(gate)
    return dict(pooled=pooled, mapped=mapped, ypool=ypool, inv_cnt=inv_cnt, taps=taps, xc=xc, xcb=xcb, r=r, ig=ig,
                sp=sp, a=a, mult=mult, gate=gate, gl=gl, gt=gt)


SUBLANES = 8


def _scan_fwd(a, b, h_in):
    tt = a.shape[0]
    sub = jnp.bitwise_and(lax.broadcasted_iota(jnp.int32, (tt, 1), 0), SUBLANES - 1)
    s = 1
    while s < SUBLANES:
        m = sub >= s
        a_s = jnp.where(m, _down(a, s), 1.0)
        b_s = jnp.where(m, _down(b, s), 0.0)
        b = a * b_s + b
        a = a * a_s
        s *= 2
    groups, h = [], h_in
    for g in range(tt // SUBLANES):
        rows = slice(SUBLANES * g, SUBLANES * (g + 1))
        hg = a[rows] * h + b[rows]
        groups.append(hg)
        h = hg[SUBLANES - 1:SUBLANES, :]
    return jnp.concatenate(groups, axis=0)


def _scan_rev(c, d, l_in):
    tt = c.shape[0]
    sub = jnp.bitwise_and(lax.broadcasted_iota(jnp.int32, (tt, 1), 0), SUBLANES - 1)
    s = 1
    while s < SUBLANES:
        m = sub < SUBLANES - s
        c_s = jnp.where(m, _up(c, s), 1.0)
        d_s = jnp.where(m, _up(d, s), 0.0)
        d = c * d_s + d
        c = c * c_s
        s *= 2
    groups, l = [], l_in
    for g in reversed(range(tt // SUBLANES)):
        rows = slice(SUBLANES * g, SUBLANES * (g + 1))
        lg = c[rows] * l + d[rows]
        groups.append(lg)
        l = lg[0:1, :]
    return jnp.concatenate(groups[::-1], axis=0)


def _halo_index(i, per_tile):
    return jnp.maximum(i * per_tile - 1, 0)


def _chip_slabs_to_columns(w_ref, wfull_ref):
    for q in range(NQ):
        wfull_ref[:, WIN_S * q:WIN_S * (q + 1)] = w_ref[q]


AUX_POOLED, AUX_MAPPED, AUX_XC, AUX_R, AUX_IG, AUX_A, AUX_MULT, AUX_GL, AUX_GT = range(9)
AUX_W = 9 * 512


def _aux(k):
    return slice(512 * k, 512 * (k + 1))


def fwd_mix(h, g, w_g, vp, pw, br, bi, job=None, meta=None):
    T = h.shape[0] + (NMETA if meta is not None else 0)
    n_h = 3 if meta is not None else 1

    def body(*refs):
        g_ref, w_ref, vp_ref, pw_ref, br_ref, bi_ref, p_ref, y_ref, hs_ref, aux_ref = refs[n_h:n_h + 10]
        wfull_ref, halo_ref, carry_ref = refs[-3:]
        i = pl.program_id(0)

        @pl.when(i == 0)
        def _():
            _chip_slabs_to_columns(w_ref, wfull_ref)
            carry_ref[...] = jnp.zeros_like(carry_ref)
            halo_ref[...] = jnp.zeros_like(halo_ref)

        if meta is not None:
            xp_ref, x_ref, meta_ref = refs[:n_h]
            hh = jnp.concatenate([jnp.where(i == 0, meta_ref[...], xp_ref[...]), x_ref[0:TT - HALO, :]], axis=0)
            refs[n_h + 10][...] = hh
        else:
            hh = refs[0][...]
        xn = (hh * _rms(hh) * g_ref[...]).astype(BF16)
        proj = _dot(xn, wfull_ref[...])
        p_ref[...] = proj
        ue = jnp.concatenate([halo_ref[...], proj], axis=0)
        halo_ref[...] = p_ref[TT - HALO:TT, :]
        vp_v = vp_ref[...]
        m = _mixer_pre(ue, i * TT, vp_v, pw_ref, br_ref, bi_ref)
        for g in range(4):
            aux_ref[:, 512 * AUX_POOLED + 128 * g:512 * AUX_POOLED + 128 * (g + 1)] = m["pooled"][g]
            aux_ref[:, 512 * AUX_MAPPED + 128 * g:512 * AUX_MAPPED + 128 * (g + 1)] = m["mapped"][g]
        for k, name in ((AUX_XC, "xc"), (AUX_R, "r"), (AUX_IG, "ig"), (AUX_A, "a"), (AUX_MULT, "mult"), (AUX_GL, "gl"),
                        (AUX_GT, "gt")):
            aux_ref[:, _aux(k)] = m[name]
        b = m["mult"] * (m["ig"] * m["xc"])
        hs = _scan_fwd(m["a"], b, carry_ref[0:1, :])
        hs_ref[...] = hs
        carry_ref[0:1, :] = hs_ref[TT - 1:TT, :]
        yr = hs * m["gl"]
        ssq = sum(jnp.sum(yp * yp, axis=-1, keepdims=True) for yp in m["ypool"])
        rp = lax.rsqrt(ssq * (1.0 / DP) + EPS)
        gnp = _row(vp_v, VP_GNP)
        for g in range(4):
            lanes = slice(128 * g, 128 * (g + 1))
            y_ref[:, lanes] = (m["ypool"][g] * rp * gnp[:, lanes]).astype(BF16)
        y_ref[:, DP:] = (yr * _rms(yr) * _row(vp_v, VP_GNR)).astype(BF16)

    if meta is not None:
        h_args, h_specs = (h, h, meta), _shifted_specs() + [_resident((NMETA, D), lambda i: (0, 0))]
    else:
        h_args, h_specs = (h,), [pl.BlockSpec((TT, D), lambda i: (i, 0))]
    full = meta is not None
    return tiled_call(
        "fwd_mix", T // TT, body, h_args + (g, w_g, vp, pw, br, bi),
        in_specs=h_specs + [_resident((1, D), lambda i: (0, 0)), _resident((NQ, D, WIN_S), lambda i: (0, 0, 0)),
                            _resident((16, 512), lambda i: (0, 0)), _resident((4, 128, 128), lambda i: (0, 0, 0)),
                            _resident((2, 256, 256), lambda i: (0, 0, 0)), _resident((2, 256, 256), lambda i: (0, 0, 0))],
        out_specs=[pl.BlockSpec((TT, DIN), lambda i: (i, 0)), pl.BlockSpec((TT, D), lambda i: (i, 0)),
                   pl.BlockSpec((TT, DR), lambda i: (i, 0)), pl.BlockSpec((TT, AUX_W), lambda i: (i, 0))]
        + [pl.BlockSpec((TT, D), lambda i: (i, 0))] * full,
        out_shape=[jax.ShapeDtypeStruct((T, DIN), F32), jax.ShapeDtypeStruct((T, D), BF16),
                   jax.ShapeDtypeStruct((T, DR), F32), jax.ShapeDtypeStruct((T, AUX_W), F32)]
        + [jax.ShapeDtypeStruct((T, D), F32)] * full,
        scratch=[pltpu.VMEM((D, DIN), BF16), pltpu.VMEM((HALO, DIN), F32), pltpu.VMEM((8, DR), F32)],
        job=job, mid_step=T // TT - 1)


def fwd_post(h, y, wo_g, g2, wu_g, wd_g, job=None, loss=None):
    T = h.shape[0]

    def body(*refs):
        h_ref, y_ref, wo_ref, g_ref, wu_ref, wd_ref = refs[:6]
        h1_ref, a_ref = refs[-5:-3] if loss is not None else refs[-3:-1]
        h1 = h_ref[...] + _dot(y_ref[...], wo_ref[...])
        h1_ref[...] = h1
        xn = (h1 * _rms(h1) * g_ref[...]).astype(BF16)
        acc = h1
        for q in range(NQ):
            a = _dot(xn, wu_ref[q])
            a_ref[:, FF_S * q:FF_S * (q + 1)] = a.astype(BF16)
            ra = jnp.maximum(a, 0.0)
            acc = acc + _dot((ra * ra).astype(BF16), wd_ref[q])
        if loss is None:
            refs[-1][...] = acc
        else:
            _loss_tile(acc, *refs[6:9], *refs[-3:])

    row = pl.BlockSpec((TT, D), lambda i: (i, 0))
    in_specs = [row, row, _resident((D, D), lambda i: (0, 0)), _resident((1, D), lambda i: (0, 0)),
                _resident((NQ, D, FF_S), lambda i: (0, 0, 0)), _resident((NQ, FF_S, D), lambda i: (0, 0, 0))]
    out_specs = [row, pl.BlockSpec((TT, DFF), lambda i: (i, 0)), row]
    out_shape = [jax.ShapeDtypeStruct((T, D), F32), jax.ShapeDtypeStruct((T, DFF), BF16), jax.ShapeDtypeStruct((T, D), F32)]
    args = (h, y, wo_g, g2, wu_g, wd_g)
    if loss is not None:
        args += (loss[0], loss[1], loss[1])
        in_specs += [_resident((1, D), lambda i: (0, 0))] + _shifted_specs()
        out_specs += [pl.BlockSpec((1, D), lambda i: (0, 0)), pl.BlockSpec((1, 1), lambda i: (0, 0))]
        out_shape += [jax.ShapeDtypeStruct((1, D), F32), jax.ShapeDtypeStruct((1, 1), F32)]
    return tiled_call("fwd_post", T // TT, body, args, in_specs=in_specs, out_specs=out_specs, out_shape=out_shape,
                      job=job, mid_step=(T // TT) * 3 // 4)


def _shifted_tile(prev_ref, cur_ref):
    return jnp.concatenate([prev_ref[...], cur_ref[0:TT - HALO, :]], axis=0)


def _shifted_specs():
    per = TT // HALO
    return [pl.BlockSpec((HALO, D), lambda i: (_halo_index(i, per), 0)), pl.BlockSpec((TT, D), lambda i: (i, 0))]


def _loss_tile(hh, g_ref, tp_ref, t_ref, dh_ref, dg_ref, loss_ref):
    i = pl.program_id(0)

    @pl.when(i == 0)
    def _():
        dg_ref[...] = jnp.zeros_like(dg_ref)
        loss_ref[...] = jnp.zeros_like(loss_ref)

    r = _rms(hh)
    n = hh * r
    gfv = g_ref[...]
    row = i * TT + lax.broadcasted_iota(jnp.int32, (TT, 1), 0)
    e = jnp.where(row >= NMETA, n * gfv - _shifted_tile(tp_ref, t_ref), 0.0)
    loss_ref[...] += 0.5 * jnp.sum(jnp.sum(e * e, axis=-1, keepdims=True) * (1.0 / D), axis=0, keepdims=True)
    dy = e * (1.0 / D)
    dg_ref[...] += jnp.sum(dy * n, axis=0, keepdims=True)
    dh_ref[...] = _rms_bwd(n, r, dy * gfv)


def bwd_mlp_dx(dh2, h1, a, g2, wu_g, wd_g, job=None):
    T = dh2.shape[0]

    def body(dh2_ref, h1_ref, a_ref, g_ref, wu_ref, wd_ref, dh1_ref, da_ref, xn_ref, dg_ref):
        @pl.when(pl.program_id(0) == 0)
        def _():
            dg_ref[...] = jnp.zeros_like(dg_ref)

        h1v = h1_ref[...]
        r = _rms(h1v)
        n = h1v * r
        gv = g_ref[...]
        xn_ref[...] = (n * gv).astype(BF16)
        dh2v = dh2_ref[...]
        dh2b = dh2v.astype(BF16)
        dxn = jnp.zeros((TT, D), F32)
        for q in range(NQ):
            cols = slice(FF_S * q, FF_S * (q + 1))
            ra = jnp.maximum(a_ref[:, cols].astype(F32), 0.0)
            da = (_nt(dh2b, wd_ref[q]) * (2.0 * ra)).astype(BF16)
            da_ref[:, cols] = da
            dxn = dxn + _nt(da, wu_ref[q])
        dg_ref[...] += jnp.sum(dxn * n, axis=0, keepdims=True)
        dh1_ref[...] = dh2v + _rms_bwd(n, r, dxn * gv)

    return tiled_call(
        "bwd_mlp_dx", T // TT, body, (dh2, h1, a, g2, wu_g, wd_g),
        in_specs=[pl.BlockSpec((TT, D), lambda i: (i, 0)), pl.BlockSpec((TT, D), lambda i: (i, 0)),
                  pl.BlockSpec((TT, DFF), lambda i: (i, 0)), _resident((1, D), lambda i: (0, 0)),
                  _resident((NQ, D, FF_S), lambda i: (0, 0, 0)), _resident((NQ, FF_S, D), lambda i: (0, 0, 0))],
        out_specs=[pl.BlockSpec((TT, D), lambda i: (i, 0)), pl.BlockSpec((TT, DFF), lambda i: (i, 0)),
                   pl.BlockSpec((TT, D), lambda i: (i, 0)), pl.BlockSpec((1, D), lambda i: (0, 0))],
        out_shape=[jax.ShapeDtypeStruct((T, D), F32), jax.ShapeDtypeStruct((T, DFF), BF16),
                   jax.ShapeDtypeStruct((T, D), BF16), jax.ShapeDtypeStruct((1, D), F32)],
        job=job)


def bwd_mlp_dw(xn, da, a, dh2):
    T = xn.shape[0]

    def body(xn_ref, da_ref, a_ref, dh2_ref, dwu_ref, dwd_ref):
        @pl.when(pl.program_id(1) == 0)
        def _():
            dwu_ref[...] = jnp.zeros_like(dwu_ref)
            dwd_ref[...] = jnp.zeros_like(dwd_ref)

        dwu_ref[...] += _tn(xn_ref[...], da_ref[...])
        ra = jnp.maximum(a_ref[...].astype(F32), 0.0)
        dwd_ref[...] += _tn((ra * ra).astype(BF16), dh2_ref[...].astype(BF16))

    return pl.pallas_call(
        body, name="bwd_mlp_dw", grid=(NQ, T // TTW),
        in_specs=[pl.BlockSpec((TTW, D), lambda q, i: (i, 0)), pl.BlockSpec((TTW, FF_S), lambda q, i: (i, q)),
                  pl.BlockSpec((TTW, FF_S), lambda q, i: (i, q)), pl.BlockSpec((TTW, D), lambda q, i: (i, 0))],
        out_specs=[pl.BlockSpec((None, D, FF_S), lambda q, i: (q, 0, 0)),
                   pl.BlockSpec((None, FF_S, D), lambda q, i: (q, 0, 0))],
        out_shape=[jax.ShapeDtypeStruct((NQ, D, FF_S), F32), jax.ShapeDtypeStruct((NQ, FF_S, D), F32)],
        compiler_params=_params(("arbitrary", "arbitrary")),
    )(xn, da, a, dh2)


def bwd_out(dh1, y, wo_g, job=None):
    T = dh1.shape[0]

    def body(dh_ref, y_ref, wo_ref, dy_ref, dwo_ref):
        @pl.when(pl.program_id(0) == 0)
        def _():
            dwo_ref[...] = jnp.zeros_like(dwo_ref)

        dhb = dh_ref[...].astype(BF16)
        dy_ref[...] = _nt(dhb, wo_ref[...])
        dwo_ref[...] += _tn(y_ref[...], dhb)

    return tiled_call(
        "bwd_out", T // TT, body, (dh1, y, wo_g),
        in_specs=[pl.BlockSpec((TT, D), lambda i: (i, 0)), pl.BlockSpec((TT, D), lambda i: (i, 0)),
                  _resident((D, D), lambda i: (0, 0))],
        out_specs=[pl.BlockSpec((TT, D), lambda i: (i, 0)), pl.BlockSpec((D, D), lambda i: (0, 0))],
        out_shape=[jax.ShapeDtypeStruct((T, D), F32), jax.ShapeDtypeStruct((D, D), F32)], job=job)


GA_PS, GA_PB, GA_CB, GA_BR, GA_BI, GA_LAM, GA_CW, GA_GNP, GA_GNR = 0, 1, 2, 3, 4, 5, 6, 10, 11


def bwd_mixer(dy, proj, hs, aux, vp, pw, br, bi, job=None):
    T = proj.shape[0]
    tt = TT
    nt = T // tt
    per = tt // HALO

    def body(dy_ref, p_ref, ph_ref, hs_ref, hsh_ref, aux_ref, vp_ref, pw_ref, br_ref, bi_ref,
             dp_ref, ga_ref, dpw_ref, dwr_ref, dwi_ref, lam_ref, q_ref, dxc_ref):
        s = pl.program_id(0)
        ti = nt - 1 - s

        @pl.when(s == 0)
        def _():
            for ref in (ga_ref, dpw_ref, dwr_ref, dwi_ref, lam_ref, q_ref, dxc_ref):
                ref[...] = jnp.zeros_like(ref)

        dyv = dy_ref[...]
        lam_in, q_in, dxc_in = lam_ref[0:1, :], q_ref[...], dxc_ref[...]
        first = ti > 0
        vp_v = vp_ref[...]
        ur = jnp.concatenate([jnp.where(first, ph_ref[:, DP:DP + DR], 0.0), p_ref[:, DP:DP + DR]], axis=0)
        tf = (ti * tt + lax.broadcasted_iota(jnp.int32, (tt, 1), 0)).astype(F32) + 1.0
        mapped = [aux_ref[:, 512 * AUX_MAPPED + 128 * g:512 * AUX_MAPPED + 128 * (g + 1)] for g in range(4)]
        ps_row = _row(vp_v, VP_PS)
        xc = aux_ref[:, _aux(AUX_XC)]
        m = dict(pooled=[aux_ref[:, 512 * AUX_POOLED + 128 * g:512 * AUX_POOLED + 128 * (g + 1)] for g in range(4)],
                 mapped=mapped, ypool=[mapped[g] * ps_row[:, 128 * g:128 * (g + 1)] for g in range(4)],
                 inv_cnt=[1.0 / jnp.minimum(tf, float(2 << g)) for g in range(4)],
                 taps=[_down(ur, 3)[HALO:], _down(ur, 2)[HALO:], _down(ur, 1)[HALO:], ur[HALO:]],
                 xc=xc, xcb=xc.astype(BF16), r=aux_ref[:, _aux(AUX_R)], ig=aux_ref[:, _aux(AUX_IG)],
                 a=aux_ref[:, _aux(AUX_A)], mult=aux_ref[:, _aux(AUX_MULT)], gl=aux_ref[:, _aux(AUX_GL)],
                 gt=aux_ref[:, _aux(AUX_GT)], gate=p_ref[:, DP + DR:], sp=_softplus(-_row(vp_v, VP_LAM)))
        hs_v = hs_ref[...]
        hprev = _down(jnp.concatenate([jnp.where(first, hsh_ref[...], 0.0), hs_v], axis=0), 1)[HALO:]

        def acc(rw, v):
            ga_ref[rw:rw + 1, :] += jnp.sum(v, axis=0, keepdims=True)

        gnp = _row(vp_v, VP_GNP)
        ps = _row(vp_v, VP_PS)
        ssq = sum(jnp.sum(yp * yp, axis=-1, keepdims=True) for yp in m["ypool"])
        rp = lax.rsqrt(ssq * (1.0 / DP) + EPS)
        npool = [yp * rp for yp in m["ypool"]]
        dnp = [dyv[:, 128 * g:128 * (g + 1)] * gnp[:, 128 * g:128 * (g + 1)] for g in range(4)]
        mean_dn = sum(jnp.sum(dnp[g] * npool[g], axis=-1, keepdims=True) for g in range(4)) * (1.0 / DP)
        for g in range(4):
            lanes = slice(128 * g, 128 * (g + 1))
            ga_ref[GA_GNP:GA_GNP + 1, lanes] += jnp.sum(dyv[:, lanes] * npool[g], axis=0, keepdims=True)
            dyp = rp * (dnp[g] - npool[g] * mean_dn)
            ga_ref[GA_PS:GA_PS + 1, lanes] += jnp.sum(dyp * m["mapped"][g], axis=0, keepdims=True)
            dmap = dyp * ps[:, lanes]
            ga_ref[GA_PB:GA_PB + 1, lanes] += jnp.sum(dmap, axis=0, keepdims=True)
            dmb = dmap.astype(BF16)
            dpw_ref[g] += _tn(m["pooled"][g].astype(BF16), dmb)
            dpool = _nt(dmb, pw_ref[g])
            qv = dpool * m["inv_cnt"][g]
            win = jnp.concatenate([qv, q_in[:, lanes]], axis=0)
            for j in range(g + 1):
                win = win + _up(win, 1 << j)
            q_ref[:, lanes] = qv[:HALO]
            dp_ref[:, lanes] = (win[:tt] - dpool).astype(BF16)

        gnr = _row(vp_v, VP_GNR)
        yr = hs_v * m["gl"]
        rr = _rms(yr)
        nr = yr * rr
        dyr_out = dyv[:, DP:]
        acc(GA_GNR, dyr_out * nr)
        dyr = _rms_bwd(nr, rr, dyr_out * gnr)
        gate, gt = m["gate"], m["gt"]
        dgl = 0.5 * (1.0 + gt) + 0.5 * gate * (1.0 - gt * gt) * (GELU_C * (1.0 + 3.0 * GELU_K * gate * gate))
        dp_ref[:, DP + DR:] = (dyr * hs_v * dgl).astype(BF16)
        dhs = dyr * m["gl"]
        a = m["a"]
        row = lax.broadcasted_iota(jnp.int32, (tt, 1), 0)
        c_next = jnp.where(row < tt - 1, _up(a, 1), 1.0)
        lam = _scan_rev(c_next, dhs, lam_in)
        lam_ref[0:1, :] = a[0:1, :] * lam[0:1, :]
        xc, ig, r, mult = m["xc"], m["ig"], m["r"], m["mult"]
        dmult = lam * ig * xc
        dig = lam * mult * xc
        dxc = lam * mult * ig
        dla = lam * hprev * a - dmult * (a * a) / mult
        acc(GA_LAM, dla * r)
        dzr = (dla * (-8.0 * m["sp"])) * (r * (1.0 - r))
        dzi = dig * (ig * (1.0 - ig))
        acc(GA_BR, dzr)
        acc(GA_BI, dzi)
        dzrb, dzib = dzr.astype(BF16), dzi.astype(BF16)
        xcb = m["xcb"]
        halves = []
        for k in range(2):
            lanes = slice(256 * k, 256 * (k + 1))
            dwr_ref[k] += _tn(xcb[:, lanes], dzrb[:, lanes])
            dwi_ref[k] += _tn(xcb[:, lanes], dzib[:, lanes])
            halves.append(_nt(dzrb[:, lanes], br_ref[k]) + _nt(dzib[:, lanes], bi_ref[k]))
        dxc = dxc + jnp.concatenate(halves, axis=1)
        acc(GA_CB, dxc)
        for k in range(4):
            acc(GA_CW + k, dxc * m["taps"][k])
        dxe = jnp.concatenate([dxc, dxc_in], axis=0)
        du = (_up(dxe, 3)[:tt] * _row(vp_v, VP_CW) + _up(dxe, 2)[:tt] * _row(vp_v, VP_CW + 1)
              + _up(dxe, 1)[:tt] * _row(vp_v, VP_CW + 2) + dxc * _row(vp_v, VP_CW + 3))
        dxc_ref[...] = dxc[:HALO]
        dp_ref[:, DP:DP + DR] = du.astype(BF16)

        @pl.when(s == nt - 1)
        def _():
            lamp = _row(vp_v, VP_LAM)
            ga_ref[GA_LAM:GA_LAM + 1, :] = ga_ref[GA_LAM:GA_LAM + 1, :] * (8.0 * jax.nn.sigmoid(-lamp))

    rev = lambda i: (nt - 1 - i, 0)
    rev_halo = lambda i: (_halo_index(nt - 1 - i, per), 0)
    return tiled_call(
        "bwd_mixer", nt, body, (dy, proj, proj, hs, hs, aux, vp, pw, br, bi),
        in_specs=[pl.BlockSpec((tt, D), rev), pl.BlockSpec((tt, DIN), rev), pl.BlockSpec((HALO, DIN), rev_halo),
                  pl.BlockSpec((tt, DR), rev), pl.BlockSpec((HALO, DR), rev_halo), pl.BlockSpec((tt, AUX_W), rev),
                  _resident((16, 512), lambda i: (0, 0)), _resident((4, 128, 128), lambda i: (0, 0, 0)),
                  _resident((2, 256, 256), lambda i: (0, 0, 0)), _resident((2, 256, 256), lambda i: (0, 0, 0))],
        out_specs=[pl.BlockSpec((tt, DIN), rev), pl.BlockSpec((16, 512), lambda i: (0, 0)),
                   pl.BlockSpec((4, 128, 128), lambda i: (0, 0, 0)), pl.BlockSpec((2, 256, 256), lambda i: (0, 0, 0)),
                   pl.BlockSpec((2, 256, 256), lambda i: (0, 0, 0))],
        out_shape=[jax.ShapeDtypeStruct((T, DIN), BF16), jax.ShapeDtypeStruct((16, 512), F32),
                   jax.ShapeDtypeStruct((4, 128, 128), F32), jax.ShapeDtypeStruct((2, 256, 256), F32),
                   jax.ShapeDtypeStruct((2, 256, 256), F32)],
        scratch=[pltpu.VMEM((8, DR), F32), pltpu.VMEM((HALO, DP), F32), pltpu.VMEM((HALO, DR), F32)], job=job)


def bwd_in(dproj, h, g1, w_g, dh1, job=None):
    T = h.shape[0]

    def body(dp_ref, h_ref, g_ref, w_ref, dh1_ref, dh_ref, dw_ref, dg_ref, wfull_ref, acc_ref):
        i = pl.program_id(0)

        @pl.when(i == 0)
        def _():
            _chip_slabs_to_columns(w_ref, wfull_ref)
            acc_ref[...] = jnp.zeros_like(acc_ref)
            dg_ref[...] = jnp.zeros_like(dg_ref)

        hv = h_ref[...]
        r = _rms(hv)
        n = hv * r
        gv = g_ref[...]
        xn = (n * gv).astype(BF16)
        dpv = dp_ref[...]
        dxn = _nt(dpv, wfull_ref[...])
        acc_ref[...] += _tn(xn, dpv)
        dg_ref[...] += jnp.sum(dxn * n, axis=0, keepdims=True)
        dh_ref[...] = dh1_ref[...] + _rms_bwd(n, r, dxn * gv)

        @pl.when(i == T // TT - 1)
        def _():
            for q in range(NQ):
                dw_ref[q] = acc_ref[:, WIN_S * q:WIN_S * (q + 1)]

    return tiled_call(
        "bwd_in", T // TT, body, (dproj, h, g1, w_g, dh1),
        in_specs=[pl.BlockSpec((TT, DIN), lambda i: (i, 0)), pl.BlockSpec((TT, D), lambda i: (i, 0)),
                  _resident((1, D), lambda i: (0, 0)), _resident((NQ, D, WIN_S), lambda i: (0, 0, 0)),
                  pl.BlockSpec((TT, D), lambda i: (i, 0))],
        out_specs=[pl.BlockSpec((TT, D), lambda i: (i, 0)), pl.BlockSpec((NQ, D, WIN_S), lambda i: (0, 0, 0)),
                   pl.BlockSpec((1, D), lambda i: (0, 0))],
        out_shape=[jax.ShapeDtypeStruct((T, D), F32), jax.ShapeDtypeStruct((NQ, D, WIN_S), F32),
                   jax.ShapeDtypeStruct((1, D), F32)],
        scratch=[pltpu.VMEM((D, DIN), BF16), pltpu.VMEM((D, DIN), F32)], job=job)


def _row_block(rows, cols, itemsize=4, budget=2 * 1024 * 1024):
    best = None
    for b in range(16, rows + 1, 16):
        if rows % b == 0 and b * cols * itemsize <= budget:
            best = b
    return best if best is not None else rows


def add_pairs(full, got, dtype):
    core = lax.axis_index("c").astype(jnp.int32).reshape(1)
    outs = []
    for k in range(len(full)):
        q, hr, c = got[k].shape
        rb = _row_block(hr, c)
        nb = hr // rb

        def body(c_ref, a_ref, b_ref, o_ref):
            o_ref[...] = (a_ref[...] + b_ref[...]).astype(dtype)

        outs.append(pl.pallas_call(
            body, name="add_pairs",
            grid_spec=pltpu.PrefetchScalarGridSpec(
                num_scalar_prefetch=1, grid=(q, nb),
                in_specs=[pl.BlockSpec((None, rb, c), lambda qi, i, c_ref, nb=nb: (qi, c_ref[0] * nb + i, 0)),
                          pl.BlockSpec((None, rb, c), lambda qi, i, c_ref: (qi, i, 0))],
                out_specs=pl.BlockSpec((None, rb, c), lambda qi, i, c_ref: (qi, i, 0))),
            out_shape=jax.ShapeDtypeStruct((q, hr, c), dtype),
            compiler_params=_params(("arbitrary", "arbitrary")),
        )(core, full[k], got[k]))
    return outs


def sum_chips(parts):
    core = lax.axis_index("c").astype(jnp.int32).reshape(1)
    outs = []
    for p in parts:
        _, hr, c = p.shape
        rb = _row_block(hr, c)
        nb = hr // rb

        def body(c_ref, p_ref, o_ref):
            s = p_ref[0].astype(F32) + p_ref[1].astype(F32)
            s = s + p_ref[2].astype(F32)
            o_ref[...] = s + p_ref[3].astype(F32)

        outs.append(pl.pallas_call(
            body, name="sum_chips",
            grid_spec=pltpu.PrefetchScalarGridSpec(
                num_scalar_prefetch=1, grid=(nb,),
                in_specs=[pl.BlockSpec((NQ, rb, c), lambda i, c_ref: (0, i, 0))],
                out_specs=pl.BlockSpec((rb, c), lambda i, c_ref, nb=nb: (c_ref[0] * nb + i, 0))),
            out_shape=jax.ShapeDtypeStruct((2 * hr, c), F32),
            compiler_params=_params(("arbitrary",)),
        )(core, p))
    return outs


def adamw(w, g, m, v, job=None):
    r, c = w.shape
    rb = _row_block(r, c, budget=1024 * 1024)
    c1 = 1.0 / (1.0 - ADAM_B1 ** ADAM_STEP)
    c2 = 1.0 / (1.0 - ADAM_B2 ** ADAM_STEP)

    def body(w_ref, g_ref, m_ref, v_ref, d_ref, nm_ref, nv_ref):
        gv = g_ref[...]
        nm = ADAM_B1 * m_ref[...] + (1.0 - ADAM_B1) * gv
        nv = ADAM_B2 * v_ref[...] + (1.0 - ADAM_B2) * (gv * gv)
        nm_ref[...] = nm
        nv_ref[...] = nv
        d_ref[...] = -ADAM_LR * ((nm * c1) / (jnp.sqrt(nv * c2) + ADAM_EPS) + ADAM_WD * w_ref[...])

    spec = pl.BlockSpec((rb, c), lambda i: (i, 0))
    return tiled_call("adamw", r // rb, body, (w, g, m, v), in_specs=[spec] * 4, out_specs=[spec] * 3,
                      out_shape=[jax.ShapeDtypeStruct((r, c), F32)] * 3, job=job)


def _place():
    return lax.axis_index("x"), lax.axis_index("y"), lax.axis_index("c")


def _other_chips(x, y):
    return [(1 - x, y), (x, 1 - y), (1 - x, 1 - y)]


LOCAL_CHUNKS = 4
ICI_CHUNKS = 2
FWD_CHUNKS = 8


def _start_remote(src_rows, dst_rows, rows, chunks, send_sem, recv_sem, dev):
    rc = rows // chunks
    for j in range(chunks):
        pltpu.make_async_remote_copy(src_rows(j * rc, rc), dst_rows(j * rc, rc), send_sem, recv_sem,
                                     device_id=dev, device_id_type=MESH).start()


def _waiter(src, dst, send_sem, recv_sem):
    x, y, c = _place()
    return pltpu.make_async_remote_copy(src, dst, send_sem, recv_sem, device_id=(x, y, c), device_id_type=MESH)


class GatherJob:
    def __init__(self, shards, layer):
        self.shards, self.layer, self.n = list(shards), layer, len(shards)
        self.operands = list(shards)
        self.out_shape = [jax.ShapeDtypeStruct((NQ,) + s.shape[1:], s.dtype) for s in shards]
        sems = pltpu.SemaphoreType.DMA((self.n, 3))
        self.scratch = [sems, sems, sems, sems, pltpu.SemaphoreType.DMA((self.n,))]

    def _half(self, k):
        return self.shards[k].shape[1] // 2

    def _src(self, ins, k, half):
        hr = self._half(k)
        return lambda r0, nr: ins[k].at[self.layer, pl.ds(half * hr + r0, nr), :]

    def _dst(self, outs, k, chip, half):
        hr = self._half(k)
        return lambda r0, nr: outs[k].at[2 * chip[0] + chip[1], pl.ds(half * hr + r0, nr), :]

    def start(self, ins, outs, scr):
        send, recv, fsend, frecv, lsem = scr
        x, y, c = _place()
        for k in range(self.n):
            rows = self.shards[k].shape[1]
            rc = rows // LOCAL_CHUNKS
            for j in range(LOCAL_CHUNKS):
                pltpu.make_async_copy(ins[k].at[self.layer, pl.ds(j * rc, rc), :],
                                      outs[k].at[2 * x + y, pl.ds(j * rc, rc), :], lsem.at[k]).start()
        for k in range(self.n):
            for j, chip in enumerate(_other_chips(x, y)):
                _start_remote(self._src(ins, k, c), self._dst(outs, k, (x, y), c), self._half(k), ICI_CHUNKS,
                              send.at[k, j], recv.at[k, j], (chip[0], chip[1], c))

    def mid(self, ins, outs, scr):
        send, recv, fsend, frecv, lsem = scr
        x, y, c = _place()
        for k in range(self.n):
            hr = self._half(k)
            for j, chip in enumerate(_other_chips(x, y)):
                got = self._dst(outs, k, chip, c)
                _waiter(got(0, hr), got(0, hr), send.at[k, j], recv.at[k, j]).wait_recv()
                _start_remote(got, got, hr, FWD_CHUNKS, fsend.at[k, j], frecv.at[k, j], (x, y, 1 - c))

    def finish(self, ins, outs, scr):
        send, recv, fsend, frecv, lsem = scr
        x, y, c = _place()
        for k in range(self.n):
            hr = self._half(k)
            for j, chip in enumerate(_other_chips(x, y)):
                theirs = self._dst(outs, k, chip, 1 - c)(0, hr)
                w = _waiter(theirs, theirs, fsend.at[k, j], frecv.at[k, j])
                w.wait_recv()
                w.wait_send()
                _waiter(theirs, theirs, send.at[k, j], recv.at[k, j]).wait_send()
            pltpu.make_async_copy(ins[k].at[self.layer], outs[k].at[2 * x + y], lsem.at[k]).wait()


class ExchangeJob:
    def __init__(self, arrs, scatter):
        self.arrs, self.scatter, self.n = list(arrs), list(scatter), len(arrs)
        self.operands = list(arrs)
        self.out_shape = [jax.ShapeDtypeStruct((NQ,) + a.shape[1:], a.dtype) for a in arrs]
        sems = pltpu.SemaphoreType.DMA((self.n, 3))
        self.scratch = [sems, sems, pltpu.SemaphoreType.DMA((self.n,))]

    def _slot(self, ref, s):
        return lambda r0, nr: ref.at[s, pl.ds(r0, nr), :]

    def start(self, ins, outs, scr):
        send, recv, lsem = scr
        x, y, c = _place()
        p = 2 * x + y
        for k in range(self.n):
            rows = self.arrs[k].shape[1]
            rc = rows // ICI_CHUNKS
            for j in range(ICI_CHUNKS):
                pltpu.make_async_copy(ins[k].at[p if self.scatter[k] else 0, pl.ds(j * rc, rc), :],
                                      outs[k].at[p, pl.ds(j * rc, rc), :], lsem.at[k]).start()
            for j, chip in enumerate(_other_chips(x, y)):
                q = 2 * chip[0] + chip[1]
                _start_remote(self._slot(ins[k], q if self.scatter[k] else 0), self._slot(outs[k], p), rows, ICI_CHUNKS,
                              send.at[k, j], recv.at[k, j], (chip[0], chip[1], c))

    def mid(self, ins, outs, scr):
        pass

    def finish(self, ins, outs, scr):
        send, recv, lsem = scr
        x, y, c = _place()
        for k in range(self.n):
            for j, chip in enumerate(_other_chips(x, y)):
                slot = outs[k].at[2 * chip[0] + chip[1]]
                w = _waiter(slot, slot, send.at[k, j], recv.at[k, j])
                w.wait_recv()
                w.wait_send()
            pltpu.make_async_copy(ins[k].at[0], outs[k].at[0], lsem.at[k]).wait()


def run_job(job, name):
    n_in, n_out = len(job.operands), len(job.out_shape)

    def body(*refs):
        ins, outs, scr = refs[:n_in], refs[n_in:n_in + n_out], refs[n_in + n_out:]
        job.start(ins, outs, scr)
        job.mid(ins, outs, scr)
        job.finish(ins, outs, scr)

    return pl.pallas_call(body, name=name, in_specs=[ANY] * n_in, out_specs=[ANY] * n_out, out_shape=job.out_shape,
                          input_output_aliases=dict(getattr(job, "aliases", {})),
                          scratch_shapes=job.scratch)(*job.operands)


def tiled_call(name, steps, body, args, in_specs, out_specs, out_shape, scratch=(), job=None, mid_step=None):
    if job is None:
        return pl.pallas_call(body, name=name, grid=(steps,), in_specs=in_specs, out_specs=out_specs, out_shape=out_shape,
                              scratch_shapes=list(scratch), compiler_params=_params(("arbitrary",)))(*args), []
    n_in, n_out, n_scr = len(args), len(out_shape), len(scratch)
    j_in, j_out = len(job.operands), len(job.out_shape)
    mid_step = steps // 2 if mid_step is None else mid_step

    def carried(*refs):
        a, ji = refs[:n_in], refs[n_in:n_in + j_in]
        o = refs[n_in + j_in:n_in + j_in + n_out]
        jo = refs[n_in + j_in + n_out:n_in + j_in + n_out + j_out]
        rest = refs[n_in + j_in + n_out + j_out:]
        sc, js = rest[:n_scr], rest[n_scr:]
        i = pl.program_id(0)

        @pl.when(i == 0)
        def _():
            job.start(ji, jo, js)

        body(*a, *o, *sc)

        @pl.when(i == mid_step)
        def _():
            job.mid(ji, jo, js)

        @pl.when(i == steps - 1)
        def _():
            job.finish(ji, jo, js)

    res = pl.pallas_call(
        carried, name=name, grid=(steps,), in_specs=list(in_specs) + [ANY] * j_in, out_specs=list(out_specs) + [ANY] * j_out,
        out_shape=list(out_shape) + list(job.out_shape), scratch_shapes=list(scratch) + list(job.scratch),
        input_output_aliases={n_in + i: n_out + o for i, o in getattr(job, "aliases", {}).items()},
        compiler_params=_params(("arbitrary",)))(*args, *job.operands)
    return res[:n_out], res[n_out:]


def gather_small(shard):
    r, c = shard.shape

    def body(in_ref, out_ref, send, recv, lsem):
        x, y, cc = _place()
        own = pltpu.make_async_copy(in_ref, out_ref.at[2 * x + y], lsem)
        own.start()
        sends = []
        for j, chip in enumerate(_other_chips(x, y)):
            cp = pltpu.make_async_remote_copy(in_ref, out_ref.at[2 * x + y], send.at[j], recv.at[j],
                                              device_id=(chip[0], chip[1], cc), device_id_type=MESH)
            cp.start()
            sends.append(cp)
        for j, chip in enumerate(_other_chips(x, y)):
            slot = out_ref.at[2 * chip[0] + chip[1]]
            pltpu.make_async_remote_copy(slot, slot, send.at[j], recv.at[j],
                                         device_id=(chip[0], chip[1], cc), device_id_type=MESH).wait_recv()
        for cp in sends:
            cp.wait_send()
        own.wait()

    vm = pl.BlockSpec(memory_space=pltpu.VMEM)
    return pl.pallas_call(
        body, name="gather_small", in_specs=[vm], out_specs=vm,
        out_shape=jax.ShapeDtypeStruct((NQ, r, c), shard.dtype),
        scratch_shapes=[pltpu.SemaphoreType.DMA((3,)), pltpu.SemaphoreType.DMA((3,)), pltpu.SemaphoreType.DMA],
    )(shard)


D2D_CHUNKS = 4


class SwapJob:
    def __init__(self, arrs):
        self.arrs, self.n = list(arrs), len(arrs)
        self.operands = list(arrs)
        self.out_shape = [jax.ShapeDtypeStruct((a.shape[0], a.shape[1] // 2, a.shape[2]), a.dtype) for a in arrs]
        self.scratch = [pltpu.SemaphoreType.DMA((self.n,)), pltpu.SemaphoreType.DMA((self.n,))]

    def start(self, ins, got, scr):
        send, recv = scr
        x, y, c = _place()
        for k in range(self.n):
            q, r, _ = self.arrs[k].shape
            hr = r // 2
            for qi in range(q):
                _start_remote(lambda r0, nr: ins[k].at[qi, pl.ds((1 - c) * hr + r0, nr), :],
                              lambda r0, nr: got[k].at[qi, pl.ds(r0, nr), :], hr, D2D_CHUNKS,
                              send.at[k], recv.at[k], (x, y, 1 - c))

    def mid(self, ins, got, scr):
        pass

    def finish(self, ins, got, scr):
        send, recv = scr
        for k in range(self.n):
            hr = self.arrs[k].shape[1] // 2
            w = _waiter(ins[k].at[:, pl.ds(0, hr), :], got[k], send.at[k], recv.at[k])
            w.wait_send()
            w.wait_recv()


def swap_halves(arrs):
    return run_job(SwapJob(arrs), "swap_halves")


JOIN_CHUNKS = 8


class JoinJob:
    def __init__(self, arrs):
        self.arrs, self.n = list(arrs), len(arrs)
        self.operands = list(arrs)
        self.out_shape = [jax.ShapeDtypeStruct(a.shape, a.dtype) for a in arrs]
        self.scratch = [pltpu.SemaphoreType.DMA((self.n,)), pltpu.SemaphoreType.DMA((self.n,))]
        self.aliases = {k: k for k in range(self.n)}

    def start(self, ins, outs, scr):
        send, recv = scr
        x, y, c = _place()
        for k in range(self.n):
            hr = self.arrs[k].shape[0] // 2
            rows = lambda r0, nr: outs[k].at[pl.ds(c * hr + r0, nr), :]
            _start_remote(rows, rows, hr, JOIN_CHUNKS, send.at[k], recv.at[k], (x, y, 1 - c))

    def mid(self, ins, outs, scr):
        pass

    def finish(self, ins, outs, scr):
        send, recv = scr
        x, y, c = _place()
        for k in range(self.n):
            hr = self.arrs[k].shape[0] // 2
            w = _waiter(outs[k].at[pl.ds(c * hr, hr), :], outs[k].at[pl.ds((1 - c) * hr, hr), :], send.at[k], recv.at[k])
            w.wait_send()
            w.wait_recv()


def join_halves(arrs):
    return run_job(JoinJob(arrs), "join_halves")


def _block_diag(w):
    eye = jnp.eye(4, dtype=F32)[None, :, None, :, None]
    return (w.reshape(-1, 4, 64, 1, 64) * eye).reshape(-1, 256, 256).astype(BF16)


def _diag_blocks(b):
    eye = jnp.eye(4, dtype=F32)[None, :, None, :, None]
    return (b.reshape(2, 4, 64, 4, 64) * eye).sum(axis=3).reshape(8, 64, 64)


def _vec_params(pool_b, pool_scale, conv_b, gate_r_b, gate_i_b, lru_lambda, conv_w_full, group_norm_g):
    rows = [pool_b, pool_scale, conv_b, gate_r_b, gate_i_b, lru_lambda,
            conv_w_full[:, 0], conv_w_full[:, 1], conv_w_full[:, 2], conv_w_full[:, 3],
            group_norm_g[:, :DP], group_norm_g[:, DP:]]
    return jnp.pad(jnp.stack(rows, axis=1), ((0, 0), (0, 4), (0, 0)))


SMALL_ROWS = 1280


def _pack_small(layers, final_g, meta):
    def tile_rows(v):
        return jnp.pad(v.reshape(2, 512), ((0, 6), (0, 0)))

    rows = []
    for vec, pw, wr, wi, g1, g2 in layers:
        rows += [vec, pw.reshape(128, 512), wr.reshape(64, 512), wi.reshape(64, 512), tile_rows(g1), tile_rows(g2)]
    rows += [tile_rows(final_g), meta.reshape(32, 512)]
    used = sum(r.shape[0] for r in rows)
    return jnp.concatenate(rows + [jnp.zeros((SMALL_ROWS - used, 512), F32)], axis=0)


def _unpack_small(flat):
    layers, o = [], 0
    for _ in range(DEPTH):
        vec = flat[o:o + 16]; o += 16
        pw = flat[o:o + 128].reshape(4, 128, 128); o += 128
        wr = flat[o:o + 64].reshape(8, 64, 64); o += 64
        wi = flat[o:o + 64].reshape(8, 64, 64); o += 64
        g1 = flat[o:o + 2].reshape(1024); o += 8
        g2 = flat[o:o + 2].reshape(1024); o += 8
        layers.append((vec, pw, wr, wi, g1, g2))
    final_g = flat[o:o + 2].reshape(1024); o += 8
    meta = flat[o:o + 32].reshape(16, 1024)
    return layers, final_g, meta


def local_step(x2d, tgt2d, meta_full, conv_w_full, sp, shards=None, gathered=None):
    exchange = gathered is None
    if exchange:
        gathered = [None] * DEPTH
        first_in = run_job(GatherJob(shards[:1], 0), "gather_first")
    h = None
    saved = []
    vp_all = _vec_params(sp["pool_b"], sp["pool_scale"], sp["conv_b"], sp["gate_r_b"], sp["gate_i_b"], sp["lru_lambda"],
                         conv_w_full, sp["group_norm_g"])
    pw_all = sp["pool_w"].astype(BF16)
    br_all = _block_diag(sp["gate_r_w"].reshape(DEPTH * 8, 64, 64)).reshape(DEPTH, 2, 256, 256)
    bi_all = _block_diag(sp["gate_i_w"].reshape(DEPTH * 8, 64, 64)).reshape(DEPTH, 2, 256, 256)
    for l in range(DEPTH):
        vp, pw, br, bi = vp_all[l], pw_all[l], br_all[l], bi_all[l]
        if l == 0:
            job = GatherJob(shards[1:], 0) if exchange else None
            win = first_in[0] if exchange else gathered[0][0]
            (proj, y, hs, aux, h), rest = fwd_mix(x2d, sp["mix_norm_g"][l][None], win, vp, pw, br, bi, job, meta=meta_full)
            wo, wu, wd = rest if exchange else gathered[0][1:]
        else:
            win, wo, wu, wd = gathered[l]
            (proj, y, hs, aux), _ = fwd_mix(h, sp["mix_norm_g"][l][None], win, vp, pw, br, bi)
        wo = wo.reshape(D, D)
        if l + 1 < DEPTH:
            job = GatherJob(shards, l + 1) if exchange else None
            (h1, a, h_next), fetched = fwd_post(h, y, wo, sp["mlp_norm_g"][l][None], wu, wd, job)
            if exchange:
                gathered[l + 1] = fetched
        else:
            (h1, a, dh, dgf, loss_part), _ = fwd_post(h, y, wo, sp["mlp_norm_g"][l][None], wu, wd,
                                                      loss=(sp["final_norm_g"][None], tgt2d))
            h_next = None
        saved.append((h, proj, y, hs, aux, h1, a, vp, pw, br, bi, win, wo, wu, wd))
        h = h_next

    big = [[None, None, None, None] for _ in range(DEPTH)]
    io_raw, io_pairs = [], None
    small_layers = [None] * DEPTH
    for l in reversed(range(DEPTH)):
        h0, proj, y, hs, aux, h1, a, vp, pw, br, bi, win, wo, wu, wd = saved[l]
        g1 = sp["mix_norm_g"][l][None]
        job = ExchangeJob(io_pairs, [True] * 2) if io_pairs is not None else None
        (dh1, da, xn2, dg2), io_parts = bwd_mlp_dx(dh, h1, a, sp["mlp_norm_g"][l][None], wu, wd, job)
        io_sums = sum_chips(io_parts) if job is not None else []
        dwu, dwd = bwd_mlp_dw(xn2, da, a, dh)
        if not exchange:
            (dy, dwo), _ = bwd_out(dh1, y, wo)
            (dproj, ga, dpw, dwr, dwi), _ = bwd_mixer(dy, proj, hs, aux, vp, pw, br, bi)
            (dh, dwin, dg1), _ = bwd_in(dproj, h0, g1, win, dh1)
            big[l] = [dwin, dwo.reshape(NQ, D // NQ, D), dwu, dwd]
        else:
            arrs = [dwu, dwd] + io_raw
            (dy, dwo), got = bwd_out(dh1, y, wo, SwapJob(arrs))
            pairs = add_pairs(arrs, got, BF16)
            crossing = pairs if l == 0 else pairs[:2]
            io_pairs = None if l == 0 or not io_raw else pairs[2:]
            (dproj, ga, dpw, dwr, dwi), parts = bwd_mixer(dy, proj, hs, aux, vp, pw, br, bi,
                                                          ExchangeJob(crossing, [True] * len(crossing)))
            (dh, dwin, dg1), sums = bwd_in(dproj, h0, g1, win, dh1, JoinJob(list(sum_chips(parts)) + list(io_sums)))
            big[l][2:] = sums[:2]
            if l == 0:
                big[1][:2] = sums[2:4]
            if io_sums:
                big[l + 2][:2] = sums[-2:]
            io_raw = [dwin, dwo.reshape(NQ, D // NQ, D)]
        small_layers[l] = (ga, dpw, _diag_blocks(dwr), _diag_blocks(dwi), dg1[0], dg2[0])
    return loss_part, dh, big, small_layers, dgf, io_raw


def kernel(x, meta_tokens, mix_norm_g, w_in, pool_w, pool_b, pool_scale, conv_w, conv_b, gate_r_w, gate_r_b, gate_i_w, gate_i_b, lru_lambda, group_norm_g, w_out, mlp_norm_g, w_up, w_down, final_norm_g, loss_target, m_meta_tokens, m_mix_norm_g, m_w_in, m_pool_w, m_pool_b, m_pool_scale, m_conv_w, m_conv_b, m_gate_r_w, m_gate_r_b, m_gate_i_w, m_gate_i_b, m_lru_lambda, m_group_norm_g, m_w_out, m_mlp_norm_g, m_w_up, m_w_down, m_final_norm_g, v_meta_tokens, v_mix_norm_g, v_w_in, v_pool_w, v_pool_b, v_pool_scale, v_conv_w, v_conv_b, v_gate_r_w, v_gate_r_b, v_gate_i_w, v_gate_i_b, v_lru_lambda, v_group_norm_g, v_w_out, v_mlp_norm_g, v_w_up, v_w_down, v_final_norm_g):
    p = 2 * lax.axis_index("x") + lax.axis_index("y")

    shards = [w_in.astype(BF16), w_out.astype(BF16), w_up.astype(BF16), w_down.astype(BF16)]
    small = jnp.concatenate([meta_tokens, jnp.pad(conv_w.reshape(16, 128), ((0, 0), (0, 128)))], axis=0)
    small_g = gather_small(small)
    meta_full = jnp.transpose(small_g[:, :16, :], (1, 0, 2)).reshape(NMETA, D)
    conv_w_full = jnp.transpose(small_g[:, 16:, :128].reshape(NQ, DEPTH, 4, 128), (1, 2, 0, 3)).reshape(DEPTH, 4, DR)

    sp = dict(mix_norm_g=mix_norm_g, pool_w=pool_w, pool_b=pool_b, pool_scale=pool_scale, conv_b=conv_b, gate_r_w=gate_r_w,
              gate_r_b=gate_r_b, gate_i_w=gate_i_w, gate_i_b=gate_i_b, lru_lambda=lru_lambda, group_norm_g=group_norm_g,
              mlp_norm_g=mlp_norm_g, final_norm_g=final_norm_g)
    loss_part, dh, big, small_layers, dgf, io0_parts = local_step(x[0], loss_target[0], meta_full, conv_w_full, sp,
                                                                  shards=shards)
    loss = lax.psum(loss_part[0, 0], ("x", "y", "c"))
    grad_x = dh[NMETA:][None]

    weights = dict(meta_tokens=meta_tokens, mix_norm_g=mix_norm_g, w_in=w_in, pool_w=pool_w, pool_b=pool_b, pool_scale=pool_scale,
                   conv_w=conv_w, conv_b=conv_b, gate_r_w=gate_r_w, gate_r_b=gate_r_b, gate_i_w=gate_i_w, gate_i_b=gate_i_b,
                   lru_lambda=lru_lambda, group_norm_g=group_norm_g, w_out=w_out, mlp_norm_g=mlp_norm_g, w_up=w_up, w_down=w_down,
                   final_norm_g=final_norm_g)
    mom_m = dict(meta_tokens=m_meta_tokens, mix_norm_g=m_mix_norm_g, w_in=m_w_in, pool_w=m_pool_w, pool_b=m_pool_b,
                 pool_scale=m_pool_scale, conv_w=m_conv_w, conv_b=m_conv_b, gate_r_w=m_gate_r_w, gate_r_b=m_gate_r_b,
                 gate_i_w=m_gate_i_w, gate_i_b=m_gate_i_b, lru_lambda=m_lru_lambda, group_norm_g=m_group_norm_g, w_out=m_w_out,
                 mlp_norm_g=m_mlp_norm_g, w_up=m_w_up, w_down=m_w_down, final_norm_g=m_final_norm_g)
    mom_v = dict(meta_tokens=v_meta_tokens, mix_norm_g=v_mix_norm_g, w_in=v_w_in, pool_w=v_pool_w, pool_b=v_pool_b,
                 pool_scale=v_pool_scale, conv_w=v_conv_w, conv_b=v_conv_b, gate_r_w=v_gate_r_w, gate_r_b=v_gate_r_b,
                 gate_i_w=v_gate_i_w, gate_i_b=v_gate_i_b, lru_lambda=v_lru_lambda, group_norm_g=v_group_norm_g, w_out=v_w_out,
                 mlp_norm_g=v_mlp_norm_g, w_up=v_w_up, w_down=v_w_down, final_norm_g=v_final_norm_g)
    names = list(weights)
    delta, new_m, new_v = {}, {}, {}

    def adam_big(nm, g, job=None):
        shp = weights[nm].shape
        two_d = lambda t: t.reshape(shp[0] * shp[1], shp[2])
        (d_, m_, v_), got = adamw(two_d(weights[nm]), two_d(g), two_d(mom_m[nm]), two_d(mom_v[nm]), job)
        delta[nm], new_m[nm], new_v[nm] = d_.reshape(shp), m_.reshape(shp), v_.reshape(shp)
        return got

    last = [_pack_small(small_layers, dgf[0], dh[:NMETA])[None]] + io0_parts
    got = swap_halves(last)
    pairs = add_pairs(last[:1], got[:1], F32) + add_pairs(last[1:], got[1:], BF16)
    sums = join_halves(sum_chips(run_job(ExchangeJob(pairs, [False, True, True]), "exchange_last")))
    small_sum, io0 = sums[0], sums[1:]
    g_up = jnp.stack([big[l][-2] for l in range(DEPTH)])
    g_down = jnp.stack([big[l][-1] for l in range(DEPTH)])
    adam_big("w_up", g_up)
    adam_big("w_down", g_down)
    g_in = jnp.stack([io0[0]] + [big[l][0] for l in range(1, DEPTH)])
    g_out = jnp.stack([io0[1]] + [big[l][1] for l in range(1, DEPTH)])
    adam_big("w_in", g_in)
    adam_big("w_out", g_out)
    g_layers, g_final, g_meta_full = _unpack_small(small_sum)

    g_vec = [gl[0] for gl in g_layers]
    grads = dict(
        meta_tokens=lax.dynamic_slice(g_meta_full, (0, p * (D // NQ)), (NMETA, D // NQ)),
        mix_norm_g=jnp.stack([gl[4] for gl in g_layers]),
        w_in=g_in,
        pool_w=jnp.stack([gl[1] for gl in g_layers]),
        pool_b=jnp.stack([gv[GA_PB] for gv in g_vec]),
        pool_scale=jnp.stack([gv[GA_PS] for gv in g_vec]),
        conv_w=lax.dynamic_slice(jnp.stack([gv[GA_CW:GA_CW + 4] for gv in g_vec]), (0, 0, p * 128), (DEPTH, 4, 128)),
        conv_b=jnp.stack([gv[GA_CB] for gv in g_vec]),
        gate_r_w=jnp.stack([gl[2] for gl in g_layers]),
        gate_r_b=jnp.stack([gv[GA_BR] for gv in g_vec]),
        gate_i_w=jnp.stack([gl[3] for gl in g_layers]),
        gate_i_b=jnp.stack([gv[GA_BI] for gv in g_vec]),
        lru_lambda=jnp.stack([gv[GA_LAM] for gv in g_vec]),
        group_norm_g=jnp.stack([jnp.concatenate([gv[GA_GNP], gv[GA_GNR]]) for gv in g_vec]),
        w_out=g_out,
        mlp_norm_g=jnp.stack([gl[5] for gl in g_layers]),
        w_up=g_up,
        w_down=g_down,
        final_norm_g=g_final,
    )

    small_names = [nm for nm in names if nm not in ("w_in", "w_out", "w_up", "w_down")]
    sizes = [weights[nm].size for nm in small_names]
    total = sum(sizes)
    rows = -(-total // 512)
    rows = -(-rows // 16) * 16

    def flat(tree, fill):
        v_ = jnp.concatenate([tree[nm].reshape(-1) for nm in small_names])
        return jnp.concatenate([v_, jnp.full((rows * 512 - total,), fill, F32)]).reshape(rows, 512)

    (d_, m_, v_), _ = adamw(flat(weights, 0.0), flat(grads, 0.0), flat(mom_m, 0.0), flat(mom_v, 1.0))
    o = 0
    for nm, sz in zip(small_names, sizes):
        shp = weights[nm].shape
        delta[nm] = d_.reshape(-1)[o:o + sz].reshape(shp)
        new_m[nm] = m_.reshape(-1)[o:o + sz].reshape(shp)
        new_v[nm] = v_.reshape(-1)[o:o + sz].reshape(shp)
        o += sz

    return (loss, grad_x, *[grads[nm] for nm in names], *[delta[nm] for nm in names],
            *[new_m[nm] for nm in names], *[new_v[nm] for nm in names])
```

```python
import jax
import jax.numpy as jnp
from jax import lax
from jax.experimental import pallas as pl
from jax.experimental.pallas import tpu as pltpu

F32 = jnp.float32
BF16 = jnp.bfloat16

D = 1024
DP = 512
DR = 512
DIN = 1536
DFF = 4096
DEPTH = 4
NMETA = 16
NQ = 4
WIN_S = DIN // NQ
FF_S = DFF // NQ
EPS = 1e-6
HALO = 16
TT = 432
TTW = 912
VMEM_LIMIT = 56 * 1024 * 1024

ADAM_LR = 0.001
ADAM_B1 = 0.9
ADAM_B2 = 0.999
ADAM_EPS = 1e-08
ADAM_WD = 0.01
ADAM_STEP = 10

MESH = pl.DeviceIdType.MESH
ANY = pl.BlockSpec(memory_space=pl.ANY)


def _params(sem=None, vmem=VMEM_LIMIT):
    return pltpu.CompilerParams(dimension_semantics=sem, vmem_limit_bytes=vmem)


def _resident(shape, index):
    return pl.BlockSpec(shape, index, pipeline_mode=pl.Buffered(1))


def _nt(x, w):
    return lax.dot_general(x, w, (((1,), (1,)), ((), ())), preferred_element_type=F32)


def _tn(a, b):
    return lax.dot_general(a, b, (((0,), (0,)), ((), ())), preferred_element_type=F32)


def _dot(x, w):
    return jnp.dot(x, w, preferred_element_type=F32)


def _rms(h):
    return lax.rsqrt(jnp.mean(h * h, axis=-1, keepdims=True) + EPS)


def _rms_bwd(n, r, dn):
    return r * (dn - n * jnp.mean(dn * n, axis=-1, keepdims=True))


def _down(x, s):
    return pltpu.roll(x, s, 0)


def _up(x, s):
    return pltpu.roll(x, x.shape[0] - s, 0)


GELU_C = 0.7978845608028654
GELU_K = 0.044715


def _gelu(x):
    t = jnp.tanh(GELU_C * (x + GELU_K * x * x * x))
    return 0.5 * x * (1.0 + t), t


def _softplus(x):
    return jnp.maximum(x, 0.0) + jnp.log1p(jnp.exp(-jnp.abs(x)))


VP_PB, VP_PS, VP_CB, VP_BR, VP_BI, VP_LAM, VP_CW, VP_GNP, VP_GNR = 0, 1, 2, 3, 4, 5, 6, 10, 11


def _row(vp, r):
    return vp[r:r + 1, :]


def _mixer_pre(ue, t0, vp, pw_ref, br_ref, bi_ref):
    tt = ue.shape[0] - HALO
    tf = (t0 + lax.broadcasted_iota(jnp.int32, (tt, 1), 0)).astype(F32) + 1.0
    pb, ps = _row(vp, VP_PB), _row(vp, VP_PS)
    pooled, mapped, inv_cnt = [], [], []
    for g in range(4):
        lanes = slice(128 * g, 128 * (g + 1))
        xe = ue[:, lanes]
        s = xe
        for j in range(g + 1):
            s = s + _down(s, 1 << j)
        inv = 1.0 / jnp.minimum(tf, float(2 << g))
        pg = s[HALO:] * inv - xe[HALO:]
        mg = _dot(pg.astype(BF16), pw_ref[g]) + pb[:, lanes]
        pooled.append(pg)
        mapped.append(mg)
        inv_cnt.append(inv)
    ypool = [mapped[g] * ps[:, 128 * g:128 * (g + 1)] for g in range(4)]

    xe = ue[:, DP:DP + DR]
    taps = [_down(xe, 3)[HALO:], _down(xe, 2)[HALO:], _down(xe, 1)[HALO:], xe[HALO:]]
    xc = _row(vp, VP_CB) + (taps[0] * _row(vp, VP_CW) + taps[1] * _row(vp, VP_CW + 1)
                            + taps[2] * _row(vp, VP_CW + 2) + taps[3] * _row(vp, VP_CW + 3))
    xcb = xc.astype(BF16)
    zr = jnp.concatenate([_dot(xcb[:, :256], br_ref[0]), _dot(xcb[:, 256:], br_ref[1])], axis=1) + _row(vp, VP_BR)
    zi = jnp.concatenate([_dot(xcb[:, :256], bi_ref[0]), _dot(xcb[:, 256:], bi_ref[1])], axis=1) + _row(vp, VP_BI)
    r = jax.nn.sigmoid(zr)
    ig = jax.nn.sigmoid(zi)
    sp = _softplus(-_row(vp, VP_LAM))
    la = (-8.0 * r) * sp
    a = jnp.exp(la)
    th = jnp.tanh(la)
    mult = jnp.sqrt((-2.0 * th) / (1.0 - th))
    gate = ue[HALO:, DP + DR:]
    gl, gt = _gelu(gate)
    return dict(pooled=pooled, mapped=mapped, ypool=ypool, inv_cnt=inv_cnt, taps=taps, xc=xc, xcb=xcb, r=r, ig=ig,
                sp=sp, a=a, mult=mult, gate=gate, gl=gl, gt=gt)


SUBLANES = 8


def _scan_fwd(a, b, h_in):
    tt = a.shape[0]
    sub = jnp.bitwise_and(lax.broadcasted_iota(jnp.int32, (tt, 1), 0), SUBLANES - 1)
    s = 1
    while s < SUBLANES:
        m = sub >= s
        a_s = jnp.where(m, _down(a, s), 1.0)
        b_s = jnp.where(m, _down(b, s), 0.0)
        b = a * b_s + b
        a = a * a_s
        s *= 2
    groups, h = [], h_in
    for g in range(tt // SUBLANES):
        rows = slice(SUBLANES * g, SUBLANES * (g + 1))
        hg = a[rows] * h + b[rows]
        groups.append(hg)
        h = hg[SUBLANES - 1:SUBLANES, :]
    return jnp.concatenate(groups, axis=0)


def _scan_rev(c, d, l_in):
    tt = c.shape[0]
    sub = jnp.bitwise_and(lax.broadcasted_iota(jnp.int32, (tt, 1), 0), SUBLANES - 1)
    s = 1
    while s < SUBLANES:
        m = sub < SUBLANES - s
        c_s = jnp.where(m, _up(c, s), 1.0)
        d_s = jnp.where(m, _up(d, s), 0.0)
        d = c * d_s + d
        c = c * c_s
        s *= 2
    groups, l = [], l_in
    for g in reversed(range(tt // SUBLANES)):
        rows = slice(SUBLANES * g, SUBLANES * (g + 1))
        lg = c[rows] * l + d[rows]
        groups.append(lg)
        l = lg[0:1, :]
    return jnp.concatenate(groups[::-1], axis=0)


def _halo_index(i, per_tile):
    return jnp.maximum(i * per_tile - 1, 0)


def _chip_slabs_to_columns(w_ref, wfull_ref):
    for q in range(NQ):
        wfull_ref[:, WIN_S * q:WIN_S * (q + 1)] = w_ref[q]


AUX_POOLED, AUX_MAPPED, AUX_XC, AUX_R, AUX_IG, AUX_A, AUX_MULT, AUX_GL, AUX_GT = range(9)
AUX_W = 9 * 512


def _aux(k):
    return slice(512 * k, 512 * (k + 1))


def fwd_mix(h, g, w_g, vp, pw, br, bi, job=None, meta=None):
    T = h.shape[0] + (NMETA if meta is not None else 0)
    n_h = 3 if meta is not None else 1

    def body(*refs):
        g_ref, w_ref, vp_ref, pw_ref, br_ref, bi_ref, p_ref, y_ref, hs_ref, aux_ref = refs[n_h:n_h + 10]
        wfull_ref, halo_ref, carry_ref = refs[-3:]
        i = pl.program_id(0)

        @pl.when(i == 0)
        def _():
            _chip_slabs_to_columns(w_ref, wfull_ref)
            carry_ref[...] = jnp.zeros_like(carry_ref)
            halo_ref[...] = jnp.zeros_like(halo_ref)

        if meta is not None:
            xp_ref, x_ref, meta_ref = refs[:n_h]
            hh = jnp.concatenate([jnp.where(i == 0, meta_ref[...], xp_ref[...]), x_ref[0:TT - HALO, :]], axis=0)
            refs[n_h + 10][...] = hh
        else:
            hh = refs[0][...]
        xn = (hh * _rms(hh) * g_ref[...]).astype(BF16)
        proj = _dot(xn, wfull_ref[...])
        p_ref[...] = proj
        ue = jnp.concatenate([halo_ref[...], proj], axis=0)
        halo_ref[...] = p_ref[TT - HALO:TT, :]
        vp_v = vp_ref[...]
        m = _mixer_pre(ue, i * TT, vp_v, pw_ref, br_ref, bi_ref)
        for g in range(4):
            aux_ref[:, 512 * AUX_POOLED + 128 * g:512 * AUX_POOLED + 128 * (g + 1)] = m["pooled"][g]
            aux_ref[:, 512 * AUX_MAPPED + 128 * g:512 * AUX_MAPPED + 128 * (g + 1)] = m["mapped"][g]
        for k, name in ((AUX_XC, "xc"), (AUX_R, "r"), (AUX_IG, "ig"), (AUX_A, "a"), (AUX_MULT, "mult"), (AUX_GL, "gl"),
                        (AUX_GT, "gt")):
            aux_ref[:, _aux(k)] = m[name]
        b = m["mult"] * (m["ig"] * m["xc"])
        hs = _scan_fwd(m["a"], b, carry_ref[0:1, :])
        hs_ref[...] = hs
        carry_ref[0:1, :] = hs_ref[TT - 1:TT, :]
        yr = hs * m["gl"]
        ssq = sum(jnp.sum(yp * yp, axis=-1, keepdims=True) for yp in m["ypool"])
        rp = lax.rsqrt(ssq * (1.0 / DP) + EPS)
        gnp = _row(vp_v, VP_GNP)
        for g in range(4):
            lanes = slice(128 * g, 128 * (g + 1))
            y_ref[:, lanes] = (m["ypool"][g] * rp * gnp[:, lanes]).astype(BF16)
        y_ref[:, DP:] = (yr * _rms(yr) * _row(vp_v, VP_GNR)).astype(BF16)

    if meta is not None:
        h_args, h_specs = (h, h, meta), _shifted_specs() + [_resident((NMETA, D), lambda i: (0, 0))]
    else:
        h_args, h_specs = (h,), [pl.BlockSpec((TT, D), lambda i: (i, 0))]
    full = meta is not None
    return tiled_call(
        "fwd_mix", T // TT, body, h_args + (g, w_g, vp, pw, br, bi),
        in_specs=h_specs + [_resident((1, D), lambda i: (0, 0)), _resident((NQ, D, WIN_S), lambda i: (0, 0, 0)),
                            _resident((16, 512), lambda i: (0, 0)), _resident((4, 128, 128), lambda i: (0, 0, 0)),
                            _resident((2, 256, 256), lambda i: (0, 0, 0)), _resident((2, 256, 256), lambda i: (0, 0, 0))],
        out_specs=[pl.BlockSpec((TT, DIN), lambda i: (i, 0)), pl.BlockSpec((TT, D), lambda i: (i, 0)),
                   pl.BlockSpec((TT, DR), lambda i: (i, 0)), pl.BlockSpec((TT, AUX_W), lambda i: (i, 0))]
        + [pl.BlockSpec((TT, D), lambda i: (i, 0))] * full,
        out_shape=[jax.ShapeDtypeStruct((T, DIN), F32), jax.ShapeDtypeStruct((T, D), BF16),
                   jax.ShapeDtypeStruct((T, DR), F32), jax.ShapeDtypeStruct((T, AUX_W), F32)]
        + [jax.ShapeDtypeStruct((T, D), F32)] * full,
        scratch=[pltpu.VMEM((D, DIN), BF16), pltpu.VMEM((HALO, DIN), F32), pltpu.VMEM((8, DR), F32)],
        job=job, mid_step=T // TT - 1)


def fwd_post(h, y, wo_g, g2, wu_g, wd_g, job=None, loss=None):
    T = h.shape[0]

    def body(*refs):
        h_ref, y_ref, wo_ref, g_ref, wu_ref, wd_ref = refs[:6]
        h1_ref, a_ref = refs[-5:-3] if loss is not None else refs[-3:-1]
        h1 = h_ref[...] + _dot(y_ref[...], wo_ref[...])
        h1_ref[...] = h1
        xn = (h1 * _rms(h1) * g_ref[...]).astype(BF16)
        acc = h1
        for q in range(NQ):
            a = _dot(xn, wu_ref[q])
            a_ref[:, FF_S * q:FF_S * (q + 1)] = a.astype(BF16)
            ra = jnp.maximum(a, 0.0)
            acc = acc + _dot((ra * ra).astype(BF16), wd_ref[q])
        if loss is None:
            refs[-1][...] = acc
        else:
            _loss_tile(acc, *refs[6:9], *refs[-3:])

    row = pl.BlockSpec((TT, D), lambda i: (i, 0))
    in_specs = [row, row, _resident((D, D), lambda i: (0, 0)), _resident((1, D), lambda i: (0, 0)),
                _resident((NQ, D, FF_S), lambda i: (0, 0, 0)), _resident((NQ, FF_S, D), lambda i: (0, 0, 0))]
    out_specs = [row, pl.BlockSpec((TT, DFF), lambda i: (i, 0)), row]
    out_shape = [jax.ShapeDtypeStruct((T, D), F32), jax.ShapeDtypeStruct((T, DFF), BF16), jax.ShapeDtypeStruct((T, D), F32)]
    args = (h, y, wo_g, g2, wu_g, wd_g)
    if loss is not None:
        args += (loss[0], loss[1], loss[1])
        in_specs += [_resident((1, D), lambda i: (0, 0))] + _shifted_specs()
        out_specs += [pl.BlockSpec((1, D), lambda i: (0, 0)), pl.BlockSpec((1, 1), lambda i: (0, 0))]
        out_shape += [jax.ShapeDtypeStruct((1, D), F32), jax.ShapeDtypeStruct((1, 1), F32)]
    return tiled_call("fwd_post", T // TT, body, args, in_specs=in_specs, out_specs=out_specs, out_shape=out_shape,
                      job=job, mid_step=(T // TT) * 3 // 4)


def _shifted_tile(prev_ref, cur_ref):
    return jnp.concatenate([prev_ref[...], cur_ref[0:TT - HALO, :]], axis=0)


def _shifted_specs():
    per = TT // HALO
    return [pl.BlockSpec((HALO, D), lambda i: (_halo_index(i, per), 0)), pl.BlockSpec((TT, D), lambda i: (i, 0))]


def _loss_tile(hh, g_ref, tp_ref, t_ref, dh_ref, dg_ref, loss_ref):
    i = pl.program_id(0)

    @pl.when(i == 0)
    def _():
        dg_ref[...] = jnp.zeros_like(dg_ref)
        loss_ref[...] = jnp.zeros_like(loss_ref)

    r = _rms(hh)
    n = hh * r
    gfv = g_ref[...]
    row = i * TT + lax.broadcasted_iota(jnp.int32, (TT, 1), 0)
    e = jnp.where(row >= NMETA, n * gfv - _shifted_tile(tp_ref, t_ref), 0.0)
    loss_ref[...] += 0.5 * jnp.sum(jnp.sum(e * e, axis=-1, keepdims=True) * (1.0 / D), axis=0, keepdims=True)
    dy = e * (1.0 / D)
    dg_ref[...] += jnp.sum(dy * n, axis=0, keepdims=True)
    dh_ref[...] = _rms_bwd(n, r, dy * gfv)


def bwd_mlp_dx(dh2, h1, a, g2, wu_g, wd_g, job=None):
    T = dh2.shape[0]

    def body(dh2_ref, h1_ref, a_ref, g_ref, wu_ref, wd_ref, dh1_ref, da_ref, xn_ref, dg_ref):
        @pl.when(pl.program_id(0) == 0)
        def _():
            dg_ref[...] = jnp.zeros_like(dg_ref)

        h1v = h1_ref[...]
        r = _rms(h1v)
        n = h1v * r
        gv = g_ref[...]
        xn_ref[...] = (n * gv).astype(BF16)
        dh2v = dh2_ref[...]
        dh2b = dh2v.astype(BF16)
        dxn = jnp.zeros((TT, D), F32)
        for q in range(NQ):
            cols = slice(FF_S * q, FF_S * (q + 1))
            ra = jnp.maximum(a_ref[:, cols].astype(F32), 0.0)
            da = (_nt(dh2b, wd_ref[q]) * (2.0 * ra)).astype(BF16)
            da_ref[:, cols] = da
            dxn = dxn + _nt(da, wu_ref[q])
        dg_ref[...] += jnp.sum(dxn * n, axis=0, keepdims=True)
        dh1_ref[...] = dh2v + _rms_bwd(n, r, dxn * gv)

    return tiled_call(
        "bwd_mlp_dx", T // TT, body, (dh2, h1, a, g2, wu_g, wd_g),
        in_specs=[pl.BlockSpec((TT, D), lambda i: (i, 0)), pl.BlockSpec((TT, D), lambda i: (i, 0)),
                  pl.BlockSpec((TT, DFF), lambda i: (i, 0)), _resident((1, D), lambda i: (0, 0)),
                  _resident((NQ, D, FF_S), lambda i: (0, 0, 0)), _resident((NQ, FF_S, D), lambda i: (0, 0, 0))],
        out_specs=[pl.BlockSpec((TT, D), lambda i: (i, 0)), pl.BlockSpec((TT, DFF), lambda i: (i, 0)),
                   pl.BlockSpec((TT, D), lambda i: (i, 0)), pl.BlockSpec((1, D), lambda i: (0, 0))],
        out_shape=[jax.ShapeDtypeStruct((T, D), F32), jax.ShapeDtypeStruct((T, DFF), BF16),
                   jax.ShapeDtypeStruct((T, D), BF16), jax.ShapeDtypeStruct((1, D), F32)],
        job=job)


def bwd_mlp_dw(xn, da, a, dh2):
    T = xn.shape[0]

    def body(xn_ref, da_ref, a_ref, dh2_ref, dwu_ref, dwd_ref):
        @pl.when(pl.program_id(1) == 0)
        def _():
            dwu_ref[...] = jnp.zeros_like(dwu_ref)
            dwd_ref[...] = jnp.zeros_like(dwd_ref)

        dwu_ref[...] += _tn(xn_ref[...], da_ref[...])
        ra = jnp.maximum(a_ref[...].astype(F32), 0.0)
        dwd_ref[...] += _tn((ra * ra).astype(BF16), dh2_ref[...].astype(BF16))

    return pl.pallas_call(
        body, name="bwd_mlp_dw", grid=(NQ, T // TTW),
        in_specs=[pl.BlockSpec((TTW, D), lambda q, i: (i, 0)), pl.BlockSpec((TTW, FF_S), lambda q, i: (i, q)),
                  pl.BlockSpec((TTW, FF_S), lambda q, i: (i, q)), pl.BlockSpec((TTW, D), lambda q, i: (i, 0))],
        out_specs=[pl.BlockSpec((None, D, FF_S), lambda q, i: (q, 0, 0)),
                   pl.BlockSpec((None, FF_S, D), lambda q, i: (q, 0, 0))],
        out_shape=[jax.ShapeDtypeStruct((NQ, D, FF_S), F32), jax.ShapeDtypeStruct((NQ, FF_S, D), F32)],
        compiler_params=_params(("arbitrary", "arbitrary")),
    )(xn, da, a, dh2)


def bwd_out(dh1, y, wo_g, job=None):
    T = dh1.shape[0]

    def body(dh_ref, y_ref, wo_ref, dy_ref, dwo_ref):
        @pl.when(pl.program_id(0) == 0)
        def _():
            dwo_ref[...] = jnp.zeros_like(dwo_ref)

        dhb = dh_ref[...].astype(BF16)
        dy_ref[...] = _nt(dhb, wo_ref[...])
        dwo_ref[...] += _tn(y_ref[...], dhb)

    return tiled_call(
        "bwd_out", T // TT, body, (dh1, y, wo_g),
        in_specs=[pl.BlockSpec((TT, D), lambda i: (i, 0)), pl.BlockSpec((TT, D), lambda i: (i, 0)),
                  _resident((D, D), lambda i: (0, 0))],
        out_specs=[pl.BlockSpec((TT, D), lambda i: (i, 0)), pl.BlockSpec((D, D), lambda i: (0, 0))],
        out_shape=[jax.ShapeDtypeStruct((T, D), F32), jax.ShapeDtypeStruct((D, D), F32)], job=job)


GA_PS, GA_PB, GA_CB, GA_BR, GA_BI, GA_LAM, GA_CW, GA_GNP, GA_GNR = 0, 1, 2, 3, 4, 5, 6, 10, 11


def bwd_mixer(dy, proj, hs, aux, vp, pw, br, bi, job=None):
    T = proj.shape[0]
    tt = TT
    nt = T // tt
    per = tt // HALO

    def body(dy_ref, p_ref, ph_ref, hs_ref, hsh_ref, aux_ref, vp_ref, pw_ref, br_ref, bi_ref,
             dp_ref, ga_ref, dpw_ref, dwr_ref, dwi_ref, lam_ref, q_ref, dxc_ref):
        s = pl.program_id(0)
        ti = nt - 1 - s

        @pl.when(s == 0)
        def _():
            for ref in (ga_ref, dpw_ref, dwr_ref, dwi_ref, lam_ref, q_ref, dxc_ref):
                ref[...] = jnp.zeros_like(ref)

        dyv = dy_ref[...]
        lam_in, q_in, dxc_in = lam_ref[0:1, :], q_ref[...], dxc_ref[...]
        first = ti > 0
        vp_v = vp_ref[...]
        ur = jnp.concatenate([jnp.where(first, ph_ref[:, DP:DP + DR], 0.0), p_ref[:, DP:DP + DR]], axis=0)
        tf = (ti * tt + lax.broadcasted_iota(jnp.int32, (tt, 1), 0)).astype(F32) + 1.0
        mapped = [aux_ref[:, 512 * AUX_MAPPED + 128 * g:512 * AUX_MAPPED + 128 * (g + 1)] for g in range(4)]
        ps_row = _row(vp_v, VP_PS)
        xc = aux_ref[:, _aux(AUX_XC)]
        m = dict(pooled=[aux_ref[:, 512 * AUX_POOLED + 128 * g:512 * AUX_POOLED + 128 * (g + 1)] for g in range(4)],
                 mapped=mapped, ypool=[mapped[g] * ps_row[:, 128 * g:128 * (g + 1)] for g in range(4)],
                 inv_cnt=[1.0 / jnp.minimum(tf, float(2 << g)) for g in range(4)],
                 taps=[_down(ur, 3)[HALO:], _down(ur, 2)[HALO:], _down(ur, 1)[HALO:], ur[HALO:]],
                 xc=xc, xcb=xc.astype(BF16), r=aux_ref[:, _aux(AUX_R)], ig=aux_ref[:, _aux(AUX_IG)],
                 a=aux_ref[:, _aux(AUX_A)], mult=aux_ref[:, _aux(AUX_MULT)], gl=aux_ref[:, _aux(AUX_GL)],
                 gt=aux_ref[:, _aux(AUX_GT)], gate=p_ref[:, DP + DR:], sp=_softplus(-_row(vp_v, VP_LAM)))
        hs_v = hs_ref[...]
        hprev = _down(jnp.concatenate([jnp.where(first, hsh_ref[...], 0.0), hs_v], axis=0), 1)[HALO:]

        def acc(rw, v):
            ga_ref[rw:rw + 1, :] += jnp.sum(v, axis=0, keepdims=True)

        gnp = _row(vp_v, VP_GNP)
        ps = _row(vp_v, VP_PS)
        ssq = sum(jnp.sum(yp * yp, axis=-1, keepdims=True) for yp in m["ypool"])
        rp = lax.rsqrt(ssq * (1.0 / DP) + EPS)
        npool = [yp * rp for yp in m["ypool"]]
        dnp = [dyv[:, 128 * g:128 * (g + 1)] * gnp[:, 128 * g:128 * (g + 1)] for g in range(4)]
        mean_dn = sum(jnp.sum(dnp[g] * npool[g], axis=-1, keepdims=True) for g in range(4)) * (1.0 / DP)
        for g in range(4):
            lanes = slice(128 * g, 128 * (g + 1))
            ga_ref[GA_GNP:GA_GNP + 1, lanes] += jnp.sum(dyv[:, lanes] * npool[g], axis=0, keepdims=True)
            dyp = rp * (dnp[g] - npool[g] * mean_dn)
            ga_ref[GA_PS:GA_PS + 1, lanes] += jnp.sum(dyp * m["mapped"][g], axis=0, keepdims=True)
            dmap = dyp * ps[:, lanes]
            ga_ref[GA_PB:GA_PB + 1, lanes] += jnp.sum(dmap, axis=0, keepdims=True)
            dmb = dmap.astype(BF16)
            dpw_ref[g] += _tn(m["pooled"][g].astype(BF16), dmb)
            dpool = _nt(dmb, pw_ref[g])
            qv = dpool * m["inv_cnt"][g]
            win = jnp.concatenate([qv, q_in[:, lanes]], axis=0)
            for j in range(g + 1):
                win = win + _up(win, 1 << j)
            q_ref[:, lanes] = qv[:HALO]
            dp_ref[:, lanes] = (win[:tt] - dpool).astype(BF16)

        gnr = _row(vp_v, VP_GNR)
        yr = hs_v * m["gl"]
        rr = _rms(yr)
        nr = yr * rr
        dyr_out = dyv[:, DP:]
        acc(GA_GNR, dyr_out * nr)
        dyr = _rms_bwd(nr, rr, dyr_out * gnr)
        gate, gt = m["gate"], m["gt"]
        dgl = 0.5 * (1.0 + gt) + 0.5 * gate * (1.0 - gt * gt) * (GELU_C * (1.0 + 3.0 * GELU_K * gate * gate))
        dp_ref[:, DP + DR:] = (dyr * hs_v * dgl).astype(BF16)
        dhs = dyr * m["gl"]
        a = m["a"]
        row = lax.broadcasted_iota(jnp.int32, (tt, 1), 0)
        c_next = jnp.where(row < tt - 1, _up(a, 1), 1.0)
        lam = _scan_rev(c_next, dhs, lam_in)
        lam_ref[0:1, :] = a[0:1, :] * lam[0:1, :]
        xc, ig, r, mult = m["xc"], m["ig"], m["r"], m["mult"]
        dmult = lam * ig * xc
        dig = lam * mult * xc
        dxc = lam * mult * ig
        dla = lam * hprev * a - dmult * (a * a) / mult
        acc(GA_LAM, dla * r)
        dzr = (dla * (-8.0 * m["sp"])) * (r * (1.0 - r))
        dzi = dig * (ig * (1.0 - ig))
        acc(GA_BR, dzr)
        acc(GA_BI, dzi)
        dzrb, dzib = dzr.astype(BF16), dzi.astype(BF16)
        xcb = m["xcb"]
        halves = []
        for k in range(2):
            lanes = slice(256 * k, 256 * (k + 1))
            dwr_ref[k] += _tn(xcb[:, lanes], dzrb[:, lanes])
            dwi_ref[k] += _tn(xcb[:, lanes], dzib[:, lanes])
            halves.append(_nt(dzrb[:, lanes], br_ref[k]) + _nt(dzib[:, lanes], bi_ref[k]))
        dxc = dxc + jnp.concatenate(halves, axis=1)
        acc(GA_CB, dxc)
        for k in range(4):
            acc(GA_CW + k, dxc * m["taps"][k])
        dxe = jnp.concatenate([dxc, dxc_in], axis=0)
        du = (_up(dxe, 3)[:tt] * _row(vp_v, VP_CW) + _up(dxe, 2)[:tt] * _row(vp_v, VP_CW + 1)
              + _up(dxe, 1)[:tt] * _row(vp_v, VP_CW + 2) + dxc * _row(vp_v, VP_CW + 3))
        dxc_ref[...] = dxc[:HALO]
        dp_ref[:, DP:DP + DR] = du.astype(BF16)

        @pl.when(s == nt - 1)
        def _():
            lamp = _row(vp_v, VP_LAM)
            ga_ref[GA_LAM:GA_LAM + 1, :] = ga_ref[GA_LAM:GA_LAM + 1, :] * (8.0 * jax.nn.sigmoid(-lamp))

    rev = lambda i: (nt - 1 - i, 0)
    rev_halo = lambda i: (_halo_index(nt - 1 - i, per), 0)
    return tiled_call(
        "bwd_mixer", nt, body, (dy, proj, proj, hs, hs, aux, vp, pw, br, bi),
        in_specs=[pl.BlockSpec((tt, D), rev), pl.BlockSpec((tt, DIN), rev), pl.BlockSpec((HALO, DIN), rev_halo),
                  pl.BlockSpec((tt, DR), rev), pl.BlockSpec((HALO, DR), rev_halo), pl.BlockSpec((tt, AUX_W), rev),
                  _resident((16, 512), lambda i: (0, 0)), _resident((4, 128, 128), lambda i: (0, 0, 0)),
                  _resident((2, 256, 256), lambda i: (0, 0, 0)), _resident((2, 256, 256), lambda i: (0, 0, 0))],
        out_specs=[pl.BlockSpec((tt, DIN), rev), pl.BlockSpec((16, 512), lambda i: (0, 0)),
                   pl.BlockSpec((4, 128, 128), lambda i: (0, 0, 0)), pl.BlockSpec((2, 256, 256), lambda i: (0, 0, 0)),
                   pl.BlockSpec((2, 256, 256), lambda i: (0, 0, 0))],
        out_shape=[jax.ShapeDtypeStruct((T, DIN), BF16), jax.ShapeDtypeStruct((16, 512), F32),
                   jax.ShapeDtypeStruct((4, 128, 128), F32), jax.ShapeDtypeStruct((2, 256, 256), F32),
                   jax.ShapeDtypeStruct((2, 256, 256), F32)],
        scratch=[pltpu.VMEM((8, DR), F32), pltpu.VMEM((HALO, DP), F32), pltpu.VMEM((HALO, DR), F32)], job=job)


def bwd_in(dproj, h, g1, w_g, dh1, job=None):
    T = h.shape[0]

    def body(dp_ref, h_ref, g_ref, w_ref, dh1_ref, dh_ref, dw_ref, dg_ref, wfull_ref, acc_ref):
        i = pl.program_id(0)

        @pl.when(i == 0)
        def _():
            _chip_slabs_to_columns(w_ref, wfull_ref)
            acc_ref[...] = jnp.zeros_like(acc_ref)
            dg_ref[...] = jnp.zeros_like(dg_ref)

        hv = h_ref[...]
        r = _rms(hv)
        n = hv * r
        gv = g_ref[...]
        xn = (n * gv).astype(BF16)
        dpv = dp_ref[...]
        dxn = _nt(dpv, wfull_ref[...])
        acc_ref[...] += _tn(xn, dpv)
        dg_ref[...] += jnp.sum(dxn * n, axis=0, keepdims=True)
        dh_ref[...] = dh1_ref[...] + _rms_bwd(n, r, dxn * gv)

        @pl.when(i == T // TT - 1)
        def _():
            for q in range(NQ):
                dw_ref[q] = acc_ref[:, WIN_S * q:WIN_S * (q + 1)]

    return tiled_call(
        "bwd_in", T // TT, body, (dproj, h, g1, w_g, dh1),
        in_specs=[pl.BlockSpec((TT, DIN), lambda i: (i, 0)), pl.BlockSpec((TT, D), lambda i: (i, 0)),
                  _resident((1, D), lambda i: (0, 0)), _resident((NQ, D, WIN_S), lambda i: (0, 0, 0)),
                  pl.BlockSpec((TT, D), lambda i: (i, 0))],
        out_specs=[pl.BlockSpec((TT, D), lambda i: (i, 0)), pl.BlockSpec((NQ, D, WIN_S), lambda i: (0, 0, 0)),
                   pl.BlockSpec((1, D), lambda i: (0, 0))],
        out_shape=[jax.ShapeDtypeStruct((T, D), F32), jax.ShapeDtypeStruct((NQ, D, WIN_S), F32),
                   jax.ShapeDtypeStruct((1, D), F32)],
        scratch=[pltpu.VMEM((D, DIN), BF16), pltpu.VMEM((D, DIN), F32)], job=job)


def _row_block(rows, cols, itemsize=4, budget=2 * 1024 * 1024):
    best = None
    for b in range(16, rows + 1, 16):
        if rows % b == 0 and b * cols * itemsize <= budget:
            best = b
    return best if best is not None else rows


def add_pairs(full, got, dtype):
    core = lax.axis_index("c").astype(jnp.int32).reshape(1)
    outs = []
    for k in range(len(full)):
        q, hr, c = got[k].shape
        rb = _row_block(hr, c)
        nb = hr // rb

        def body(c_ref, a_ref, b_ref, o_ref):
            o_ref[...] = (a_ref[...] + b_ref[...]).astype(dtype)

        outs.append(pl.pallas_call(
            body, name="add_pairs",
            grid_spec=pltpu.PrefetchScalarGridSpec(
                num_scalar_prefetch=1, grid=(q, nb),
                in_specs=[pl.BlockSpec((None, rb, c), lambda qi, i, c_ref, nb=nb: (qi, c_ref[0] * nb + i, 0)),
                          pl.BlockSpec((None, rb, c), lambda qi, i, c_ref: (qi, i, 0))],
                out_specs=pl.BlockSpec((None, rb, c), lambda qi, i, c_ref: (qi, i, 0))),
            out_shape=jax.ShapeDtypeStruct((q, hr, c), dtype),
            compiler_params=_params(("arbitrary", "arbitrary")),
        )(core, full[k], got[k]))
    return outs


def sum_chips(parts):
    core = lax.axis_index("c").astype(jnp.int32).reshape(1)
    outs = []
    for p in parts:
        _, hr, c = p.shape
        rb = _row_block(hr, c)
        nb = hr // rb

        def body(c_ref, p_ref, o_ref):
            s = p_ref[0].astype(F32) + p_ref[1].astype(F32)
            s = s + p_ref[2].astype(F32)
            o_ref[...] = s + p_ref[3].astype(F32)

        outs.append(pl.pallas_call(
            body, name="sum_chips",
            grid_spec=pltpu.PrefetchScalarGridSpec(
                num_scalar_prefetch=1, grid=(nb,),
                in_specs=[pl.BlockSpec((NQ, rb, c), lambda i, c_ref: (0, i, 0))],
                out_specs=pl.BlockSpec((rb, c), lambda i, c_ref, nb=nb: (c_ref[0] * nb + i, 0))),
            out_shape=jax.ShapeDtypeStruct((2 * hr, c), F32),
            compiler_params=_params(("arbitrary",)),
        )(core, p))
    return outs


def adamw(w, g, m, v, job=None):
    r, c = w.shape
    rb = _row_block(r, c, budget=1024 * 1024)
    c1 = 1.0 / (1.0 - ADAM_B1 ** ADAM_STEP)
    c2 = 1.0 / (1.0 - ADAM_B2 ** ADAM_STEP)

    def body(w_ref, g_ref, m_ref, v_ref, d_ref, nm_ref, nv_ref):
        gv = g_ref[...]
        nm = ADAM_B1 * m_ref[...] + (1.0 - ADAM_B1) * gv
        nv = ADAM_B2 * v_ref[...] + (1.0 - ADAM_B2) * (gv * gv)
        nm_ref[...] = nm
        nv_ref[...] = nv
        d_ref[...] = -ADAM_LR * ((nm * c1) / (jnp.sqrt(nv * c2) + ADAM_EPS) + ADAM_WD * w_ref[...])

    spec = pl.BlockSpec((rb, c), lambda i: (i, 0))
    return tiled_call("adamw", r // rb, body, (w, g, m, v), in_specs=[spec] * 4, out_specs=[spec] * 3,
                      out_shape=[jax.ShapeDtypeStruct((r, c), F32)] * 3, job=job)


def _place():
    return lax.axis_index("x"), lax.axis_index("y"), lax.axis_index("c")


def _other_chips(x, y):
    return [(1 - x, y), (x, 1 - y), (1 - x, 1 - y)]


LOCAL_CHUNKS = 4
ICI_CHUNKS = 2
FWD_CHUNKS = 8


def _start_remote(src_rows, dst_rows, rows, chunks, send_sem, recv_sem, dev):
    rc = rows // chunks
    for j in range(chunks):
        pltpu.make_async_remote_copy(src_rows(j * rc, rc), dst_rows(j * rc, rc), send_sem, recv_sem,
                                     device_id=dev, device_id_type=MESH).start()


def _waiter(src, dst, send_sem, recv_sem):
    x, y, c = _place()
    return pltpu.make_async_remote_copy(src, dst, send_sem, recv_sem, device_id=(x, y, c), device_id_type=MESH)


class GatherJob:
    def __init__(self, shards, layer):
        self.shards, self.layer, self.n = list(shards), layer, len(shards)
        self.operands = list(shards)
        self.out_shape = [jax.ShapeDtypeStruct((NQ,) + s.shape[1:], s.dtype) for s in shards]
        sems = pltpu.SemaphoreType.DMA((self.n, 3))
        self.scratch = [sems, sems, sems, sems, pltpu.SemaphoreType.DMA((self.n,))]

    def _half(self, k):
        return self.shards[k].shape[1] // 2

    def _src(self, ins, k, half):
        hr = self._half(k)
        return lambda r0, nr: ins[k].at[self.layer, pl.ds(half * hr + r0, nr), :]

    def _dst(self, outs, k, chip, half):
        hr = self._half(k)
        return lambda r0, nr: outs[k].at[2 * chip[0] + chip[1], pl.ds(half * hr + r0, nr), :]

    def start(self, ins, outs, scr):
        send, recv, fsend, frecv, lsem = scr
        x, y, c = _place()
        for k in range(self.n):
            rows = self.shards[k].shape[1]
            rc = rows // LOCAL_CHUNKS
            for j in range(LOCAL_CHUNKS):
                pltpu.make_async_copy(ins[k].at[self.layer, pl.ds(j * rc, rc), :],
                                      outs[k].at[2 * x + y, pl.ds(j * rc, rc), :], lsem.at[k]).start()
        for k in range(self.n):
            for j, chip in enumerate(_other_chips(x, y)):
                _start_remote(self._src(ins, k, c), self._dst(outs, k, (x, y), c), self._half(k), ICI_CHUNKS,
                              send.at[k, j], recv.at[k, j], (chip[0], chip[1], c))

    def mid(self, ins, outs, scr):
        send, recv, fsend, frecv, lsem = scr
        x, y, c = _place()
        for k in range(self.n):
            hr = self._half(k)
            for j, chip in enumerate(_other_chips(x, y)):
                got = self._dst(outs, k, chip, c)
                _waiter(got(0, hr), got(0, hr), send.at[k, j], recv.at[k, j]).wait_recv()
                _start_remote(got, got, hr, FWD_CHUNKS, fsend.at[k, j], frecv.at[k, j], (x, y, 1 - c))

    def finish(self, ins, outs, scr):
        send, recv, fsend, frecv, lsem = scr
        x, y, c = _place()
        for k in range(self.n):
            hr = self._half(k)
            for j, chip in enumerate(_other_chips(x, y)):
                theirs = self._dst(outs, k, chip, 1 - c)(0, hr)
                w = _waiter(theirs, theirs, fsend.at[k, j], frecv.at[k, j])
                w.wait_recv()
                w.wait_send()
                _waiter(theirs, theirs, send.at[k, j], recv.at[k, j]).wait_send()
            pltpu.make_async_copy(ins[k].at[self.layer], outs[k].at[2 * x + y], lsem.at[k]).wait()


class ExchangeJob:
    def __init__(self, arrs, scatter):
        self.arrs, self.scatter, self.n = list(arrs), list(scatter), len(arrs)
        self.operands = list(arrs)
        self.out_shape = [jax.ShapeDtypeStruct((NQ,) + a.shape[1:], a.dtype) for a in arrs]
        sems = pltpu.SemaphoreType.DMA((self.n, 3))
        self.scratch = [sems, sems, pltpu.SemaphoreType.DMA((self.n,))]

    def _slot(self, ref, s):
        return lambda r0, nr: ref.at[s, pl.ds(r0, nr), :]

    def start(self, ins, outs, scr):
        send, recv, lsem = scr
        x, y, c = _place()
        p = 2 * x + y
        for k in range(self.n):
            rows = self.arrs[k].shape[1]
            rc = rows // ICI_CHUNKS
            for j in range(ICI_CHUNKS):
                pltpu.make_async_copy(ins[k].at[p if self.scatter[k] else 0, pl.ds(j * rc, rc), :],
                                      outs[k].at[p, pl.ds(j * rc, rc), :], lsem.at[k]).start()
            for j, chip in enumerate(_other_chips(x, y)):
                q = 2 * chip[0] + chip[1]
                _start_remote(self._slot(ins[k], q if self.scatter[k] else 0), self._slot(outs[k], p), rows, ICI_CHUNKS,
                              send.at[k, j], recv.at[k, j], (chip[0], chip[1], c))

    def mid(self, ins, outs, scr):
        pass

    def finish(self, ins, outs, scr):
        send, recv, lsem = scr
        x, y, c = _place()
        for k in range(self.n):
            for j, chip in enumerate(_other_chips(x, y)):
                slot = outs[k].at[2 * chip[0] + chip[1]]
                w = _waiter(slot, slot, send.at[k, j], recv.at[k, j])
                w.wait_recv()
                w.wait_send()
            pltpu.make_async_copy(ins[k].at[0], outs[k].at[0], lsem.at[k]).wait()


def run_job(job, name):
    n_in, n_out = len(job.operands), len(job.out_shape)

    def body(*refs):
        ins, outs, scr = refs[:n_in], refs[n_in:n_in + n_out], refs[n_in + n_out:]
        job.start(ins, outs, scr)
        job.mid(ins, outs, scr)
        job.finish(ins, outs, scr)

    return pl.pallas_call(body, name=name, in_specs=[ANY] * n_in, out_specs=[ANY] * n_out, out_shape=job.out_shape,
                          input_output_aliases=dict(getattr(job, "aliases", {})),
                          scratch_shapes=job.scratch)(*job.operands)


def tiled_call(name, steps, body, args, in_specs, out_specs, out_shape, scratch=(), job=None, mid_step=None):
    if job is None:
        return pl.pallas_call(body, name=name, grid=(steps,), in_specs=in_specs, out_specs=out_specs, out_shape=out_shape,
                              scratch_shapes=list(scratch), compiler_params=_params(("arbitrary",)))(*args), []
    n_in, n_out, n_scr = len(args), len(out_shape), len(scratch)
    j_in, j_out = len(job.operands), len(job.out_shape)
    mid_step = steps // 2 if mid_step is None else mid_step

    def carried(*refs):
        a, ji = refs[:n_in], refs[n_in:n_in + j_in]
        o = refs[n_in + j_in:n_in + j_in + n_out]
        jo = refs[n_in + j_in + n_out:n_in + j_in + n_out + j_out]
        rest = refs[n_in + j_in + n_out + j_out:]
        sc, js = rest[:n_scr], rest[n_scr:]
        i = pl.program_id(0)

        @pl.when(i == 0)
        def _():
            job.start(ji, jo, js)

        body(*a, *o, *sc)

        @pl.when(i == mid_step)
        def _():
            job.mid(ji, jo, js)

        @pl.when(i == steps - 1)
        def _():
            job.finish(ji, jo, js)

    res = pl.pallas_call(
        carried, name=name, grid=(steps,), in_specs=list(in_specs) + [ANY] * j_in, out_specs=list(out_specs) + [ANY] * j_out,
        out_shape=list(out_shape) + list(job.out_shape), scratch_shapes=list(scratch) + list(job.scratch),
        input_output_aliases={n_in + i: n_out + o for i, o in getattr(job, "aliases", {}).items()},
        compiler_params=_params(("arbitrary",)))(*args, *job.operands)
    return res[:n_out], res[n_out:]


D2D_CHUNKS = 4


class SwapJob:
    def __init__(self, arrs):
        self.arrs, self.n = list(arrs), len(arrs)
        self.operands = list(arrs)
        self.out_shape = [jax.ShapeDtypeStruct((a.shape[0], a.shape[1] // 2, a.shape[2]), a.dtype) for a in arrs]
        self.scratch = [pltpu.SemaphoreType.DMA((self.n,)), pltpu.SemaphoreType.DMA((self.n,))]

    def start(self, ins, got, scr):
        send, recv = scr
        x, y, c = _place()
        for k in range(self.n):
            q, r, _ = self.arrs[k].shape
            hr = r // 2
            for qi in range(q):
                _start_remote(lambda r0, nr: ins[k].at[qi, pl.ds((1 - c) * hr + r0, nr), :],
                              lambda r0, nr: got[k].at[qi, pl.ds(r0, nr), :], hr, D2D_CHUNKS,
                              send.at[k], recv.at[k], (x, y, 1 - c))

    def mid(self, ins, got, scr):
        pass

    def finish(self, ins, got, scr):
        send, recv = scr
        for k in range(self.n):
            hr = self.arrs[k].shape[1] // 2
            w = _waiter(ins[k].at[:, pl.ds(0, hr), :], got[k], send.at[k], recv.at[k])
            w.wait_send()
            w.wait_recv()


def swap_halves(arrs):
    return run_job(SwapJob(arrs), "swap_halves")


JOIN_CHUNKS = 8


class JoinJob:
    def __init__(self, arrs):
        self.arrs, self.n = list(arrs), len(arrs)
        self.operands = list(arrs)
        self.out_shape = [jax.ShapeDtypeStruct(a.shape, a.dtype) for a in arrs]
        self.scratch = [pltpu.SemaphoreType.DMA((self.n,)), pltpu.SemaphoreType.DMA((self.n,))]
        self.aliases = {k: k for k in range(self.n)}

    def start(self, ins, outs, scr):
        send, recv = scr
        x, y, c = _place()
        for k in range(self.n):
            hr = self.arrs[k].shape[0] // 2
            rows = lambda r0, nr: outs[k].at[pl.ds(c * hr + r0, nr), :]
            _start_remote(rows, rows, hr, JOIN_CHUNKS, send.at[k], recv.at[k], (x, y, 1 - c))

    def mid(self, ins, outs, scr):
        pass

    def finish(self, ins, outs, scr):
        send, recv = scr
        x, y, c = _place()
        for k in range(self.n):
            hr = self.arrs[k].shape[0] // 2
            w = _waiter(outs[k].at[pl.ds(c * hr, hr), :], outs[k].at[pl.ds((1 - c) * hr, hr), :], send.at[k], recv.at[k])
            w.wait_send()
            w.wait_recv()


def join_halves(arrs):
    return run_job(JoinJob(arrs), "join_halves")


def _block_diag(w):
    eye = jnp.eye(4, dtype=F32)[None, :, None, :, None]
    return (w.reshape(-1, 4, 64, 1, 64) * eye).reshape(-1, 256, 256).astype(BF16)


def _diag_blocks(b):
    eye = jnp.eye(4, dtype=F32)[None, :, None, :, None]
    return (b.reshape(2, 4, 64, 4, 64) * eye).sum(axis=3).reshape(8, 64, 64)


def _vec_params(pool_b, pool_scale, conv_b, gate_r_b, gate_i_b, lru_lambda, conv_w_full, group_norm_g):
    rows = [pool_b, pool_scale, conv_b, gate_r_b, gate_i_b, lru_lambda,
            conv_w_full[:, 0], conv_w_full[:, 1], conv_w_full[:, 2], conv_w_full[:, 3],
            group_norm_g[:, :DP], group_norm_g[:, DP:]]
    return jnp.pad(jnp.stack(rows, axis=1), ((0, 0), (0, 4), (0, 0)))


SMALL_ROWS = 1280


def _pack_small(layers, final_g, meta):
    def tile_rows(v):
        return jnp.pad(v.reshape(2, 512), ((0, 6), (0, 0)))

    rows = []
    for vec, pw, wr, wi, g1, g2 in layers:
        rows += [vec, pw.reshape(128, 512), wr.reshape(64, 512), wi.reshape(64, 512), tile_rows(g1), tile_rows(g2)]
    rows += [tile_rows(final_g), meta.reshape(32, 512)]
    used = sum(r.shape[0] for r in rows)
    return jnp.concatenate(rows + [jnp.zeros((SMALL_ROWS - used, 512), F32)], axis=0)


def _unpack_small(flat):
    layers, o = [], 0
    for _ in range(DEPTH):
        vec = flat[o:o + 16]; o += 16
        pw = flat[o:o + 128].reshape(4, 128, 128); o += 128
        wr = flat[o:o + 64].reshape(8, 64, 64); o += 64
        wi = flat[o:o + 64].reshape(8, 64, 64); o += 64
        g1 = flat[o:o + 2].reshape(1024); o += 8
        g2 = flat[o:o + 2].reshape(1024); o += 8
        layers.append((vec, pw, wr, wi, g1, g2))
    final_g = flat[o:o + 2].reshape(1024); o += 8
    meta = flat[o:o + 32].reshape(16, 1024)
    return layers, final_g, meta


def local_step(x2d, tgt2d, meta_full, conv_w_full, sp, shards=None, win0=None, gathered=None):
    exchange = gathered is None
    if exchange:
        gathered = [None] * DEPTH
    h = None
    saved = []
    vp_all = _vec_params(sp["pool_b"], sp["pool_scale"], sp["conv_b"], sp["gate_r_b"], sp["gate_i_b"], sp["lru_lambda"],
                         conv_w_full, sp["group_norm_g"])
    pw_all = sp["pool_w"].astype(BF16)
    br_all = _block_diag(sp["gate_r_w"].reshape(DEPTH * 8, 64, 64)).reshape(DEPTH, 2, 256, 256)
    bi_all = _block_diag(sp["gate_i_w"].reshape(DEPTH * 8, 64, 64)).reshape(DEPTH, 2, 256, 256)
    for l in range(DEPTH):
        vp, pw, br, bi = vp_all[l], pw_all[l], br_all[l], bi_all[l]
        if l == 0:
            job = GatherJob(shards[1:], 0) if exchange else None
            win = win0 if exchange else gathered[0][0]
            (proj, y, hs, aux, h), rest = fwd_mix(x2d, sp["mix_norm_g"][l][None], win, vp, pw, br, bi, job, meta=meta_full)
            wo, wu, wd = rest if exchange else gathered[0][1:]
        else:
            win, wo, wu, wd = gathered[l]
            (proj, y, hs, aux), _ = fwd_mix(h, sp["mix_norm_g"][l][None], win, vp, pw, br, bi)
        wo = wo.reshape(D, D)
        if l + 1 < DEPTH:
            job = GatherJob(shards, l + 1) if exchange else None
            (h1, a, h_next), fetched = fwd_post(h, y, wo, sp["mlp_norm_g"][l][None], wu, wd, job)
            if exchange:
                gathered[l + 1] = fetched
        else:
            (h1, a, dh, dgf, loss_part), _ = fwd_post(h, y, wo, sp["mlp_norm_g"][l][None], wu, wd,
                                                      loss=(sp["final_norm_g"][None], tgt2d))
            h_next = None
        saved.append((h, proj, y, hs, aux, h1, a, vp, pw, br, bi, win, wo, wu, wd))
        h = h_next

    big = [[None, None, None, None] for _ in range(DEPTH)]
    io_raw, io_pairs = [], None
    small_layers = [None] * DEPTH
    for l in reversed(range(DEPTH)):
        h0, proj, y, hs, aux, h1, a, vp, pw, br, bi, win, wo, wu, wd = saved[l]
        g1 = sp["mix_norm_g"][l][None]
        job = ExchangeJob(io_pairs, [True] * 2) if io_pairs is not None else None
        (dh1, da, xn2, dg2), io_parts = bwd_mlp_dx(dh, h1, a, sp["mlp_norm_g"][l][None], wu, wd, job)
        io_sums = sum_chips(io_parts) if job is not None else []
        dwu, dwd = bwd_mlp_dw(xn2, da, a, dh)
        if not exchange:
            (dy, dwo), _ = bwd_out(dh1, y, wo)
            (dproj, ga, dpw, dwr, dwi), _ = bwd_mixer(dy, proj, hs, aux, vp, pw, br, bi)
            (dh, dwin, dg1), _ = bwd_in(dproj, h0, g1, win, dh1)
            big[l] = [dwin, dwo.reshape(NQ, D // NQ, D), dwu, dwd]
        else:
            arrs = [dwu, dwd] + io_raw
            (dy, dwo), got = bwd_out(dh1, y, wo, SwapJob(arrs))
            pairs = add_pairs(arrs, got, BF16)
            crossing = pairs if l == 0 else pairs[:2]
            io_pairs = None if l == 0 or not io_raw else pairs[2:]
            (dproj, ga, dpw, dwr, dwi), parts = bwd_mixer(dy, proj, hs, aux, vp, pw, br, bi,
                                                          ExchangeJob(crossing, [True] * len(crossing)))
            (dh, dwin, dg1), sums = bwd_in(dproj, h0, g1, win, dh1, JoinJob(list(sum_chips(parts)) + list(io_sums)))
            big[l][2:] = sums[:2]
            if l == 0:
                big[1][:2] = sums[2:4]
            if io_sums:
                big[l + 2][:2] = sums[-2:]
            io_raw = [dwin, dwo.reshape(NQ, D // NQ, D)]
        small_layers[l] = (ga, dpw, _diag_blocks(dwr), _diag_blocks(dwi), dg1[0], dg2[0])
    return loss_part, dh, big, small_layers, dgf, io_raw


def kernel(x, meta_tokens, mix_norm_g, w_in, pool_w, pool_b, pool_scale, conv_w, conv_b, gate_r_w, gate_r_b, gate_i_w, gate_i_b, lru_lambda, group_norm_g, w_out, mlp_norm_g, w_up, w_down, final_norm_g, loss_target, m_meta_tokens, m_mix_norm_g, m_w_in, m_pool_w, m_pool_b, m_pool_scale, m_conv_w, m_conv_b, m_gate_r_w, m_gate_r_b, m_gate_i_w, m_gate_i_b, m_lru_lambda, m_group_norm_g, m_w_out, m_mlp_norm_g, m_w_up, m_w_down, m_final_norm_g, v_meta_tokens, v_mix_norm_g, v_w_in, v_pool_w, v_pool_b, v_pool_scale, v_conv_w, v_conv_b, v_gate_r_w, v_gate_r_b, v_gate_i_w, v_gate_i_b, v_lru_lambda, v_group_norm_g, v_w_out, v_mlp_norm_g, v_w_up, v_w_down, v_final_norm_g):
    p = 2 * lax.axis_index("x") + lax.axis_index("y")

    shards = [w_in.astype(BF16), w_out.astype(BF16), w_up.astype(BF16), w_down.astype(BF16)]
    small = jnp.concatenate([meta_tokens, jnp.pad(conv_w.reshape(16, 128), ((0, 0), (0, 128)))], axis=0)
    small = jnp.pad(small, ((0, 96), (0, 0)))[None]
    win0, small_g = run_job(GatherJob([shards[0], small], 0), "gather_first")
    meta_full = jnp.transpose(small_g[:, :16, :], (1, 0, 2)).reshape(NMETA, D)
    conv_w_full = jnp.transpose(small_g[:, 16:32, :128].reshape(NQ, DEPTH, 4, 128), (1, 2, 0, 3)).reshape(DEPTH, 4, DR)

    sp = dict(mix_norm_g=mix_norm_g, pool_w=pool_w, pool_b=pool_b, pool_scale=pool_scale, conv_b=conv_b, gate_r_w=gate_r_w,
              gate_r_b=gate_r_b, gate_i_w=gate_i_w, gate_i_b=gate_i_b, lru_lambda=lru_lambda, group_norm_g=group_norm_g,
              mlp_norm_g=mlp_norm_g, final_norm_g=final_norm_g)
    loss_part, dh, big, small_layers, dgf, io0_parts = local_step(x[0], loss_target[0], meta_full, conv_w_full, sp,
                                                                  shards=shards, win0=win0)
    loss = lax.psum(loss_part[0, 0], ("x", "y", "c"))
    grad_x = dh[NMETA:][None]

    weights = dict(meta_tokens=meta_tokens, mix_norm_g=mix_norm_g, w_in=w_in, pool_w=pool_w, pool_b=pool_b, pool_scale=pool_scale,
                   conv_w=conv_w, conv_b=conv_b, gate_r_w=gate_r_w, gate_r_b=gate_r_b, gate_i_w=gate_i_w, gate_i_b=gate_i_b,
                   lru_lambda=lru_lambda, group_norm_g=group_norm_g, w_out=w_out, mlp_norm_g=mlp_norm_g, w_up=w_up, w_down=w_down,
                   final_norm_g=final_norm_g)
    mom_m = dict(meta_tokens=m_meta_tokens, mix_norm_g=m_mix_norm_g, w_in=m_w_in, pool_w=m_pool_w, pool_b=m_pool_b,
                 pool_scale=m_pool_scale, conv_w=m_conv_w, conv_b=m_conv_b, gate_r_w=m_gate_r_w, gate_r_b=m_gate_r_b,
                 gate_i_w=m_gate_i_w, gate_i_b=m_gate_i_b, lru_lambda=m_lru_lambda, group_norm_g=m_group_norm_g, w_out=m_w_out,
                 mlp_norm_g=m_mlp_norm_g, w_up=m_w_up, w_down=m_w_down, final_norm_g=m_final_norm_g)
    mom_v = dict(meta_tokens=v_meta_tokens, mix_norm_g=v_mix_norm_g, w_in=v_w_in, pool_w=v_pool_w, pool_b=v_pool_b,
                 pool_scale=v_pool_scale, conv_w=v_conv_w, conv_b=v_conv_b, gate_r_w=v_gate_r_w, gate_r_b=v_gate_r_b,
                 gate_i_w=v_gate_i_w, gate_i_b=v_gate_i_b, lru_lambda=v_lru_lambda, group_norm_g=v_group_norm_g, w_out=v_w_out,
                 mlp_norm_g=v_mlp_norm_g, w_up=v_w_up, w_down=v_w_down, final_norm_g=v_final_norm_g)
    names = list(weights)
    delta, new_m, new_v = {}, {}, {}

    def adam_big(nm, g, job=None):
        shp = weights[nm].shape
        two_d = lambda t: t.reshape(shp[0] * shp[1], shp[2])
        (d_, m_, v_), got = adamw(two_d(weights[nm]), two_d(g), two_d(mom_m[nm]), two_d(mom_v[nm]), job)
        delta[nm], new_m[nm], new_v[nm] = d_.reshape(shp), m_.reshape(shp), v_.reshape(shp)
        return got

    last = [_pack_small(small_layers, dgf[0], dh[:NMETA])[None]] + io0_parts
    got = swap_halves(last)
    pairs = add_pairs(last[:1], got[:1], F32) + add_pairs(last[1:], got[1:], BF16)
    sums = join_halves(sum_chips(run_job(ExchangeJob(pairs, [False, True, True]), "exchange_last")))
    small_sum, io0 = sums[0], sums[1:]
    g_up = jnp.stack([big[l][-2] for l in range(DEPTH)])
    g_down = jnp.stack([big[l][-1] for l in range(DEPTH)])
    adam_big("w_up", g_up)
    adam_big("w_down", g_down)
    g_in = jnp.stack([io0[0]] + [big[l][0] for l in range(1, DEPTH)])
    g_out = jnp.stack([io0[1]] + [big[l][1] for l in range(1, DEPTH)])
    adam_big("w_in", g_in)
    adam_big("w_out", g_out)
    g_layers, g_final, g_meta_full = _unpack_small(small_sum)

    g_vec = [gl[0] for gl in g_layers]
    grads = dict(
        meta_tokens=lax.dynamic_slice(g_meta_full, (0, p * (D // NQ)), (NMETA, D // NQ)),
        mix_norm_g=jnp.stack([gl[4] for gl in g_layers]),
        w_in=g_in,
        pool_w=jnp.stack([gl[1] for gl in g_layers]),
        pool_b=jnp.stack([gv[GA_PB] for gv in g_vec]),
        pool_scale=jnp.stack([gv[GA_PS] for gv in g_vec]),
        conv_w=lax.dynamic_slice(jnp.stack([gv[GA_CW:GA_CW + 4] for gv in g_vec]), (0, 0, p * 128), (DEPTH, 4, 128)),
        conv_b=jnp.stack([gv[GA_CB] for gv in g_vec]),
        gate_r_w=jnp.stack([gl[2] for gl in g_layers]),
        gate_r_b=jnp.stack([gv[GA_BR] for gv in g_vec]),
        gate_i_w=jnp.stack([gl[3] for gl in g_layers]),
        gate_i_b=jnp.stack([gv[GA_BI] for gv in g_vec]),
        lru_lambda=jnp.stack([gv[GA_LAM] for gv in g_vec]),
        group_norm_g=jnp.stack([jnp.concatenate([gv[GA_GNP], gv[GA_GNR]]) for gv in g_vec]),
        w_out=g_out,
        mlp_norm_g=jnp.stack([gl[5] for gl in g_layers]),
        w_up=g_up,
        w_down=g_down,
        final_norm_g=g_final,
    )

    small_names = [nm for nm in names if nm not in ("w_in", "w_out", "w_up", "w_down")]
    sizes = [weights[nm].size for nm in small_names]
    total = sum(sizes)
    rows = -(-total // 512)
    rows = -(-rows // 16) * 16

    def flat(tree, fill):
        v_ = jnp.concatenate([tree[nm].reshape(-1) for nm in small_names])
        return jnp.concatenate([v_, jnp.full((rows * 512 - total,), fill, F32)]).reshape(rows, 512)

    (d_, m_, v_), _ = adamw(flat(weights, 0.0), flat(grads, 0.0), flat(mom_m, 0.0), flat(mom_v, 1.0))
    o = 0
    for nm, sz in zip(small_names, sizes):
        shp = weights[nm].shape
        delta[nm] = d_.reshape(-1)[o:o + sz].reshape(shp)
        new_m[nm] = m_.reshape(-1)[o:o + sz].reshape(shp)
        new_v[nm] = v_.reshape(-1)[o:o + sz].reshape(shp)
        o += sz

    return (loss, grad_x, *[grads[nm] for nm in names], *[delta[nm] for nm in names],
            *[new_m[nm] for nm in names], *[new_v[nm] for nm in names])
```

```python
import jax
import jax.numpy as jnp
from jax import lax
from jax.experimental import pallas as pl
from jax.experimental.pallas import tpu as pltpu

F32 = jnp.float32
BF16 = jnp.bfloat16

D = 1024
DP = 512
DR = 512
DIN = 1536
DFF = 4096
DEPTH = 4
NMETA = 16
NQ = 4
WIN_S = DIN // NQ
FF_S = DFF // NQ
EPS = 1e-6
HALO = 16
TT = 432
TTW = 912
VMEM_LIMIT = 56 * 1024 * 1024

ADAM_LR = 0.001
ADAM_B1 = 0.9
ADAM_B2 = 0.999
ADAM_EPS = 1e-08
ADAM_WD = 0.01
ADAM_STEP = 10

MESH = pl.DeviceIdType.MESH
ANY = pl.BlockSpec(memory_space=pl.ANY)


def _params(sem=None, vmem=VMEM_LIMIT):
    return pltpu.CompilerParams(dimension_semantics=sem, vmem_limit_bytes=vmem)


def _resident(shape, index):
    return pl.BlockSpec(shape, index, pipeline_mode=pl.Buffered(1))


def _nt(x, w):
    return lax.dot_general(x, w, (((1,), (1,)), ((), ())), preferred_element_type=F32)


def _tn(a, b):
    return lax.dot_general(a, b, (((0,), (0,)), ((), ())), preferred_element_type=F32)


def _dot(x, w):
    return jnp.dot(x, w, preferred_element_type=F32)


def _rms(h):
    return lax.rsqrt(jnp.mean(h * h, axis=-1, keepdims=True) + EPS)


def _rms_bwd(n, r, dn):
    return r * (dn - n * jnp.mean(dn * n, axis=-1, keepdims=True))


def _down(x, s):
    return pltpu.roll(x, s, 0)


def _up(x, s):
    return pltpu.roll(x, x.shape[0] - s, 0)


GELU_C = 0.7978845608028654
GELU_K = 0.044715


def _gelu(x):
    t = jnp.tanh(GELU_C * (x + GELU_K * x * x * x))
    return 0.5 * x * (1.0 + t), t


def _softplus(x):
    return jnp.maximum(x, 0.0) + jnp.log1p(jnp.exp(-jnp.abs(x)))


VP_PB, VP_PS, VP_CB, VP_BR, VP_BI, VP_LAM, VP_CW, VP_GNP, VP_GNR = 0, 1, 2, 3, 4, 5, 6, 10, 11


def _row(vp, r):
    return vp[r:r + 1, :]


def _mixer_pre(ue, t0, vp, pw_ref, br_ref, bi_ref):
    tt = ue.shape[0] - HALO
    tf = (t0 + lax.broadcasted_iota(jnp.int32, (tt, 1), 0)).astype(F32) + 1.0
    pb, ps = _row(vp, VP_PB), _row(vp, VP_PS)
    pooled, mapped, inv_cnt = [], [], []
    for g in range(4):
        lanes = slice(128 * g, 128 * (g + 1))
        xe = ue[:, lanes]
        s = xe
        for j in range(g + 1):
            s = s + _down(s, 1 << j)
        inv = 1.0 / jnp.minimum(tf, float(2 << g))
        pg = s[HALO:] * inv - xe[HALO:]
        mg = _dot(pg.astype(BF16), pw_ref[g]) + pb[:, lanes]
        pooled.append(pg)
        mapped.append(mg)
        inv_cnt.append(inv)
    ypool = [mapped[g] * ps[:, 128 * g:128 * (g + 1)] for g in range(4)]

    xe = ue[:, DP:DP + DR]
    taps = [_down(xe, 3)[HALO:], _down(xe, 2)[HALO:], _down(xe, 1)[HALO:], xe[HALO:]]
    xc = _row(vp, VP_CB) + (taps[0] * _row(vp, VP_CW) + taps[1] * _row(vp, VP_CW + 1)
                            + taps[2] * _row(vp, VP_CW + 2) + taps[3] * _row(vp, VP_CW + 3))
    xcb = xc.astype(BF16)
    zr = jnp.concatenate([_dot(xcb[:, :256], br_ref[0]), _dot(xcb[:, 256:], br_ref[1])], axis=1) + _row(vp, VP_BR)
    zi = jnp.concatenate([_dot(xcb[:, :256], bi_ref[0]), _dot(xcb[:, 256:], bi_ref[1])], axis=1) + _row(vp, VP_BI)
    r = jax.nn.sigmoid(zr)
    ig = jax.nn.sigmoid(zi)
    sp = _softplus(-_row(vp, VP_LAM))
    la = (-8.0 * r) * sp
    a = jnp.exp(la)
    th = jnp.tanh(la)
    mult = jnp.sqrt((-2.0 * th) / (1.0 - th))
    gate = ue[HALO:, DP + DR:]
    gl, gt = _gelu(gate)
    return dict(pooled=pooled, mapped=mapped, ypool=ypool, inv_cnt=inv_cnt, taps=taps, xc=xc, xcb=xcb, r=r, ig=ig,
                sp=sp, a=a, mult=mult, gate=gate, gl=gl, gt=gt)


SUBLANES = 8


def _scan_fwd(a, b, h_in):
    tt = a.shape[0]
    sub = jnp.bitwise_and(lax.broadcasted_iota(jnp.int32, (tt, 1), 0), SUBLANES - 1)
    s = 1
    while s < SUBLANES:
        m = sub >= s
        a_s = jnp.where(m, _down(a, s), 1.0)
        b_s = jnp.where(m, _down(b, s), 0.0)
        b = a * b_s + b
        a = a * a_s
        s *= 2
    groups, h = [], h_in
    for g in range(tt // SUBLANES):
        rows = slice(SUBLANES * g, SUBLANES * (g + 1))
        hg = a[rows] * h + b[rows]
        groups.append(hg)
        h = hg[SUBLANES - 1:SUBLANES, :]
    return jnp.concatenate(groups, axis=0)


def _scan_rev(c, d, l_in):
    tt = c.shape[0]
    sub = jnp.bitwise_and(lax.broadcasted_iota(jnp.int32, (tt, 1), 0), SUBLANES - 1)
    s = 1
    while s < SUBLANES:
        m = sub < SUBLANES - s
        c_s = jnp.where(m, _up(c, s), 1.0)
        d_s = jnp.where(m, _up(d, s), 0.0)
        d = c * d_s + d
        c = c * c_s
        s *= 2
    groups, l = [], l_in
    for g in reversed(range(tt // SUBLANES)):
        rows = slice(SUBLANES * g, SUBLANES * (g + 1))
        lg = c[rows] * l + d[rows]
        groups.append(lg)
        l = lg[0:1, :]
    return jnp.concatenate(groups[::-1], axis=0)


def _halo_index(i, per_tile):
    return jnp.maximum(i * per_tile - 1, 0)


def _chip_slabs_to_columns(w_ref, wfull_ref):
    for q in range(NQ):
        wfull_ref[:, WIN_S * q:WIN_S * (q + 1)] = w_ref[q]


AUX_POOLED, AUX_MAPPED, AUX_XC, AUX_R, AUX_IG, AUX_A, AUX_MULT, AUX_GL, AUX_GT = range(9)
AUX_W = 9 * 512


def _aux(k):
    return slice(512 * k, 512 * (k + 1))


def fwd_mix(h, g, w_g, vp, pw, br, bi, job=None, meta=None):
    T = h.shape[0] + (NMETA if meta is not None else 0)
    n_h = 3 if meta is not None else 1

    def body(*refs):
        g_ref, w_ref, vp_ref, pw_ref, br_ref, bi_ref, p_ref, y_ref, hs_ref, aux_ref = refs[n_h:n_h + 10]
        wfull_ref, halo_ref, carry_ref = refs[-3:]
        i = pl.program_id(0)

        @pl.when(i == 0)
        def _():
            _chip_slabs_to_columns(w_ref, wfull_ref)
            carry_ref[...] = jnp.zeros_like(carry_ref)
            halo_ref[...] = jnp.zeros_like(halo_ref)

        if meta is not None:
            xp_ref, x_ref, meta_ref = refs[:n_h]
            hh = jnp.concatenate([jnp.where(i == 0, meta_ref[...], xp_ref[...]), x_ref[0:TT - HALO, :]], axis=0)
            refs[n_h + 10][...] = hh
        else:
            hh = refs[0][...]
        xn = (hh * _rms(hh) * g_ref[...]).astype(BF16)
        proj = _dot(xn, wfull_ref[...])
        p_ref[...] = proj
        ue = jnp.concatenate([halo_ref[...], proj], axis=0)
        halo_ref[...] = p_ref[TT - HALO:TT, :]
        vp_v = vp_ref[...]
        m = _mixer_pre(ue, i * TT, vp_v, pw_ref, br_ref, bi_ref)
        for g in range(4):
            aux_ref[:, 512 * AUX_POOLED + 128 * g:512 * AUX_POOLED + 128 * (g + 1)] = m["pooled"][g]
            aux_ref[:, 512 * AUX_MAPPED + 128 * g:512 * AUX_MAPPED + 128 * (g + 1)] = m["mapped"][g]
        for k, name in ((AUX_XC, "xc"), (AUX_R, "r"), (AUX_IG, "ig"), (AUX_A, "a"), (AUX_MULT, "mult"), (AUX_GL, "gl"),
                        (AUX_GT, "gt")):
            aux_ref[:, _aux(k)] = m[name]
        b = m["mult"] * (m["ig"] * m["xc"])
        hs = _scan_fwd(m["a"], b, carry_ref[0:1, :])
        hs_ref[...] = hs
        carry_ref[0:1, :] = hs_ref[TT - 1:TT, :]
        yr = hs * m["gl"]
        ssq = sum(jnp.sum(yp * yp, axis=-1, keepdims=True) for yp in m["ypool"])
        rp = lax.rsqrt(ssq * (1.0 / DP) + EPS)
        gnp = _row(vp_v, VP_GNP)
        for g in range(4):
            lanes = slice(128 * g, 128 * (g + 1))
            y_ref[:, lanes] = (m["ypool"][g] * rp * gnp[:, lanes]).astype(BF16)
        y_ref[:, DP:] = (yr * _rms(yr) * _row(vp_v, VP_GNR)).astype(BF16)

    if meta is not None:
        h_args, h_specs = (h, h, meta), _shifted_specs() + [_resident((NMETA, D), lambda i: (0, 0))]
    else:
        h_args, h_specs = (h,), [pl.BlockSpec((TT, D), lambda i: (i, 0))]
    full = meta is not None
    return tiled_call(
        "fwd_mix", T // TT, body, h_args + (g, w_g, vp, pw, br, bi),
        in_specs=h_specs + [_resident((1, D), lambda i: (0, 0)), _resident((NQ, D, WIN_S), lambda i: (0, 0, 0)),
                            _resident((16, 512), lambda i: (0, 0)), _resident((4, 128, 128), lambda i: (0, 0, 0)),
                            _resident((2, 256, 256), lambda i: (0, 0, 0)), _resident((2, 256, 256), lambda i: (0, 0, 0))],
        out_specs=[pl.BlockSpec((TT, DIN), lambda i: (i, 0)), pl.BlockSpec((TT, D), lambda i: (i, 0)),
                   pl.BlockSpec((TT, DR), lambda i: (i, 0)), pl.BlockSpec((TT, AUX_W), lambda i: (i, 0))]
        + [pl.BlockSpec((TT, D), lambda i: (i, 0))] * full,
        out_shape=[jax.ShapeDtypeStruct((T, DIN), F32), jax.ShapeDtypeStruct((T, D), BF16),
                   jax.ShapeDtypeStruct((T, DR), F32), jax.ShapeDtypeStruct((T, AUX_W), F32)]
        + [jax.ShapeDtypeStruct((T, D), F32)] * full,
        scratch=[pltpu.VMEM((D, DIN), BF16), pltpu.VMEM((HALO, DIN), F32), pltpu.VMEM((8, DR), F32)],
        job=job, mid_step=T // TT - 1)


def fwd_post(h, y, wo_g, g2, wu_g, wd_g, job=None, loss=None):
    T = h.shape[0]

    def body(*refs):
        h_ref, y_ref, wo_ref, g_ref, wu_ref, wd_ref = refs[:6]
        h1_ref, a_ref = refs[-5:-3] if loss is not None else refs[-3:-1]
        h1 = h_ref[...] + _dot(y_ref[...], wo_ref[...])
        h1_ref[...] = h1
        xn = (h1 * _rms(h1) * g_ref[...]).astype(BF16)
        acc = h1
        for q in range(NQ):
            a = _dot(xn, wu_ref[q])
            a_ref[:, FF_S * q:FF_S * (q + 1)] = a.astype(BF16)
            ra = jnp.maximum(a, 0.0)
            acc = acc + _dot((ra * ra).astype(BF16), wd_ref[q])
        if loss is None:
            refs[-1][...] = acc
        else:
            _loss_tile(acc, *refs[6:9], *refs[-3:])

    row = pl.BlockSpec((TT, D), lambda i: (i, 0))
    in_specs = [row, row, _resident((D, D), lambda i: (0, 0)), _resident((1, D), lambda i: (0, 0)),
                _resident((NQ, D, FF_S), lambda i: (0, 0, 0)), _resident((NQ, FF_S, D), lambda i: (0, 0, 0))]
    out_specs = [row, pl.BlockSpec((TT, DFF), lambda i: (i, 0)), row]
    out_shape = [jax.ShapeDtypeStruct((T, D), F32), jax.ShapeDtypeStruct((T, DFF), BF16), jax.ShapeDtypeStruct((T, D), F32)]
    args = (h, y, wo_g, g2, wu_g, wd_g)
    if loss is not None:
        args += (loss[0], loss[1], loss[1])
        in_specs += [_resident((1, D), lambda i: (0, 0))] + _shifted_specs()
        out_specs += [pl.BlockSpec((1, D), lambda i: (0, 0)), pl.BlockSpec((1, 1), lambda i: (0, 0))]
        out_shape += [jax.ShapeDtypeStruct((1, D), F32), jax.ShapeDtypeStruct((1, 1), F32)]
    return tiled_call("fwd_post", T // TT, body, args, in_specs=in_specs, out_specs=out_specs, out_shape=out_shape,
                      job=job, mid_step=(T // TT) * 3 // 4)


def _shifted_tile(prev_ref, cur_ref):
    return jnp.concatenate([prev_ref[...], cur_ref[0:TT - HALO, :]], axis=0)


def _shifted_specs():
    per = TT // HALO
    return [pl.BlockSpec((HALO, D), lambda i: (_halo_index(i, per), 0)), pl.BlockSpec((TT, D), lambda i: (i, 0))]


def _loss_tile(hh, g_ref, tp_ref, t_ref, dh_ref, dg_ref, loss_ref):
    i = pl.program_id(0)

    @pl.when(i == 0)
    def _():
        dg_ref[...] = jnp.zeros_like(dg_ref)
        loss_ref[...] = jnp.zeros_like(loss_ref)

    r = _rms(hh)
    n = hh * r
    gfv = g_ref[...]
    row = i * TT + lax.broadcasted_iota(jnp.int32, (TT, 1), 0)
    e = jnp.where(row >= NMETA, n * gfv - _shifted_tile(tp_ref, t_ref), 0.0)
    loss_ref[...] += 0.5 * jnp.sum(jnp.sum(e * e, axis=-1, keepdims=True) * (1.0 / D), axis=0, keepdims=True)
    dy = e * (1.0 / D)
    dg_ref[...] += jnp.sum(dy * n, axis=0, keepdims=True)
    dh_ref[...] = _rms_bwd(n, r, dy * gfv)


def bwd_mlp_dx(dh2, h1, a, g2, wu_g, wd_g, job=None):
    T = dh2.shape[0]

    def body(dh2_ref, h1_ref, a_ref, g_ref, wu_ref, wd_ref, dh1_ref, da_ref, xn_ref, dg_ref, dh2b_ref):
        @pl.when(pl.program_id(0) == 0)
        def _():
            dg_ref[...] = jnp.zeros_like(dg_ref)

        h1v = h1_ref[...]
        r = _rms(h1v)
        n = h1v * r
        gv = g_ref[...]
        xn_ref[...] = (n * gv).astype(BF16)
        dh2v = dh2_ref[...]
        dh2b = dh2v.astype(BF16)
        dh2b_ref[...] = dh2b
        dxn = jnp.zeros((TT, D), F32)
        for q in range(NQ):
            cols = slice(FF_S * q, FF_S * (q + 1))
            ra = jnp.maximum(a_ref[:, cols].astype(F32), 0.0)
            da = (_nt(dh2b, wd_ref[q]) * (2.0 * ra)).astype(BF16)
            da_ref[:, cols] = da
            dxn = dxn + _nt(da, wu_ref[q])
        dg_ref[...] += jnp.sum(dxn * n, axis=0, keepdims=True)
        dh1_ref[...] = dh2v + _rms_bwd(n, r, dxn * gv)

    return tiled_call(
        "bwd_mlp_dx", T // TT, body, (dh2, h1, a, g2, wu_g, wd_g),
        in_specs=[pl.BlockSpec((TT, D), lambda i: (i, 0)), pl.BlockSpec((TT, D), lambda i: (i, 0)),
                  pl.BlockSpec((TT, DFF), lambda i: (i, 0)), _resident((1, D), lambda i: (0, 0)),
                  _resident((NQ, D, FF_S), lambda i: (0, 0, 0)), _resident((NQ, FF_S, D), lambda i: (0, 0, 0))],
        out_specs=[pl.BlockSpec((TT, D), lambda i: (i, 0)), pl.BlockSpec((TT, DFF), lambda i: (i, 0)),
                   pl.BlockSpec((TT, D), lambda i: (i, 0)), pl.BlockSpec((1, D), lambda i: (0, 0)),
                   pl.BlockSpec((TT, D), lambda i: (i, 0))],
        out_shape=[jax.ShapeDtypeStruct((T, D), F32), jax.ShapeDtypeStruct((T, DFF), BF16),
                   jax.ShapeDtypeStruct((T, D), BF16), jax.ShapeDtypeStruct((1, D), F32),
                   jax.ShapeDtypeStruct((T, D), BF16)],
        job=job)


def bwd_mlp_dw(xn, da, a, dh2):
    T = xn.shape[0]

    def body(xn_ref, da_ref, a_ref, dh2_ref, dwu_ref, dwd_ref):
        @pl.when(pl.program_id(1) == 0)
        def _():
            dwu_ref[...] = jnp.zeros_like(dwu_ref)
            dwd_ref[...] = jnp.zeros_like(dwd_ref)

        dwu_ref[...] += _tn(xn_ref[...], da_ref[...])
        ra = jnp.maximum(a_ref[...].astype(F32), 0.0)
        dwd_ref[...] += _tn((ra * ra).astype(BF16), dh2_ref[...])

    return pl.pallas_call(
        body, name="bwd_mlp_dw", grid=(NQ, T // TTW),
        in_specs=[pl.BlockSpec((TTW, D), lambda q, i: (i, 0)), pl.BlockSpec((TTW, FF_S), lambda q, i: (i, q)),
                  pl.BlockSpec((TTW, FF_S), lambda q, i: (i, q)), pl.BlockSpec((TTW, D), lambda q, i: (i, 0))],
        out_specs=[pl.BlockSpec((None, D, FF_S), lambda q, i: (q, 0, 0)),
                   pl.BlockSpec((None, FF_S, D), lambda q, i: (q, 0, 0))],
        out_shape=[jax.ShapeDtypeStruct((NQ, D, FF_S), F32), jax.ShapeDtypeStruct((NQ, FF_S, D), F32)],
        compiler_params=_params(("arbitrary", "arbitrary")),
    )(xn, da, a, dh2)


def bwd_out(dh1, y, wo_g, job=None):
    T = dh1.shape[0]

    def body(dh_ref, y_ref, wo_ref, dy_ref, dwo_ref):
        @pl.when(pl.program_id(0) == 0)
        def _():
            dwo_ref[...] = jnp.zeros_like(dwo_ref)

        dhb = dh_ref[...].astype(BF16)
        dy_ref[...] = _nt(dhb, wo_ref[...])
        dwo_ref[...] += _tn(y_ref[...], dhb)

    return tiled_call(
        "bwd_out", T // TT, body, (dh1, y, wo_g),
        in_specs=[pl.BlockSpec((TT, D), lambda i: (i, 0)), pl.BlockSpec((TT, D), lambda i: (i, 0)),
                  _resident((D, D), lambda i: (0, 0))],
        out_specs=[pl.BlockSpec((TT, D), lambda i: (i, 0)), pl.BlockSpec((D, D), lambda i: (0, 0))],
        out_shape=[jax.ShapeDtypeStruct((T, D), F32), jax.ShapeDtypeStruct((D, D), F32)], job=job)


GA_PS, GA_PB, GA_CB, GA_BR, GA_BI, GA_LAM, GA_CW, GA_GNP, GA_GNR = 0, 1, 2, 3, 4, 5, 6, 10, 11


def bwd_mixer(dy, proj, hs, aux, vp, pw, br, bi, job=None):
    T = proj.shape[0]
    tt = TT
    nt = T // tt
    per = tt // HALO

    def body(dy_ref, p_ref, ph_ref, hs_ref, hsh_ref, aux_ref, vp_ref, pw_ref, br_ref, bi_ref,
             dp_ref, ga_ref, dpw_ref, dwr_ref, dwi_ref, lam_ref, q_ref, dxc_ref):
        s = pl.program_id(0)
        ti = nt - 1 - s

        @pl.when(s == 0)
        def _():
            for ref in (ga_ref, dpw_ref, dwr_ref, dwi_ref, lam_ref, q_ref, dxc_ref):
                ref[...] = jnp.zeros_like(ref)

        dyv = dy_ref[...]
        lam_in, q_in, dxc_in = lam_ref[0:1, :], q_ref[...], dxc_ref[...]
        first = ti > 0
        vp_v = vp_ref[...]
        ur = jnp.concatenate([jnp.where(first, ph_ref[:, DP:DP + DR], 0.0), p_ref[:, DP:DP + DR]], axis=0)
        tf = (ti * tt + lax.broadcasted_iota(jnp.int32, (tt, 1), 0)).astype(F32) + 1.0
        mapped = [aux_ref[:, 512 * AUX_MAPPED + 128 * g:512 * AUX_MAPPED + 128 * (g + 1)] for g in range(4)]
        ps_row = _row(vp_v, VP_PS)
        xc = aux_ref[:, _aux(AUX_XC)]
        m = dict(pooled=[aux_ref[:, 512 * AUX_POOLED + 128 * g:512 * AUX_POOLED + 128 * (g + 1)] for g in range(4)],
                 mapped=mapped, ypool=[mapped[g] * ps_row[:, 128 * g:128 * (g + 1)] for g in range(4)],
                 inv_cnt=[1.0 / jnp.minimum(tf, float(2 << g)) for g in range(4)],
                 taps=[_down(ur, 3)[HALO:], _down(ur, 2)[HALO:], _down(ur, 1)[HALO:], ur[HALO:]],
                 xc=xc, xcb=xc.astype(BF16), r=aux_ref[:, _aux(AUX_R)], ig=aux_ref[:, _aux(AUX_IG)],
                 a=aux_ref[:, _aux(AUX_A)], mult=aux_ref[:, _aux(AUX_MULT)], gl=aux_ref[:, _aux(AUX_GL)],
                 gt=aux_ref[:, _aux(AUX_GT)], gate=p_ref[:, DP + DR:], sp=_softplus(-_row(vp_v, VP_LAM)))
        hs_v = hs_ref[...]
        hprev = _down(jnp.concatenate([jnp.where(first, hsh_ref[...], 0.0), hs_v], axis=0), 1)[HALO:]

        def acc(rw, v):
            ga_ref[rw:rw + 1, :] += jnp.sum(v, axis=0, keepdims=True)

        gnp = _row(vp_v, VP_GNP)
        ps = _row(vp_v, VP_PS)
        ssq = sum(jnp.sum(yp * yp, axis=-1, keepdims=True) for yp in m["ypool"])
        rp = lax.rsqrt(ssq * (1.0 / DP) + EPS)
        npool = [yp * rp for yp in m["ypool"]]
        dnp = [dyv[:, 128 * g:128 * (g + 1)] * gnp[:, 128 * g:128 * (g + 1)] for g in range(4)]
        mean_dn = sum(jnp.sum(dnp[g] * npool[g], axis=-1, keepdims=True) for g in range(4)) * (1.0 / DP)
        for g in range(4):
            lanes = slice(128 * g, 128 * (g + 1))
            ga_ref[GA_GNP:GA_GNP + 1, lanes] += jnp.sum(dyv[:, lanes] * npool[g], axis=0, keepdims=True)
            dyp = rp * (dnp[g] - npool[g] * mean_dn)
            ga_ref[GA_PS:GA_PS + 1, lanes] += jnp.sum(dyp * m["mapped"][g], axis=0, keepdims=True)
            dmap = dyp * ps[:, lanes]
            ga_ref[GA_PB:GA_PB + 1, lanes] += jnp.sum(dmap, axis=0, keepdims=True)
            dmb = dmap.astype(BF16)
            dpw_ref[g] += _tn(m["pooled"][g].astype(BF16), dmb)
            dpool = _nt(dmb, pw_ref[g])
            qv = dpool * m["inv_cnt"][g]
            win = jnp.concatenate([qv, q_in[:, lanes]], axis=0)
            for j in range(g + 1):
                win = win + _up(win, 1 << j)
            q_ref[:, lanes] = qv[:HALO]
            dp_ref[:, lanes] = (win[:tt] - dpool).astype(BF16)

        gnr = _row(vp_v, VP_GNR)
        yr = hs_v * m["gl"]
        rr = _rms(yr)
        nr = yr * rr
        dyr_out = dyv[:, DP:]
        acc(GA_GNR, dyr_out * nr)
        dyr = _rms_bwd(nr, rr, dyr_out * gnr)
        gate, gt = m["gate"], m["gt"]
        dgl = 0.5 * (1.0 + gt) + 0.5 * gate * (1.0 - gt * gt) * (GELU_C * (1.0 + 3.0 * GELU_K * gate * gate))
        dp_ref[:, DP + DR:] = (dyr * hs_v * dgl).astype(BF16)
        dhs = dyr * m["gl"]
        a = m["a"]
        row = lax.broadcasted_iota(jnp.int32, (tt, 1), 0)
        c_next = jnp.where(row < tt - 1, _up(a, 1), 1.0)
        lam = _scan_rev(c_next, dhs, lam_in)
        lam_ref[0:1, :] = a[0:1, :] * lam[0:1, :]
        xc, ig, r, mult = m["xc"], m["ig"], m["r"], m["mult"]
        dmult = lam * ig * xc
        dig = lam * mult * xc
        dxc = lam * mult * ig
        dla = lam * hprev * a - dmult * (a * a) / mult
        acc(GA_LAM, dla * r)
        dzr = (dla * (-8.0 * m["sp"])) * (r * (1.0 - r))
        dzi = dig * (ig * (1.0 - ig))
        acc(GA_BR, dzr)
        acc(GA_BI, dzi)
        dzrb, dzib = dzr.astype(BF16), dzi.astype(BF16)
        xcb = m["xcb"]
        halves = []
        for k in range(2):
            lanes = slice(256 * k, 256 * (k + 1))
            dwr_ref[k] += _tn(xcb[:, lanes], dzrb[:, lanes])
            dwi_ref[k] += _tn(xcb[:, lanes], dzib[:, lanes])
            halves.append(_nt(dzrb[:, lanes], br_ref[k]) + _nt(dzib[:, lanes], bi_ref[k]))
        dxc = dxc + jnp.concatenate(halves, axis=1)
        acc(GA_CB, dxc)
        for k in range(4):
            acc(GA_CW + k, dxc * m["taps"][k])
        dxe = jnp.concatenate([dxc, dxc_in], axis=0)
        du = (_up(dxe, 3)[:tt] * _row(vp_v, VP_CW) + _up(dxe, 2)[:tt] * _row(vp_v, VP_CW + 1)
              + _up(dxe, 1)[:tt] * _row(vp_v, VP_CW + 2) + dxc * _row(vp_v, VP_CW + 3))
        dxc_ref[...] = dxc[:HALO]
        dp_ref[:, DP:DP + DR] = du.astype(BF16)

        @pl.when(s == nt - 1)
        def _():
            lamp = _row(vp_v, VP_LAM)
            ga_ref[GA_LAM:GA_LAM + 1, :] = ga_ref[GA_LAM:GA_LAM + 1, :] * (8.0 * jax.nn.sigmoid(-lamp))

    rev = lambda i: (nt - 1 - i, 0)
    rev_halo = lambda i: (_halo_index(nt - 1 - i, per), 0)
    return tiled_call(
        "bwd_mixer", nt, body, (dy, proj, proj, hs, hs, aux, vp, pw, br, bi),
        in_specs=[pl.BlockSpec((tt, D), rev), pl.BlockSpec((tt, DIN), rev), pl.BlockSpec((HALO, DIN), rev_halo),
                  pl.BlockSpec((tt, DR), rev), pl.BlockSpec((HALO, DR), rev_halo), pl.BlockSpec((tt, AUX_W), rev),
                  _resident((16, 512), lambda i: (0, 0)), _resident((4, 128, 128), lambda i: (0, 0, 0)),
                  _resident((2, 256, 256), lambda i: (0, 0, 0)), _resident((2, 256, 256), lambda i: (0, 0, 0))],
        out_specs=[pl.BlockSpec((tt, DIN), rev), pl.BlockSpec((16, 512), lambda i: (0, 0)),
                   pl.BlockSpec((4, 128, 128), lambda i: (0, 0, 0)), pl.BlockSpec((2, 256, 256), lambda i: (0, 0, 0)),
                   pl.BlockSpec((2, 256, 256), lambda i: (0, 0, 0))],
        out_shape=[jax.ShapeDtypeStruct((T, DIN), BF16), jax.ShapeDtypeStruct((16, 512), F32),
                   jax.ShapeDtypeStruct((4, 128, 128), F32), jax.ShapeDtypeStruct((2, 256, 256), F32),
                   jax.ShapeDtypeStruct((2, 256, 256), F32)],
        scratch=[pltpu.VMEM((8, DR), F32), pltpu.VMEM((HALO, DP), F32), pltpu.VMEM((HALO, DR), F32)], job=job)


def bwd_in(dproj, h, g1, w_g, dh1, job=None):
    T = h.shape[0]

    def body(dp_ref, h_ref, g_ref, w_ref, dh1_ref, dh_ref, dw_ref, dg_ref, wfull_ref, acc_ref):
        i = pl.program_id(0)

        @pl.when(i == 0)
        def _():
            _chip_slabs_to_columns(w_ref, wfull_ref)
            acc_ref[...] = jnp.zeros_like(acc_ref)
            dg_ref[...] = jnp.zeros_like(dg_ref)

        hv = h_ref[...]
        r = _rms(hv)
        n = hv * r
        gv = g_ref[...]
        xn = (n * gv).astype(BF16)
        dpv = dp_ref[...]
        dxn = _nt(dpv, wfull_ref[...])
        acc_ref[...] += _tn(xn, dpv)
        dg_ref[...] += jnp.sum(dxn * n, axis=0, keepdims=True)
        dh_ref[...] = dh1_ref[...] + _rms_bwd(n, r, dxn * gv)

        @pl.when(i == T // TT - 1)
        def _():
            for q in range(NQ):
                dw_ref[q] = acc_ref[:, WIN_S * q:WIN_S * (q + 1)]

    return tiled_call(
        "bwd_in", T // TT, body, (dproj, h, g1, w_g, dh1),
        in_specs=[pl.BlockSpec((TT, DIN), lambda i: (i, 0)), pl.BlockSpec((TT, D), lambda i: (i, 0)),
                  _resident((1, D), lambda i: (0, 0)), _resident((NQ, D, WIN_S), lambda i: (0, 0, 0)),
                  pl.BlockSpec((TT, D), lambda i: (i, 0))],
        out_specs=[pl.BlockSpec((TT, D), lambda i: (i, 0)), pl.BlockSpec((NQ, D, WIN_S), lambda i: (0, 0, 0)),
                   pl.BlockSpec((1, D), lambda i: (0, 0))],
        out_shape=[jax.ShapeDtypeStruct((T, D), F32), jax.ShapeDtypeStruct((NQ, D, WIN_S), F32),
                   jax.ShapeDtypeStruct((1, D), F32)],
        scratch=[pltpu.VMEM((D, DIN), BF16), pltpu.VMEM((D, DIN), F32)], job=job)


def _row_block(rows, cols, itemsize=4, budget=2 * 1024 * 1024):
    best = None
    for b in range(16, rows + 1, 16):
        if rows % b == 0 and b * cols * itemsize <= budget:
            best = b
    return best if best is not None else rows


def add_pairs(full, got, dtype):
    core = lax.axis_index("c").astype(jnp.int32).reshape(1)
    outs = []
    for k in range(len(full)):
        q, hr, c = got[k].shape
        rb = _row_block(hr, c)
        nb = hr // rb

        def body(c_ref, a_ref, b_ref, o_ref):
            o_ref[...] = (a_ref[...] + b_ref[...]).astype(dtype)

        outs.append(pl.pallas_call(
            body, name="add_pairs",
            grid_spec=pltpu.PrefetchScalarGridSpec(
                num_scalar_prefetch=1, grid=(q, nb),
                in_specs=[pl.BlockSpec((None, rb, c), lambda qi, i, c_ref, nb=nb: (qi, c_ref[0] * nb + i, 0)),
                          pl.BlockSpec((None, rb, c), lambda qi, i, c_ref: (qi, i, 0))],
                out_specs=pl.BlockSpec((None, rb, c), lambda qi, i, c_ref: (qi, i, 0))),
            out_shape=jax.ShapeDtypeStruct((q, hr, c), dtype),
            compiler_params=_params(("arbitrary", "arbitrary")),
        )(core, full[k], got[k]))
    return outs


def sum_chips(parts):
    core = lax.axis_index("c").astype(jnp.int32).reshape(1)
    outs = []
    for p in parts:
        _, hr, c = p.shape
        rb = _row_block(hr, c)
        nb = hr // rb

        def body(c_ref, p_ref, o_ref):
            s = p_ref[0].astype(F32) + p_ref[1].astype(F32)
            s = s + p_ref[2].astype(F32)
            o_ref[...] = s + p_ref[3].astype(F32)

        outs.append(pl.pallas_call(
            body, name="sum_chips",
            grid_spec=pltpu.PrefetchScalarGridSpec(
                num_scalar_prefetch=1, grid=(nb,),
                in_specs=[pl.BlockSpec((NQ, rb, c), lambda i, c_ref: (0, i, 0))],
                out_specs=pl.BlockSpec((rb, c), lambda i, c_ref, nb=nb: (c_ref[0] * nb + i, 0))),
            out_shape=jax.ShapeDtypeStruct((2 * hr, c), F32),
            compiler_params=_params(("arbitrary",)),
        )(core, p))
    return outs


def adamw(w, g, m, v, job=None):
    r, c = w.shape
    rb = _row_block(r, c, budget=1024 * 1024)
    c1 = 1.0 / (1.0 - ADAM_B1 ** ADAM_STEP)
    c2 = 1.0 / (1.0 - ADAM_B2 ** ADAM_STEP)

    def body(w_ref, g_ref, m_ref, v_ref, d_ref, nm_ref, nv_ref):
        gv = g_ref[...]
        nm = ADAM_B1 * m_ref[...] + (1.0 - ADAM_B1) * gv
        nv = ADAM_B2 * v_ref[...] + (1.0 - ADAM_B2) * (gv * gv)
        nm_ref[...] = nm
        nv_ref[...] = nv
        d_ref[...] = -ADAM_LR * ((nm * c1) / (jnp.sqrt(nv * c2) + ADAM_EPS) + ADAM_WD * w_ref[...])

    spec = pl.BlockSpec((rb, c), lambda i: (i, 0))
    return tiled_call("adamw", r // rb, body, (w, g, m, v), in_specs=[spec] * 4, out_specs=[spec] * 3,
                      out_shape=[jax.ShapeDtypeStruct((r, c), F32)] * 3, job=job)


def _place():
    return lax.axis_index("x"), lax.axis_index("y"), lax.axis_index("c")


def _other_chips(x, y):
    return [(1 - x, y), (x, 1 - y), (1 - x, 1 - y)]


LOCAL_CHUNKS = 4
ICI_CHUNKS = 2
FWD_CHUNKS = 8


def _start_remote(src_rows, dst_rows, rows, chunks, send_sem, recv_sem, dev):
    rc = rows // chunks
    for j in range(chunks):
        pltpu.make_async_remote_copy(src_rows(j * rc, rc), dst_rows(j * rc, rc), send_sem, recv_sem,
                                     device_id=dev, device_id_type=MESH).start()


def _waiter(src, dst, send_sem, recv_sem):
    x, y, c = _place()
    return pltpu.make_async_remote_copy(src, dst, send_sem, recv_sem, device_id=(x, y, c), device_id_type=MESH)


class GatherJob:
    def __init__(self, shards, layer):
        self.shards, self.layer, self.n = list(shards), layer, len(shards)
        self.operands = list(shards)
        self.out_shape = [jax.ShapeDtypeStruct((NQ,) + s.shape[1:], s.dtype) for s in shards]
        sems = pltpu.SemaphoreType.DMA((self.n, 3))
        self.scratch = [sems, sems, sems, sems, pltpu.SemaphoreType.DMA((self.n,))]

    def _half(self, k):
        return self.shards[k].shape[1] // 2

    def _src(self, ins, k, half):
        hr = self._half(k)
        return lambda r0, nr: ins[k].at[self.layer, pl.ds(half * hr + r0, nr), :]

    def _dst(self, outs, k, chip, half):
        hr = self._half(k)
        return lambda r0, nr: outs[k].at[2 * chip[0] + chip[1], pl.ds(half * hr + r0, nr), :]

    def start(self, ins, outs, scr):
        send, recv, fsend, frecv, lsem = scr
        x, y, c = _place()
        for k in range(self.n):
            rows = self.shards[k].shape[1]
            rc = rows // LOCAL_CHUNKS
            for j in range(LOCAL_CHUNKS):
                pltpu.make_async_copy(ins[k].at[self.layer, pl.ds(j * rc, rc), :],
                                      outs[k].at[2 * x + y, pl.ds(j * rc, rc), :], lsem.at[k]).start()
        for k in range(self.n):
            for j, chip in enumerate(_other_chips(x, y)):
                _start_remote(self._src(ins, k, c), self._dst(outs, k, (x, y), c), self._half(k), ICI_CHUNKS,
                              send.at[k, j], recv.at[k, j], (chip[0], chip[1], c))

    def mid(self, ins, outs, scr):
        send, recv, fsend, frecv, lsem = scr
        x, y, c = _place()
        for k in range(self.n):
            hr = self._half(k)
            for j, chip in enumerate(_other_chips(x, y)):
                got = self._dst(outs, k, chip, c)
                _waiter(got(0, hr), got(0, hr), send.at[k, j], recv.at[k, j]).wait_recv()
                _start_remote(got, got, hr, FWD_CHUNKS, fsend.at[k, j], frecv.at[k, j], (x, y, 1 - c))

    def finish(self, ins, outs, scr):
        send, recv, fsend, frecv, lsem = scr
        x, y, c = _place()
        for k in range(self.n):
            hr = self._half(k)
            for j, chip in enumerate(_other_chips(x, y)):
                theirs = self._dst(outs, k, chip, 1 - c)(0, hr)
                w = _waiter(theirs, theirs, fsend.at[k, j], frecv.at[k, j])
                w.wait_recv()
                w.wait_send()
                _waiter(theirs, theirs, send.at[k, j], recv.at[k, j]).wait_send()
            pltpu.make_async_copy(ins[k].at[self.layer], outs[k].at[2 * x + y], lsem.at[k]).wait()


class ExchangeJob:
    def __init__(self, arrs, scatter):
        self.arrs, self.scatter, self.n = list(arrs), list(scatter), len(arrs)
        self.operands = list(arrs)
        self.out_shape = [jax.ShapeDtypeStruct((NQ,) + a.shape[1:], a.dtype) for a in arrs]
        sems = pltpu.SemaphoreType.DMA((self.n, 3))
        self.scratch = [sems, sems, pltpu.SemaphoreType.DMA((self.n,))]

    def _slot(self, ref, s):
        return lambda r0, nr: ref.at[s, pl.ds(r0, nr), :]

    def start(self, ins, outs, scr):
        send, recv, lsem = scr
        x, y, c = _place()
        p = 2 * x + y
        for k in range(self.n):
            rows = self.arrs[k].shape[1]
            rc = rows // ICI_CHUNKS
            for j in range(ICI_CHUNKS):
                pltpu.make_async_copy(ins[k].at[p if self.scatter[k] else 0, pl.ds(j * rc, rc), :],
                                      outs[k].at[p, pl.ds(j * rc, rc), :], lsem.at[k]).start()
            for j, chip in enumerate(_other_chips(x, y)):
                q = 2 * chip[0] + chip[1]
                _start_remote(self._slot(ins[k], q if self.scatter[k] else 0), self._slot(outs[k], p), rows, ICI_CHUNKS,
                              send.at[k, j], recv.at[k, j], (chip[0], chip[1], c))

    def mid(self, ins, outs, scr):
        pass

    def finish(self, ins, outs, scr):
        send, recv, lsem = scr
        x, y, c = _place()
        for k in range(self.n):
            for j, chip in enumerate(_other_chips(x, y)):
                slot = outs[k].at[2 * chip[0] + chip[1]]
                w = _waiter(slot, slot, send.at[k, j], recv.at[k, j])
                w.wait_recv()
                w.wait_send()
            pltpu.make_async_copy(ins[k].at[0], outs[k].at[0], lsem.at[k]).wait()


def run_job(job, name):
    n_in, n_out = len(job.operands), len(job.out_shape)

    def body(*refs):
        ins, outs, scr = refs[:n_in], refs[n_in:n_in + n_out], refs[n_in + n_out:]
        job.start(ins, outs, scr)
        job.mid(ins, outs, scr)
        job.finish(ins, outs, scr)

    return pl.pallas_call(body, name=name, in_specs=[ANY] * n_in, out_specs=[ANY] * n_out, out_shape=job.out_shape,
                          input_output_aliases=dict(getattr(job, "aliases", {})),
                          scratch_shapes=job.scratch)(*job.operands)


def tiled_call(name, steps, body, args, in_specs, out_specs, out_shape, scratch=(), job=None, mid_step=None):
    if job is None:
        return pl.pallas_call(body, name=name, grid=(steps,), in_specs=in_specs, out_specs=out_specs, out_shape=out_shape,
                              scratch_shapes=list(scratch), compiler_params=_params(("arbitrary",)))(*args), []
    n_in, n_out, n_scr = len(args), len(out_shape), len(scratch)
    j_in, j_out = len(job.operands), len(job.out_shape)
    mid_step = steps // 2 if mid_step is None else mid_step

    def carried(*refs):
        a, ji = refs[:n_in], refs[n_in:n_in + j_in]
        o = refs[n_in + j_in:n_in + j_in + n_out]
        jo = refs[n_in + j_in + n_out:n_in + j_in + n_out + j_out]
        rest = refs[n_in + j_in + n_out + j_out:]
        sc, js = rest[:n_scr], rest[n_scr:]
        i = pl.program_id(0)

        @pl.when(i == 0)
        def _():
            job.start(ji, jo, js)

        body(*a, *o, *sc)

        @pl.when(i == mid_step)
        def _():
            job.mid(ji, jo, js)

        @pl.when(i == steps - 1)
        def _():
            job.finish(ji, jo, js)

    res = pl.pallas_call(
        carried, name=name, grid=(steps,), in_specs=list(in_specs) + [ANY] * j_in, out_specs=list(out_specs) + [ANY] * j_out,
        out_shape=list(out_shape) + list(job.out_shape), scratch_shapes=list(scratch) + list(job.scratch),
        input_output_aliases={n_in + i: n_out + o for i, o in getattr(job, "aliases", {}).items()},
        compiler_params=_params(("arbitrary",)))(*args, *job.operands)
    return res[:n_out], res[n_out:]


D2D_CHUNKS = 4


class SwapJob:
    def __init__(self, arrs):
        self.arrs, self.n = list(arrs), len(arrs)
        self.operands = list(arrs)
        self.out_shape = [jax.ShapeDtypeStruct((a.shape[0], a.shape[1] // 2, a.shape[2]), a.dtype) for a in arrs]
        self.scratch = [pltpu.SemaphoreType.DMA((self.n,)), pltpu.SemaphoreType.DMA((self.n,))]

    def start(self, ins, got, scr):
        send, recv = scr
        x, y, c = _place()
        for k in range(self.n):
            q, r, _ = self.arrs[k].shape
            hr = r // 2
            for qi in range(q):
                _start_remote(lambda r0, nr: ins[k].at[qi, pl.ds((1 - c) * hr + r0, nr), :],
                              lambda r0, nr: got[k].at[qi, pl.ds(r0, nr), :], hr, D2D_CHUNKS,
                              send.at[k], recv.at[k], (x, y, 1 - c))

    def mid(self, ins, got, scr):
        pass

    def finish(self, ins, got, scr):
        send, recv = scr
        for k in range(self.n):
            hr = self.arrs[k].shape[1] // 2
            w = _waiter(ins[k].at[:, pl.ds(0, hr), :], got[k], send.at[k], recv.at[k])
            w.wait_send()
            w.wait_recv()


def swap_halves(arrs):
    return run_job(SwapJob(arrs), "swap_halves")


JOIN_CHUNKS = 8


class JoinJob:
    def __init__(self, arrs):
        self.arrs, self.n = list(arrs), len(arrs)
        self.operands = list(arrs)
        self.out_shape = [jax.ShapeDtypeStruct(a.shape, a.dtype) for a in arrs]
        self.scratch = [pltpu.SemaphoreType.DMA((self.n,)), pltpu.SemaphoreType.DMA((self.n,))]
        self.aliases = {k: k for k in range(self.n)}

    def start(self, ins, outs, scr):
        send, recv = scr
        x, y, c = _place()
        for k in range(self.n):
            hr = self.arrs[k].shape[0] // 2
            rows = lambda r0, nr: outs[k].at[pl.ds(c * hr + r0, nr), :]
            _start_remote(rows, rows, hr, JOIN_CHUNKS, send.at[k], recv.at[k], (x, y, 1 - c))

    def mid(self, ins, outs, scr):
        pass

    def finish(self, ins, outs, scr):
        send, recv = scr
        x, y, c = _place()
        for k in range(self.n):
            hr = self.arrs[k].shape[0] // 2
            w = _waiter(outs[k].at[pl.ds(c * hr, hr), :], outs[k].at[pl.ds((1 - c) * hr, hr), :], send.at[k], recv.at[k])
            w.wait_send()
            w.wait_recv()


def join_halves(arrs):
    return run_job(JoinJob(arrs), "join_halves")


def _block_diag(w):
    eye = jnp.eye(4, dtype=F32)[None, :, None, :, None]
    return (w.reshape(-1, 4, 64, 1, 64) * eye).reshape(-1, 256, 256).astype(BF16)


def _diag_blocks(b):
    eye = jnp.eye(4, dtype=F32)[None, :, None, :, None]
    return (b.reshape(2, 4, 64, 4, 64) * eye).sum(axis=3).reshape(8, 64, 64)


def _vec_params(pool_b, pool_scale, conv_b, gate_r_b, gate_i_b, lru_lambda, conv_w_full, group_norm_g):
    rows = [pool_b, pool_scale, conv_b, gate_r_b, gate_i_b, lru_lambda,
            conv_w_full[:, 0], conv_w_full[:, 1], conv_w_full[:, 2], conv_w_full[:, 3],
            group_norm_g[:, :DP], group_norm_g[:, DP:]]
    return jnp.pad(jnp.stack(rows, axis=1), ((0, 0), (0, 4), (0, 0)))


SMALL_ROWS = 1280


def _pack_small(layers, final_g, meta):
    def tile_rows(v):
        return jnp.pad(v.reshape(2, 512), ((0, 6), (0, 0)))

    rows = []
    for vec, pw, wr, wi, g1, g2 in layers:
        rows += [vec, pw.reshape(128, 512), wr.reshape(64, 512), wi.reshape(64, 512), tile_rows(g1), tile_rows(g2)]
    rows += [tile_rows(final_g), meta.reshape(32, 512)]
    used = sum(r.shape[0] for r in rows)
    return jnp.concatenate(rows + [jnp.zeros((SMALL_ROWS - used, 512), F32)], axis=0)


def _unpack_small(flat):
    layers, o = [], 0
    for _ in range(DEPTH):
        vec = flat[o:o + 16]; o += 16
        pw = flat[o:o + 128].reshape(4, 128, 128); o += 128
        wr = flat[o:o + 64].reshape(8, 64, 64); o += 64
        wi = flat[o:o + 64].reshape(8, 64, 64); o += 64
        g1 = flat[o:o + 2].reshape(1024); o += 8
        g2 = flat[o:o + 2].reshape(1024); o += 8
        layers.append((vec, pw, wr, wi, g1, g2))
    final_g = flat[o:o + 2].reshape(1024); o += 8
    meta = flat[o:o + 32].reshape(16, 1024)
    return layers, final_g, meta


def local_step(x2d, tgt2d, meta_full, conv_w_full, sp, shards=None, win0=None, gathered=None):
    exchange = gathered is None
    if exchange:
        gathered = [None] * DEPTH
    h = None
    saved = []
    vp_all = _vec_params(sp["pool_b"], sp["pool_scale"], sp["conv_b"], sp["gate_r_b"], sp["gate_i_b"], sp["lru_lambda"],
                         conv_w_full, sp["group_norm_g"])
    pw_all = sp["pool_w"].astype(BF16)
    br_all = _block_diag(sp["gate_r_w"].reshape(DEPTH * 8, 64, 64)).reshape(DEPTH, 2, 256, 256)
    bi_all = _block_diag(sp["gate_i_w"].reshape(DEPTH * 8, 64, 64)).reshape(DEPTH, 2, 256, 256)
    for l in range(DEPTH):
        vp, pw, br, bi = vp_all[l], pw_all[l], br_all[l], bi_all[l]
        if l == 0:
            job = GatherJob(shards[1:], 0) if exchange else None
            win = win0 if exchange else gathered[0][0]
            (proj, y, hs, aux, h), rest = fwd_mix(x2d, sp["mix_norm_g"][l][None], win, vp, pw, br, bi, job, meta=meta_full)
            wo, wu, wd = rest if exchange else gathered[0][1:]
        else:
            win, wo, wu, wd = gathered[l]
            (proj, y, hs, aux), _ = fwd_mix(h, sp["mix_norm_g"][l][None], win, vp, pw, br, bi)
        wo = wo.reshape(D, D)
        if l + 1 < DEPTH:
            job = GatherJob(shards, l + 1) if exchange else None
            (h1, a, h_next), fetched = fwd_post(h, y, wo, sp["mlp_norm_g"][l][None], wu, wd, job)
            if exchange:
                gathered[l + 1] = fetched
        else:
            (h1, a, dh, dgf, loss_part), _ = fwd_post(h, y, wo, sp["mlp_norm_g"][l][None], wu, wd,
                                                      loss=(sp["final_norm_g"][None], tgt2d))
            h_next = None
        saved.append((h, proj, y, hs, aux, h1, a, vp, pw, br, bi, win, wo, wu, wd))
        h = h_next

    big = [[None, None, None, None] for _ in range(DEPTH)]
    io_raw, io_pairs = [], None
    small_layers = [None] * DEPTH
    for l in reversed(range(DEPTH)):
        h0, proj, y, hs, aux, h1, a, vp, pw, br, bi, win, wo, wu, wd = saved[l]
        g1 = sp["mix_norm_g"][l][None]
        job = ExchangeJob(io_pairs, [True] * 2) if io_pairs is not None else None
        (dh1, da, xn2, dg2, dhb), io_parts = bwd_mlp_dx(dh, h1, a, sp["mlp_norm_g"][l][None], wu, wd, job)
        io_sums = sum_chips(io_parts) if job is not None else []
        dwu, dwd = bwd_mlp_dw(xn2, da, a, dhb)
        if not exchange:
            (dy, dwo), _ = bwd_out(dh1, y, wo)
            (dproj, ga, dpw, dwr, dwi), _ = bwd_mixer(dy, proj, hs, aux, vp, pw, br, bi)
            (dh, dwin, dg1), _ = bwd_in(dproj, h0, g1, win, dh1)
            big[l] = [dwin, dwo.reshape(NQ, D // NQ, D), dwu, dwd]
        else:
            arrs = [dwu, dwd] + io_raw
            (dy, dwo), got = bwd_out(dh1, y, wo, SwapJob(arrs))
            pairs = add_pairs(arrs, got, BF16)
            crossing = pairs if l == 0 else pairs[:2]
            io_pairs = None if l == 0 or not io_raw else pairs[2:]
            (dproj, ga, dpw, dwr, dwi), parts = bwd_mixer(dy, proj, hs, aux, vp, pw, br, bi,
                                                          ExchangeJob(crossing, [True] * len(crossing)))
            (dh, dwin, dg1), sums = bwd_in(dproj, h0, g1, win, dh1, JoinJob(list(sum_chips(parts)) + list(io_sums)))
            big[l][2:] = sums[:2]
            if l == 0:
                big[1][:2] = sums[2:4]
            if io_sums:
                big[l + 2][:2] = sums[-2:]
            io_raw = [dwin, dwo.reshape(NQ, D // NQ, D)]
        small_layers[l] = (ga, dpw, _diag_blocks(dwr), _diag_blocks(dwi), dg1[0], dg2[0])
    return loss_part, dh, big, small_layers, dgf, io_raw


def kernel(x, meta_tokens, mix_norm_g, w_in, pool_w, pool_b, pool_scale, conv_w, conv_b, gate_r_w, gate_r_b, gate_i_w, gate_i_b, lru_lambda, group_norm_g, w_out, mlp_norm_g, w_up, w_down, final_norm_g, loss_target, m_meta_tokens, m_mix_norm_g, m_w_in, m_pool_w, m_pool_b, m_pool_scale, m_conv_w, m_conv_b, m_gate_r_w, m_gate_r_b, m_gate_i_w, m_gate_i_b, m_lru_lambda, m_group_norm_g, m_w_out, m_mlp_norm_g, m_w_up, m_w_down, m_final_norm_g, v_meta_tokens, v_mix_norm_g, v_w_in, v_pool_w, v_pool_b, v_pool_scale, v_conv_w, v_conv_b, v_gate_r_w, v_gate_r_b, v_gate_i_w, v_gate_i_b, v_lru_lambda, v_group_norm_g, v_w_out, v_mlp_norm_g, v_w_up, v_w_down, v_final_norm_g):
    p = 2 * lax.axis_index("x") + lax.axis_index("y")

    shards = [w_in.astype(BF16), w_out.astype(BF16), w_up.astype(BF16), w_down.astype(BF16)]
    small = jnp.concatenate([meta_tokens, jnp.pad(conv_w.reshape(16, 128), ((0, 0), (0, 128)))], axis=0)
    small = jnp.pad(small, ((0, 96), (0, 0)))[None]
    win0, small_g = run_job(GatherJob([shards[0], small], 0), "gather_first")
    meta_full = jnp.transpose(small_g[:, :16, :], (1, 0, 2)).reshape(NMETA, D)
    conv_w_full = jnp.transpose(small_g[:, 16:32, :128].reshape(NQ, DEPTH, 4, 128), (1, 2, 0, 3)).reshape(DEPTH, 4, DR)

    sp = dict(mix_norm_g=mix_norm_g, pool_w=pool_w, pool_b=pool_b, pool_scale=pool_scale, conv_b=conv_b, gate_r_w=gate_r_w,
              gate_r_b=gate_r_b, gate_i_w=gate_i_w, gate_i_b=gate_i_b, lru_lambda=lru_lambda, group_norm_g=group_norm_g,
              mlp_norm_g=mlp_norm_g, final_norm_g=final_norm_g)
    loss_part, dh, big, small_layers, dgf, io0_parts = local_step(x[0], loss_target[0], meta_full, conv_w_full, sp,
                                                                  shards=shards, win0=win0)
    loss = lax.psum(loss_part[0, 0], ("x", "y", "c"))
    grad_x = dh[NMETA:][None]

    weights = dict(meta_tokens=meta_tokens, mix_norm_g=mix_norm_g, w_in=w_in, pool_w=pool_w, pool_b=pool_b, pool_scale=pool_scale,
                   conv_w=conv_w, conv_b=conv_b, gate_r_w=gate_r_w, gate_r_b=gate_r_b, gate_i_w=gate_i_w, gate_i_b=gate_i_b,
                   lru_lambda=lru_lambda, group_norm_g=group_norm_g, w_out=w_out, mlp_norm_g=mlp_norm_g, w_up=w_up, w_down=w_down,
                   final_norm_g=final_norm_g)
    mom_m = dict(meta_tokens=m_meta_tokens, mix_norm_g=m_mix_norm_g, w_in=m_w_in, pool_w=m_pool_w, pool_b=m_pool_b,
                 pool_scale=m_pool_scale, conv_w=m_conv_w, conv_b=m_conv_b, gate_r_w=m_gate_r_w, gate_r_b=m_gate_r_b,
                 gate_i_w=m_gate_i_w, gate_i_b=m_gate_i_b, lru_lambda=m_lru_lambda, group_norm_g=m_group_norm_g, w_out=m_w_out,
                 mlp_norm_g=m_mlp_norm_g, w_up=m_w_up, w_down=m_w_down, final_norm_g=m_final_norm_g)
    mom_v = dict(meta_tokens=v_meta_tokens, mix_norm_g=v_mix_norm_g, w_in=v_w_in, pool_w=v_pool_w, pool_b=v_pool_b,
                 pool_scale=v_pool_scale, conv_w=v_conv_w, conv_b=v_conv_b, gate_r_w=v_gate_r_w, gate_r_b=v_gate_r_b,
                 gate_i_w=v_gate_i_w, gate_i_b=v_gate_i_b, lru_lambda=v_lru_lambda, group_norm_g=v_group_norm_g, w_out=v_w_out,
                 mlp_norm_g=v_mlp_norm_g, w_up=v_w_up, w_down=v_w_down, final_norm_g=v_final_norm_g)
    names = list(weights)
    delta, new_m, new_v = {}, {}, {}

    def adam_big(nm, g, job=None):
        shp = weights[nm].shape
        two_d = lambda t: t.reshape(shp[0] * shp[1], shp[2])
        (d_, m_, v_), got = adamw(two_d(weights[nm]), two_d(g), two_d(mom_m[nm]), two_d(mom_v[nm]), job)
        delta[nm], new_m[nm], new_v[nm] = d_.reshape(shp), m_.reshape(shp), v_.reshape(shp)
        return got

    last = [_pack_small(small_layers, dgf[0], dh[:NMETA])[None]] + io0_parts
    got = swap_halves(last)
    pairs = add_pairs(last[:1], got[:1], F32) + add_pairs(last[1:], got[1:], BF16)
    sums = join_halves(sum_chips(run_job(ExchangeJob(pairs, [False, True, True]), "exchange_last")))
    small_sum, io0 = sums[0], sums[1:]
    g_up = jnp.stack([big[l][-2] for l in range(DEPTH)])
    g_down = jnp.stack([big[l][-1] for l in range(DEPTH)])
    adam_big("w_up", g_up)
    adam_big("w_down", g_down)
    g_in = jnp.stack([io0[0]] + [big[l][0] for l in range(1, DEPTH)])
    g_out = jnp.stack([io0[1]] + [big[l][1] for l in range(1, DEPTH)])
    adam_big("w_in", g_in)
    adam_big("w_out", g_out)
    g_layers, g_final, g_meta_full = _unpack_small(small_sum)

    g_vec = [gl[0] for gl in g_layers]
    grads = dict(
        meta_tokens=lax.dynamic_slice(g_meta_full, (0, p * (D // NQ)), (NMETA, D // NQ)),
        mix_norm_g=jnp.stack([gl[4] for gl in g_layers]),
        w_in=g_in,
        pool_w=jnp.stack([gl[1] for gl in g_layers]),
        pool_b=jnp.stack([gv[GA_PB] for gv in g_vec]),
        pool_scale=jnp.stack([gv[GA_PS] for gv in g_vec]),
        conv_w=lax.dynamic_slice(jnp.stack([gv[GA_CW:GA_CW + 4] for gv in g_vec]), (0, 0, p * 128), (DEPTH, 4, 128)),
        conv_b=jnp.stack([gv[GA_CB] for gv in g_vec]),
        gate_r_w=jnp.stack([gl[2] for gl in g_layers]),
        gate_r_b=jnp.stack([gv[GA_BR] for gv in g_vec]),
        gate_i_w=jnp.stack([gl[3] for gl in g_layers]),
        gate_i_b=jnp.stack([gv[GA_BI] for gv in g_vec]),
        lru_lambda=jnp.stack([gv[GA_LAM] for gv in g_vec]),
        group_norm_g=jnp.stack([jnp.concatenate([gv[GA_GNP], gv[GA_GNR]]) for gv in g_vec]),
        w_out=g_out,
        mlp_norm_g=jnp.stack([gl[5] for gl in g_layers]),
        w_up=g_up,
        w_down=g_down,
        final_norm_g=g_final,
    )

    small_names = [nm for nm in names if nm not in ("w_in", "w_out", "w_up", "w_down")]
    sizes = [weights[nm].size for nm in small_names]
    total = sum(sizes)
    rows = -(-total // 512)
    rows = -(-rows // 16) * 16

    def flat(tree, fill):
        v_ = jnp.concatenate([tree[nm].reshape(-1) for nm in small_names])
        return jnp.concatenate([v_, jnp.full((rows * 512 - total,), fill, F32)]).reshape(rows, 512)

    (d_, m_, v_), _ = adamw(flat(weights, 0.0), flat(grads, 0.0), flat(mom_m, 0.0), flat(mom_v, 1.0))
    o = 0
    for nm, sz in zip(small_names, sizes):
        shp = weights[nm].shape
        delta[nm] = d_.reshape(-1)[o:o + sz].reshape(shp)
        new_m[nm] = m_.reshape(-1)[o:o + sz].reshape(shp)
        new_v[nm] = v_.reshape(-1)[o:o + sz].reshape(shp)
        o += sz

    return (loss, grad_x, *[grads[nm] for nm in names], *[delta[nm] for nm in names],
            *[new_m[nm] for nm in names], *[new_v[nm] for nm in names])
```
